```python
import jax, jax.numpy as jnp
from jax import lax
import numpy as np

D_MODEL = 1024
BATCH = 8
SEQ = 8192
DEPTH = 1

CHUNK = 64

RWKV_WIDTH = D_MODEL // 2
RWKV_HEAD = 64
RWKV_HEADS = RWKV_WIDTH // RWKV_HEAD
DECAY_LORA = 64
AAA_LORA = 64
GATE_LORA = 128
GMLP_WIDTH = D_MODEL // 2
GMLP_BLOCK = 128
GMLP_GROUPS = 8
GMLP_GROUP_DIM = GMLP_WIDTH // GMLP_GROUPS
N_BRANCH = 2
RWKV_COLS = 3 * RWKV_WIDTH + DECAY_LORA + AAA_LORA + GATE_LORA
IN_COLS = RWKV_COLS + 2 * GMLP_WIDTH + N_BRANCH * D_MODEL
N_EXPERTS = 256
TOP_K = 8
N_EXPERT_GROUPS = 8
TOPK_GROUPS = 4
EXPERT_FF = D_MODEL // 4
SHARED_FF = D_MODEL // 4
ROUTED_SCALE = 2.5
DISPATCH_BLOCK = 128
RMS_EPS = 1e-6
LN_EPS = 1e-5
GN_EPS = 64e-5

kernel_name = "hybrid_rwkv7_gmlp_moe_adaln"


def rmsnorm(x, g):
    xf = x.astype(jnp.float32)
    y = xf * lax.rsqrt(jnp.mean(xf * xf, axis=-1, keepdims=True) + RMS_EPS)
    return (y * g.astype(jnp.float32)).astype(x.dtype)


def layernorm(x, g, b, eps):
    xf = x.astype(jnp.float32)
    mu = jnp.mean(xf, axis=-1, keepdims=True)
    var = jnp.mean(jnp.square(xf - mu), axis=-1, keepdims=True)
    return (xf - mu) * lax.rsqrt(var + eps) * g.astype(jnp.float32) + b.astype(jnp.float32)


def token_shift(p, mu):
    prev = jnp.pad(p, ((0, 0), (1, 0), (0, 0)))[:, :-1]
    return p + (prev - p) * mu


def rwkv7_scan(r, w, k, v, a, b):
    bsz, _, nh, n = r.shape

    def step(state, inp):
        r_t, w_t, k_t, v_t, a_t, b_t = inp
        sa = jnp.einsum('bhvk,bhk->bhv', state, a_t)
        state = (state * w_t[:, :, None, :] + sa[..., None] * b_t[:, :, None, :]
                 + v_t[..., None] * k_t[:, :, None, :])
        y = jnp.einsum('bhvk,bhk->bhv', state, r_t)
        return state, y

    s0 = jnp.zeros((bsz, nh, n, n), jnp.float32)
    xs = tuple(jnp.swapaxes(t, 0, 1) for t in (r, w, k, v, a, b))
    _, y = lax.scan(step, s0, xs)
    return jnp.swapaxes(y, 0, 1)


def rwkv7_mixer(p, w0, w_up, a0, a_up, g_up, k_k, k_a, r_k, ln_g, ln_b):
    p = p.astype(jnp.float32)
    bsz, s, _ = p.shape
    r, k, v, xw, xa, xg = jnp.split(
        p, [RWKV_WIDTH, 2 * RWKV_WIDTH, 3 * RWKV_WIDTH, 3 * RWKV_WIDTH + DECAY_LORA,
            3 * RWKV_WIDTH + DECAY_LORA + AAA_LORA], axis=-1)
    f32 = lambda t: t.astype(jnp.float32)
    w = -jax.nn.softplus(-(f32(w0) + jnp.tanh(xw) @ f32(w_up))) - 0.5
    decay = jnp.exp(-jnp.exp(w))
    a = jax.nn.sigmoid(f32(a0) + xa @ f32(a_up))
    g = jax.nn.sigmoid(xg) @ f32(g_up)
    hs = lambda t: t.reshape(bsz, s, RWKV_HEADS, RWKV_HEAD)
    kk = hs(k * f32(k_k))
    kk = kk / jnp.maximum(jnp.sqrt(jnp.sum(kk * kk, axis=-1, keepdims=True)), 1e-12)
    k = k * (1.0 + (a - 1.0) * f32(k_a))
    rh, kh, vh, ah = hs(r), hs(k), hs(v), hs(a)
    y = rwkv7_scan(rh, hs(decay), kh, vh, -kk, kk * ah)
    mu = jnp.mean(y, axis=-1, keepdims=True)
    var = jnp.mean(jnp.square(y - mu), axis=-1, keepdims=True)
    y = ((y - mu) * lax.rsqrt(var + GN_EPS)).reshape(bsz, s, RWKV_WIDTH)
    y = y * f32(ln_g) + f32(ln_b)
    bonus = jnp.sum(rh * kh * f32(r_k), axis=-1, keepdims=True) * vh
    y = y + bonus.reshape(bsz, s, RWKV_WIDTH)
    return y * g


def gmlp_mixer(u, v, ln_g, ln_b, ws, bs):
    bsz, s, _ = v.shape
    u = jax.nn.gelu(u.astype(jnp.float32))
    v = layernorm(jax.nn.gelu(v.astype(jnp.float32)), ln_g, ln_b, LN_EPS)
    nb = s // GMLP_BLOCK
    vb = v.reshape(bsz, nb, GMLP_BLOCK, GMLP_GROUPS, GMLP_GROUP_DIM)
    mask = jnp.tril(jnp.ones((GMLP_BLOCK, GMLP_BLOCK), dtype=bool))
    wsm = jnp.where(mask[None], ws.astype(jnp.float32), 0.0)
    sv = jnp.einsum('gij,bnjgc->bnigc', wsm, vb) + bs.astype(jnp.float32).T[None, None, :, :, None]
    return u * sv.reshape(bsz, s, GMLP_WIDTH)


def moe(h, router_w, router_bias, w1, w3, w2, sw1, sw3, sw2):
    bsz, s, d = h.shape
    t = bsz * s
    hf = h.reshape(t, d)
    scores = jax.nn.sigmoid(hf.astype(jnp.float32) @ router_w.astype(jnp.float32))
    sel = scores + router_bias.astype(jnp.float32)
    grp = sel.reshape(t, N_EXPERT_GROUPS, N_EXPERTS // N_EXPERT_GROUPS)
    grp_score = jnp.sum(lax.top_k(grp, 2)[0], axis=-1)
    _, gidx = lax.top_k(grp_score, TOPK_GROUPS)
    gmask = jnp.any(gidx[..., None] == jnp.arange(N_EXPERT_GROUPS)[None, None, :], axis=1)
    emask = jnp.repeat(gmask, N_EXPERTS // N_EXPERT_GROUPS, axis=-1)
    _, eidx = lax.top_k(jnp.where(emask, sel, -jnp.inf), TOP_K)
    wts = jnp.take_along_axis(scores, eidx, axis=-1)
    wts = wts / jnp.sum(wts, axis=-1, keepdims=True) * ROUTED_SCALE

    n_assign = t * TOP_K
    flat_e = eidx.reshape(n_assign)
    flat_tok = jnp.arange(n_assign, dtype=jnp.int32) // TOP_K
    flat_w = wts.reshape(n_assign)
    order = jnp.argsort(flat_e)
    se, stok, sw = flat_e[order], flat_tok[order], flat_w[order]
    counts = jnp.bincount(flat_e, length=N_EXPERTS)
    starts = jnp.cumsum(counts) - counts
    padded = (counts + DISPATCH_BLOCK - 1) // DISPATCH_BLOCK * DISPATCH_BLOCK
    pends = jnp.cumsum(padded)
    pstarts = pends - padded
    dest = pstarts[se] + jnp.arange(n_assign, dtype=jnp.int32) - starts[se]
    n_rows = n_assign + N_EXPERTS * DISPATCH_BLOCK
    n_blocks = n_rows // DISPATCH_BLOCK
    row_tok = jnp.zeros((n_rows,), jnp.int32).at[dest].set(stok)
    row_w = jnp.zeros((n_rows,), jnp.float32).at[dest].set(sw)
    block_start = jnp.arange(n_blocks, dtype=jnp.int32) * DISPATCH_BLOCK
    block_e = jnp.clip(jnp.searchsorted(pends, block_start, side='right'), 0, N_EXPERTS - 1)

    def block_step(acc, inp):
        tok, wt, e = inp
        xb = hf[tok]
        hid = jax.nn.silu(xb @ w1[e]) * (xb @ w3[e])
        yb = hid @ w2[e]
        acc = acc.at[tok].add((yb * wt[:, None]).astype(jnp.float32))
        return acc, None

    acc0 = jnp.zeros((t, d), jnp.float32)
    routed, _ = lax.scan(block_step, acc0,
                         (row_tok.reshape(n_blocks, DISPATCH_BLOCK),
                          row_w.reshape(n_blocks, DISPATCH_BLOCK), block_e))
    shared = (jax.nn.silu(hf @ sw1) * (hf @ sw3)) @ sw2
    return (routed + shared.astype(jnp.float32)).reshape(bsz, s, d)


def setup_inputs(seed: int = 0) -> dict:
    key = jax.random.key(seed)
    ks = jax.random.split(key, 40)
    L, D = DEPTH, D_MODEL
    nrm = lambda k, shape, scale: jax.random.normal(k, shape, jnp.float32) * scale
    gain = lambda k, shape: 1.0 + nrm(k, shape, 0.02)
    return {
        "x": nrm(ks[0], (BATCH, SEQ, D), 1.0),
        "c": nrm(ks[1], (BATCH, D), 1.0),
        "ada_w": nrm(ks[2], (L, D, 6 * D), 0.3 * D ** -0.5),
        "ada_b": nrm(ks[3], (L, 6 * D), 0.02),
        "norm1_g": gain(ks[4], (L, D)),
        "norm2_g": gain(ks[5], (L, D)),
        "w_in": nrm(ks[6], (L, D, IN_COLS), D ** -0.5),
        "tshift_mu": jax.random.uniform(ks[7], (L, RWKV_COLS), jnp.float32),
        "rwkv_w0": jax.random.uniform(ks[8], (L, RWKV_WIDTH), jnp.float32, -6.0, -1.0),
        "rwkv_w_up": nrm(ks[9], (L, DECAY_LORA, RWKV_WIDTH), 0.3 * DECAY_LORA ** -0.5),
        "rwkv_a0": nrm(ks[10], (L, RWKV_WIDTH), 0.1),
        "rwkv_a_up": nrm(ks[11], (L, AAA_LORA, RWKV_WIDTH), 0.5 * AAA_LORA ** -0.5),
        "rwkv_g_up": nrm(ks[12], (L, GATE_LORA, RWKV_WIDTH), GATE_LORA ** -0.5),
        "rwkv_k_k": 0.85 + nrm(ks[13], (L, RWKV_WIDTH), 0.02),
        "rwkv_k_a": gain(ks[14], (L, RWKV_WIDTH)),
        "rwkv_r_k": nrm(ks[15], (L, RWKV_HEADS, RWKV_HEAD), 0.1),
        "rwkv_ln_g": gain(ks[16], (L, RWKV_WIDTH)),
        "rwkv_ln_b": nrm(ks[17], (L, RWKV_WIDTH), 0.02),
        "gmlp_ln_g": gain(ks[18], (L, GMLP_WIDTH)),
        "gmlp_ln_b": nrm(ks[19], (L, GMLP_WIDTH), 0.02),
        "gmlp_ws": nrm(ks[20], (L, GMLP_GROUPS, GMLP_BLOCK, GMLP_BLOCK), GMLP_BLOCK ** -0.5),
        "gmlp_bs": 1.0 + nrm(ks[21], (L, GMLP_GROUPS, GMLP_BLOCK), 0.1),
        "w_out_a": nrm(ks[22], (L, RWKV_WIDTH, D), RWKV_WIDTH ** -0.5),
        "w_out_b": nrm(ks[23], (L, GMLP_WIDTH, D), GMLP_WIDTH ** -0.5),
        "w_out": nrm(ks[24], (L, D, D), D ** -0.5),
        "router_w": nrm(ks[25], (L, D, N_EXPERTS), D ** -0.5),
        "router_bias": nrm(ks[26], (L, N_EXPERTS), 0.01),
        "exp_w1": nrm(ks[27], (L, N_EXPERTS, D, EXPERT_FF), D ** -0.5),
        "exp_w3": nrm(ks[28], (L, N_EXPERTS, D, EXPERT_FF), D ** -0.5),
        "exp_w2": nrm(ks[29], (L, N_EXPERTS, EXPERT_FF, D), EXPERT_FF ** -0.5),
        "shared_w1": nrm(ks[30], (L, D, SHARED_FF), D ** -0.5),
        "shared_w3": nrm(ks[31], (L, D, SHARED_FF), D ** -0.5),
        "shared_w2": nrm(ks[32], (L, SHARED_FF, D), SHARED_FF ** -0.5),
        "normf_g": gain(ks[33], (D,)),
    }


def reference(x, c, ada_w, ada_b, norm1_g, norm2_g, w_in, tshift_mu,
              rwkv_w0, rwkv_w_up, rwkv_a0, rwkv_a_up, rwkv_g_up, rwkv_k_k, rwkv_k_a, rwkv_r_k,
              rwkv_ln_g, rwkv_ln_b, gmlp_ln_g, gmlp_ln_b, gmlp_ws, gmlp_bs,
              w_out_a, w_out_b, w_out, router_w, router_bias,
              exp_w1, exp_w3, exp_w2, shared_w1, shared_w3, shared_w2, normf_g):
    for l in range(DEPTH):
        mod = jax.nn.silu(c) @ ada_w[l] + ada_b[l]
        sh1, sc1, g1, sh2, sc2, g2 = jnp.split(mod[:, None, :], 6, axis=-1)

        h = rmsnorm(x, norm1_g[l]) * (1.0 + sc1) + sh1
        p = h @ w_in[l]
        p_rwkv, p_u, p_v, p_gate = jnp.split(
            p, [RWKV_COLS, RWKV_COLS + GMLP_WIDTH, RWKV_COLS + 2 * GMLP_WIDTH], axis=-1)
        p_rwkv = token_shift(p_rwkv, tshift_mu[l])
        y_a = rwkv7_mixer(p_rwkv, rwkv_w0[l], rwkv_w_up[l], rwkv_a0[l], rwkv_a_up[l], rwkv_g_up[l],
                          rwkv_k_k[l], rwkv_k_a[l], rwkv_r_k[l], rwkv_ln_g[l], rwkv_ln_b[l])
        y_b = gmlp_mixer(p_u, p_v, gmlp_ln_g[l], gmlp_ln_b[l], gmlp_ws[l], gmlp_bs[l])
        gate_a, gate_b = jnp.split(jax.nn.sigmoid(p_gate.astype(jnp.float32)), 2, axis=-1)
        merged = gate_a * (y_a @ w_out_a[l]) + gate_b * (y_b @ w_out_b[l])
        x = x + (g1 * (merged @ w_out[l])).astype(x.dtype)

        h = rmsnorm(x, norm2_g[l]) * (1.0 + sc2) + sh2
        y = moe(h, router_w[l], router_bias[l], exp_w1[l], exp_w3[l], exp_w2[l],
                shared_w1[l], shared_w3[l], shared_w2[l])
        x = x + (g2 * y).astype(x.dtype)
    return rmsnorm(x, normf_g)
```

```python
import functools

import jax
import jax.numpy as jnp
from jax import lax
from jax.experimental import pallas as pl
from jax.experimental.pallas import tpu as pltpu

F32 = jnp.float32
BF16 = jnp.bfloat16

RWKV_HEAD = 64
RWKV_HEADS = 8
RWKV_WIDTH = RWKV_HEAD * RWKV_HEADS
DECAY_LORA = 64
AAA_LORA = 64
GATE_LORA = 128
RWKV_COLS = 3 * RWKV_WIDTH + DECAY_LORA + AAA_LORA + GATE_LORA
GMLP_WIDTH = 512
GMLP_BLOCK = 128
GMLP_GROUPS = 8
GMLP_GROUP_DIM = GMLP_WIDTH // GMLP_GROUPS
N_EXPERTS = 256
TOP_K = 8
N_EXPERT_GROUPS = 8
GROUP_SIZE = N_EXPERTS // N_EXPERT_GROUPS
TOPK_GROUPS = 4
ROUTED_SCALE = 2.5
RMS_EPS = 1e-6
LN_EPS = 1e-5
GN_EPS = 64e-5

VMEM_LIMIT_BYTES = 56 * 1024 * 1024

CHUNK = 64
ROW_BLOCK = 256
INPROJ_TILE = 256
MIX_TILE = 256
ROUTE_TILE = 512
RANK_TILE = 512
COMBINE_TILE = 128


def _params(*sem):
    return pltpu.CompilerParams(dimension_semantics=sem, vmem_limit_bytes=VMEM_LIMIT_BYTES)


def _mm(a, b):
    return jnp.dot(a.astype(BF16), b.astype(BF16), preferred_element_type=F32)


def _mm_nt(a, b):
    return lax.dot_general(a.astype(BF16), b.astype(BF16), (((1,), (1,)), ((), ())),
                           preferred_element_type=F32)


def _mm_tn(a, b):
    return lax.dot_general(a.astype(BF16), b.astype(BF16), (((0,), (0,)), ((), ())),
                           preferred_element_type=F32)


def _split3(a):
    hi = a.astype(BF16)
    r1 = a - hi.astype(F32)
    mid = r1.astype(BF16)
    lo = (r1 - mid.astype(F32)).astype(BF16)
    return hi, mid, lo


def _silu(x):
    return x * jax.nn.sigmoid(x)


def _rms(x):
    return x * lax.rsqrt(jnp.mean(x * x, axis=-1, keepdims=True) + RMS_EPS)


def _mod_kernel(c_ref, w_ref, b_ref, o_ref):
    s = _silu(c_ref[...])
    hi, mid, lo = _split3(s)
    whi, wmid, wlo = _split3(w_ref[...])
    dot = lambda p, q: jnp.dot(p, q, preferred_element_type=F32)
    acc = dot(hi, whi) + (dot(hi, wmid) + dot(mid, whi)) + (dot(hi, wlo) + dot(mid, wmid) + dot(lo, whi))
    o_ref[...] = acc + b_ref[...]


def _mod(c, ada_w, ada_b):
    bsz, d = c.shape
    n = ada_w.shape[1]
    tn = d
    return pl.pallas_call(
        _mod_kernel,
        grid=(n // tn,),
        in_specs=[pl.BlockSpec((bsz, d), lambda j: (0, 0)),
                  pl.BlockSpec((d, tn), lambda j: (0, j)),
                  pl.BlockSpec((1, tn), lambda j: (0, j))],
        out_specs=pl.BlockSpec((bsz, tn), lambda j: (0, j)),
        out_shape=jax.ShapeDtypeStruct((bsz, n), F32),
        compiler_params=_params("arbitrary"),
        name="mod",
    )(c, ada_w, ada_b.reshape(1, n))


def _inproj_kernel(x_ref, mod_ref, g_ref, wr_ref, wuv_ref, wg_ref, pr_ref, puv_ref, pg_ref):
    sh = mod_ref[0, 0:1, :]
    sc = mod_ref[0, 1:2, :]
    h = (_rms(x_ref[...]) * g_ref[...]) * (1.0 + sc) + sh
    hb = h.astype(BF16)
    pr_ref[...] = jnp.dot(hb, wr_ref[...], preferred_element_type=F32)
    puv_ref[...] = jnp.dot(hb, wuv_ref[...], preferred_element_type=F32)
    pg_ref[...] = jnp.dot(hb, wg_ref[...], preferred_element_type=F32)


def _inproj(xf, mod3, norm1_g, w_in, seq):
    t, d = xf.shape
    tm = min(INPROJ_TILE, seq)
    per_seq = seq // tm
    wr = w_in[:, :RWKV_COLS].astype(BF16)
    wuv = w_in[:, RWKV_COLS:RWKV_COLS + 2 * GMLP_WIDTH].astype(BF16)
    wg = w_in[:, RWKV_COLS + 2 * GMLP_WIDTH:].astype(BF16)
    full = lambda a: pl.BlockSpec(a.shape, lambda i: (0,) * a.ndim)
    g = norm1_g.reshape(1, d)
    return pl.pallas_call(
        _inproj_kernel,
        grid=(t // tm,),
        in_specs=[pl.BlockSpec((tm, d), lambda i: (i, 0)),
                  pl.BlockSpec((1,) + mod3.shape[1:], lambda i: (i // per_seq, 0, 0)),
                  full(g), full(wr), full(wuv), full(wg)],
        out_specs=[pl.BlockSpec((tm, wr.shape[1]), lambda i: (i, 0)),
                   pl.BlockSpec((tm, wuv.shape[1]), lambda i: (i, 0)),
                   pl.BlockSpec((tm, wg.shape[1]), lambda i: (i, 0))],
        out_shape=[jax.ShapeDtypeStruct((t, wr.shape[1]), F32),
                   jax.ShapeDtypeStruct((t, wuv.shape[1]), F32),
                   jax.ShapeDtypeStruct((t, wg.shape[1]), F32)],
        compiler_params=_params("arbitrary"),
        name="inproj",
    )(xf, mod3, g, wr, wuv, wg)


def _unit_lower_inverse(n_strict, row, col):
    c = n_strict.shape[0]
    eye = jnp.where(row == col, 1.0, 0.0).astype(F32)
    n8 = jnp.where((row // 8) == (col // 8), n_strict, 0.0)
    n8_2 = _mm(n8, n8)
    n8_4 = _mm(n8_2, n8_2)
    inv = eye + n8
    inv = inv + _mm(inv, n8_2)
    inv = inv + _mm(inv, n8_4)
    s = 8
    while s < c:
        low = jnp.where(((row // (2 * s)) == (col // (2 * s))) & ((row // s) != (col // s)), n_strict, 0.0)
        inv = inv + _mm(inv, _mm(low, inv))
        s *= 2
    return inv


def _rwkv_kernel(p_ref, mu_ref, w0_ref, wup_ref, a0_ref, aup_ref, gup_ref, kk_ref, ka_ref, rk_ref,
                 lng_ref, lnb_ref, y_ref, state_ref, carry_ref, ycat_ref):
    c = p_ref.shape[0]
    hd = RWKV_HEAD

    @pl.when(pl.program_id(1) == 0)
    def _():
        state_ref[...] = jnp.zeros_like(state_ref)
        carry_ref[...] = jnp.zeros_like(carry_ref)

    p = p_ref[...]
    rowc = lax.broadcasted_iota(jnp.int32, (c, 1), 0)
    prev = jnp.where(rowc == 0, carry_ref[...], pltpu.roll(p, 1, axis=0))
    carry_ref[...] = p[c - 1:c, :]
    ps = p + (prev - p) * mu_ref[...]

    o = RWKV_WIDTH
    r = ps[:, 0:o]
    k = ps[:, o:2 * o]
    v = ps[:, 2 * o:3 * o]
    xw = ps[:, 3 * o:3 * o + DECAY_LORA]
    xa = ps[:, 3 * o + DECAY_LORA:3 * o + DECAY_LORA + AAA_LORA]
    xg = ps[:, 3 * o + DECAY_LORA + AAA_LORA:]

    z = -(w0_ref[...] + _mm(jnp.tanh(xw), wup_ref[...]))
    softplus = jnp.maximum(z, 0.0) + jnp.log1p(jnp.exp(-jnp.abs(z)))
    wlog = -softplus - 0.5
    ld = -jnp.exp(wlog)
    a = jax.nn.sigmoid(a0_ref[...] + _mm(xa, aup_ref[...]))
    g = _mm(jax.nn.sigmoid(xg), gup_ref[...])

    kkf = k * kk_ref[...]
    kk2 = kkf * kkf
    inv_norm = jnp.concatenate(
        [jnp.broadcast_to(
            1.0 / jnp.maximum(jnp.sqrt(jnp.sum(kk2[:, h * hd:(h + 1) * hd], axis=-1, keepdims=True)), 1e-12),
            (c, hd)) for h in range(RWKV_HEADS)], axis=1)
    kkn = kkf * inv_norm
    k2 = k * (1.0 + (a - 1.0) * ka_ref[...])
    avec = -kkn
    bvec = kkn * a

    row = lax.broadcasted_iota(jnp.int32, (c, c), 0)
    col = lax.broadcasted_iota(jnp.int32, (c, c), 1)
    tri = jnp.where(row >= col, 1.0, 0.0).astype(BF16)
    hi, mid, lo = _split3(ld)
    dot = lambda q: jnp.dot(tri, q, preferred_element_type=F32)
    cl = dot(hi) + dot(mid) + dot(lo)
    cl_end = cl[c - 1:c, :]
    g_in = jnp.exp(cl)
    g_inv = jnp.exp(-cl)
    g_end = jnp.exp(cl_end)
    g_tail = jnp.exp(cl_end - cl)
    rt = r * g_in
    at = avec * jnp.exp(cl - ld)
    kt = k2 * g_inv
    bt = bvec * g_inv
    kend = k2 * g_tail
    bend = bvec * g_tail

    strict = row > col
    incl = row >= col
    for h in range(RWKV_HEADS):
        sl = slice(h * hd, (h + 1) * hd)
        lhs = jnp.concatenate([at[:, sl], rt[:, sl]], axis=0)
        rhs = jnp.concatenate([bt[:, sl], kt[:, sl]], axis=0)
        amat = _mm_nt(lhs, rhs)
        a_ab = jnp.where(strict, amat[:c, :c], 0.0)
        a_ak = jnp.where(strict, amat[:c, c:], 0.0)
        a_rb = jnp.where(incl, amat[c:, :c], 0.0)
        a_rk = jnp.where(incl, amat[c:, c:], 0.0)
        tinv = _unit_lower_inverse(a_ab, row, col)
        vh = v[:, sl]
        wmat = _mm(tinv, at[:, sl])
        ut = _mm(tinv, _mm(a_ak, vh))
        s_old = state_ref[h]
        u = _mm_nt(wmat, s_old) + ut
        yh = _mm_nt(rt[:, sl], s_old) + _mm(a_rb, u) + _mm(a_rk, vh)
        state_ref[h] = s_old * g_end[:, sl] + _mm_tn(u, bend[:, sl]) + _mm_tn(vh, kend[:, sl])
        ycat_ref[:, sl] = yh

    y = ycat_ref[...]
    rkk = r * k2 * rk_ref[...]
    mean_l, rstd_l, bonus_l = [], [], []
    for h in range(RWKV_HEADS):
        sl = slice(h * hd, (h + 1) * hd)
        yh = y[:, sl]
        mu = jnp.mean(yh, axis=-1, keepdims=True)
        var = jnp.mean(jnp.square(yh - mu), axis=-1, keepdims=True)
        mean_l.append(jnp.broadcast_to(mu, (c, hd)))
        rstd_l.append(jnp.broadcast_to(lax.rsqrt(var + GN_EPS), (c, hd)))
        bonus_l.append(jnp.broadcast_to(jnp.sum(rkk[:, sl], axis=-1, keepdims=True), (c, hd)))
    mean = jnp.concatenate(mean_l, axis=1)
    rstd = jnp.concatenate(rstd_l, axis=1)
    bonus = jnp.concatenate(bonus_l, axis=1)
    yn = (y - mean) * rstd * lng_ref[...] + lnb_ref[...]
    y_ref[...] = (yn + bonus * v) * g


def _rwkv(p_rwkv, bsz, seq, tshift_mu, w0, w_up, a0, a_up, g_up, k_k, k_a, r_k, ln_g, ln_b):
    t = p_rwkv.shape[0]
    c = min(CHUNK, seq)
    nchunk = seq // c
    row = lambda a: a.reshape(1, -1)
    args = (row(tshift_mu), row(w0), w_up, row(a0), a_up, g_up, row(k_k), row(k_a), row(r_k),
            row(ln_g), row(ln_b))
    full = lambda a: pl.BlockSpec(a.shape, lambda b, j: (0,) * a.ndim)
    return pl.pallas_call(
        _rwkv_kernel,
        grid=(bsz, nchunk),
        in_specs=[pl.BlockSpec((c, RWKV_COLS), lambda b, j: (b * nchunk + j, 0))] + [full(a) for a in args],
        out_specs=pl.BlockSpec((c, RWKV_WIDTH), lambda b, j: (b * nchunk + j, 0)),
        out_shape=jax.ShapeDtypeStruct((t, RWKV_WIDTH), F32),
        scratch_shapes=[pltpu.VMEM((RWKV_HEADS, RWKV_HEAD, RWKV_HEAD), F32),
                        pltpu.VMEM((1, RWKV_COLS), F32),
                        pltpu.VMEM((c, RWKV_WIDTH), F32)],
        compiler_params=_params("arbitrary", "arbitrary"),
        name="rwkv",
    )(p_rwkv, *args)


def _mix_kernel(puv_ref, pg_ref, ya_ref, x_ref, mod_ref, lng_ref, lnb_ref, ws_ref, bsm_ref,
                woa_ref, wob_ref, wo_ref, n2g_ref, x1_ref, h2_ref, yb_ref):
    tm = x_ref.shape[0]
    gw = GMLP_WIDTH
    puv = puv_ref[...]
    u = jax.nn.gelu(puv[:, :gw])
    vg = jax.nn.gelu(puv[:, gw:])
    mu = jnp.mean(vg, axis=-1, keepdims=True)
    var = jnp.mean(jnp.square(vg - mu), axis=-1, keepdims=True)
    vln = (vg - mu) * lax.rsqrt(var + LN_EPS) * lng_ref[...] + lnb_ref[...]

    row = lax.broadcasted_iota(jnp.int32, (GMLP_BLOCK, GMLP_BLOCK), 0)
    col = lax.broadcasted_iota(jnp.int32, (GMLP_BLOCK, GMLP_BLOCK), 1)
    gd = GMLP_GROUP_DIM
    for gi in range(GMLP_GROUPS):
        wsm = jnp.where(row >= col, ws_ref[gi], 0.0).astype(BF16)
        for nb in range(tm // GMLP_BLOCK):
            rs = slice(nb * GMLP_BLOCK, (nb + 1) * GMLP_BLOCK)
            cs = slice(gi * gd, (gi + 1) * gd)
            sv = jnp.dot(wsm, vln[rs, cs].astype(BF16), preferred_element_type=F32) + bsm_ref[:, cs]
            yb_ref[rs, cs] = u[rs, cs] * sv

    pg = pg_ref[...]
    d = x_ref.shape[1]
    gate_a = jax.nn.sigmoid(pg[:, :d])
    gate_b = jax.nn.sigmoid(pg[:, d:])
    merged = gate_a * _mm(ya_ref[...], woa_ref[...]) + gate_b * _mm(yb_ref[...], wob_ref[...])
    g1 = mod_ref[0, 2:3, :]
    x1 = x_ref[...] + g1 * _mm(merged, wo_ref[...])
    x1_ref[...] = x1
    sh2 = mod_ref[0, 3:4, :]
    sc2 = mod_ref[0, 4:5, :]
    h2_ref[...] = (_rms(x1) * n2g_ref[...]) * (1.0 + sc2) + sh2


def _mix(p_uv, p_gate, y_a, xf, mod3, seq, ln_g, ln_b, ws, bs, w_out_a, w_out_b, w_out, norm2_g):
    t, d = xf.shape
    tm = min(MIX_TILE, seq)
    per_seq = seq // tm
    row = lambda a: a.reshape(1, -1)
    bsm = jnp.repeat(bs.T, GMLP_GROUP_DIM, axis=1)
    consts = (row(ln_g), row(ln_b), ws, bsm, w_out_a.astype(BF16), w_out_b.astype(BF16),
              w_out.astype(BF16), row(norm2_g))
    full = lambda a: pl.BlockSpec(a.shape, lambda i: (0,) * a.ndim)
    tile = lambda a: pl.BlockSpec((tm, a.shape[1]), lambda i: (i, 0))
    return pl.pallas_call(
        _mix_kernel,
        grid=(t // tm,),
        in_specs=[tile(p_uv), tile(p_gate), tile(y_a), tile(xf),
                  pl.BlockSpec((1,) + mod3.shape[1:], lambda i: (i // per_seq, 0, 0))]
                 + [full(a) for a in consts],
        out_specs=[pl.BlockSpec((tm, d), lambda i: (i, 0)), pl.BlockSpec((tm, d), lambda i: (i, 0))],
        out_shape=[jax.ShapeDtypeStruct((t, d), F32), jax.ShapeDtypeStruct((t, d), F32)],
        scratch_shapes=[pltpu.VMEM((tm, GMLP_WIDTH), F32)],
        compiler_params=_params("arbitrary"),
        name="mix",
    )(p_uv, p_gate, y_a, xf, mod3, *consts)


def _first_argmax(vals, idx, big):
    m = jnp.max(vals, axis=0, keepdims=True)
    first = jnp.min(jnp.where(vals == m, idx, big), axis=0, keepdims=True)
    return m, first


def _route_kernel(h_ref, rwt_ref, bias_ref, eidx_ref, wts_ref):
    tm = h_ref.shape[0]
    ne = N_EXPERTS
    hhi, hmid, hlo = _split3(h_ref[...])
    whi, wmid, wlo = _split3(rwt_ref[...])
    nt = lambda p, q: lax.dot_general(p, q, (((1,), (1,)), ((), ())), preferred_element_type=F32)
    logits = (nt(whi, hhi) + (nt(whi, hmid) + nt(wmid, hhi))
              + (nt(whi, hlo) + nt(wmid, hmid) + nt(wlo, hhi)))
    scores = jax.nn.sigmoid(logits)
    sel = scores + bias_ref[...]
    neg = -jnp.inf

    gs = GROUP_SIZE
    gidx = lax.broadcasted_iota(jnp.int32, (gs, tm), 0)
    grp_rows = []
    for gi in range(N_EXPERT_GROUPS):
        blk = sel[gi * gs:(gi + 1) * gs, :]
        m1, i1 = _first_argmax(blk, gidx, gs)
        m2 = jnp.max(jnp.where(gidx == i1, neg, blk), axis=0, keepdims=True)
        grp_rows.append(m1 + m2)
    grp = jnp.concatenate(grp_rows, axis=0)

    ng = N_EXPERT_GROUPS
    giota = lax.broadcasted_iota(jnp.int32, (ng, tm), 0)
    gsel = jnp.zeros((ng, tm), jnp.bool_)
    work = grp
    for _ in range(TOPK_GROUPS):
        _, gi1 = _first_argmax(work, giota, ng)
        hit = giota == gi1
        gsel = gsel | hit
        work = jnp.where(hit, neg, work)
    gself = jnp.where(gsel, 1.0, 0.0)
    emask = jnp.concatenate(
        [jnp.broadcast_to(gself[gi:gi + 1, :], (gs, tm)) for gi in range(ng)], axis=0) > 0.5

    eiota = lax.broadcasted_iota(jnp.int32, (ne, tm), 0)
    work = jnp.where(emask, sel, neg)
    idx_rows, w_rows = [], []
    for _ in range(TOP_K):
        _, e1 = _first_argmax(work, eiota, ne)
        hit = eiota == e1
        idx_rows.append(e1)
        w_rows.append(jnp.sum(jnp.where(hit, scores, 0.0), axis=0, keepdims=True))
        work = jnp.where(hit, neg, work)
    eidx_ref[...] = jnp.concatenate(idx_rows, axis=0)
    w = jnp.concatenate(w_rows, axis=0)
    wts_ref[...] = w / jnp.sum(w, axis=0, keepdims=True) * ROUTED_SCALE


def _route(h2, router_w, router_bias):
    t, d = h2.shape
    tm = min(ROUTE_TILE, t)
    rwt = router_w.T
    bias = router_bias.reshape(N_EXPERTS, 1)
    return pl.pallas_call(
        _route_kernel,
        grid=(t // tm,),
        in_specs=[pl.BlockSpec((tm, d), lambda i: (i, 0)),
                  pl.BlockSpec(rwt.shape, lambda i: (0, 0)),
                  pl.BlockSpec(bias.shape, lambda i: (0, 0))],
        out_specs=[pl.BlockSpec((TOP_K, tm), lambda i: (0, i)), pl.BlockSpec((TOP_K, tm), lambda i: (0, i))],
        out_shape=[jax.ShapeDtypeStruct((TOP_K, t), jnp.int32), jax.ShapeDtypeStruct((TOP_K, t), F32)],
        compiler_params=_params("arbitrary"),
        name="route",
    )(h2, rwt, bias)


def _rank_kernel(e_ref, upper_ref, rank_ref, count_ref):
    tm = e_ref.shape[1]
    ne = N_EXPERTS

    @pl.when(pl.program_id(0) == 0)
    def _():
        count_ref[...] = jnp.zeros_like(count_ref)

    e = e_ref[...]
    eiota = lax.broadcasted_iota(jnp.int32, (ne, tm), 0)
    onehot = jnp.zeros((ne, tm), F32)
    for kk in range(TOP_K):
        onehot = onehot + jnp.where(eiota == e[kk:kk + 1, :], 1.0, 0.0)
    before = jnp.dot(onehot.astype(BF16), upper_ref[...], preferred_element_type=F32)
    base = count_ref[:, 0:1]
    tot = before + base
    rows = [jnp.sum(jnp.where(eiota == e[kk:kk + 1, :], tot, 0.0), axis=0, keepdims=True)
            for kk in range(TOP_K)]
    rank_ref[...] = jnp.concatenate(rows, axis=0).astype(jnp.int32)
    count_ref[...] = count_ref[...] + jnp.sum(onehot, axis=1, keepdims=True)


def _rank(eidx):
    k, t = eidx.shape
    tm = min(RANK_TILE, t)
    upper = jnp.triu(jnp.ones((tm, tm), F32), 1).astype(BF16)
    return pl.pallas_call(
        _rank_kernel,
        grid=(t // tm,),
        in_specs=[pl.BlockSpec((k, tm), lambda i: (0, i)), pl.BlockSpec((tm, tm), lambda i: (0, 0))],
        out_specs=[pl.BlockSpec((k, tm), lambda i: (0, i)), pl.BlockSpec((N_EXPERTS, 128), lambda i: (0, 0))],
        out_shape=[jax.ShapeDtypeStruct((k, t), jnp.int32), jax.ShapeDtypeStruct((N_EXPERTS, 128), F32)],
        compiler_params=_params("arbitrary"),
        name="rank",
    )(eidx, upper)


def _row_copy(src_hbm, dst, sem, src_row, dst_row):
    return pltpu.make_async_copy(src_hbm.at[pl.ds(src_row, 1), :], dst.at[pl.ds(dst_row, 1), :], sem)


def _expert_kernel(be_ref, nv_ref, tok_ref, tokn_ref, h_hbm, w1_ref, w3_ref, w2_ref, y_ref, xbuf, sems):
    i = pl.program_id(0)
    nvalid = nv_ref[0]
    rb = xbuf.shape[1]
    slot = i % 2

    def gather(idx_ref, s):
        def body(r, carry):
            _row_copy(h_hbm, xbuf.at[s], sems.at[s], idx_ref[0, 0, r], r).start()
            return carry
        lax.fori_loop(0, rb, body, 0)

    @pl.when(i == 0)
    def _():
        gather(tok_ref, 0)

    @pl.when(i + 1 < nvalid)
    def _():
        gather(tokn_ref, 1 - slot)

    @pl.when(i < nvalid)
    def _():
        pltpu.make_async_copy(xbuf.at[slot], xbuf.at[slot], sems.at[slot]).wait()
        xb = xbuf[slot].astype(BF16)
        hid = _silu(jnp.dot(xb, w1_ref[0], preferred_element_type=F32)) * jnp.dot(
            xb, w3_ref[0], preferred_element_type=F32)
        y_ref[...] = jnp.dot(hid.astype(BF16), w2_ref[0], preferred_element_type=F32)

    @pl.when(i >= nvalid)
    def _():
        y_ref[...] = jnp.zeros_like(y_ref)


def _expert(h2, row_tok, block_e, nvalid, w1, w3, w2):
    t, d = h2.shape
    nblk = block_e.shape[0]
    rb = ROW_BLOCK
    f = w1.shape[2]
    tok3 = row_tok.reshape(nblk, 1, rb)
    grid_spec = pltpu.PrefetchScalarGridSpec(
        num_scalar_prefetch=2,
        grid=(nblk,),
        in_specs=[
            pl.BlockSpec((1, 1, rb), lambda i, be, nv: (i, 0, 0), memory_space=pltpu.SMEM),
            pl.BlockSpec((1, 1, rb), lambda i, be, nv: (jnp.minimum(i + 1, nblk - 1), 0, 0),
                         memory_space=pltpu.SMEM),
            pl.BlockSpec(memory_space=pl.ANY),
            pl.BlockSpec((1, d, f), lambda i, be, nv: (be[i], 0, 0)),
            pl.BlockSpec((1, d, f), lambda i, be, nv: (be[i], 0, 0)),
            pl.BlockSpec((1, f, d), lambda i, be, nv: (be[i], 0, 0)),
        ],
        out_specs=pl.BlockSpec((rb, d), lambda i, be, nv: (i, 0)),
        scratch_shapes=[pltpu.VMEM((2, rb, d), F32), pltpu.SemaphoreType.DMA((2,))],
    )
    return pl.pallas_call(
        _expert_kernel,
        grid_spec=grid_spec,
        out_shape=jax.ShapeDtypeStruct((nblk * rb, d), F32),
        compiler_params=_params("arbitrary"),
        name="expert",
    )(block_e, nvalid, tok3, tok3, h2, w1, w3, w2)


def _combine_kernel(dst_ref, dstn_ref, y_hbm, wts_ref, h_ref, x1_ref, mod_ref, sw1_ref, sw3_ref, sw2_ref,
                    nfg_ref, o_ref, ybuf, sems):
    i = pl.program_id(0)
    n = pl.num_programs(0)
    tm = ybuf.shape[2]
    slot = i % 2

    def gather(idx_ref, s):
        def body(j, carry):
            for kk in range(TOP_K):
                _row_copy(y_hbm, ybuf.at[s, kk], sems.at[s], idx_ref[0, 0, kk * tm + j], j).start()
            return carry
        lax.fori_loop(0, tm, body, 0)

    @pl.when(i == 0)
    def _():
        gather(dst_ref, 0)

    @pl.when(i + 1 < n)
    def _():
        gather(dstn_ref, 1 - slot)

    pltpu.make_async_copy(ybuf.at[slot], ybuf.at[slot], sems.at[slot]).wait()
    w = wts_ref[...]
    routed = ybuf[slot, 0] * w[:, 0:1]
    for kk in range(1, TOP_K):
        routed = routed + ybuf[slot, kk] * w[:, kk:kk + 1]
    hb = h_ref[...].astype(BF16)
    hid = _silu(jnp.dot(hb, sw1_ref[...], preferred_element_type=F32)) * jnp.dot(
        hb, sw3_ref[...], preferred_element_type=F32)
    shared = jnp.dot(hid.astype(BF16), sw2_ref[...], preferred_element_type=F32)
    g2 = mod_ref[0, 5:6, :]
    x2 = x1_ref[...] + g2 * (routed + shared)
    o_ref[...] = _rms(x2) * nfg_ref[...]


def _combine(y_sorted, dest_tk, wts_tk, h2, x1, mod3, seq, sw1, sw3, sw2, normf_g):
    t, d = h2.shape
    tm = min(COMBINE_TILE, seq)
    per_seq = seq // tm
    nt = t // tm
    dst3 = dest_tk.reshape(TOP_K, nt, tm).transpose(1, 0, 2).reshape(nt, 1, TOP_K * tm)
    consts = (sw1.astype(BF16), sw3.astype(BF16), sw2.astype(BF16), normf_g.reshape(1, d))
    full = lambda a: pl.BlockSpec(a.shape, lambda i: (0,) * a.ndim)
    tile = lambda a: pl.BlockSpec((tm, a.shape[1]), lambda i: (i, 0))
    return pl.pallas_call(
        _combine_kernel,
        grid=(nt,),
        in_specs=[pl.BlockSpec((1, 1, TOP_K * tm), lambda i: (i, 0, 0), memory_space=pltpu.SMEM),
                  pl.BlockSpec((1, 1, TOP_K * tm), lambda i: (jnp.minimum(i + 1, nt - 1), 0, 0),
                               memory_space=pltpu.SMEM),
                  pl.BlockSpec(memory_space=pl.ANY),
                  tile(wts_tk), tile(h2), tile(x1),
                  pl.BlockSpec((1,) + mod3.shape[1:], lambda i: (i // per_seq, 0, 0))]
                 + [full(a) for a in consts],
        out_specs=pl.BlockSpec((tm, d), lambda i: (i, 0)),
        out_shape=jax.ShapeDtypeStruct((t, d), F32),
        scratch_shapes=[pltpu.VMEM((2, TOP_K, tm, d), F32), pltpu.SemaphoreType.DMA((2,))],
        compiler_params=_params("arbitrary"),
        name="combine",
    )(dst3, dst3, y_sorted, wts_tk, h2, x1, mod3, *consts)


def _dispatch_plan(eidx, rank, counts, t):
    rb = ROW_BLOCK
    counts = counts.astype(jnp.int32)
    padded = (counts + rb - 1) // rb * rb
    pends = jnp.cumsum(padded)
    pstarts = pends - padded
    dest = pstarts[eidx] + rank
    n_rows = t * TOP_K + N_EXPERTS * rb
    nblk = n_rows // rb
    tok = jnp.broadcast_to(jnp.arange(t, dtype=jnp.int32)[None, :], (TOP_K, t))
    row_tok = jnp.zeros((n_rows,), jnp.int32).at[dest.reshape(-1)].set(tok.reshape(-1))
    block_start = jnp.arange(nblk, dtype=jnp.int32) * rb
    block_e = jnp.clip(jnp.searchsorted(pends, block_start, side="right"), 0, N_EXPERTS - 1).astype(jnp.int32)
    nvalid = (pends[-1] // rb).astype(jnp.int32).reshape(1)
    return dest, row_tok, block_e, nvalid


def kernel(x, c, ada_w, ada_b, norm1_g, norm2_g, w_in, tshift_mu, rwkv_w0, rwkv_w_up, rwkv_a0, rwkv_a_up,
           rwkv_g_up, rwkv_k_k, rwkv_k_a, rwkv_r_k, rwkv_ln_g, rwkv_ln_b, gmlp_ln_g, gmlp_ln_b, gmlp_ws,
           gmlp_bs, w_out_a, w_out_b, w_out, router_w, router_bias, exp_w1, exp_w3, exp_w2, shared_w1,
           shared_w3, shared_w2, normf_g):
    bsz, seq, d = x.shape
    t = bsz * seq
    xf = x.reshape(t, d)
    assert ada_w.shape[0] == 1, "single-layer block only"
    l = 0
    mod3 = _mod(c, ada_w[l], ada_b[l]).reshape(bsz, 6, d)
    p_rwkv, p_uv, p_gate = _inproj(xf, mod3, norm1_g[l], w_in[l], seq)
    y_a = _rwkv(p_rwkv, bsz, seq, tshift_mu[l], rwkv_w0[l], rwkv_w_up[l], rwkv_a0[l], rwkv_a_up[l],
                rwkv_g_up[l], rwkv_k_k[l], rwkv_k_a[l], rwkv_r_k[l], rwkv_ln_g[l], rwkv_ln_b[l])
    x1, h2 = _mix(p_uv, p_gate, y_a, xf, mod3, seq, gmlp_ln_g[l], gmlp_ln_b[l], gmlp_ws[l], gmlp_bs[l],
                  w_out_a[l], w_out_b[l], w_out[l], norm2_g[l])
    eidx, wts = _route(h2, router_w[l], router_bias[l])
    rank, counts = _rank(eidx)
    dest, row_tok, block_e, nvalid = _dispatch_plan(eidx, rank, counts[:, 0], t)
    y_sorted = _expert(h2, row_tok, block_e, nvalid, exp_w1[l].astype(BF16), exp_w3[l].astype(BF16),
                       exp_w2[l].astype(BF16))
    out = _combine(y_sorted, dest, wts.T, h2, x1, mod3, seq, shared_w1[l], shared_w3[l], shared_w2[l],
                   normf_g)
    return out.reshape(bsz, seq, d)
```

```python
import functools

import jax
import jax.numpy as jnp
from jax import lax
from jax.experimental import pallas as pl
from jax.experimental.pallas import tpu as pltpu

F32 = jnp.float32
BF16 = jnp.bfloat16

RWKV_HEAD = 64
RWKV_HEADS = 8
RWKV_WIDTH = RWKV_HEAD * RWKV_HEADS
DECAY_LORA = 64
AAA_LORA = 64
GATE_LORA = 128
RWKV_COLS = 3 * RWKV_WIDTH + DECAY_LORA + AAA_LORA + GATE_LORA
GMLP_WIDTH = 512
GMLP_BLOCK = 128
GMLP_GROUPS = 8
GMLP_GROUP_DIM = GMLP_WIDTH // GMLP_GROUPS
N_EXPERTS = 256
TOP_K = 8
N_EXPERT_GROUPS = 8
GROUP_SIZE = N_EXPERTS // N_EXPERT_GROUPS
TOPK_GROUPS = 4
ROUTED_SCALE = 2.5
RMS_EPS = 1e-6
LN_EPS = 1e-5
GN_EPS = 64e-5

VMEM_LIMIT_BYTES = 56 * 1024 * 1024

CHUNK = 64
RWKV_TILE = 256
ROW_BLOCK = 256
INPROJ_TILE = 256
MIX_TILE = 256
ROUTE_TILE = 512
RANK_TILE = 512
DISPATCH_TILE = 256
COMBINE_TILE = 128


def _params(*sem):
    return pltpu.CompilerParams(dimension_semantics=sem, vmem_limit_bytes=VMEM_LIMIT_BYTES)


def _mm(a, b):
    return jnp.dot(a.astype(BF16), b.astype(BF16), preferred_element_type=F32)


def _mm_nt(a, b):
    return lax.dot_general(a.astype(BF16), b.astype(BF16), (((1,), (1,)), ((), ())),
                           preferred_element_type=F32)


def _mm_tn(a, b):
    return lax.dot_general(a.astype(BF16), b.astype(BF16), (((0,), (0,)), ((), ())),
                           preferred_element_type=F32)


def _split3(a):
    hi = a.astype(BF16)
    r1 = a - hi.astype(F32)
    mid = r1.astype(BF16)
    lo = (r1 - mid.astype(F32)).astype(BF16)
    return hi, mid, lo


def _silu(x):
    return x * jax.nn.sigmoid(x)


def _rms(x):
    return x * lax.rsqrt(jnp.mean(x * x, axis=-1, keepdims=True) + RMS_EPS)


def _row_copy(src, dst, sem, src_row, dst_row):
    return pltpu.make_async_copy(src.at[pl.ds(src_row, 1), :], dst.at[pl.ds(dst_row, 1), :], sem)


def _mod_kernel(c_ref, w_ref, b_ref, o_ref):
    s = _silu(c_ref[...])
    hi, mid, lo = _split3(s)
    whi, wmid, wlo = _split3(w_ref[...])
    dot = lambda p, q: jnp.dot(p, q, preferred_element_type=F32)
    acc = dot(hi, whi) + (dot(hi, wmid) + dot(mid, whi)) + (dot(hi, wlo) + dot(mid, wmid) + dot(lo, whi))
    o_ref[...] = acc + b_ref[...]


def _mod(c, ada_w, ada_b):
    bsz, d = c.shape
    n = ada_w.shape[1]
    tn = d
    return pl.pallas_call(
        _mod_kernel,
        grid=(n // tn,),
        in_specs=[pl.BlockSpec((bsz, d), lambda j: (0, 0)),
                  pl.BlockSpec((d, tn), lambda j: (0, j)),
                  pl.BlockSpec((1, tn), lambda j: (0, j))],
        out_specs=pl.BlockSpec((bsz, tn), lambda j: (0, j)),
        out_shape=jax.ShapeDtypeStruct((bsz, n), F32),
        compiler_params=_params("arbitrary"),
        name="mod",
    )(c, ada_w, ada_b.reshape(1, n))


def _inproj_kernel(x_ref, mod_ref, g_ref, wr_ref, wuv_ref, wg_ref, pr_ref, puv_ref, pg_ref):
    sh = mod_ref[0, 0:1, :]
    sc = mod_ref[0, 1:2, :]
    h = (_rms(x_ref[...]) * g_ref[...]) * (1.0 + sc) + sh
    hb = h.astype(BF16)
    pr_ref[...] = jnp.dot(hb, wr_ref[...], preferred_element_type=F32)
    puv_ref[...] = jnp.dot(hb, wuv_ref[...], preferred_element_type=F32)
    pg_ref[...] = jnp.dot(hb, wg_ref[...], preferred_element_type=F32)


def _inproj(xf, mod3, norm1_g, w_in, seq):
    t, d = xf.shape
    tm = min(INPROJ_TILE, seq)
    per_seq = seq // tm
    wr = w_in[:, :RWKV_COLS].astype(BF16)
    wuv = w_in[:, RWKV_COLS:RWKV_COLS + 2 * GMLP_WIDTH].astype(BF16)
    wg = w_in[:, RWKV_COLS + 2 * GMLP_WIDTH:].astype(BF16)
    full = lambda a: pl.BlockSpec(a.shape, lambda i: (0,) * a.ndim)
    g = norm1_g.reshape(1, d)
    return pl.pallas_call(
        _inproj_kernel,
        grid=(t // tm,),
        in_specs=[pl.BlockSpec((tm, d), lambda i: (i, 0)),
                  pl.BlockSpec((1,) + mod3.shape[1:], lambda i: (i // per_seq, 0, 0)),
                  full(g), full(wr), full(wuv), full(wg)],
        out_specs=[pl.BlockSpec((tm, wr.shape[1]), lambda i: (i, 0)),
                   pl.BlockSpec((tm, wuv.shape[1]), lambda i: (i, 0)),
                   pl.BlockSpec((tm, wg.shape[1]), lambda i: (i, 0))],
        out_shape=[jax.ShapeDtypeStruct((t, wr.shape[1]), F32),
                   jax.ShapeDtypeStruct((t, wuv.shape[1]), F32),
                   jax.ShapeDtypeStruct((t, wg.shape[1]), F32)],
        compiler_params=_params("arbitrary"),
        name="inproj",
    )(xf, mod3, g, wr, wuv, wg)


def _unit_lower_inverses(ns, row, col):
    c = ns[0].shape[0]
    eye = jnp.where(row == col, 1.0, 0.0).astype(F32)
    blk8 = (row // 8) == (col // 8)
    n8 = [jnp.where(blk8, n, 0.0) for n in ns]
    n8_2 = [_mm(a, a) for a in n8]
    n8_4 = [_mm(a, a) for a in n8_2]
    inv = [eye + a for a in n8]
    inv = [i + _mm(i, b) for i, b in zip(inv, n8_2)]
    inv = [i + _mm(i, b) for i, b in zip(inv, n8_4)]
    s = 8
    while s < c:
        sel = ((row // (2 * s)) == (col // (2 * s))) & ((row // s) != (col // s))
        low = [jnp.where(sel, n, 0.0) for n in ns]
        t1 = [_mm(l, i) for l, i in zip(low, inv)]
        inv = [i + _mm(i, t) for i, t in zip(inv, t1)]
        s *= 2
    return inv


def _rwkv_kernel(p_ref, mu_ref, w0_ref, wup_ref, a0_ref, aup_ref, gup_ref, kk_ref, ka_ref, rk_ref,
                 lng_ref, lnb_ref, y_ref, state_ref, carry_ref, ycat_ref, *, c):
    tm = p_ref.shape[0]
    nq = tm // c
    hd = RWKV_HEAD
    nh = RWKV_HEADS

    @pl.when(pl.program_id(1) == 0)
    def _():
        state_ref[...] = jnp.zeros_like(state_ref)
        carry_ref[...] = jnp.zeros_like(carry_ref)

    p = p_ref[...]
    rowc = lax.broadcasted_iota(jnp.int32, (tm, 1), 0)
    prev = jnp.where(rowc == 0, carry_ref[...], pltpu.roll(p, 1, axis=0))
    carry_ref[...] = p[tm - 1:tm, :]
    ps = p + (prev - p) * mu_ref[...]

    o = RWKV_WIDTH
    r = ps[:, 0:o]
    k = ps[:, o:2 * o]
    v = ps[:, 2 * o:3 * o]
    xw = ps[:, 3 * o:3 * o + DECAY_LORA]
    xa = ps[:, 3 * o + DECAY_LORA:3 * o + DECAY_LORA + AAA_LORA]
    xg = ps[:, 3 * o + DECAY_LORA + AAA_LORA:]

    z = -(w0_ref[...] + _mm(jnp.tanh(xw), wup_ref[...]))
    softplus = jnp.maximum(z, 0.0) + jnp.log1p(jnp.exp(-jnp.abs(z)))
    wlog = -softplus - 0.5
    ld = -jnp.exp(wlog)
    a = jax.nn.sigmoid(a0_ref[...] + _mm(xa, aup_ref[...]))
    g = _mm(jax.nn.sigmoid(xg), gup_ref[...])

    def per_head(x, fn):
        return jnp.concatenate(
            [jnp.broadcast_to(fn(x[:, h * hd:(h + 1) * hd]), (tm, hd)) for h in range(nh)], axis=1)

    kkf = k * kk_ref[...]
    inv_norm = per_head(
        kkf * kkf, lambda q: 1.0 / jnp.maximum(jnp.sqrt(jnp.sum(q, axis=-1, keepdims=True)), 1e-12))
    kkn = kkf * inv_norm
    k2 = k * (1.0 + (a - 1.0) * ka_ref[...])
    avec = -kkn
    bvec = kkn * a

    rowt = lax.broadcasted_iota(jnp.int32, (tm, tm), 0)
    colt = lax.broadcasted_iota(jnp.int32, (tm, tm), 1)
    tri = jnp.where((rowt >= colt) & ((rowt // c) == (colt // c)), 1.0, 0.0).astype(BF16)
    hi, mid, lo = _split3(ld)
    dot = lambda q: jnp.dot(tri, q, preferred_element_type=F32)
    cl = dot(hi) + dot(mid) + dot(lo)
    cl_end = jnp.concatenate(
        [jnp.broadcast_to(cl[(q + 1) * c - 1:(q + 1) * c, :], (c, o)) for q in range(nq)], axis=0)
    g_inv = jnp.exp(-cl)
    g_tail = jnp.exp(cl_end - cl)
    g_end = jnp.exp(cl_end)
    rt = (r * jnp.exp(cl)).astype(BF16)
    at = (avec * jnp.exp(cl - ld)).astype(BF16)
    kt = (k2 * g_inv).astype(BF16)
    bt = (bvec * g_inv).astype(BF16)
    kend = (k2 * g_tail).astype(BF16)
    bend = (bvec * g_tail).astype(BF16)
    vb = v.astype(BF16)

    row = lax.broadcasted_iota(jnp.int32, (c, c), 0)
    col = lax.broadcasted_iota(jnp.int32, (c, c), 1)
    strict = row > col
    incl = row >= col
    items = [(q, h) for q in range(nq) for h in range(nh)]
    blk = lambda x, q, h: x[q * c:(q + 1) * c, h * hd:(h + 1) * hd]

    lhs = [jnp.concatenate([blk(at, q, h), blk(rt, q, h)], axis=0) for q, h in items]
    mat_b = [_mm_nt(l, blk(bt, q, h)) for l, (q, h) in zip(lhs, items)]
    mat_k = [_mm_nt(l, blk(kt, q, h)) for l, (q, h) in zip(lhs, items)]
    a_ab = [jnp.where(strict, m[:c], 0.0) for m in mat_b]
    a_rb = [jnp.where(incl, m[c:], 0.0).astype(BF16) for m in mat_b]
    a_ak = [jnp.where(strict, m[:c], 0.0) for m in mat_k]
    a_rk = [jnp.where(incl, m[c:], 0.0) for m in mat_k]
    tinv = _unit_lower_inverses(a_ab, row, col)
    akv = [_mm(m, blk(vb, q, h)) for m, (q, h) in zip(a_ak, items)]
    y_loc = [_mm(m, blk(vb, q, h)) for m, (q, h) in zip(a_rk, items)]
    s_loc = [_mm_tn(blk(vb, q, h), blk(kend, q, h)) for q, h in items]
    wmat = [_mm(t, blk(at, q, h)).astype(BF16) for t, (q, h) in zip(tinv, items)]
    ut = [_mm(t, m) for t, m in zip(tinv, akv)]

    state = [state_ref[h] for h in range(nh)]
    for q in range(nq):
        base = q * nh
        sb = [s.astype(BF16) for s in state]
        u = [_mm_nt(wmat[base + h], sb[h]) + ut[base + h] for h in range(nh)]
        yh = [_mm_nt(blk(rt, q, h), sb[h]) + y_loc[base + h] for h in range(nh)]
        ub = [x.astype(BF16) for x in u]
        yh = [y0 + _mm(a_rb[base + h], ub[h]) for h, y0 in enumerate(yh)]
        state = [state[h] * g_end[q * c:q * c + 1, h * hd:(h + 1) * hd]
                 + _mm_tn(ub[h], blk(bend, q, h)) + s_loc[base + h] for h in range(nh)]
        for h in range(nh):
            ycat_ref[q * c:(q + 1) * c, h * hd:(h + 1) * hd] = yh[h]
    for h in range(nh):
        state_ref[h] = state[h]

    y = ycat_ref[...]
    mean = per_head(y, lambda q: jnp.mean(q, axis=-1, keepdims=True))
    dev = y - mean
    rstd = per_head(dev * dev, lambda q: lax.rsqrt(jnp.mean(q, axis=-1, keepdims=True) + GN_EPS))
    bonus = per_head(r * k2 * rk_ref[...], lambda q: jnp.sum(q, axis=-1, keepdims=True))
    yn = dev * rstd * lng_ref[...] + lnb_ref[...]
    y_ref[...] = (yn + bonus * v) * g


def _rwkv(p_rwkv, bsz, seq, tshift_mu, w0, w_up, a0, a_up, g_up, k_k, k_a, r_k, ln_g, ln_b):
    t = p_rwkv.shape[0]
    c = min(CHUNK, seq)
    tm = min(RWKV_TILE, seq)
    nstep = seq // tm
    row = lambda a: a.reshape(1, -1)
    args = (row(tshift_mu), row(w0), w_up, row(a0), a_up, g_up, row(k_k), row(k_a), row(r_k),
            row(ln_g), row(ln_b))
    full = lambda a: pl.BlockSpec(a.shape, lambda b, j: (0,) * a.ndim)
    return pl.pallas_call(
        functools.partial(_rwkv_kernel, c=c),
        grid=(bsz, nstep),
        in_specs=[pl.BlockSpec((tm, RWKV_COLS), lambda b, j: (b * nstep + j, 0))] + [full(a) for a in args],
        out_specs=pl.BlockSpec((tm, RWKV_WIDTH), lambda b, j: (b * nstep + j, 0)),
        out_shape=jax.ShapeDtypeStruct((t, RWKV_WIDTH), F32),
        scratch_shapes=[pltpu.VMEM((RWKV_HEADS, RWKV_HEAD, RWKV_HEAD), F32),
                        pltpu.VMEM((1, RWKV_COLS), F32),
                        pltpu.VMEM((tm, RWKV_WIDTH), F32)],
        compiler_params=_params("arbitrary", "arbitrary"),
        name="rwkv",
    )(p_rwkv, *args)


def _mix_kernel(puv_ref, pg_ref, ya_ref, x_ref, mod_ref, lng_ref, lnb_ref, ws_ref, bsm_ref,
                woa_ref, wob_ref, wo_ref, n2g_ref, x1_ref, h2_ref, yb_ref):
    tm = x_ref.shape[0]
    gw = GMLP_WIDTH
    puv = puv_ref[...]
    u = jax.nn.gelu(puv[:, :gw])
    vg = jax.nn.gelu(puv[:, gw:])
    mu = jnp.mean(vg, axis=-1, keepdims=True)
    var = jnp.mean(jnp.square(vg - mu), axis=-1, keepdims=True)
    vln = (vg - mu) * lax.rsqrt(var + LN_EPS) * lng_ref[...] + lnb_ref[...]

    row = lax.broadcasted_iota(jnp.int32, (GMLP_BLOCK, GMLP_BLOCK), 0)
    col = lax.broadcasted_iota(jnp.int32, (GMLP_BLOCK, GMLP_BLOCK), 1)
    gd = GMLP_GROUP_DIM
    for gi in range(GMLP_GROUPS):
        wsm = jnp.where(row >= col, ws_ref[gi], 0.0).astype(BF16)
        for nb in range(tm // GMLP_BLOCK):
            rs = slice(nb * GMLP_BLOCK, (nb + 1) * GMLP_BLOCK)
            cs = slice(gi * gd, (gi + 1) * gd)
            sv = jnp.dot(wsm, vln[rs, cs].astype(BF16), preferred_element_type=F32) + bsm_ref[:, cs]
            yb_ref[rs, cs] = u[rs, cs] * sv

    pg = pg_ref[...]
    d = x_ref.shape[1]
    gate_a = jax.nn.sigmoid(pg[:, :d])
    gate_b = jax.nn.sigmoid(pg[:, d:])
    merged = gate_a * _mm(ya_ref[...], woa_ref[...]) + gate_b * _mm(yb_ref[...], wob_ref[...])
    g1 = mod_ref[0, 2:3, :]
    x1 = x_ref[...] + g1 * _mm(merged, wo_ref[...])
    x1_ref[...] = x1
    sh2 = mod_ref[0, 3:4, :]
    sc2 = mod_ref[0, 4:5, :]
    h2_ref[...] = (_rms(x1) * n2g_ref[...]) * (1.0 + sc2) + sh2


def _mix(p_uv, p_gate, y_a, xf, mod3, seq, ln_g, ln_b, ws, bs, w_out_a, w_out_b, w_out, norm2_g):
    t, d = xf.shape
    tm = min(MIX_TILE, seq)
    per_seq = seq // tm
    row = lambda a: a.reshape(1, -1)
    bsm = jnp.repeat(bs.T, GMLP_GROUP_DIM, axis=1)
    consts = (row(ln_g), row(ln_b), ws, bsm, w_out_a.astype(BF16), w_out_b.astype(BF16),
              w_out.astype(BF16), row(norm2_g))
    full = lambda a: pl.BlockSpec(a.shape, lambda i: (0,) * a.ndim)
    tile = lambda a: pl.BlockSpec((tm, a.shape[1]), lambda i: (i, 0))
    return pl.pallas_call(
        _mix_kernel,
        grid=(t // tm,),
        in_specs=[tile(p_uv), tile(p_gate), tile(y_a), tile(xf),
                  pl.BlockSpec((1,) + mod3.shape[1:], lambda i: (i // per_seq, 0, 0))]
                 + [full(a) for a in consts],
        out_specs=[pl.BlockSpec((tm, d), lambda i: (i, 0)), pl.BlockSpec((tm, d), lambda i: (i, 0))],
        out_shape=[jax.ShapeDtypeStruct((t, d), F32), jax.ShapeDtypeStruct((t, d), F32)],
        scratch_shapes=[pltpu.VMEM((tm, GMLP_WIDTH), F32)],
        compiler_params=_params("arbitrary"),
        name="mix",
    )(p_uv, p_gate, y_a, xf, mod3, *consts)


def _first_argmax(vals, idx, big):
    m = jnp.max(vals, axis=0, keepdims=True)
    first = jnp.min(jnp.where(vals == m, idx, big), axis=0, keepdims=True)
    return m, first


def _route_kernel(h_ref, rwt_ref, bias_ref, eidx_ref, wts_ref):
    tm = h_ref.shape[0]
    ne = N_EXPERTS
    hhi, hmid, hlo = _split3(h_ref[...])
    whi, wmid, wlo = _split3(rwt_ref[...])
    nt = lambda p, q: lax.dot_general(p, q, (((1,), (1,)), ((), ())), preferred_element_type=F32)
    logits = (nt(whi, hhi) + (nt(whi, hmid) + nt(wmid, hhi))
              + (nt(whi, hlo) + nt(wmid, hmid) + nt(wlo, hhi)))
    scores = jax.nn.sigmoid(logits)
    sel = scores + bias_ref[...]
    neg = -jnp.inf

    gs = GROUP_SIZE
    gidx = lax.broadcasted_iota(jnp.int32, (gs, tm), 0)
    grp_rows = []
    for gi in range(N_EXPERT_GROUPS):
        blk = sel[gi * gs:(gi + 1) * gs, :]
        m1, i1 = _first_argmax(blk, gidx, gs)
        m2 = jnp.max(jnp.where(gidx == i1, neg, blk), axis=0, keepdims=True)
        grp_rows.append(m1 + m2)
    grp = jnp.concatenate(grp_rows, axis=0)

    ng = N_EXPERT_GROUPS
    giota = lax.broadcasted_iota(jnp.int32, (ng, tm), 0)
    gsel = jnp.zeros((ng, tm), jnp.bool_)
    work = grp
    for _ in range(TOPK_GROUPS):
        _, gi1 = _first_argmax(work, giota, ng)
        hit = giota == gi1
        gsel = gsel | hit
        work = jnp.where(hit, neg, work)
    gself = jnp.where(gsel, 1.0, 0.0)
    emask = jnp.concatenate(
        [jnp.broadcast_to(gself[gi:gi + 1, :], (gs, tm)) for gi in range(ng)], axis=0) > 0.5

    eiota = lax.broadcasted_iota(jnp.int32, (ne, tm), 0)
    work = jnp.where(emask, sel, neg)
    idx_rows, w_rows = [], []
    for _ in range(TOP_K):
        _, e1 = _first_argmax(work, eiota, ne)
        hit = eiota == e1
        idx_rows.append(e1)
        w_rows.append(jnp.sum(jnp.where(hit, scores, 0.0), axis=0, keepdims=True))
        work = jnp.where(hit, neg, work)
    eidx_ref[...] = jnp.concatenate(idx_rows, axis=0)
    w = jnp.concatenate(w_rows, axis=0)
    wts_ref[...] = w / jnp.sum(w, axis=0, keepdims=True) * ROUTED_SCALE


def _route(h2, router_w, router_bias):
    t, d = h2.shape
    tm = min(ROUTE_TILE, t)
    rwt = router_w.T
    bias = router_bias.reshape(N_EXPERTS, 1)
    return pl.pallas_call(
        _route_kernel,
        grid=(t // tm,),
        in_specs=[pl.BlockSpec((tm, d), lambda i: (i, 0)),
                  pl.BlockSpec(rwt.shape, lambda i: (0, 0)),
                  pl.BlockSpec(bias.shape, lambda i: (0, 0))],
        out_specs=[pl.BlockSpec((TOP_K, tm), lambda i: (0, i)), pl.BlockSpec((TOP_K, tm), lambda i: (0, i))],
        out_shape=[jax.ShapeDtypeStruct((TOP_K, t), jnp.int32), jax.ShapeDtypeStruct((TOP_K, t), F32)],
        compiler_params=_params("arbitrary"),
        name="route",
    )(h2, rwt, bias)


def _rank_kernel(e_ref, upper_ref, rank_ref, count_ref):
    tm = e_ref.shape[1]
    ne = N_EXPERTS

    @pl.when(pl.program_id(0) == 0)
    def _():
        count_ref[...] = jnp.zeros_like(count_ref)

    e = e_ref[...]
    eiota = lax.broadcasted_iota(jnp.int32, (ne, tm), 0)
    onehot = jnp.zeros((ne, tm), F32)
    for kk in range(TOP_K):
        onehot = onehot + jnp.where(eiota == e[kk:kk + 1, :], 1.0, 0.0)
    before = jnp.dot(onehot.astype(BF16), upper_ref[...], preferred_element_type=F32)
    base = count_ref[:, 0:1]
    tot = before + base
    rows = [jnp.sum(jnp.where(eiota == e[kk:kk + 1, :], tot, 0.0), axis=0, keepdims=True)
            for kk in range(TOP_K)]
    rank_ref[...] = jnp.concatenate(rows, axis=0).astype(jnp.int32)
    count_ref[...] = count_ref[...] + jnp.sum(onehot, axis=1, keepdims=True)


def _rank(eidx):
    k, t = eidx.shape
    tm = min(RANK_TILE, t)
    upper = jnp.triu(jnp.ones((tm, tm), F32), 1).astype(BF16)
    return pl.pallas_call(
        _rank_kernel,
        grid=(t // tm,),
        in_specs=[pl.BlockSpec((k, tm), lambda i: (0, i)), pl.BlockSpec((tm, tm), lambda i: (0, 0))],
        out_specs=[pl.BlockSpec((k, tm), lambda i: (0, i)), pl.BlockSpec((N_EXPERTS, 128), lambda i: (0, 0))],
        out_shape=[jax.ShapeDtypeStruct((k, t), jnp.int32), jax.ShapeDtypeStruct((N_EXPERTS, 128), F32)],
        compiler_params=_params("arbitrary"),
        name="rank",
    )(eidx, upper)


def _dest_kernel(e_ref, rank_ref, pst_ref, dest_ref):
    tm = e_ref.shape[1]
    e = e_ref[...]
    eiota = lax.broadcasted_iota(jnp.int32, (N_EXPERTS, tm), 0)
    pst = pst_ref[...]
    rows = [jnp.sum(jnp.where(eiota == e[kk:kk + 1, :], pst, 0.0), axis=0, keepdims=True)
            for kk in range(TOP_K)]
    dest_ref[...] = jnp.concatenate(rows, axis=0).astype(jnp.int32) + rank_ref[...]


def _dest(eidx, rank, pstarts):
    k, t = eidx.shape
    tm = min(RANK_TILE, t)
    pst = pstarts.astype(F32).reshape(N_EXPERTS, 1)
    return pl.pallas_call(
        _dest_kernel,
        grid=(t // tm,),
        in_specs=[pl.BlockSpec((k, tm), lambda i: (0, i)), pl.BlockSpec((k, tm), lambda i: (0, i)),
                  pl.BlockSpec((N_EXPERTS, 1), lambda i: (0, 0))],
        out_specs=pl.BlockSpec((k, tm), lambda i: (0, i)),
        out_shape=jax.ShapeDtypeStruct((k, t), jnp.int32),
        compiler_params=_params("arbitrary"),
        name="dest",
    )(eidx, rank, pst)


def _dispatch_kernel(pst_ref, cnt_ref, nv_ref, dst_ref, h_ref, xs_hbm, zbuf, sem, zsem):
    i = pl.program_id(0)
    tm = h_ref.shape[0]
    rb = zbuf.shape[0]
    nblk = xs_hbm.shape[0] // rb

    @pl.when(i == 0)
    def _():
        zbuf[...] = jnp.zeros_like(zbuf)

        def pads(e, wait):
            off = pst_ref[e] + cnt_ref[e]
            n = (rb - cnt_ref[e] % rb) % rb

            def one(r, carry):
                cp = _row_copy(zbuf, xs_hbm, zsem, 0, off + r)
                cp.wait() if wait else cp.start()
                return carry
            lax.fori_loop(0, n, one, 0)

        def tail(b, wait):
            cp = pltpu.make_async_copy(zbuf, xs_hbm.at[pl.ds(b * rb, rb), :], zsem)
            cp.wait() if wait else cp.start()

        for wait in (False, True):
            lax.fori_loop(0, N_EXPERTS, lambda e, carry: (pads(e, wait), carry)[1], 0)
            lax.fori_loop(nv_ref[0], nblk, lambda b, carry: (tail(b, wait), carry)[1], 0)

    def body(j, carry):
        for kk in range(TOP_K):
            _row_copy(h_ref, xs_hbm, sem, j, dst_ref[0, 0, kk * tm + j]).start()
        return carry
    lax.fori_loop(0, tm, body, 0)
    for kk in range(TOP_K):
        pltpu.make_async_copy(h_ref, h_ref, sem).wait()


def _dispatch(h2, dest_tk, pstarts, counts, nvalid, n_rows):
    t, d = h2.shape
    tm = min(DISPATCH_TILE, t)
    nt = t // tm
    dst3 = dest_tk.reshape(TOP_K, nt, tm).transpose(1, 0, 2).reshape(nt, 1, TOP_K * tm)
    grid_spec = pltpu.PrefetchScalarGridSpec(
        num_scalar_prefetch=3,
        grid=(nt,),
        in_specs=[pl.BlockSpec((1, 1, TOP_K * tm), lambda i, *_: (i, 0, 0), memory_space=pltpu.SMEM),
                  pl.BlockSpec((tm, d), lambda i, *_: (i, 0))],
        out_specs=pl.BlockSpec(memory_space=pl.ANY),
        scratch_shapes=[pltpu.VMEM((ROW_BLOCK, d), F32), pltpu.SemaphoreType.DMA(()),
                        pltpu.SemaphoreType.DMA(())],
    )
    return pl.pallas_call(
        _dispatch_kernel,
        grid_spec=grid_spec,
        out_shape=jax.ShapeDtypeStruct((n_rows, d), F32),
        compiler_params=_params("arbitrary"),
        name="dispatch",
    )(pstarts, counts, nvalid, dst3, h2)


def _expert_kernel(be_ref, nv_ref, x_ref, w1_ref, w3_ref, w2_ref, y_ref):
    i = pl.program_id(0)

    @pl.when(i < nv_ref[0])
    def _():
        xb = x_ref[...].astype(BF16)
        hid = _silu(jnp.dot(xb, w1_ref[0], preferred_element_type=F32)) * jnp.dot(
            xb, w3_ref[0], preferred_element_type=F32)
        y_ref[...] = jnp.dot(hid.astype(BF16), w2_ref[0], preferred_element_type=F32)

    @pl.when(i >= nv_ref[0])
    def _():
        y_ref[...] = jnp.zeros_like(y_ref)


def _expert(x_sorted, block_e, nvalid, w1, w3, w2):
    n_rows, d = x_sorted.shape
    rb = ROW_BLOCK
    nblk = n_rows // rb
    f = w1.shape[2]
    grid_spec = pltpu.PrefetchScalarGridSpec(
        num_scalar_prefetch=2,
        grid=(nblk,),
        in_specs=[
            pl.BlockSpec((rb, d), lambda i, be, nv: (jnp.minimum(i, nv[0] - 1), 0)),
            pl.BlockSpec((1, d, f), lambda i, be, nv: (be[i], 0, 0)),
            pl.BlockSpec((1, d, f), lambda i, be, nv: (be[i], 0, 0)),
            pl.BlockSpec((1, f, d), lambda i, be, nv: (be[i], 0, 0)),
        ],
        out_specs=pl.BlockSpec((rb, d), lambda i, be, nv: (i, 0)),
    )
    return pl.pallas_call(
        _expert_kernel,
        grid_spec=grid_spec,
        out_shape=jax.ShapeDtypeStruct((n_rows, d), F32),
        compiler_params=_params("arbitrary"),
        name="expert",
    )(block_e, nvalid, x_sorted, w1, w3, w2)


def _combine_kernel(dst_ref, dstn_ref, y_hbm, wts_ref, h_ref, x1_ref, mod_ref, sw1_ref, sw3_ref, sw2_ref,
                    nfg_ref, o_ref, ybuf, sems):
    i = pl.program_id(0)
    n = pl.num_programs(0)
    tm = ybuf.shape[2]
    slot = i % 2

    def gather(idx_ref, s):
        def body(j, carry):
            for kk in range(TOP_K):
                _row_copy(y_hbm, ybuf.at[s, kk], sems.at[s], idx_ref[0, 0, kk * tm + j], j).start()
            return carry
        lax.fori_loop(0, tm, body, 0)

    @pl.when(i == 0)
    def _():
        gather(dst_ref, 0)

    @pl.when(i + 1 < n)
    def _():
        gather(dstn_ref, 1 - slot)

    pltpu.make_async_copy(ybuf.at[slot], ybuf.at[slot], sems.at[slot]).wait()
    w = wts_ref[...]
    routed = ybuf[slot, 0] * w[:, 0:1]
    for kk in range(1, TOP_K):
        routed = routed + ybuf[slot, kk] * w[:, kk:kk + 1]
    hb = h_ref[...].astype(BF16)
    hid = _silu(jnp.dot(hb, sw1_ref[...], preferred_element_type=F32)) * jnp.dot(
        hb, sw3_ref[...], preferred_element_type=F32)
    shared = jnp.dot(hid.astype(BF16), sw2_ref[...], preferred_element_type=F32)
    g2 = mod_ref[0, 5:6, :]
    x2 = x1_ref[...] + g2 * (routed + shared)
    o_ref[...] = _rms(x2) * nfg_ref[...]


def _combine(y_sorted, dest_tk, wts_tk, h2, x1, mod3, seq, sw1, sw3, sw2, normf_g):
    t, d = h2.shape
    tm = min(COMBINE_TILE, seq)
    per_seq = seq // tm
    nt = t // tm
    dst3 = dest_tk.reshape(TOP_K, nt, tm).transpose(1, 0, 2).reshape(nt, 1, TOP_K * tm)
    consts = (sw1.astype(BF16), sw3.astype(BF16), sw2.astype(BF16), normf_g.reshape(1, d))
    full = lambda a: pl.BlockSpec(a.shape, lambda i: (0,) * a.ndim)
    tile = lambda a: pl.BlockSpec((tm, a.shape[1]), lambda i: (i, 0))
    return pl.pallas_call(
        _combine_kernel,
        grid=(nt,),
        in_specs=[pl.BlockSpec((1, 1, TOP_K * tm), lambda i: (i, 0, 0), memory_space=pltpu.SMEM),
                  pl.BlockSpec((1, 1, TOP_K * tm), lambda i: (jnp.minimum(i + 1, nt - 1), 0, 0),
                               memory_space=pltpu.SMEM),
                  pl.BlockSpec(memory_space=pl.ANY),
                  tile(wts_tk), tile(h2), tile(x1),
                  pl.BlockSpec((1,) + mod3.shape[1:], lambda i: (i // per_seq, 0, 0))]
                 + [full(a) for a in consts],
        out_specs=pl.BlockSpec((tm, d), lambda i: (i, 0)),
        out_shape=jax.ShapeDtypeStruct((t, d), F32),
        scratch_shapes=[pltpu.VMEM((2, TOP_K, tm, d), F32), pltpu.SemaphoreType.DMA((2,))],
        compiler_params=_params("arbitrary"),
        name="combine",
    )(dst3, dst3, y_sorted, wts_tk, h2, x1, mod3, *consts)


def _dispatch_plan(counts, t):
    rb = ROW_BLOCK
    padded = (counts + rb - 1) // rb * rb
    pends = jnp.cumsum(padded)
    pstarts = pends - padded
    n_rows = t * TOP_K + N_EXPERTS * rb
    block_start = jnp.arange(n_rows // rb, dtype=jnp.int32) * rb
    block_e = jnp.sum((pends[None, :] <= block_start[:, None]).astype(jnp.int32), axis=1)
    block_e = jnp.minimum(block_e, N_EXPERTS - 1)
    nvalid = (pends[-1:] // rb).astype(jnp.int32)
    return pstarts, block_e, nvalid, n_rows


def kernel(x, c, ada_w, ada_b, norm1_g, norm2_g, w_in, tshift_mu, rwkv_w0, rwkv_w_up, rwkv_a0, rwkv_a_up,
           rwkv_g_up, rwkv_k_k, rwkv_k_a, rwkv_r_k, rwkv_ln_g, rwkv_ln_b, gmlp_ln_g, gmlp_ln_b, gmlp_ws,
           gmlp_bs, w_out_a, w_out_b, w_out, router_w, router_bias, exp_w1, exp_w3, exp_w2, shared_w1,
           shared_w3, shared_w2, normf_g):
    bsz, seq, d = x.shape
    t = bsz * seq
    xf = x.reshape(t, d)
    assert ada_w.shape[0] == 1, "single-layer block only"
    l = 0
    mod3 = _mod(c, ada_w[l], ada_b[l]).reshape(bsz, 6, d)
    p_rwkv, p_uv, p_gate = _inproj(xf, mod3, norm1_g[l], w_in[l], seq)
    y_a = _rwkv(p_rwkv, bsz, seq, tshift_mu[l], rwkv_w0[l], rwkv_w_up[l], rwkv_a0[l], rwkv_a_up[l],
                rwkv_g_up[l], rwkv_k_k[l], rwkv_k_a[l], rwkv_r_k[l], rwkv_ln_g[l], rwkv_ln_b[l])
    x1, h2 = _mix(p_uv, p_gate, y_a, xf, mod3, seq, gmlp_ln_g[l], gmlp_ln_b[l], gmlp_ws[l], gmlp_bs[l],
                  w_out_a[l], w_out_b[l], w_out[l], norm2_g[l])
    eidx, wts = _route(h2, router_w[l], router_bias[l])
    rank, counts = _rank(eidx)
    counts = counts[:, 0].astype(jnp.int32)
    pstarts, block_e, nvalid, n_rows = _dispatch_plan(counts, t)
    dest = _dest(eidx, rank, pstarts)
    x_sorted = _dispatch(h2, dest, pstarts, counts, nvalid, n_rows)
    y_sorted = _expert(x_sorted, block_e, nvalid, exp_w1[l].astype(BF16), exp_w3[l].astype(BF16),
                       exp_w2[l].astype(BF16))
    out = _combine(y_sorted, dest, wts.T, h2, x1, mod3, seq, shared_w1[l], shared_w3[l], shared_w2[l],
                   normf_g)
    return out.reshape(bsz, seq, d)
```

```python
import functools

import jax
import jax.numpy as jnp
from jax import lax
from jax.experimental import pallas as pl
from jax.experimental.pallas import tpu as pltpu

F32 = jnp.float32
BF16 = jnp.bfloat16

RWKV_HEAD = 64
RWKV_HEADS = 8
RWKV_WIDTH = RWKV_HEAD * RWKV_HEADS
DECAY_LORA = 64
AAA_LORA = 64
GATE_LORA = 128
RWKV_COLS = 3 * RWKV_WIDTH + DECAY_LORA + AAA_LORA + GATE_LORA
GMLP_WIDTH = 512
GMLP_BLOCK = 128
GMLP_GROUPS = 8
GMLP_GROUP_DIM = GMLP_WIDTH // GMLP_GROUPS
N_EXPERTS = 256
TOP_K = 8
N_EXPERT_GROUPS = 8
GROUP_SIZE = N_EXPERTS // N_EXPERT_GROUPS
TOPK_GROUPS = 4
ROUTED_SCALE = 2.5
RMS_EPS = 1e-6
LN_EPS = 1e-5
GN_EPS = 64e-5

VMEM_LIMIT_BYTES = 56 * 1024 * 1024

CHUNK = 64
RWKV_TILE = 256
ROW_BLOCK = 256
INPROJ_TILE = 256
MIX_TILE = 256
ROUTE_TILE = 512
RANK_TILE = 512
DISPATCH_TILE = 256
COMBINE_TILE = 128


def _params(*sem):
    return pltpu.CompilerParams(dimension_semantics=sem, vmem_limit_bytes=VMEM_LIMIT_BYTES)


def _mm(a, b):
    return jnp.dot(a.astype(BF16), b.astype(BF16), preferred_element_type=F32)


def _mm_nt(a, b):
    return lax.dot_general(a.astype(BF16), b.astype(BF16), (((1,), (1,)), ((), ())),
                           preferred_element_type=F32)


def _mm_tn(a, b):
    return lax.dot_general(a.astype(BF16), b.astype(BF16), (((0,), (0,)), ((), ())),
                           preferred_element_type=F32)


def _split3(a):
    hi = a.astype(BF16)
    r1 = a - hi.astype(F32)
    mid = r1.astype(BF16)
    lo = (r1 - mid.astype(F32)).astype(BF16)
    return hi, mid, lo


def _silu(x):
    return x * jax.nn.sigmoid(x)


def _rms(x):
    return x * lax.rsqrt(jnp.mean(x * x, axis=-1, keepdims=True) + RMS_EPS)


def _pack_halves(x):
    n = x.shape[1] // 2
    bits = lax.bitcast_convert_type(x.astype(BF16).astype(F32), jnp.uint32)
    return bits[:, :n] | (bits[:, n:] >> 16)


def _unpack_halves(u):
    hi = lax.bitcast_convert_type(u & jnp.uint32(0xFFFF0000), F32)
    lo = lax.bitcast_convert_type(u << 16, F32)
    return hi, lo


def _row_copy(src, dst, sem, src_row, dst_row):
    return pltpu.make_async_copy(src.at[pl.ds(src_row, 1), :], dst.at[pl.ds(dst_row, 1), :], sem)


def _mod_kernel(c_ref, w_ref, b_ref, o_ref):
    s = _silu(c_ref[...])
    hi, mid, lo = _split3(s)
    whi, wmid, wlo = _split3(w_ref[...])
    dot = lambda p, q: jnp.dot(p, q, preferred_element_type=F32)
    acc = dot(hi, whi) + (dot(hi, wmid) + dot(mid, whi)) + (dot(hi, wlo) + dot(mid, wmid) + dot(lo, whi))
    o_ref[...] = acc + b_ref[...]


def _mod(c, ada_w, ada_b):
    bsz, d = c.shape
    n = ada_w.shape[1]
    tn = d
    return pl.pallas_call(
        _mod_kernel,
        grid=(n // tn,),
        in_specs=[pl.BlockSpec((bsz, d), lambda j: (0, 0)),
                  pl.BlockSpec((d, tn), lambda j: (0, j)),
                  pl.BlockSpec((1, tn), lambda j: (0, j))],
        out_specs=pl.BlockSpec((bsz, tn), lambda j: (0, j)),
        out_shape=jax.ShapeDtypeStruct((bsz, n), F32),
        compiler_params=_params("arbitrary"),
        name="mod",
    )(c, ada_w, ada_b.reshape(1, n))


def _inproj_kernel(x_ref, mod_ref, g_ref, wr_ref, wuv_ref, wg_ref, pr_ref, puv_ref, pg_ref):
    sh = mod_ref[0, 0:1, :]
    sc = mod_ref[0, 1:2, :]
    h = (_rms(x_ref[...]) * g_ref[...]) * (1.0 + sc) + sh
    hb = h.astype(BF16)
    pr_ref[...] = jnp.dot(hb, wr_ref[...], preferred_element_type=F32).astype(pr_ref.dtype)
    puv_ref[...] = jnp.dot(hb, wuv_ref[...], preferred_element_type=F32).astype(puv_ref.dtype)
    pg_ref[...] = jnp.dot(hb, wg_ref[...], preferred_element_type=F32).astype(pg_ref.dtype)


def _inproj(xf, mod3, norm1_g, w_in, seq):
    t, d = xf.shape
    tm = min(INPROJ_TILE, seq)
    per_seq = seq // tm
    wr = w_in[:, :RWKV_COLS].astype(BF16)
    wuv = w_in[:, RWKV_COLS:RWKV_COLS + 2 * GMLP_WIDTH].astype(BF16)
    wg = w_in[:, RWKV_COLS + 2 * GMLP_WIDTH:].astype(BF16)
    full = lambda a: pl.BlockSpec(a.shape, lambda i: (0,) * a.ndim)
    g = norm1_g.reshape(1, d)
    return pl.pallas_call(
        _inproj_kernel,
        grid=(t // tm,),
        in_specs=[pl.BlockSpec((tm, d), lambda i: (i, 0)),
                  pl.BlockSpec((1,) + mod3.shape[1:], lambda i: (i // per_seq, 0, 0)),
                  full(g), full(wr), full(wuv), full(wg)],
        out_specs=[pl.BlockSpec((tm, wr.shape[1]), lambda i: (i, 0)),
                   pl.BlockSpec((tm, wuv.shape[1]), lambda i: (i, 0)),
                   pl.BlockSpec((tm, wg.shape[1]), lambda i: (i, 0))],
        out_shape=[jax.ShapeDtypeStruct((t, wr.shape[1]), BF16),
                   jax.ShapeDtypeStruct((t, wuv.shape[1]), BF16),
                   jax.ShapeDtypeStruct((t, wg.shape[1]), BF16)],
        compiler_params=_params("arbitrary"),
        name="inproj",
    )(xf, mod3, g, wr, wuv, wg)


def _unit_lower_inverses(ns, row, col):
    c = ns[0].shape[0]
    eye = jnp.where(row == col, 1.0, 0.0).astype(F32)
    blk8 = (row // 8) == (col // 8)
    n8 = [jnp.where(blk8, n, 0.0) for n in ns]
    n8_2 = [_mm(a, a) for a in n8]
    n8_4 = [_mm(a, a) for a in n8_2]
    inv = [eye + a for a in n8]
    inv = [i + _mm(i, b) for i, b in zip(inv, n8_2)]
    inv = [i + _mm(i, b) for i, b in zip(inv, n8_4)]
    s = 8
    while s < c:
        sel = ((row // (2 * s)) == (col // (2 * s))) & ((row // s) != (col // s))
        low = [jnp.where(sel, n, 0.0) for n in ns]
        t1 = [_mm(l, i) for l, i in zip(low, inv)]
        inv = [i + _mm(i, t) for i, t in zip(inv, t1)]
        s *= 2
    return inv


def _rwkv_kernel(p_ref, mu_ref, w0_ref, wup_ref, a0_ref, aup_ref, gup_ref, kk_ref, ka_ref, rk_ref,
                 lng_ref, lnb_ref, y_ref, state_ref, carry_ref, ycat_ref, *, c):
    tm = p_ref.shape[0]
    nq = tm // c
    hd = RWKV_HEAD
    nh = RWKV_HEADS

    @pl.when(pl.program_id(1) == 0)
    def _():
        state_ref[...] = jnp.zeros_like(state_ref)
        carry_ref[...] = jnp.zeros_like(carry_ref)

    p = p_ref[...].astype(F32)
    rowc = lax.broadcasted_iota(jnp.int32, (tm, 1), 0)
    prev = jnp.where(rowc == 0, carry_ref[...], pltpu.roll(p, 1, axis=0))
    carry_ref[...] = p[tm - 1:tm, :]
    ps = p + (prev - p) * mu_ref[...]

    o = RWKV_WIDTH
    r = ps[:, 0:o]
    k = ps[:, o:2 * o]
    v = ps[:, 2 * o:3 * o]
    xw = ps[:, 3 * o:3 * o + DECAY_LORA]
    xa = ps[:, 3 * o + DECAY_LORA:3 * o + DECAY_LORA + AAA_LORA]
    xg = ps[:, 3 * o + DECAY_LORA + AAA_LORA:]

    z = -(w0_ref[...] + _mm(jnp.tanh(xw), wup_ref[...]))
    softplus = jnp.maximum(z, 0.0) + jnp.log1p(jnp.exp(-jnp.abs(z)))
    wlog = -softplus - 0.5
    ld = -jnp.exp(wlog)
    a = jax.nn.sigmoid(a0_ref[...] + _mm(xa, aup_ref[...]))
    g = _mm(jax.nn.sigmoid(xg), gup_ref[...])

    def per_head(x, fn):
        return jnp.concatenate(
            [jnp.broadcast_to(fn(x[:, h * hd:(h + 1) * hd]), (tm, hd)) for h in range(nh)], axis=1)

    kkf = k * kk_ref[...]
    inv_norm = per_head(
        kkf * kkf, lambda q: 1.0 / jnp.maximum(jnp.sqrt(jnp.sum(q, axis=-1, keepdims=True)), 1e-12))
    kkn = kkf * inv_norm
    k2 = k * (1.0 + (a - 1.0) * ka_ref[...])
    avec = -kkn
    bvec = kkn * a

    rowt = lax.broadcasted_iota(jnp.int32, (tm, tm), 0)
    colt = lax.broadcasted_iota(jnp.int32, (tm, tm), 1)
    tri = jnp.where((rowt >= colt) & ((rowt // c) == (colt // c)), 1.0, 0.0).astype(BF16)
    hi, mid, lo = _split3(ld)
    dot = lambda q: jnp.dot(tri, q, preferred_element_type=F32)
    cl = dot(hi) + dot(mid) + dot(lo)
    cl_end = jnp.concatenate(
        [jnp.broadcast_to(cl[(q + 1) * c - 1:(q + 1) * c, :], (c, o)) for q in range(nq)], axis=0)
    g_inv = jnp.exp(-cl)
    g_tail = jnp.exp(cl_end - cl)
    g_end = jnp.exp(cl_end)
    rt = (r * jnp.exp(cl)).astype(BF16)
    at = (avec * jnp.exp(cl - ld)).astype(BF16)
    kt = (k2 * g_inv).astype(BF16)
    bt = (bvec * g_inv).astype(BF16)
    kend = (k2 * g_tail).astype(BF16)
    bend = (bvec * g_tail).astype(BF16)
    vb = v.astype(BF16)

    row = lax.broadcasted_iota(jnp.int32, (c, c), 0)
    col = lax.broadcasted_iota(jnp.int32, (c, c), 1)
    strict = row > col
    incl = row >= col
    items = [(q, h) for q in range(nq) for h in range(nh)]
    blk = lambda x, q, h: x[q * c:(q + 1) * c, h * hd:(h + 1) * hd]

    lhs = [jnp.concatenate([blk(at, q, h), blk(rt, q, h)], axis=0) for q, h in items]
    mat_b = [_mm_nt(l, blk(bt, q, h)) for l, (q, h) in zip(lhs, items)]
    mat_k = [_mm_nt(l, blk(kt, q, h)) for l, (q, h) in zip(lhs, items)]
    a_ab = [jnp.where(strict, m[:c], 0.0) for m in mat_b]
    a_rb = [jnp.where(incl, m[c:], 0.0).astype(BF16) for m in mat_b]
    a_ak = [jnp.where(strict, m[:c], 0.0) for m in mat_k]
    a_rk = [jnp.where(incl, m[c:], 0.0) for m in mat_k]
    tinv = _unit_lower_inverses(a_ab, row, col)
    akv = [_mm(m, blk(vb, q, h)) for m, (q, h) in zip(a_ak, items)]
    y_loc = [_mm(m, blk(vb, q, h)) for m, (q, h) in zip(a_rk, items)]
    s_loc = [_mm_tn(blk(vb, q, h), blk(kend, q, h)) for q, h in items]
    wmat = [_mm(t, blk(at, q, h)).astype(BF16) for t, (q, h) in zip(tinv, items)]
    ut = [_mm(t, m) for t, m in zip(tinv, akv)]

    state = [state_ref[h] for h in range(nh)]
    for q in range(nq):
        base = q * nh
        sb = [s.astype(BF16) for s in state]
        u = [_mm_nt(wmat[base + h], sb[h]) + ut[base + h] for h in range(nh)]
        yh = [_mm_nt(blk(rt, q, h), sb[h]) + y_loc[base + h] for h in range(nh)]
        ub = [x.astype(BF16) for x in u]
        yh = [y0 + _mm(a_rb[base + h], ub[h]) for h, y0 in enumerate(yh)]
        state = [state[h] * g_end[q * c:q * c + 1, h * hd:(h + 1) * hd]
                 + _mm_tn(ub[h], blk(bend, q, h)) + s_loc[base + h] for h in range(nh)]
        for h in range(nh):
            ycat_ref[q * c:(q + 1) * c, h * hd:(h + 1) * hd] = yh[h]
    for h in range(nh):
        state_ref[h] = state[h]

    y = ycat_ref[...]
    mean = per_head(y, lambda q: jnp.mean(q, axis=-1, keepdims=True))
    dev = y - mean
    rstd = per_head(dev * dev, lambda q: lax.rsqrt(jnp.mean(q, axis=-1, keepdims=True) + GN_EPS))
    bonus = per_head(r * k2 * rk_ref[...], lambda q: jnp.sum(q, axis=-1, keepdims=True))
    yn = dev * rstd * lng_ref[...] + lnb_ref[...]
    y_ref[...] = (yn + bonus * v) * g


def _rwkv(p_rwkv, bsz, seq, tshift_mu, w0, w_up, a0, a_up, g_up, k_k, k_a, r_k, ln_g, ln_b):
    t = p_rwkv.shape[0]
    c = min(CHUNK, seq)
    tm = min(RWKV_TILE, seq)
    nstep = seq // tm
    row = lambda a: a.reshape(1, -1)
    args = (row(tshift_mu), row(w0), w_up, row(a0), a_up, g_up, row(k_k), row(k_a), row(r_k),
            row(ln_g), row(ln_b))
    full = lambda a: pl.BlockSpec(a.shape, lambda b, j: (0,) * a.ndim)
    return pl.pallas_call(
        functools.partial(_rwkv_kernel, c=c),
        grid=(bsz, nstep),
        in_specs=[pl.BlockSpec((tm, RWKV_COLS), lambda b, j: (b * nstep + j, 0))] + [full(a) for a in args],
        out_specs=pl.BlockSpec((tm, RWKV_WIDTH), lambda b, j: (b * nstep + j, 0)),
        out_shape=jax.ShapeDtypeStruct((t, RWKV_WIDTH), F32),
        scratch_shapes=[pltpu.VMEM((RWKV_HEADS, RWKV_HEAD, RWKV_HEAD), F32),
                        pltpu.VMEM((1, RWKV_COLS), F32),
                        pltpu.VMEM((tm, RWKV_WIDTH), F32)],
        compiler_params=_params("arbitrary", "arbitrary"),
        name="rwkv",
    )(p_rwkv, *args)


def _mix_kernel(puv_ref, pg_ref, ya_ref, x_ref, mod_ref, lng_ref, lnb_ref, ws_ref, bsm_ref,
                woa_ref, wob_ref, wo_ref, n2g_ref, x1_ref, h2_ref, h2p_ref, yb_ref):
    tm = x_ref.shape[0]
    gw = GMLP_WIDTH
    puv = puv_ref[...].astype(F32)
    u = jax.nn.gelu(puv[:, :gw])
    vg = jax.nn.gelu(puv[:, gw:])
    mu = jnp.mean(vg, axis=-1, keepdims=True)
    var = jnp.mean(jnp.square(vg - mu), axis=-1, keepdims=True)
    vln = (vg - mu) * lax.rsqrt(var + LN_EPS) * lng_ref[...] + lnb_ref[...]

    row = lax.broadcasted_iota(jnp.int32, (GMLP_BLOCK, GMLP_BLOCK), 0)
    col = lax.broadcasted_iota(jnp.int32, (GMLP_BLOCK, GMLP_BLOCK), 1)
    gd = GMLP_GROUP_DIM
    for gi in range(GMLP_GROUPS):
        wsm = jnp.where(row >= col, ws_ref[gi], 0.0).astype(BF16)
        for nb in range(tm // GMLP_BLOCK):
            rs = slice(nb * GMLP_BLOCK, (nb + 1) * GMLP_BLOCK)
            cs = slice(gi * gd, (gi + 1) * gd)
            sv = jnp.dot(wsm, vln[rs, cs].astype(BF16), preferred_element_type=F32) + bsm_ref[:, cs]
            yb_ref[rs, cs] = u[rs, cs] * sv

    pg = pg_ref[...].astype(F32)
    d = x_ref.shape[1]
    gate_a = jax.nn.sigmoid(pg[:, :d])
    gate_b = jax.nn.sigmoid(pg[:, d:])
    merged = gate_a * _mm(ya_ref[...], woa_ref[...]) + gate_b * _mm(yb_ref[...], wob_ref[...])
    g1 = mod_ref[0, 2:3, :]
    x1 = x_ref[...] + g1 * _mm(merged, wo_ref[...])
    x1_ref[...] = x1
    sh2 = mod_ref[0, 3:4, :]
    sc2 = mod_ref[0, 4:5, :]
    h2 = (_rms(x1) * n2g_ref[...]) * (1.0 + sc2) + sh2
    h2_ref[...] = h2
    h2p_ref[...] = _pack_halves(h2)


def _mix(p_uv, p_gate, y_a, xf, mod3, seq, ln_g, ln_b, ws, bs, w_out_a, w_out_b, w_out, norm2_g):
    t, d = xf.shape
    tm = min(MIX_TILE, seq)
    per_seq = seq // tm
    row = lambda a: a.reshape(1, -1)
    bsm = jnp.repeat(bs.T, GMLP_GROUP_DIM, axis=1)
    consts = (row(ln_g), row(ln_b), ws, bsm, w_out_a.astype(BF16), w_out_b.astype(BF16),
              w_out.astype(BF16), row(norm2_g))
    full = lambda a: pl.BlockSpec(a.shape, lambda i: (0,) * a.ndim)
    tile = lambda a: pl.BlockSpec((tm, a.shape[1]), lambda i: (i, 0))
    return pl.pallas_call(
        _mix_kernel,
        grid=(t // tm,),
        in_specs=[tile(p_uv), tile(p_gate), tile(y_a), tile(xf),
                  pl.BlockSpec((1,) + mod3.shape[1:], lambda i: (i // per_seq, 0, 0))]
                 + [full(a) for a in consts],
        out_specs=[pl.BlockSpec((tm, d), lambda i: (i, 0)), pl.BlockSpec((tm, d), lambda i: (i, 0)),
                   pl.BlockSpec((tm, d // 2), lambda i: (i, 0))],
        out_shape=[jax.ShapeDtypeStruct((t, d), F32), jax.ShapeDtypeStruct((t, d), F32),
                   jax.ShapeDtypeStruct((t, d // 2), jnp.uint32)],
        scratch_shapes=[pltpu.VMEM((tm, GMLP_WIDTH), F32)],
        compiler_params=_params("arbitrary"),
        name="mix",
    )(p_uv, p_gate, y_a, xf, mod3, *consts)


def _first_argmax(vals, idx, big):
    m = jnp.max(vals, axis=0, keepdims=True)
    first = jnp.min(jnp.where(vals == m, idx, big), axis=0, keepdims=True)
    return m, first


def _route_kernel(h_ref, rwt_ref, bias_ref, eidx_ref, wts_ref):
    tm = h_ref.shape[0]
    ne = N_EXPERTS
    hhi, hmid, hlo = _split3(h_ref[...])
    whi, wmid, wlo = _split3(rwt_ref[...])
    nt = lambda p, q: lax.dot_general(p, q, (((1,), (1,)), ((), ())), preferred_element_type=F32)
    logits = (nt(whi, hhi) + (nt(whi, hmid) + nt(wmid, hhi))
              + (nt(whi, hlo) + nt(wmid, hmid) + nt(wlo, hhi)))
    scores = jax.nn.sigmoid(logits)
    sel = scores + bias_ref[...]
    neg = -jnp.inf

    gs = GROUP_SIZE
    gidx = lax.broadcasted_iota(jnp.int32, (gs, tm), 0)
    grp_rows = []
    for gi in range(N_EXPERT_GROUPS):
        blk = sel[gi * gs:(gi + 1) * gs, :]
        m1, i1 = _first_argmax(blk, gidx, gs)
        m2 = jnp.max(jnp.where(gidx == i1, neg, blk), axis=0, keepdims=True)
        grp_rows.append(m1 + m2)
    grp = jnp.concatenate(grp_rows, axis=0)

    ng = N_EXPERT_GROUPS
    giota = lax.broadcasted_iota(jnp.int32, (ng, tm), 0)
    gsel = jnp.zeros((ng, tm), jnp.bool_)
    work = grp
    for _ in range(TOPK_GROUPS):
        _, gi1 = _first_argmax(work, giota, ng)
        hit = giota == gi1
        gsel = gsel | hit
        work = jnp.where(hit, neg, work)
    gself = jnp.where(gsel, 1.0, 0.0)
    emask = jnp.concatenate(
        [jnp.broadcast_to(gself[gi:gi + 1, :], (gs, tm)) for gi in range(ng)], axis=0) > 0.5

    eiota = lax.broadcasted_iota(jnp.int32, (ne, tm), 0)
    work = jnp.where(emask, sel, neg)
    idx_rows, w_rows = [], []
    for _ in range(TOP_K):
        _, e1 = _first_argmax(work, eiota, ne)
        hit = eiota == e1
        idx_rows.append(e1)
        w_rows.append(jnp.sum(jnp.where(hit, scores, 0.0), axis=0, keepdims=True))
        work = jnp.where(hit, neg, work)
    eidx_ref[...] = jnp.concatenate(idx_rows, axis=0)
    w = jnp.concatenate(w_rows, axis=0)
    wts_ref[...] = w / jnp.sum(w, axis=0, keepdims=True) * ROUTED_SCALE


def _route(h2, router_w, router_bias):
    t, d = h2.shape
    tm = min(ROUTE_TILE, t)
    rwt = router_w.T
    bias = router_bias.reshape(N_EXPERTS, 1)
    return pl.pallas_call(
        _route_kernel,
        grid=(t // tm,),
        in_specs=[pl.BlockSpec((tm, d), lambda i: (i, 0)),
                  pl.BlockSpec(rwt.shape, lambda i: (0, 0)),
                  pl.BlockSpec(bias.shape, lambda i: (0, 0))],
        out_specs=[pl.BlockSpec((TOP_K, tm), lambda i: (0, i)), pl.BlockSpec((TOP_K, tm), lambda i: (0, i))],
        out_shape=[jax.ShapeDtypeStruct((TOP_K, t), jnp.int32), jax.ShapeDtypeStruct((TOP_K, t), F32)],
        compiler_params=_params("arbitrary"),
        name="route",
    )(h2, rwt, bias)


def _rank_kernel(e_ref, upper_ref, rank_ref, count_ref):
    tm = e_ref.shape[1]
    ne = N_EXPERTS

    @pl.when(pl.program_id(0) == 0)
    def _():
        count_ref[...] = jnp.zeros_like(count_ref)

    e = e_ref[...]
    eiota = lax.broadcasted_iota(jnp.int32, (ne, tm), 0)
    onehot = jnp.zeros((ne, tm), F32)
    for kk in range(TOP_K):
        onehot = onehot + jnp.where(eiota == e[kk:kk + 1, :], 1.0, 0.0)
    before = jnp.dot(onehot.astype(BF16), upper_ref[...], preferred_element_type=F32)
    base = count_ref[:, 0:1]
    tot = before + base
    rows = [jnp.sum(jnp.where(eiota == e[kk:kk + 1, :], tot, 0.0), axis=0, keepdims=True)
            for kk in range(TOP_K)]
    rank_ref[...] = jnp.concatenate(rows, axis=0).astype(jnp.int32)
    count_ref[...] = count_ref[...] + jnp.sum(onehot, axis=1, keepdims=True)


def _rank(eidx):
    k, t = eidx.shape
    tm = min(RANK_TILE, t)
    upper = jnp.triu(jnp.ones((tm, tm), F32), 1).astype(BF16)
    return pl.pallas_call(
        _rank_kernel,
        grid=(t // tm,),
        in_specs=[pl.BlockSpec((k, tm), lambda i: (0, i)), pl.BlockSpec((tm, tm), lambda i: (0, 0))],
        out_specs=[pl.BlockSpec((k, tm), lambda i: (0, i)), pl.BlockSpec((N_EXPERTS, 128), lambda i: (0, 0))],
        out_shape=[jax.ShapeDtypeStruct((k, t), jnp.int32), jax.ShapeDtypeStruct((N_EXPERTS, 128), F32)],
        compiler_params=_params("arbitrary"),
        name="rank",
    )(eidx, upper)


def _dest_kernel(e_ref, rank_ref, pst_ref, dest_ref):
    tm = e_ref.shape[1]
    e = e_ref[...]
    eiota = lax.broadcasted_iota(jnp.int32, (N_EXPERTS, tm), 0)
    pst = pst_ref[...]
    rows = [jnp.sum(jnp.where(eiota == e[kk:kk + 1, :], pst, 0.0), axis=0, keepdims=True)
            for kk in range(TOP_K)]
    dest_ref[...] = jnp.concatenate(rows, axis=0).astype(jnp.int32) + rank_ref[...]


def _dest(eidx, rank, pstarts):
    k, t = eidx.shape
    tm = min(RANK_TILE, t)
    pst = pstarts.astype(F32).reshape(N_EXPERTS, 1)
    return pl.pallas_call(
        _dest_kernel,
        grid=(t // tm,),
        in_specs=[pl.BlockSpec((k, tm), lambda i: (0, i)), pl.BlockSpec((k, tm), lambda i: (0, i)),
                  pl.BlockSpec((N_EXPERTS, 1), lambda i: (0, 0))],
        out_specs=pl.BlockSpec((k, tm), lambda i: (0, i)),
        out_shape=jax.ShapeDtypeStruct((k, t), jnp.int32),
        compiler_params=_params("arbitrary"),
        name="dest",
    )(eidx, rank, pst)


def _dispatch_kernel(pst_ref, cnt_ref, nv_ref, dst_ref, h_ref, xs_hbm, zbuf, sem, zsem):
    i = pl.program_id(0)
    tm = h_ref.shape[0]
    rb = zbuf.shape[0]
    nblk = xs_hbm.shape[0] // rb

    @pl.when(i == 0)
    def _():
        zbuf[...] = jnp.zeros_like(zbuf)

        def pads(e, wait):
            off = pst_ref[e] + cnt_ref[e]
            n = (rb - cnt_ref[e] % rb) % rb

            def one(r, carry):
                cp = _row_copy(zbuf, xs_hbm, zsem, 0, off + r)
                cp.wait() if wait else cp.start()
                return carry
            lax.fori_loop(0, n, one, 0)

        def tail(b, wait):
            cp = pltpu.make_async_copy(zbuf, xs_hbm.at[pl.ds(b * rb, rb), :], zsem)
            cp.wait() if wait else cp.start()

        for wait in (False, True):
            lax.fori_loop(0, N_EXPERTS, lambda e, carry: (pads(e, wait), carry)[1], 0)
            lax.fori_loop(nv_ref[0], nblk, lambda b, carry: (tail(b, wait), carry)[1], 0)

    def body(j, carry):
        for kk in range(TOP_K):
            _row_copy(h_ref, xs_hbm, sem, j, dst_ref[0, 0, kk * tm + j]).start(priority=kk % 2)
        return carry
    lax.fori_loop(0, tm, body, 0)
    for kk in range(TOP_K):
        pltpu.make_async_copy(h_ref, h_ref, sem).wait()


def _dispatch(h2p, dest_tk, pstarts, counts, nvalid, n_rows):
    t, d = h2p.shape
    tm = min(DISPATCH_TILE, t)
    nt = t // tm
    dst3 = dest_tk.reshape(TOP_K, nt, tm).transpose(1, 0, 2).reshape(nt, 1, TOP_K * tm)
    grid_spec = pltpu.PrefetchScalarGridSpec(
        num_scalar_prefetch=3,
        grid=(nt,),
        in_specs=[pl.BlockSpec((1, 1, TOP_K * tm), lambda i, *_: (i, 0, 0), memory_space=pltpu.SMEM),
                  pl.BlockSpec((tm, d), lambda i, *_: (i, 0))],
        out_specs=pl.BlockSpec(memory_space=pl.ANY),
        scratch_shapes=[pltpu.VMEM((ROW_BLOCK, d), h2p.dtype), pltpu.SemaphoreType.DMA(()),
                        pltpu.SemaphoreType.DMA(())],
    )
    return pl.pallas_call(
        _dispatch_kernel,
        grid_spec=grid_spec,
        out_shape=jax.ShapeDtypeStruct((n_rows, d), h2p.dtype),
        compiler_params=_params("arbitrary"),
        name="dispatch",
    )(pstarts, counts, nvalid, dst3, h2p)


def _expert_kernel(be_ref, nv_ref, x_ref, w1_ref, w3_ref, w2_ref, y_ref, w1b, w3b, w2b):
    i = pl.program_id(0)
    half = x_ref.shape[1]

    @pl.when((i == 0) | (be_ref[i] != be_ref[jnp.maximum(i - 1, 0)]))
    def _():
        w1b[...] = w1_ref[0].astype(BF16)
        w3b[...] = w3_ref[0].astype(BF16)
        w2b[...] = w2_ref[0].astype(BF16)

    @pl.when(i < nv_ref[0])
    def _():
        hi, lo = _unpack_halves(x_ref[...])
        xa = hi.astype(BF16)
        xb = lo.astype(BF16)
        dot = lambda p, q: jnp.dot(p, q, preferred_element_type=F32)
        h1 = dot(xa, w1b[:half, :]) + dot(xb, w1b[half:, :])
        h3 = dot(xa, w3b[:half, :]) + dot(xb, w3b[half:, :])
        hid = _silu(h1) * h3
        y_ref[...] = _pack_halves(dot(hid.astype(BF16), w2b[...]))

    @pl.when(i >= nv_ref[0])
    def _():
        y_ref[...] = jnp.zeros_like(y_ref)


def _expert(x_sorted, block_e, nvalid, w1, w3, w2):
    n_rows, half = x_sorted.shape
    rb = ROW_BLOCK
    nblk = n_rows // rb
    d, f = w1.shape[1], w1.shape[2]
    grid_spec = pltpu.PrefetchScalarGridSpec(
        num_scalar_prefetch=2,
        grid=(nblk,),
        in_specs=[
            pl.BlockSpec((rb, half), lambda i, be, nv: (jnp.minimum(i, nv[0] - 1), 0)),
            pl.BlockSpec((1, d, f), lambda i, be, nv: (be[i], 0, 0)),
            pl.BlockSpec((1, d, f), lambda i, be, nv: (be[i], 0, 0)),
            pl.BlockSpec((1, f, d), lambda i, be, nv: (be[i], 0, 0)),
        ],
        out_specs=pl.BlockSpec((rb, half), lambda i, be, nv: (i, 0)),
        scratch_shapes=[pltpu.VMEM((d, f), BF16), pltpu.VMEM((d, f), BF16), pltpu.VMEM((f, d), BF16)],
    )
    return pl.pallas_call(
        _expert_kernel,
        grid_spec=grid_spec,
        out_shape=jax.ShapeDtypeStruct((n_rows, half), jnp.uint32),
        compiler_params=_params("arbitrary"),
        name="expert",
    )(block_e, nvalid, x_sorted, w1, w3, w2)


def _combine_kernel(dst_ref, dstn_ref, y_hbm, wts_ref, h_ref, x1_ref, mod_ref, sw1_ref, sw3_ref, sw2_ref,
                    nfg_ref, o_ref, ybuf, sems):
    i = pl.program_id(0)
    n = pl.num_programs(0)
    tm = ybuf.shape[2]
    slot = i % 2

    def gather(idx_ref, s):
        def body(j, carry):
            for kk in range(TOP_K):
                _row_copy(y_hbm, ybuf.at[s, kk], sems.at[s], idx_ref[0, 0, kk * tm + j], j).start(
                    priority=kk % 2)
            return carry
        lax.fori_loop(0, tm, body, 0)

    @pl.when(i == 0)
    def _():
        gather(dst_ref, 0)

    @pl.when(i + 1 < n)
    def _():
        gather(dstn_ref, 1 - slot)

    pltpu.make_async_copy(ybuf.at[slot], ybuf.at[slot], sems.at[slot]).wait()
    w = wts_ref[...]
    acc_hi = acc_lo = None
    for kk in range(TOP_K):
        hi, lo = _unpack_halves(ybuf[slot, kk])
        wk = w[:, kk:kk + 1]
        acc_hi = hi * wk if acc_hi is None else acc_hi + hi * wk
        acc_lo = lo * wk if acc_lo is None else acc_lo + lo * wk
    routed = jnp.concatenate([acc_hi, acc_lo], axis=1)
    hb = h_ref[...].astype(BF16)
    hid = _silu(jnp.dot(hb, sw1_ref[...], preferred_element_type=F32)) * jnp.dot(
        hb, sw3_ref[...], preferred_element_type=F32)
    shared = jnp.dot(hid.astype(BF16), sw2_ref[...], preferred_element_type=F32)
    g2 = mod_ref[0, 5:6, :]
    x2 = x1_ref[...] + g2 * (routed + shared)
    o_ref[...] = _rms(x2) * nfg_ref[...]


def _combine(y_sorted, dest_tk, wts_tk, h2, x1, mod3, seq, sw1, sw3, sw2, normf_g):
    t, d = h2.shape
    tm = min(COMBINE_TILE, seq)
    per_seq = seq // tm
    nt = t // tm
    dst3 = dest_tk.reshape(TOP_K, nt, tm).transpose(1, 0, 2).reshape(nt, 1, TOP_K * tm)
    consts = (sw1.astype(BF16), sw3.astype(BF16), sw2.astype(BF16), normf_g.reshape(1, d))
    full = lambda a: pl.BlockSpec(a.shape, lambda i: (0,) * a.ndim)
    tile = lambda a: pl.BlockSpec((tm, a.shape[1]), lambda i: (i, 0))
    return pl.pallas_call(
        _combine_kernel,
        grid=(nt,),
        in_specs=[pl.BlockSpec((1, 1, TOP_K * tm), lambda i: (i, 0, 0), memory_space=pltpu.SMEM),
                  pl.BlockSpec((1, 1, TOP_K * tm), lambda i: (jnp.minimum(i + 1, nt - 1), 0, 0),
                               memory_space=pltpu.SMEM),
                  pl.BlockSpec(memory_space=pl.ANY),
                  tile(wts_tk), tile(h2), tile(x1),
                  pl.BlockSpec((1,) + mod3.shape[1:], lambda i: (i // per_seq, 0, 0))]
                 + [full(a) for a in consts],
        out_specs=pl.BlockSpec((tm, d), lambda i: (i, 0)),
        out_shape=jax.ShapeDtypeStruct((t, d), F32),
        scratch_shapes=[pltpu.VMEM((2, TOP_K, tm, d // 2), jnp.uint32), pltpu.SemaphoreType.DMA((2,))],
        compiler_params=_params("arbitrary"),
        name="combine",
    )(dst3, dst3, y_sorted, wts_tk, h2, x1, mod3, *consts)


def _dispatch_plan(counts, t):
    rb = ROW_BLOCK
    padded = (counts + rb - 1) // rb * rb
    pends = jnp.cumsum(padded)
    pstarts = pends - padded
    n_rows = t * TOP_K + N_EXPERTS * rb
    block_start = jnp.arange(n_rows // rb, dtype=jnp.int32) * rb
    block_e = jnp.sum((pends[None, :] <= block_start[:, None]).astype(jnp.int32), axis=1)
    block_e = jnp.minimum(block_e, N_EXPERTS - 1)
    nvalid = (pends[-1:] // rb).astype(jnp.int32)
    return pstarts, block_e, nvalid, n_rows


def kernel(x, c, ada_w, ada_b, norm1_g, norm2_g, w_in, tshift_mu, rwkv_w0, rwkv_w_up, rwkv_a0, rwkv_a_up,
           rwkv_g_up, rwkv_k_k, rwkv_k_a, rwkv_r_k, rwkv_ln_g, rwkv_ln_b, gmlp_ln_g, gmlp_ln_b, gmlp_ws,
           gmlp_bs, w_out_a, w_out_b, w_out, router_w, router_bias, exp_w1, exp_w3, exp_w2, shared_w1,
           shared_w3, shared_w2, normf_g):
    bsz, seq, d = x.shape
    t = bsz * seq
    xf = x.reshape(t, d)
    assert ada_w.shape[0] == 1, "single-layer block only"
    l = 0
    mod3 = _mod(c, ada_w[l], ada_b[l]).reshape(bsz, 6, d)
    p_rwkv, p_uv, p_gate = _inproj(xf, mod3, norm1_g[l], w_in[l], seq)
    y_a = _rwkv(p_rwkv, bsz, seq, tshift_mu[l], rwkv_w0[l], rwkv_w_up[l], rwkv_a0[l], rwkv_a_up[l],
                rwkv_g_up[l], rwkv_k_k[l], rwkv_k_a[l], rwkv_r_k[l], rwkv_ln_g[l], rwkv_ln_b[l])
    x1, h2, h2p = _mix(p_uv, p_gate, y_a, xf, mod3, seq, gmlp_ln_g[l], gmlp_ln_b[l], gmlp_ws[l], gmlp_bs[l],
                  w_out_a[l], w_out_b[l], w_out[l], norm2_g[l])
    eidx, wts = _route(h2, router_w[l], router_bias[l])
    rank, counts = _rank(eidx)
    counts = counts[:, 0].astype(jnp.int32)
    pstarts, block_e, nvalid, n_rows = _dispatch_plan(counts, t)
    dest = _dest(eidx, rank, pstarts)
    x_sorted = _dispatch(h2p, dest, pstarts, counts, nvalid, n_rows)
    y_sorted = _expert(x_sorted, block_e, nvalid, exp_w1[l], exp_w3[l], exp_w2[l])
    out = _combine(y_sorted, dest, wts.T, h2, x1, mod3, seq, shared_w1[l], shared_w3[l], shared_w2[l],
                   normf_g)
    return out.reshape(bsz, seq, d)
```

```python
import functools

import jax
import jax.numpy as jnp
from jax import lax
from jax.experimental import pallas as pl
from jax.experimental.pallas import tpu as pltpu

F32 = jnp.float32
BF16 = jnp.bfloat16

RWKV_HEAD = 64
RWKV_HEADS = 8
RWKV_WIDTH = RWKV_HEAD * RWKV_HEADS
DECAY_LORA = 64
AAA_LORA = 64
GATE_LORA = 128
RWKV_COLS = 3 * RWKV_WIDTH + DECAY_LORA + AAA_LORA + GATE_LORA
GMLP_WIDTH = 512
GMLP_BLOCK = 128
GMLP_GROUPS = 8
GMLP_GROUP_DIM = GMLP_WIDTH // GMLP_GROUPS
N_EXPERTS = 256
TOP_K = 8
N_EXPERT_GROUPS = 8
GROUP_SIZE = N_EXPERTS // N_EXPERT_GROUPS
TOPK_GROUPS = 4
ROUTED_SCALE = 2.5
RMS_EPS = 1e-6
LN_EPS = 1e-5
GN_EPS = 64e-5

VMEM_LIMIT_BYTES = 56 * 1024 * 1024

CHUNK = 64
RWKV_TILE = 256
ROW_BLOCK = 512
INPROJ_TILE = 256
MIX_TILE = 256
ROUTE_TILE = 512
RANK_TILE = 512
DISPATCH_TILE = 256
COMBINE_TILE = 128


def _params(*sem):
    return pltpu.CompilerParams(dimension_semantics=sem, vmem_limit_bytes=VMEM_LIMIT_BYTES)


def _mm(a, b):
    return jnp.dot(a.astype(BF16), b.astype(BF16), preferred_element_type=F32)


def _mm_nt(a, b):
    return lax.dot_general(a.astype(BF16), b.astype(BF16), (((1,), (1,)), ((), ())),
                           preferred_element_type=F32)


def _mm_tn(a, b):
    return lax.dot_general(a.astype(BF16), b.astype(BF16), (((0,), (0,)), ((), ())),
                           preferred_element_type=F32)


def _split3(a):
    hi = a.astype(BF16)
    r1 = a - hi.astype(F32)
    mid = r1.astype(BF16)
    lo = (r1 - mid.astype(F32)).astype(BF16)
    return hi, mid, lo


def _silu(x):
    return x * jax.nn.sigmoid(x)


def _rms(x):
    return x * lax.rsqrt(jnp.mean(x * x, axis=-1, keepdims=True) + RMS_EPS)


def _pack_halves(x):
    n = x.shape[1] // 2
    bits = lax.bitcast_convert_type(x.astype(BF16).astype(F32), jnp.uint32)
    return bits[:, :n] | (bits[:, n:] >> 16)


def _unpack_halves(u):
    hi = lax.bitcast_convert_type(u & jnp.uint32(0xFFFF0000), F32)
    lo = lax.bitcast_convert_type(u << 16, F32)
    return hi, lo


def _row_copy(src, dst, sem, src_row, dst_row):
    return pltpu.make_async_copy(src.at[pl.ds(src_row, 1), :], dst.at[pl.ds(dst_row, 1), :], sem)


def _mod_kernel(c_ref, w_ref, b_ref, o_ref):
    s = _silu(c_ref[...])
    hi, mid, lo = _split3(s)
    whi, wmid, wlo = _split3(w_ref[...])
    dot = lambda p, q: jnp.dot(p, q, preferred_element_type=F32)
    acc = dot(hi, whi) + (dot(hi, wmid) + dot(mid, whi)) + (dot(hi, wlo) + dot(mid, wmid) + dot(lo, whi))
    o_ref[...] = acc + b_ref[...]


def _mod(c, ada_w, ada_b):
    bsz, d = c.shape
    n = ada_w.shape[1]
    tn = d
    return pl.pallas_call(
        _mod_kernel,
        grid=(n // tn,),
        in_specs=[pl.BlockSpec((bsz, d), lambda j: (0, 0)),
                  pl.BlockSpec((d, tn), lambda j: (0, j)),
                  pl.BlockSpec((1, tn), lambda j: (0, j))],
        out_specs=pl.BlockSpec((bsz, tn), lambda j: (0, j)),
        out_shape=jax.ShapeDtypeStruct((bsz, n), F32),
        compiler_params=_params("arbitrary"),
        name="mod",
    )(c, ada_w, ada_b.reshape(1, n))


def _inproj_kernel(x_ref, mod_ref, g_ref, wr_ref, wuv_ref, wg_ref, pr_ref, puv_ref, pg_ref):
    sh = mod_ref[0, 0:1, :]
    sc = mod_ref[0, 1:2, :]
    h = (_rms(x_ref[...]) * g_ref[...]) * (1.0 + sc) + sh
    hb = h.astype(BF16)
    pr_ref[...] = jnp.dot(hb, wr_ref[...], preferred_element_type=F32).astype(pr_ref.dtype)
    puv_ref[...] = jnp.dot(hb, wuv_ref[...], preferred_element_type=F32).astype(puv_ref.dtype)
    pg_ref[...] = jnp.dot(hb, wg_ref[...], preferred_element_type=F32).astype(pg_ref.dtype)


def _inproj(xf, mod3, norm1_g, w_in, seq):
    t, d = xf.shape
    tm = min(INPROJ_TILE, seq)
    per_seq = seq // tm
    wr = w_in[:, :RWKV_COLS].astype(BF16)
    wuv = w_in[:, RWKV_COLS:RWKV_COLS + 2 * GMLP_WIDTH].astype(BF16)
    wg = w_in[:, RWKV_COLS + 2 * GMLP_WIDTH:].astype(BF16)
    full = lambda a: pl.BlockSpec(a.shape, lambda i: (0,) * a.ndim)
    g = norm1_g.reshape(1, d)
    return pl.pallas_call(
        _inproj_kernel,
        grid=(t // tm,),
        in_specs=[pl.BlockSpec((tm, d), lambda i: (i, 0)),
                  pl.BlockSpec((1,) + mod3.shape[1:], lambda i: (i // per_seq, 0, 0)),
                  full(g), full(wr), full(wuv), full(wg)],
        out_specs=[pl.BlockSpec((tm, wr.shape[1]), lambda i: (i, 0)),
                   pl.BlockSpec((tm, wuv.shape[1]), lambda i: (i, 0)),
                   pl.BlockSpec((tm, wg.shape[1]), lambda i: (i, 0))],
        out_shape=[jax.ShapeDtypeStruct((t, wr.shape[1]), BF16),
                   jax.ShapeDtypeStruct((t, wuv.shape[1]), BF16),
                   jax.ShapeDtypeStruct((t, wg.shape[1]), BF16)],
        compiler_params=_params("arbitrary"),
        name="inproj",
    )(xf, mod3, g, wr, wuv, wg)


def _unit_lower_inverses(ns, row, col):
    c = ns[0].shape[0]
    eye = jnp.where(row == col, 1.0, 0.0).astype(F32)
    blk8 = (row // 8) == (col // 8)
    n8 = [jnp.where(blk8, n, 0.0) for n in ns]
    n8_2 = [_mm(a, a) for a in n8]
    n8_4 = [_mm(a, a) for a in n8_2]
    inv = [eye + a for a in n8]
    inv = [i + _mm(i, b) for i, b in zip(inv, n8_2)]
    inv = [i + _mm(i, b) for i, b in zip(inv, n8_4)]
    s = 8
    while s < c:
        sel = ((row // (2 * s)) == (col // (2 * s))) & ((row // s) != (col // s))
        low = [jnp.where(sel, n, 0.0) for n in ns]
        t1 = [_mm(l, i) for l, i in zip(low, inv)]
        inv = [i + _mm(i, t) for i, t in zip(inv, t1)]
        s *= 2
    return inv


def _rwkv_kernel(p_ref, mu_ref, w0_ref, wup_ref, a0_ref, aup_ref, gup_ref, kk_ref, ka_ref, rk_ref,
                 lng_ref, lnb_ref, bd_ref, y_ref, state_ref, carry_ref, ycat_ref, *, c):
    tm = p_ref.shape[0]
    nq = tm // c
    hd = RWKV_HEAD
    nh = RWKV_HEADS

    @pl.when(pl.program_id(1) == 0)
    def _():
        state_ref[...] = jnp.zeros_like(state_ref)
        carry_ref[...] = jnp.zeros_like(carry_ref)

    p = p_ref[...].astype(F32)
    rowc = lax.broadcasted_iota(jnp.int32, (tm, 1), 0)
    prev = jnp.where(rowc == 0, carry_ref[...], pltpu.roll(p, 1, axis=0))
    carry_ref[...] = p[tm - 1:tm, :]
    ps = p + (prev - p) * mu_ref[...]

    o = RWKV_WIDTH
    r = ps[:, 0:o]
    k = ps[:, o:2 * o]
    v = ps[:, 2 * o:3 * o]
    xw = ps[:, 3 * o:3 * o + DECAY_LORA]
    xa = ps[:, 3 * o + DECAY_LORA:3 * o + DECAY_LORA + AAA_LORA]
    xg = ps[:, 3 * o + DECAY_LORA + AAA_LORA:]

    z = -(w0_ref[...] + _mm(jnp.tanh(xw), wup_ref[...]))
    softplus = jnp.maximum(z, 0.0) + jnp.log1p(jnp.exp(-jnp.abs(z)))
    wlog = -softplus - 0.5
    ld = -jnp.exp(wlog)
    a = jax.nn.sigmoid(a0_ref[...] + _mm(xa, aup_ref[...]))
    g = _mm(jax.nn.sigmoid(xg), gup_ref[...])

    def head_sum(x):
        hi = x.astype(BF16)
        lo = (x - hi.astype(F32)).astype(BF16)
        bd = bd_ref[...]
        return jnp.dot(hi, bd, preferred_element_type=F32) + jnp.dot(lo, bd, preferred_element_type=F32)

    kkf = k * kk_ref[...]
    kkn = kkf * (1.0 / jnp.maximum(jnp.sqrt(head_sum(kkf * kkf)), 1e-12))
    k2 = k * (1.0 + (a - 1.0) * ka_ref[...])
    avec = -kkn
    bvec = kkn * a

    rowt = lax.broadcasted_iota(jnp.int32, (tm, tm), 0)
    colt = lax.broadcasted_iota(jnp.int32, (tm, tm), 1)
    tri = jnp.where((rowt >= colt) & ((rowt // c) == (colt // c)), 1.0, 0.0).astype(BF16)
    hi, mid, lo = _split3(ld)
    dot = lambda q: jnp.dot(tri, q, preferred_element_type=F32)
    cl = dot(hi) + dot(mid) + dot(lo)
    cl_end = jnp.concatenate(
        [jnp.broadcast_to(cl[(q + 1) * c - 1:(q + 1) * c, :], (c, o)) for q in range(nq)], axis=0)
    g_inv = jnp.exp(-cl)
    g_tail = jnp.exp(cl_end - cl)
    g_end = jnp.exp(cl_end)
    rt = (r * jnp.exp(cl)).astype(BF16)
    at = (avec * jnp.exp(cl - ld)).astype(BF16)
    kt = (k2 * g_inv).astype(BF16)
    bt = (bvec * g_inv).astype(BF16)
    kend = (k2 * g_tail).astype(BF16)
    bend = (bvec * g_tail).astype(BF16)
    vb = v.astype(BF16)

    row = lax.broadcasted_iota(jnp.int32, (c, c), 0)
    col = lax.broadcasted_iota(jnp.int32, (c, c), 1)
    strict = row > col
    incl = row >= col
    items = [(q, h) for q in range(nq) for h in range(nh)]
    blk = lambda x, q, h: x[q * c:(q + 1) * c, h * hd:(h + 1) * hd]

    lhs = [jnp.concatenate([blk(at, q, h), blk(rt, q, h)], axis=0) for q, h in items]
    mat_b = [_mm_nt(l, blk(bt, q, h)) for l, (q, h) in zip(lhs, items)]
    mat_k = [_mm_nt(l, blk(kt, q, h)) for l, (q, h) in zip(lhs, items)]
    a_ab = [jnp.where(strict, m[:c], 0.0) for m in mat_b]
    a_rb = [jnp.where(incl, m[c:], 0.0).astype(BF16) for m in mat_b]
    a_ak = [jnp.where(strict, m[:c], 0.0) for m in mat_k]
    a_rk = [jnp.where(incl, m[c:], 0.0) for m in mat_k]
    tinv = _unit_lower_inverses(a_ab, row, col)
    akv = [_mm(m, blk(vb, q, h)) for m, (q, h) in zip(a_ak, items)]
    y_loc = [_mm(m, blk(vb, q, h)) for m, (q, h) in zip(a_rk, items)]
    s_loc = [_mm_tn(blk(vb, q, h), blk(kend, q, h)) for q, h in items]
    wmat = [_mm(t, blk(at, q, h)).astype(BF16) for t, (q, h) in zip(tinv, items)]
    ut = [_mm(t, m) for t, m in zip(tinv, akv)]

    state = [state_ref[h] for h in range(nh)]
    for q in range(nq):
        base = q * nh
        sb = [s.astype(BF16) for s in state]
        u = [_mm_nt(wmat[base + h], sb[h]) + ut[base + h] for h in range(nh)]
        yh = [_mm_nt(blk(rt, q, h), sb[h]) + y_loc[base + h] for h in range(nh)]
        ub = [x.astype(BF16) for x in u]
        yh = [y0 + _mm(a_rb[base + h], ub[h]) for h, y0 in enumerate(yh)]
        state = [state[h] * g_end[q * c:q * c + 1, h * hd:(h + 1) * hd]
                 + _mm_tn(ub[h], blk(bend, q, h)) + s_loc[base + h] for h in range(nh)]
        for h in range(nh):
            ycat_ref[q * c:(q + 1) * c, h * hd:(h + 1) * hd] = yh[h]
    for h in range(nh):
        state_ref[h] = state[h]

    y = ycat_ref[...]
    dev = y - head_sum(y) * (1.0 / hd)
    rstd = lax.rsqrt(head_sum(dev * dev) * (1.0 / hd) + GN_EPS)
    bonus = head_sum(r * k2 * rk_ref[...])
    yn = dev * rstd * lng_ref[...] + lnb_ref[...]
    y_ref[...] = (yn + bonus * v) * g


def _rwkv(p_rwkv, bsz, seq, tshift_mu, w0, w_up, a0, a_up, g_up, k_k, k_a, r_k, ln_g, ln_b):
    t = p_rwkv.shape[0]
    c = min(CHUNK, seq)
    tm = min(RWKV_TILE, seq)
    nstep = seq // tm
    row = lambda a: a.reshape(1, -1)
    head_of = jnp.arange(RWKV_WIDTH, dtype=jnp.int32) // RWKV_HEAD
    block_diag = (head_of[:, None] == head_of[None, :]).astype(BF16)
    args = (row(tshift_mu), row(w0), w_up, row(a0), a_up, g_up, row(k_k), row(k_a), row(r_k),
            row(ln_g), row(ln_b), block_diag)
    full = lambda a: pl.BlockSpec(a.shape, lambda b, j: (0,) * a.ndim)
    return pl.pallas_call(
        functools.partial(_rwkv_kernel, c=c),
        grid=(bsz, nstep),
        in_specs=[pl.BlockSpec((tm, RWKV_COLS), lambda b, j: (b * nstep + j, 0))] + [full(a) for a in args],
        out_specs=pl.BlockSpec((tm, RWKV_WIDTH), lambda b, j: (b * nstep + j, 0)),
        out_shape=jax.ShapeDtypeStruct((t, RWKV_WIDTH), F32),
        scratch_shapes=[pltpu.VMEM((RWKV_HEADS, RWKV_HEAD, RWKV_HEAD), F32),
                        pltpu.VMEM((1, RWKV_COLS), F32),
                        pltpu.VMEM((tm, RWKV_WIDTH), F32)],
        compiler_params=_params("arbitrary", "arbitrary"),
        name="rwkv",
    )(p_rwkv, *args)


def _mix_kernel(puv_ref, pg_ref, ya_ref, x_ref, mod_ref, lng_ref, lnb_ref, ws_ref, bsm_ref,
                woa_ref, wob_ref, wo_ref, n2g_ref, x1_ref, h2_ref, h2p_ref, yb_ref):
    tm = x_ref.shape[0]
    gw = GMLP_WIDTH
    puv = puv_ref[...].astype(F32)
    u = jax.nn.gelu(puv[:, :gw])
    vg = jax.nn.gelu(puv[:, gw:])
    mu = jnp.mean(vg, axis=-1, keepdims=True)
    var = jnp.mean(jnp.square(vg - mu), axis=-1, keepdims=True)
    vln = (vg - mu) * lax.rsqrt(var + LN_EPS) * lng_ref[...] + lnb_ref[...]

    row = lax.broadcasted_iota(jnp.int32, (GMLP_BLOCK, GMLP_BLOCK), 0)
    col = lax.broadcasted_iota(jnp.int32, (GMLP_BLOCK, GMLP_BLOCK), 1)
    gd = GMLP_GROUP_DIM
    for gi in range(GMLP_GROUPS):
        wsm = jnp.where(row >= col, ws_ref[gi], 0.0).astype(BF16)
        for nb in range(tm // GMLP_BLOCK):
            rs = slice(nb * GMLP_BLOCK, (nb + 1) * GMLP_BLOCK)
            cs = slice(gi * gd, (gi + 1) * gd)
            sv = jnp.dot(wsm, vln[rs, cs].astype(BF16), preferred_element_type=F32) + bsm_ref[:, cs]
            yb_ref[rs, cs] = u[rs, cs] * sv

    pg = pg_ref[...].astype(F32)
    d = x_ref.shape[1]
    gate_a = jax.nn.sigmoid(pg[:, :d])
    gate_b = jax.nn.sigmoid(pg[:, d:])
    merged = gate_a * _mm(ya_ref[...], woa_ref[...]) + gate_b * _mm(yb_ref[...], wob_ref[...])
    g1 = mod_ref[0, 2:3, :]
    x1 = x_ref[...] + g1 * _mm(merged, wo_ref[...])
    x1_ref[...] = x1
    sh2 = mod_ref[0, 3:4, :]
    sc2 = mod_ref[0, 4:5, :]
    h2 = (_rms(x1) * n2g_ref[...]) * (1.0 + sc2) + sh2
    h2_ref[...] = h2
    h2p_ref[...] = _pack_halves(h2)


def _mix(p_uv, p_gate, y_a, xf, mod3, seq, ln_g, ln_b, ws, bs, w_out_a, w_out_b, w_out, norm2_g):
    t, d = xf.shape
    tm = min(MIX_TILE, seq)
    per_seq = seq // tm
    row = lambda a: a.reshape(1, -1)
    bsm = jnp.repeat(bs.T, GMLP_GROUP_DIM, axis=1)
    consts = (row(ln_g), row(ln_b), ws, bsm, w_out_a.astype(BF16), w_out_b.astype(BF16),
              w_out.astype(BF16), row(norm2_g))
    full = lambda a: pl.BlockSpec(a.shape, lambda i: (0,) * a.ndim)
    tile = lambda a: pl.BlockSpec((tm, a.shape[1]), lambda i: (i, 0))
    return pl.pallas_call(
        _mix_kernel,
        grid=(t // tm,),
        in_specs=[tile(p_uv), tile(p_gate), tile(y_a), tile(xf),
                  pl.BlockSpec((1,) + mod3.shape[1:], lambda i: (i // per_seq, 0, 0))]
                 + [full(a) for a in consts],
        out_specs=[pl.BlockSpec((tm, d), lambda i: (i, 0)), pl.BlockSpec((tm, d), lambda i: (i, 0)),
                   pl.BlockSpec((tm, d // 2), lambda i: (i, 0))],
        out_shape=[jax.ShapeDtypeStruct((t, d), F32), jax.ShapeDtypeStruct((t, d), F32),
                   jax.ShapeDtypeStruct((t, d // 2), jnp.uint32)],
        scratch_shapes=[pltpu.VMEM((tm, GMLP_WIDTH), F32)],
        compiler_params=_params("arbitrary"),
        name="mix",
    )(p_uv, p_gate, y_a, xf, mod3, *consts)


def _first_argmax(vals, idx, big):
    m = jnp.max(vals, axis=0, keepdims=True)
    first = jnp.min(jnp.where(vals == m, idx, big), axis=0, keepdims=True)
    return m, first


def _route_kernel(h_ref, rwt_ref, bias_ref, eidx_ref, wts_ref):
    tm = h_ref.shape[0]
    ne = N_EXPERTS
    hhi, hmid, _ = _split3(h_ref[...])
    whi, wmid, _ = _split3(rwt_ref[...])
    nt = lambda p, q: lax.dot_general(p, q, (((1,), (1,)), ((), ())), preferred_element_type=F32)
    logits = nt(whi, hhi) + (nt(whi, hmid) + nt(wmid, hhi))
    scores = jax.nn.sigmoid(logits)
    sel = scores + bias_ref[...]
    neg = -jnp.inf

    gs = GROUP_SIZE
    gidx = lax.broadcasted_iota(jnp.int32, (gs, tm), 0)
    grp_rows = []
    for gi in range(N_EXPERT_GROUPS):
        blk = sel[gi * gs:(gi + 1) * gs, :]
        m1, i1 = _first_argmax(blk, gidx, gs)
        m2 = jnp.max(jnp.where(gidx == i1, neg, blk), axis=0, keepdims=True)
        grp_rows.append(m1 + m2)
    grp = jnp.concatenate(grp_rows, axis=0)

    ng = N_EXPERT_GROUPS
    giota = lax.broadcasted_iota(jnp.int32, (ng, tm), 0)
    gsel = jnp.zeros((ng, tm), jnp.bool_)
    work = grp
    for _ in range(TOPK_GROUPS):
        _, gi1 = _first_argmax(work, giota, ng)
        hit = giota == gi1
        gsel = gsel | hit
        work = jnp.where(hit, neg, work)
    gself = jnp.where(gsel, 1.0, 0.0)
    emask = jnp.concatenate(
        [jnp.broadcast_to(gself[gi:gi + 1, :], (gs, tm)) for gi in range(ng)], axis=0) > 0.5

    eiota = lax.broadcasted_iota(jnp.int32, (ne, tm), 0)
    work = jnp.where(emask, sel, neg)
    idx_rows, w_rows = [], []
    for _ in range(TOP_K):
        _, e1 = _first_argmax(work, eiota, ne)
        hit = eiota == e1
        idx_rows.append(e1)
        w_rows.append(jnp.sum(jnp.where(hit, scores, 0.0), axis=0, keepdims=True))
        work = jnp.where(hit, neg, work)
    eidx_ref[...] = jnp.concatenate(idx_rows, axis=0)
    w = jnp.concatenate(w_rows, axis=0)
    wts_ref[...] = w / jnp.sum(w, axis=0, keepdims=True) * ROUTED_SCALE


def _route(h2, router_w, router_bias):
    t, d = h2.shape
    tm = min(ROUTE_TILE, t)
    rwt = router_w.T
    bias = router_bias.reshape(N_EXPERTS, 1)
    return pl.pallas_call(
        _route_kernel,
        grid=(t // tm,),
        in_specs=[pl.BlockSpec((tm, d), lambda i: (i, 0)),
                  pl.BlockSpec(rwt.shape, lambda i: (0, 0)),
                  pl.BlockSpec(bias.shape, lambda i: (0, 0))],
        out_specs=[pl.BlockSpec((TOP_K, tm), lambda i: (0, i)), pl.BlockSpec((TOP_K, tm), lambda i: (0, i))],
        out_shape=[jax.ShapeDtypeStruct((TOP_K, t), jnp.int32), jax.ShapeDtypeStruct((TOP_K, t), F32)],
        compiler_params=_params("arbitrary"),
        name="route",
    )(h2, rwt, bias)


def _rank_kernel(e_ref, upper_ref, rank_ref, count_ref):
    tm = e_ref.shape[1]
    ne = N_EXPERTS

    @pl.when(pl.program_id(0) == 0)
    def _():
        count_ref[...] = jnp.zeros_like(count_ref)

    e = e_ref[...]
    eiota = lax.broadcasted_iota(jnp.int32, (ne, tm), 0)
    onehot = jnp.zeros((ne, tm), F32)
    for kk in range(TOP_K):
        onehot = onehot + jnp.where(eiota == e[kk:kk + 1, :], 1.0, 0.0)
    before = jnp.dot(onehot.astype(BF16), upper_ref[...], preferred_element_type=F32)
    base = count_ref[:, 0:1]
    tot = before + base
    rows = [jnp.sum(jnp.where(eiota == e[kk:kk + 1, :], tot, 0.0), axis=0, keepdims=True)
            for kk in range(TOP_K)]
    rank_ref[...] = jnp.concatenate(rows, axis=0).astype(jnp.int32)
    count_ref[...] = count_ref[...] + jnp.sum(onehot, axis=1, keepdims=True)


def _rank(eidx):
    k, t = eidx.shape
    tm = min(RANK_TILE, t)
    upper = jnp.triu(jnp.ones((tm, tm), F32), 1).astype(BF16)
    return pl.pallas_call(
        _rank_kernel,
        grid=(t // tm,),
        in_specs=[pl.BlockSpec((k, tm), lambda i: (0, i)), pl.BlockSpec((tm, tm), lambda i: (0, 0))],
        out_specs=[pl.BlockSpec((k, tm), lambda i: (0, i)), pl.BlockSpec((N_EXPERTS, 128), lambda i: (0, 0))],
        out_shape=[jax.ShapeDtypeStruct((k, t), jnp.int32), jax.ShapeDtypeStruct((N_EXPERTS, 128), F32)],
        compiler_params=_params("arbitrary"),
        name="rank",
    )(eidx, upper)


def _dest_kernel(e_ref, rank_ref, pst_ref, dest_ref):
    tm = e_ref.shape[1]
    e = e_ref[...]
    eiota = lax.broadcasted_iota(jnp.int32, (N_EXPERTS, tm), 0)
    pst = pst_ref[...]
    rows = [jnp.sum(jnp.where(eiota == e[kk:kk + 1, :], pst, 0.0), axis=0, keepdims=True)
            for kk in range(TOP_K)]
    dest_ref[...] = jnp.concatenate(rows, axis=0).astype(jnp.int32) + rank_ref[...]


def _dest(eidx, rank, pstarts):
    k, t = eidx.shape
    tm = min(RANK_TILE, t)
    pst = pstarts.astype(F32).reshape(N_EXPERTS, 1)
    return pl.pallas_call(
        _dest_kernel,
        grid=(t // tm,),
        in_specs=[pl.BlockSpec((k, tm), lambda i: (0, i)), pl.BlockSpec((k, tm), lambda i: (0, i)),
                  pl.BlockSpec((N_EXPERTS, 1), lambda i: (0, 0))],
        out_specs=pl.BlockSpec((k, tm), lambda i: (0, i)),
        out_shape=jax.ShapeDtypeStruct((k, t), jnp.int32),
        compiler_params=_params("arbitrary"),
        name="dest",
    )(eidx, rank, pst)


def _dispatch_kernel(pst_ref, cnt_ref, nv_ref, dst_ref, h_ref, xs_hbm, zbuf, sem, zsem):
    i = pl.program_id(0)
    tm = h_ref.shape[0]
    rb = zbuf.shape[0]
    nblk = xs_hbm.shape[0] // rb

    @pl.when(i == 0)
    def _():
        zbuf[...] = jnp.zeros_like(zbuf)

        def pads(e, wait):
            off = pst_ref[e] + cnt_ref[e]
            n = (rb - cnt_ref[e] % rb) % rb
            head = jnp.minimum((8 - off % 8) % 8, n)

            def one(r, carry):
                cp = _row_copy(zbuf, xs_hbm, zsem, 0, off + r)
                cp.wait() if wait else cp.start()
                return carry
            lax.fori_loop(0, head, one, 0)

            def group(g, carry):
                start = pl.multiple_of(off + head + g * 8, 8)
                cp = pltpu.make_async_copy(zbuf.at[pl.ds(0, 8), :], xs_hbm.at[pl.ds(start, 8), :], zsem)
                cp.wait() if wait else cp.start()
                return carry
            lax.fori_loop(0, (n - head) // 8, group, 0)

        def tail(b, wait):
            cp = pltpu.make_async_copy(zbuf, xs_hbm.at[pl.ds(b * rb, rb), :], zsem)
            cp.wait() if wait else cp.start()

        for wait in (False, True):
            lax.fori_loop(0, N_EXPERTS, lambda e, carry: (pads(e, wait), carry)[1], 0)
            lax.fori_loop(nv_ref[0], nblk, lambda b, carry: (tail(b, wait), carry)[1], 0)

    def body(j, carry):
        for kk in range(TOP_K):
            _row_copy(h_ref, xs_hbm, sem, j, dst_ref[0, 0, kk * tm + j]).start(priority=kk % 2)
        return carry
    lax.fori_loop(0, tm, body, 0)
    for kk in range(TOP_K):
        pltpu.make_async_copy(h_ref, h_ref, sem).wait()


def _dispatch(h2p, dest_tk, pstarts, counts, nvalid, n_rows):
    t, d = h2p.shape
    tm = min(DISPATCH_TILE, t)
    nt = t // tm
    dst3 = dest_tk.reshape(TOP_K, nt, tm).transpose(1, 0, 2).reshape(nt, 1, TOP_K * tm)
    grid_spec = pltpu.PrefetchScalarGridSpec(
        num_scalar_prefetch=3,
        grid=(nt,),
        in_specs=[pl.BlockSpec((1, 1, TOP_K * tm), lambda i, *_: (i, 0, 0), memory_space=pltpu.SMEM),
                  pl.BlockSpec((tm, d), lambda i, *_: (i, 0))],
        out_specs=pl.BlockSpec(memory_space=pl.ANY),
        scratch_shapes=[pltpu.VMEM((ROW_BLOCK, d), h2p.dtype), pltpu.SemaphoreType.DMA(()),
                        pltpu.SemaphoreType.DMA(())],
    )
    return pl.pallas_call(
        _dispatch_kernel,
        grid_spec=grid_spec,
        out_shape=jax.ShapeDtypeStruct((n_rows, d), h2p.dtype),
        compiler_params=_params("arbitrary"),
        name="dispatch",
    )(pstarts, counts, nvalid, dst3, h2p)


def _expert_kernel(be_ref, nv_ref, x_ref, w1_ref, w3_ref, w2_ref, y_ref, w1b, w3b, w2b):
    i = pl.program_id(0)
    half = x_ref.shape[1]

    @pl.when((i == 0) | (be_ref[i] != be_ref[jnp.maximum(i - 1, 0)]))
    def _():
        w1b[...] = w1_ref[0].astype(BF16)
        w3b[...] = w3_ref[0].astype(BF16)
        w2b[...] = w2_ref[0].astype(BF16)

    @pl.when(i < nv_ref[0])
    def _():
        hi, lo = _unpack_halves(x_ref[...])
        xa = hi.astype(BF16)
        xb = lo.astype(BF16)
        dot = lambda p, q: jnp.dot(p, q, preferred_element_type=F32)
        h1 = dot(xa, w1b[:half, :]) + dot(xb, w1b[half:, :])
        h3 = dot(xa, w3b[:half, :]) + dot(xb, w3b[half:, :])
        hid = _silu(h1) * h3
        y_ref[...] = _pack_halves(dot(hid.astype(BF16), w2b[...]))

    @pl.when(i >= nv_ref[0])
    def _():
        y_ref[...] = jnp.zeros_like(y_ref)


def _expert(x_sorted, block_e, nvalid, w1, w3, w2):
    n_rows, half = x_sorted.shape
    rb = ROW_BLOCK
    nblk = n_rows // rb
    d, f = w1.shape[1], w1.shape[2]
    grid_spec = pltpu.PrefetchScalarGridSpec(
        num_scalar_prefetch=2,
        grid=(nblk,),
        in_specs=[
            pl.BlockSpec((rb, half), lambda i, be, nv: (jnp.minimum(i, nv[0] - 1), 0)),
            pl.BlockSpec((1, d, f), lambda i, be, nv: (be[i], 0, 0)),
            pl.BlockSpec((1, d, f), lambda i, be, nv: (be[i], 0, 0)),
            pl.BlockSpec((1, f, d), lambda i, be, nv: (be[i], 0, 0)),
        ],
        out_specs=pl.BlockSpec((rb, half), lambda i, be, nv: (i, 0)),
        scratch_shapes=[pltpu.VMEM((d, f), BF16), pltpu.VMEM((d, f), BF16), pltpu.VMEM((f, d), BF16)],
    )
    return pl.pallas_call(
        _expert_kernel,
        grid_spec=grid_spec,
        out_shape=jax.ShapeDtypeStruct((n_rows, half), jnp.uint32),
        compiler_params=_params("arbitrary"),
        name="expert",
    )(block_e, nvalid, x_sorted, w1, w3, w2)


def _combine_kernel(dst_ref, dstn_ref, y_hbm, wts_ref, h_ref, x1_ref, mod_ref, sw1_ref, sw3_ref, sw2_ref,
                    nfg_ref, o_ref, ybuf, sems):
    i = pl.program_id(0)
    n = pl.num_programs(0)
    tm = ybuf.shape[2]
    slot = i % 2

    def gather(idx_ref, s):
        def body(j, carry):
            for kk in range(TOP_K):
                _row_copy(y_hbm, ybuf.at[s, kk], sems.at[s], idx_ref[0, 0, kk * tm + j], j).start(
                    priority=kk % 2)
            return carry
        lax.fori_loop(0, tm, body, 0)

    @pl.when(i == 0)
    def _():
        gather(dst_ref, 0)

    @pl.when(i + 1 < n)
    def _():
        gather(dstn_ref, 1 - slot)

    pltpu.make_async_copy(ybuf.at[slot], ybuf.at[slot], sems.at[slot]).wait()
    w = wts_ref[...]
    acc_hi = acc_lo = None
    for kk in range(TOP_K):
        hi, lo = _unpack_halves(ybuf[slot, kk])
        wk = w[:, kk:kk + 1]
        acc_hi = hi * wk if acc_hi is None else acc_hi + hi * wk
        acc_lo = lo * wk if acc_lo is None else acc_lo + lo * wk
    routed = jnp.concatenate([acc_hi, acc_lo], axis=1)
    hb = h_ref[...].astype(BF16)
    hid = _silu(jnp.dot(hb, sw1_ref[...], preferred_element_type=F32)) * jnp.dot(
        hb, sw3_ref[...], preferred_element_type=F32)
    shared = jnp.dot(hid.astype(BF16), sw2_ref[...], preferred_element_type=F32)
    g2 = mod_ref[0, 5:6, :]
    x2 = x1_ref[...] + g2 * (routed + shared)
    o_ref[...] = _rms(x2) * nfg_ref[...]


def _combine(y_sorted, dest_tk, wts_tk, h2, x1, mod3, seq, sw1, sw3, sw2, normf_g):
    t, d = h2.shape
    tm = min(COMBINE_TILE, seq)
    per_seq = seq // tm
    nt = t // tm
    dst3 = dest_tk.reshape(TOP_K, nt, tm).transpose(1, 0, 2).reshape(nt, 1, TOP_K * tm)
    consts = (sw1.astype(BF16), sw3.astype(BF16), sw2.astype(BF16), normf_g.reshape(1, d))
    full = lambda a: pl.BlockSpec(a.shape, lambda i: (0,) * a.ndim)
    tile = lambda a: pl.BlockSpec((tm, a.shape[1]), lambda i: (i, 0))
    return pl.pallas_call(
        _combine_kernel,
        grid=(nt,),
        in_specs=[pl.BlockSpec((1, 1, TOP_K * tm), lambda i: (i, 0, 0), memory_space=pltpu.SMEM),
                  pl.BlockSpec((1, 1, TOP_K * tm), lambda i: (jnp.minimum(i + 1, nt - 1), 0, 0),
                               memory_space=pltpu.SMEM),
                  pl.BlockSpec(memory_space=pl.ANY),
                  tile(wts_tk), tile(h2), tile(x1),
                  pl.BlockSpec((1,) + mod3.shape[1:], lambda i: (i // per_seq, 0, 0))]
                 + [full(a) for a in consts],
        out_specs=pl.BlockSpec((tm, d), lambda i: (i, 0)),
        out_shape=jax.ShapeDtypeStruct((t, d), F32),
        scratch_shapes=[pltpu.VMEM((2, TOP_K, tm, d // 2), jnp.uint32), pltpu.SemaphoreType.DMA((2,))],
        compiler_params=_params("arbitrary"),
        name="combine",
    )(dst3, dst3, y_sorted, wts_tk, h2, x1, mod3, *consts)


def _dispatch_plan(counts, t):
    rb = ROW_BLOCK
    padded = (counts + rb - 1) // rb * rb
    pends = jnp.cumsum(padded)
    pstarts = pends - padded
    n_rows = t * TOP_K + N_EXPERTS * rb
    block_start = jnp.arange(n_rows // rb, dtype=jnp.int32) * rb
    block_e = jnp.sum((pends[None, :] <= block_start[:, None]).astype(jnp.int32), axis=1)
    block_e = jnp.minimum(block_e, N_EXPERTS - 1)
    nvalid = (pends[-1:] // rb).astype(jnp.int32)
    return pstarts, block_e, nvalid, n_rows


def kernel(x, c, ada_w, ada_b, norm1_g, norm2_g, w_in, tshift_mu, rwkv_w0, rwkv_w_up, rwkv_a0, rwkv_a_up,
           rwkv_g_up, rwkv_k_k, rwkv_k_a, rwkv_r_k, rwkv_ln_g, rwkv_ln_b, gmlp_ln_g, gmlp_ln_b, gmlp_ws,
           gmlp_bs, w_out_a, w_out_b, w_out, router_w, router_bias, exp_w1, exp_w3, exp_w2, shared_w1,
           shared_w3, shared_w2, normf_g):
    bsz, seq, d = x.shape
    t = bsz * seq
    xf = x.reshape(t, d)
    assert ada_w.shape[0] == 1, "single-layer block only"
    l = 0
    mod3 = _mod(c, ada_w[l], ada_b[l]).reshape(bsz, 6, d)
    p_rwkv, p_uv, p_gate = _inproj(xf, mod3, norm1_g[l], w_in[l], seq)
    y_a = _rwkv(p_rwkv, bsz, seq, tshift_mu[l], rwkv_w0[l], rwkv_w_up[l], rwkv_a0[l], rwkv_a_up[l],
                rwkv_g_up[l], rwkv_k_k[l], rwkv_k_a[l], rwkv_r_k[l], rwkv_ln_g[l], rwkv_ln_b[l])
    x1, h2, h2p = _mix(p_uv, p_gate, y_a, xf, mod3, seq, gmlp_ln_g[l], gmlp_ln_b[l], gmlp_ws[l], gmlp_bs[l],
                  w_out_a[l], w_out_b[l], w_out[l], norm2_g[l])
    eidx, wts = _route(h2, router_w[l], router_bias[l])
    rank, counts = _rank(eidx)
    counts = counts[:, 0].astype(jnp.int32)
    pstarts, block_e, nvalid, n_rows = _dispatch_plan(counts, t)
    dest = _dest(eidx, rank, pstarts)
    x_sorted = _dispatch(h2p, dest, pstarts, counts, nvalid, n_rows)
    y_sorted = _expert(x_sorted, block_e, nvalid, exp_w1[l], exp_w3[l], exp_w2[l])
    out = _combine(y_sorted, dest, wts.T, h2, x1, mod3, seq, shared_w1[l], shared_w3[l], shared_w2[l],
                   normf_g)
    return out.reshape(bsz, seq, d)
```

```python
import functools

import jax
import jax.numpy as jnp
from jax import lax
from jax.experimental import pallas as pl
from jax.experimental.pallas import tpu as pltpu
from jax.experimental.pallas import tpu_sc as plsc

F32 = jnp.float32
BF16 = jnp.bfloat16

RWKV_HEAD = 64
RWKV_HEADS = 8
RWKV_WIDTH = RWKV_HEAD * RWKV_HEADS
DECAY_LORA = 64
AAA_LORA = 64
GATE_LORA = 128
RWKV_COLS = 3 * RWKV_WIDTH + DECAY_LORA + AAA_LORA + GATE_LORA
GMLP_WIDTH = 512
GMLP_BLOCK = 128
GMLP_GROUPS = 8
GMLP_GROUP_DIM = GMLP_WIDTH // GMLP_GROUPS
N_EXPERTS = 256
TOP_K = 8
N_EXPERT_GROUPS = 8
GROUP_SIZE = N_EXPERTS // N_EXPERT_GROUPS
TOPK_GROUPS = 4
ROUTED_SCALE = 2.5
RMS_EPS = 1e-6
LN_EPS = 1e-5
GN_EPS = 64e-5

VMEM_LIMIT_BYTES = 56 * 1024 * 1024

CHUNK = 64
RWKV_TILE = 256
ROW_BLOCK = 512
INPROJ_TILE = 256
MIX_TILE = 256
ROUTE_TILE = 512
RANK_TILE = 512
DISPATCH_TILE = 256
COMBINE_TILE = 256
LANES = 128
SC_WINDOW = 128


def _params(*sem):
    return pltpu.CompilerParams(dimension_semantics=sem, vmem_limit_bytes=VMEM_LIMIT_BYTES)


def _mm(a, b):
    return jnp.dot(a.astype(BF16), b.astype(BF16), preferred_element_type=F32)


def _mm_nt(a, b):
    return lax.dot_general(a.astype(BF16), b.astype(BF16), (((1,), (1,)), ((), ())),
                           preferred_element_type=F32)


def _mm_tn(a, b):
    return lax.dot_general(a.astype(BF16), b.astype(BF16), (((0,), (0,)), ((), ())),
                           preferred_element_type=F32)


def _split3(a):
    hi = a.astype(BF16)
    r1 = a - hi.astype(F32)
    mid = r1.astype(BF16)
    lo = (r1 - mid.astype(F32)).astype(BF16)
    return hi, mid, lo


def _silu(x):
    return x * jax.nn.sigmoid(x)


def _rms(x):
    return x * lax.rsqrt(jnp.mean(x * x, axis=-1, keepdims=True) + RMS_EPS)


def _pack_halves(x):
    n = x.shape[1] // 2
    bits = lax.bitcast_convert_type(x.astype(BF16).astype(F32), jnp.uint32)
    return bits[:, :n] | (bits[:, n:] >> 16)


def _unpack_halves(u):
    hi = lax.bitcast_convert_type(u & jnp.uint32(0xFFFF0000), F32)
    lo = lax.bitcast_convert_type(u << 16, F32)
    return hi, lo


def _row_copy(src, dst, sem, src_row, dst_row):
    return pltpu.make_async_copy(src.at[pl.ds(src_row, 1), :], dst.at[pl.ds(dst_row, 1), :], sem)


def _mod_kernel(c_ref, w_ref, b_ref, o_ref):
    s = _silu(c_ref[...])
    hi, mid, lo = _split3(s)
    whi, wmid, wlo = _split3(w_ref[...])
    dot = lambda p, q: jnp.dot(p, q, preferred_element_type=F32)
    acc = dot(hi, whi) + (dot(hi, wmid) + dot(mid, whi)) + (dot(hi, wlo) + dot(mid, wmid) + dot(lo, whi))
    o_ref[...] = acc + b_ref[...]


def _mod(c, ada_w, ada_b):
    bsz, d = c.shape
    n = ada_w.shape[1]
    tn = d
    return pl.pallas_call(
        _mod_kernel,
        grid=(n // tn,),
        in_specs=[pl.BlockSpec((bsz, d), lambda j: (0, 0)),
                  pl.BlockSpec((d, tn), lambda j: (0, j)),
                  pl.BlockSpec((1, tn), lambda j: (0, j))],
        out_specs=pl.BlockSpec((bsz, tn), lambda j: (0, j)),
        out_shape=jax.ShapeDtypeStruct((bsz, n), F32),
        compiler_params=_params("arbitrary"),
        name="mod",
    )(c, ada_w, ada_b.reshape(1, n))


def _inproj_kernel(x_ref, mod_ref, g_ref, wr_ref, wuv_ref, wg_ref, pr_ref, puv_ref, pg_ref):
    sh = mod_ref[0, 0:1, :]
    sc = mod_ref[0, 1:2, :]
    h = (_rms(x_ref[...]) * g_ref[...]) * (1.0 + sc) + sh
    hb = h.astype(BF16)
    pr_ref[...] = jnp.dot(hb, wr_ref[...], preferred_element_type=F32).astype(pr_ref.dtype)
    puv_ref[...] = jnp.dot(hb, wuv_ref[...], preferred_element_type=F32).astype(puv_ref.dtype)
    pg_ref[...] = jnp.dot(hb, wg_ref[...], preferred_element_type=F32).astype(pg_ref.dtype)


def _inproj(xf, mod3, norm1_g, w_in, seq):
    t, d = xf.shape
    tm = min(INPROJ_TILE, seq)
    per_seq = seq // tm
    wr = w_in[:, :RWKV_COLS].astype(BF16)
    wuv = w_in[:, RWKV_COLS:RWKV_COLS + 2 * GMLP_WIDTH].astype(BF16)
    wg = w_in[:, RWKV_COLS + 2 * GMLP_WIDTH:].astype(BF16)
    full = lambda a: pl.BlockSpec(a.shape, lambda i: (0,) * a.ndim)
    g = norm1_g.reshape(1, d)
    return pl.pallas_call(
        _inproj_kernel,
        grid=(t // tm,),
        in_specs=[pl.BlockSpec((tm, d), lambda i: (i, 0)),
                  pl.BlockSpec((1,) + mod3.shape[1:], lambda i: (i // per_seq, 0, 0)),
                  full(g), full(wr), full(wuv), full(wg)],
        out_specs=[pl.BlockSpec((tm, wr.shape[1]), lambda i: (i, 0)),
                   pl.BlockSpec((tm, wuv.shape[1]), lambda i: (i, 0)),
                   pl.BlockSpec((tm, wg.shape[1]), lambda i: (i, 0))],
        out_shape=[jax.ShapeDtypeStruct((t, wr.shape[1]), BF16),
                   jax.ShapeDtypeStruct((t, wuv.shape[1]), BF16),
                   jax.ShapeDtypeStruct((t, wg.shape[1]), BF16)],
        compiler_params=_params("arbitrary"),
        name="inproj",
    )(xf, mod3, g, wr, wuv, wg)


def _unit_lower_inverses(ns, row, col):
    c = ns[0].shape[0]
    eye = jnp.where(row == col, 1.0, 0.0).astype(F32)
    blk8 = (row // 8) == (col // 8)
    n8 = [jnp.where(blk8, n, 0.0) for n in ns]
    n8_2 = [_mm(a, a) for a in n8]
    n8_4 = [_mm(a, a) for a in n8_2]
    inv = [eye + a for a in n8]
    inv = [i + _mm(i, b) for i, b in zip(inv, n8_2)]
    inv = [i + _mm(i, b) for i, b in zip(inv, n8_4)]
    s = 8
    while s < c:
        sel = ((row // (2 * s)) == (col // (2 * s))) & ((row // s) != (col // s))
        low = [jnp.where(sel, n, 0.0) for n in ns]
        t1 = [_mm(l, i) for l, i in zip(low, inv)]
        inv = [i + _mm(i, t) for i, t in zip(inv, t1)]
        s *= 2
    return inv


def _rwkv_kernel(p_ref, mu_ref, w0_ref, wup_ref, a0_ref, aup_ref, gup_ref, kk_ref, ka_ref, rk_ref,
                 lng_ref, lnb_ref, bd_ref, y_ref, state_ref, carry_ref, ycat_ref, *, c):
    tm = p_ref.shape[0]
    nq = tm // c
    hd = RWKV_HEAD
    nh = RWKV_HEADS

    @pl.when(pl.program_id(1) == 0)
    def _():
        state_ref[...] = jnp.zeros_like(state_ref)
        carry_ref[...] = jnp.zeros_like(carry_ref)

    p = p_ref[...].astype(F32)
    rowc = lax.broadcasted_iota(jnp.int32, (tm, 1), 0)
    prev = jnp.where(rowc == 0, carry_ref[...], pltpu.roll(p, 1, axis=0))
    carry_ref[...] = p[tm - 1:tm, :]
    ps = p + (prev - p) * mu_ref[...]

    o = RWKV_WIDTH
    r = ps[:, 0:o]
    k = ps[:, o:2 * o]
    v = ps[:, 2 * o:3 * o]
    xw = ps[:, 3 * o:3 * o + DECAY_LORA]
    xa = ps[:, 3 * o + DECAY_LORA:3 * o + DECAY_LORA + AAA_LORA]
    xg = ps[:, 3 * o + DECAY_LORA + AAA_LORA:]

    z = -(w0_ref[...] + _mm(jnp.tanh(xw), wup_ref[...]))
    softplus = jnp.maximum(z, 0.0) + jnp.log1p(jnp.exp(-jnp.abs(z)))
    wlog = -softplus - 0.5
    ld = -jnp.exp(wlog)
    a = jax.nn.sigmoid(a0_ref[...] + _mm(xa, aup_ref[...]))
    g = _mm(jax.nn.sigmoid(xg), gup_ref[...])

    def head_sum(x):
        hi = x.astype(BF16)
        lo = (x - hi.astype(F32)).astype(BF16)
        bd = bd_ref[...]
        return jnp.dot(hi, bd, preferred_element_type=F32) + jnp.dot(lo, bd, preferred_element_type=F32)

    kkf = k * kk_ref[...]
    kkn = kkf * (1.0 / jnp.maximum(jnp.sqrt(head_sum(kkf * kkf)), 1e-12))
    k2 = k * (1.0 + (a - 1.0) * ka_ref[...])
    avec = -kkn
    bvec = kkn * a

    rowt = lax.broadcasted_iota(jnp.int32, (tm, tm), 0)
    colt = lax.broadcasted_iota(jnp.int32, (tm, tm), 1)
    tri = jnp.where((rowt >= colt) & ((rowt // c) == (colt // c)), 1.0, 0.0).astype(BF16)
    hi, mid, lo = _split3(ld)
    dot = lambda q: jnp.dot(tri, q, preferred_element_type=F32)
    cl = dot(hi) + dot(mid) + dot(lo)
    cl_end = jnp.concatenate(
        [jnp.broadcast_to(cl[(q + 1) * c - 1:(q + 1) * c, :], (c, o)) for q in range(nq)], axis=0)
    g_inv = jnp.exp(-cl)
    g_tail = jnp.exp(cl_end - cl)
    g_end = jnp.exp(cl_end)
    rt = (r * jnp.exp(cl)).astype(BF16)
    at = (avec * jnp.exp(cl - ld)).astype(BF16)
    kt = (k2 * g_inv).astype(BF16)
    bt = (bvec * g_inv).astype(BF16)
    kend = (k2 * g_tail).astype(BF16)
    bend = (bvec * g_tail).astype(BF16)
    vb = v.astype(BF16)

    row = lax.broadcasted_iota(jnp.int32, (c, c), 0)
    col = lax.broadcasted_iota(jnp.int32, (c, c), 1)
    strict = row > col
    incl = row >= col
    items = [(q, h) for q in range(nq) for h in range(nh)]
    blk = lambda x, q, h: x[q * c:(q + 1) * c, h * hd:(h + 1) * hd]

    lhs = [jnp.concatenate([blk(at, q, h), blk(rt, q, h)], axis=0) for q, h in items]
    mat_b = [_mm_nt(l, blk(bt, q, h)) for l, (q, h) in zip(lhs, items)]
    mat_k = [_mm_nt(l, blk(kt, q, h)) for l, (q, h) in zip(lhs, items)]
    a_ab = [jnp.where(strict, m[:c], 0.0) for m in mat_b]
    a_rb = [jnp.where(incl, m[c:], 0.0).astype(BF16) for m in mat_b]
    a_ak = [jnp.where(strict, m[:c], 0.0) for m in mat_k]
    a_rk = [jnp.where(incl, m[c:], 0.0) for m in mat_k]
    tinv = _unit_lower_inverses(a_ab, row, col)
    akv = [_mm(m, blk(vb, q, h)) for m, (q, h) in zip(a_ak, items)]
    y_loc = [_mm(m, blk(vb, q, h)) for m, (q, h) in zip(a_rk, items)]
    s_loc = [_mm_tn(blk(vb, q, h), blk(kend, q, h)) for q, h in items]
    wmat = [_mm(t, blk(at, q, h)).astype(BF16) for t, (q, h) in zip(tinv, items)]
    ut = [_mm(t, m) for t, m in zip(tinv, akv)]

    state = [state_ref[h] for h in range(nh)]
    for q in range(nq):
        base = q * nh
        sb = [s.astype(BF16) for s in state]
        u = [_mm_nt(wmat[base + h], sb[h]) + ut[base + h] for h in range(nh)]
        yh = [_mm_nt(blk(rt, q, h), sb[h]) + y_loc[base + h] for h in range(nh)]
        ub = [x.astype(BF16) for x in u]
        yh = [y0 + _mm(a_rb[base + h], ub[h]) for h, y0 in enumerate(yh)]
        state = [state[h] * g_end[q * c:q * c + 1, h * hd:(h + 1) * hd]
                 + _mm_tn(ub[h], blk(bend, q, h)) + s_loc[base + h] for h in range(nh)]
        for h in range(nh):
            ycat_ref[q * c:(q + 1) * c, h * hd:(h + 1) * hd] = yh[h]
    for h in range(nh):
        state_ref[h] = state[h]

    y = ycat_ref[...]
    dev = y - head_sum(y) * (1.0 / hd)
    rstd = lax.rsqrt(head_sum(dev * dev) * (1.0 / hd) + GN_EPS)
    bonus = head_sum(r * k2 * rk_ref[...])
    yn = dev * rstd * lng_ref[...] + lnb_ref[...]
    y_ref[...] = (yn + bonus * v) * g


def _rwkv(p_rwkv, bsz, seq, tshift_mu, w0, w_up, a0, a_up, g_up, k_k, k_a, r_k, ln_g, ln_b):
    t = p_rwkv.shape[0]
    c = min(CHUNK, seq)
    tm = min(RWKV_TILE, seq)
    nstep = seq // tm
    row = lambda a: a.reshape(1, -1)
    head_of = jnp.arange(RWKV_WIDTH, dtype=jnp.int32) // RWKV_HEAD
    block_diag = (head_of[:, None] == head_of[None, :]).astype(BF16)
    args = (row(tshift_mu), row(w0), w_up, row(a0), a_up, g_up, row(k_k), row(k_a), row(r_k),
            row(ln_g), row(ln_b), block_diag)
    full = lambda a: pl.BlockSpec(a.shape, lambda b, j: (0,) * a.ndim)
    return pl.pallas_call(
        functools.partial(_rwkv_kernel, c=c),
        grid=(bsz, nstep),
        in_specs=[pl.BlockSpec((tm, RWKV_COLS), lambda b, j: (b * nstep + j, 0))] + [full(a) for a in args],
        out_specs=pl.BlockSpec((tm, RWKV_WIDTH), lambda b, j: (b * nstep + j, 0)),
        out_shape=jax.ShapeDtypeStruct((t, RWKV_WIDTH), F32),
        scratch_shapes=[pltpu.VMEM((RWKV_HEADS, RWKV_HEAD, RWKV_HEAD), F32),
                        pltpu.VMEM((1, RWKV_COLS), F32),
                        pltpu.VMEM((tm, RWKV_WIDTH), F32)],
        compiler_params=_params("arbitrary", "arbitrary"),
        name="rwkv",
    )(p_rwkv, *args)


def _mix_kernel(puv_ref, pg_ref, ya_ref, x_ref, mod_ref, lng_ref, lnb_ref, ws_ref, bsm_ref,
                woa_ref, wob_ref, wo_ref, n2g_ref, x1_ref, h2_ref, h2p_ref, yb_ref):
    tm = x_ref.shape[0]
    gw = GMLP_WIDTH
    puv = puv_ref[...].astype(F32)
    u = jax.nn.gelu(puv[:, :gw])
    vg = jax.nn.gelu(puv[:, gw:])
    mu = jnp.mean(vg, axis=-1, keepdims=True)
    var = jnp.mean(jnp.square(vg - mu), axis=-1, keepdims=True)
    vln = (vg - mu) * lax.rsqrt(var + LN_EPS) * lng_ref[...] + lnb_ref[...]

    row = lax.broadcasted_iota(jnp.int32, (GMLP_BLOCK, GMLP_BLOCK), 0)
    col = lax.broadcasted_iota(jnp.int32, (GMLP_BLOCK, GMLP_BLOCK), 1)
    gd = GMLP_GROUP_DIM
    for gi in range(GMLP_GROUPS):
        wsm = jnp.where(row >= col, ws_ref[gi], 0.0).astype(BF16)
        for nb in range(tm // GMLP_BLOCK):
            rs = slice(nb * GMLP_BLOCK, (nb + 1) * GMLP_BLOCK)
            cs = slice(gi * gd, (gi + 1) * gd)
            sv = jnp.dot(wsm, vln[rs, cs].astype(BF16), preferred_element_type=F32) + bsm_ref[:, cs]
            yb_ref[rs, cs] = u[rs, cs] * sv

    pg = pg_ref[...].astype(F32)
    d = x_ref.shape[1]
    gate_a = jax.nn.sigmoid(pg[:, :d])
    gate_b = jax.nn.sigmoid(pg[:, d:])
    merged = gate_a * _mm(ya_ref[...], woa_ref[...]) + gate_b * _mm(yb_ref[...], wob_ref[...])
    g1 = mod_ref[0, 2:3, :]
    x1 = x_ref[...] + g1 * _mm(merged, wo_ref[...])
    x1_ref[...] = x1
    sh2 = mod_ref[0, 3:4, :]
    sc2 = mod_ref[0, 4:5, :]
    h2 = (_rms(x1) * n2g_ref[...]) * (1.0 + sc2) + sh2
    h2_ref[...] = h2
    h2p_ref[...] = _pack_halves(h2)


def _mix(p_uv, p_gate, y_a, xf, mod3, seq, ln_g, ln_b, ws, bs, w_out_a, w_out_b, w_out, norm2_g):
    t, d = xf.shape
    tm = min(MIX_TILE, seq)
    per_seq = seq // tm
    row = lambda a: a.reshape(1, -1)
    bsm = jnp.repeat(bs.T, GMLP_GROUP_DIM, axis=1)
    consts = (row(ln_g), row(ln_b), ws, bsm, w_out_a.astype(BF16), w_out_b.astype(BF16),
              w_out.astype(BF16), row(norm2_g))
    full = lambda a: pl.BlockSpec(a.shape, lambda i: (0,) * a.ndim)
    tile = lambda a: pl.BlockSpec((tm, a.shape[1]), lambda i: (i, 0))
    return pl.pallas_call(
        _mix_kernel,
        grid=(t // tm,),
        in_specs=[tile(p_uv), tile(p_gate), tile(y_a), tile(xf),
                  pl.BlockSpec((1,) + mod3.shape[1:], lambda i: (i // per_seq, 0, 0))]
                 + [full(a) for a in consts],
        out_specs=[pl.BlockSpec((tm, d), lambda i: (i, 0)), pl.BlockSpec((tm, d), lambda i: (i, 0)),
                   pl.BlockSpec((tm, d // 2), lambda i: (i, 0))],
        out_shape=[jax.ShapeDtypeStruct((t, d), F32), jax.ShapeDtypeStruct((t, d), F32),
                   jax.ShapeDtypeStruct((t, d // 2), jnp.uint32)],
        scratch_shapes=[pltpu.VMEM((tm, GMLP_WIDTH), F32)],
        compiler_params=_params("arbitrary"),
        name="mix",
    )(p_uv, p_gate, y_a, xf, mod3, *consts)


def _first_argmax(vals, idx, big):
    m = jnp.max(vals, axis=0, keepdims=True)
    first = jnp.min(jnp.where(vals == m, idx, big), axis=0, keepdims=True)
    return m, first


def _route_kernel(h_ref, rwt_ref, bias_ref, eidx_ref, wts_ref):
    tm = h_ref.shape[0]
    ne = N_EXPERTS
    hhi, hmid, _ = _split3(h_ref[...])
    whi, wmid, _ = _split3(rwt_ref[...])
    nt = lambda p, q: lax.dot_general(p, q, (((1,), (1,)), ((), ())), preferred_element_type=F32)
    logits = nt(whi, hhi) + (nt(whi, hmid) + nt(wmid, hhi))
    scores = jax.nn.sigmoid(logits)
    sel = scores + bias_ref[...]
    neg = -jnp.inf

    gs = GROUP_SIZE
    gidx = lax.broadcasted_iota(jnp.int32, (gs, tm), 0)
    grp_rows = []
    for gi in range(N_EXPERT_GROUPS):
        blk = sel[gi * gs:(gi + 1) * gs, :]
        m1, i1 = _first_argmax(blk, gidx, gs)
        m2 = jnp.max(jnp.where(gidx == i1, neg, blk), axis=0, keepdims=True)
        grp_rows.append(m1 + m2)
    grp = jnp.concatenate(grp_rows, axis=0)

    ng = N_EXPERT_GROUPS
    giota = lax.broadcasted_iota(jnp.int32, (ng, tm), 0)
    gsel = jnp.zeros((ng, tm), jnp.bool_)
    work = grp
    for _ in range(TOPK_GROUPS):
        _, gi1 = _first_argmax(work, giota, ng)
        hit = giota == gi1
        gsel = gsel | hit
        work = jnp.where(hit, neg, work)
    gself = jnp.where(gsel, 1.0, 0.0)
    emask = jnp.concatenate(
        [jnp.broadcast_to(gself[gi:gi + 1, :], (gs, tm)) for gi in range(ng)], axis=0) > 0.5

    eiota = lax.broadcasted_iota(jnp.int32, (ne, tm), 0)
    work = jnp.where(emask, sel, neg)
    idx_rows, w_rows = [], []
    for _ in range(TOP_K):
        _, e1 = _first_argmax(work, eiota, ne)
        hit = eiota == e1
        idx_rows.append(e1)
        w_rows.append(jnp.sum(jnp.where(hit, scores, 0.0), axis=0, keepdims=True))
        work = jnp.where(hit, neg, work)
    eidx_ref[...] = jnp.concatenate(idx_rows, axis=0)
    w = jnp.concatenate(w_rows, axis=0)
    wts_ref[...] = w / jnp.sum(w, axis=0, keepdims=True) * ROUTED_SCALE


def _route(h2, router_w, router_bias):
    t, d = h2.shape
    tm = min(ROUTE_TILE, t)
    rwt = router_w.T
    bias = router_bias.reshape(N_EXPERTS, 1)
    return pl.pallas_call(
        _route_kernel,
        grid=(t // tm,),
        in_specs=[pl.BlockSpec((tm, d), lambda i: (i, 0)),
                  pl.BlockSpec(rwt.shape, lambda i: (0, 0)),
                  pl.BlockSpec(bias.shape, lambda i: (0, 0))],
        out_specs=[pl.BlockSpec((TOP_K, tm), lambda i: (0, i)), pl.BlockSpec((TOP_K, tm), lambda i: (0, i))],
        out_shape=[jax.ShapeDtypeStruct((TOP_K, t), jnp.int32), jax.ShapeDtypeStruct((TOP_K, t), F32)],
        compiler_params=_params("arbitrary"),
        name="route",
    )(h2, rwt, bias)


def _rank_kernel(e_ref, upper_ref, rank_ref, count_ref):
    tm = e_ref.shape[1]
    ne = N_EXPERTS

    @pl.when(pl.program_id(0) == 0)
    def _():
        count_ref[...] = jnp.zeros_like(count_ref)

    e = e_ref[...]
    eiota = lax.broadcasted_iota(jnp.int32, (ne, tm), 0)
    onehot = jnp.zeros((ne, tm), F32)
    for kk in range(TOP_K):
        onehot = onehot + jnp.where(eiota == e[kk:kk + 1, :], 1.0, 0.0)
    before = jnp.dot(onehot.astype(BF16), upper_ref[...], preferred_element_type=F32)
    base = count_ref[:, 0:1]
    tot = before + base
    rows = [jnp.sum(jnp.where(eiota == e[kk:kk + 1, :], tot, 0.0), axis=0, keepdims=True)
            for kk in range(TOP_K)]
    rank_ref[...] = jnp.concatenate(rows, axis=0).astype(jnp.int32)
    count_ref[...] = count_ref[...] + jnp.sum(onehot, axis=1, keepdims=True)


def _rank(eidx):
    k, t = eidx.shape
    tm = min(RANK_TILE, t)
    upper = jnp.triu(jnp.ones((tm, tm), F32), 1).astype(BF16)
    return pl.pallas_call(
        _rank_kernel,
        grid=(t // tm,),
        in_specs=[pl.BlockSpec((k, tm), lambda i: (0, i)), pl.BlockSpec((tm, tm), lambda i: (0, 0))],
        out_specs=[pl.BlockSpec((k, tm), lambda i: (0, i)), pl.BlockSpec((N_EXPERTS, 128), lambda i: (0, 0))],
        out_shape=[jax.ShapeDtypeStruct((k, t), jnp.int32), jax.ShapeDtypeStruct((N_EXPERTS, 128), F32)],
        compiler_params=_params("arbitrary"),
        name="rank",
    )(eidx, upper)


def _dest_kernel(e_ref, rank_ref, pst_ref, dest_ref):
    tm = e_ref.shape[1]
    e = e_ref[...]
    eiota = lax.broadcasted_iota(jnp.int32, (N_EXPERTS, tm), 0)
    pst = pst_ref[...]
    rows = [jnp.sum(jnp.where(eiota == e[kk:kk + 1, :], pst, 0.0), axis=0, keepdims=True)
            for kk in range(TOP_K)]
    dest_ref[...] = jnp.concatenate(rows, axis=0).astype(jnp.int32) + rank_ref[...]


def _dest(eidx, rank, pstarts):
    k, t = eidx.shape
    tm = min(RANK_TILE, t)
    pst = pstarts.astype(F32).reshape(N_EXPERTS, 1)
    return pl.pallas_call(
        _dest_kernel,
        grid=(t // tm,),
        in_specs=[pl.BlockSpec((k, tm), lambda i: (0, i)), pl.BlockSpec((k, tm), lambda i: (0, i)),
                  pl.BlockSpec((N_EXPERTS, 1), lambda i: (0, 0))],
        out_specs=pl.BlockSpec((k, tm), lambda i: (0, i)),
        out_shape=jax.ShapeDtypeStruct((k, t), jnp.int32),
        compiler_params=_params("arbitrary"),
        name="dest",
    )(eidx, rank, pst)


def _dispatch_kernel(pst_ref, cnt_ref, nv_ref, dst_ref, h_ref, xs_hbm, zbuf, sem, zsem):
    i = pl.program_id(0)
    tm = h_ref.shape[0]
    rb = zbuf.shape[0]
    nblk = xs_hbm.shape[0] // rb

    @pl.when(i == 0)
    def _():
        zbuf[...] = jnp.zeros_like(zbuf)

        def pads(e, wait):
            off = pst_ref[e] + cnt_ref[e]
            n = (rb - cnt_ref[e] % rb) % rb
            head = jnp.minimum((8 - off % 8) % 8, n)

            def one(r, carry):
                cp = _row_copy(zbuf, xs_hbm, zsem, 0, off + r)
                cp.wait() if wait else cp.start()
                return carry
            lax.fori_loop(0, head, one, 0)

            def group(g, carry):
                start = pl.multiple_of(off + head + g * 8, 8)
                cp = pltpu.make_async_copy(zbuf.at[pl.ds(0, 8), :], xs_hbm.at[pl.ds(start, 8), :], zsem)
                cp.wait() if wait else cp.start()
                return carry
            lax.fori_loop(0, (n - head) // 8, group, 0)

        def tail(b, wait):
            cp = pltpu.make_async_copy(zbuf, xs_hbm.at[pl.ds(b * rb, rb), :], zsem)
            cp.wait() if wait else cp.start()

        for wait in (False, True):
            lax.fori_loop(0, N_EXPERTS, lambda e, carry: (pads(e, wait), carry)[1], 0)
            lax.fori_loop(nv_ref[0], nblk, lambda b, carry: (tail(b, wait), carry)[1], 0)

    def body(j, carry):
        for kk in range(TOP_K):
            _row_copy(h_ref, xs_hbm, sem, j, dst_ref[0, 0, kk * tm + j]).start(priority=kk % 2)
        return carry
    lax.fori_loop(0, tm, body, 0)
    for kk in range(TOP_K):
        pltpu.make_async_copy(h_ref, h_ref, sem).wait()


def _dispatch(h2p, dest_tk, pstarts, counts, nvalid, n_rows):
    t, d = h2p.shape
    tm = min(DISPATCH_TILE, t)
    nt = t // tm
    dst3 = dest_tk.reshape(TOP_K, nt, tm).transpose(1, 0, 2).reshape(nt, 1, TOP_K * tm)
    grid_spec = pltpu.PrefetchScalarGridSpec(
        num_scalar_prefetch=3,
        grid=(nt,),
        in_specs=[pl.BlockSpec((1, 1, TOP_K * tm), lambda i, *_: (i, 0, 0), memory_space=pltpu.SMEM),
                  pl.BlockSpec((tm, d), lambda i, *_: (i, 0))],
        out_specs=pl.BlockSpec(memory_space=pl.ANY),
        scratch_shapes=[pltpu.VMEM((ROW_BLOCK, d), h2p.dtype), pltpu.SemaphoreType.DMA(()),
                        pltpu.SemaphoreType.DMA(())],
    )
    return pl.pallas_call(
        _dispatch_kernel,
        grid_spec=grid_spec,
        out_shape=jax.ShapeDtypeStruct((n_rows, d), h2p.dtype),
        compiler_params=_params("arbitrary"),
        name="dispatch",
    )(pstarts, counts, nvalid, dst3, h2p)


def _expert_kernel(be_ref, nv_ref, x_ref, w1_ref, w3_ref, w2_ref, y_ref, w1b, w3b, w2b):
    i = pl.program_id(0)
    half = x_ref.shape[1]

    @pl.when((i == 0) | (be_ref[i] != be_ref[jnp.maximum(i - 1, 0)]))
    def _():
        w1b[...] = w1_ref[0].astype(BF16)
        w3b[...] = w3_ref[0].astype(BF16)
        w2b[...] = w2_ref[0].astype(BF16)

    @pl.when(i < nv_ref[0])
    def _():
        hi, lo = _unpack_halves(x_ref[...])
        xa = hi.astype(BF16)
        xb = lo.astype(BF16)
        dot = lambda p, q: jnp.dot(p, q, preferred_element_type=F32)
        h1 = dot(xa, w1b[:half, :]) + dot(xb, w1b[half:, :])
        h3 = dot(xa, w3b[:half, :]) + dot(xb, w3b[half:, :])
        hid = _silu(h1) * h3
        packed = _pack_halves(dot(hid.astype(BF16), w2b[...]))
        for j in range(y_ref.shape[0]):
            y_ref[j] = packed[:, j * LANES:(j + 1) * LANES]

    @pl.when(i >= nv_ref[0])
    def _():
        y_ref[...] = jnp.zeros_like(y_ref)


def _expert(x_sorted, block_e, nvalid, w1, w3, w2):
    n_rows, half = x_sorted.shape
    rb = ROW_BLOCK
    nblk = n_rows // rb
    d, f = w1.shape[1], w1.shape[2]
    grid_spec = pltpu.PrefetchScalarGridSpec(
        num_scalar_prefetch=2,
        grid=(nblk,),
        in_specs=[
            pl.BlockSpec((rb, half), lambda i, be, nv: (jnp.minimum(i, nv[0] - 1), 0)),
            pl.BlockSpec((1, d, f), lambda i, be, nv: (be[i], 0, 0)),
            pl.BlockSpec((1, d, f), lambda i, be, nv: (be[i], 0, 0)),
            pl.BlockSpec((1, f, d), lambda i, be, nv: (be[i], 0, 0)),
        ],
        out_specs=pl.BlockSpec((half // LANES, rb, LANES), lambda i, be, nv: (0, i, 0)),
        scratch_shapes=[pltpu.VMEM((d, f), BF16), pltpu.VMEM((d, f), BF16), pltpu.VMEM((f, d), BF16)],
    )
    return pl.pallas_call(
        _expert_kernel,
        grid_spec=grid_spec,
        out_shape=jax.ShapeDtypeStruct((half // LANES, n_rows, LANES), jnp.uint32),
        compiler_params=_params("arbitrary"),
        name="expert",
    )(block_e, nvalid, x_sorted, w1, w3, w2)


def _sc_gather(table, idx):
    n = idx.shape[0]
    mesh = plsc.VectorSubcoreMesh(core_axis_name="core", subcore_axis_name="subcore")

    @functools.partial(pl.kernel, out_type=jax.ShapeDtypeStruct((n, table.shape[1]), table.dtype), mesh=mesh)
    def gather_kernel(table_hbm, idx_hbm, out_hbm):
        def body(idx_vmem, out_vmem):
            pltpu.sync_copy(table_hbm.at[idx_vmem.at[0]], out_vmem)

        pltpu.emit_pipeline(
            body,
            grid=(n // SC_WINDOW,),
            in_specs=[pl.BlockSpec((1, SC_WINDOW), index_map=lambda i: (0, i))],
            out_specs=[pl.BlockSpec((SC_WINDOW, table.shape[1]), index_map=lambda i: (i, 0))],
            core_axis_name=("core", "subcore"),
            dimension_semantics=(pltpu.PARALLEL,),
        )(idx_hbm, out_hbm)

    return gather_kernel(table, idx.reshape(1, n))


def _gather_expert_rows(y_sorted, dest_tk):
    nchunk, n_rows, lanes = y_sorted.shape
    k, t = dest_tk.shape
    idx = dest_tk[None, :, :] + (jnp.arange(nchunk, dtype=jnp.int32) * n_rows)[:, None, None]
    rows = _sc_gather(y_sorted.reshape(nchunk * n_rows, lanes), idx.reshape(-1))
    return rows.reshape(nchunk, k, t, lanes)


def _combine_kernel(yg_ref, wts_ref, h_ref, x1_ref, mod_ref, sw1_ref, sw3_ref, sw2_ref, nfg_ref, o_ref):
    nchunk = yg_ref.shape[0]
    w = wts_ref[...]
    acc_hi = [None] * nchunk
    acc_lo = [None] * nchunk
    for kk in range(TOP_K):
        wk = w[:, kk:kk + 1]
        for j in range(nchunk):
            hi, lo = _unpack_halves(yg_ref[j, kk])
            acc_hi[j] = hi * wk if kk == 0 else acc_hi[j] + hi * wk
            acc_lo[j] = lo * wk if kk == 0 else acc_lo[j] + lo * wk
    routed = jnp.concatenate(acc_hi + acc_lo, axis=1)
    hb = h_ref[...].astype(BF16)
    hid = _silu(jnp.dot(hb, sw1_ref[...], preferred_element_type=F32)) * jnp.dot(
        hb, sw3_ref[...], preferred_element_type=F32)
    shared = jnp.dot(hid.astype(BF16), sw2_ref[...], preferred_element_type=F32)
    g2 = mod_ref[0, 5:6, :]
    x2 = x1_ref[...] + g2 * (routed + shared)
    o_ref[...] = _rms(x2) * nfg_ref[...]


def _combine(y_gathered, wts_tk, h2, x1, mod3, seq, sw1, sw3, sw2, normf_g):
    t, d = h2.shape
    tm = min(COMBINE_TILE, seq)
    per_seq = seq // tm
    nchunk, k, _, lanes = y_gathered.shape
    consts = (sw1.astype(BF16), sw3.astype(BF16), sw2.astype(BF16), normf_g.reshape(1, d))
    full = lambda a: pl.BlockSpec(a.shape, lambda i: (0,) * a.ndim)
    tile = lambda a: pl.BlockSpec((tm, a.shape[1]), lambda i: (i, 0))
    return pl.pallas_call(
        _combine_kernel,
        grid=(t // tm,),
        in_specs=[pl.BlockSpec((nchunk, k, tm, lanes), lambda i: (0, 0, i, 0)),
                  tile(wts_tk), tile(h2), tile(x1),
                  pl.BlockSpec((1,) + mod3.shape[1:], lambda i: (i // per_seq, 0, 0))]
                 + [full(a) for a in consts],
        out_specs=pl.BlockSpec((tm, d), lambda i: (i, 0)),
        out_shape=jax.ShapeDtypeStruct((t, d), F32),
        compiler_params=_params("arbitrary"),
        name="combine",
    )(y_gathered, wts_tk, h2, x1, mod3, *consts)


def _dispatch_plan(counts, t):
    rb = ROW_BLOCK
    padded = (counts + rb - 1) // rb * rb
    pends = jnp.cumsum(padded)
    pstarts = pends - padded
    n_rows = t * TOP_K + N_EXPERTS * rb
    block_start = jnp.arange(n_rows // rb, dtype=jnp.int32) * rb
    block_e = jnp.sum((pends[None, :] <= block_start[:, None]).astype(jnp.int32), axis=1)
    block_e = jnp.minimum(block_e, N_EXPERTS - 1)
    nvalid = (pends[-1:] // rb).astype(jnp.int32)
    return pstarts, block_e, nvalid, n_rows


def kernel(x, c, ada_w, ada_b, norm1_g, norm2_g, w_in, tshift_mu, rwkv_w0, rwkv_w_up, rwkv_a0, rwkv_a_up,
           rwkv_g_up, rwkv_k_k, rwkv_k_a, rwkv_r_k, rwkv_ln_g, rwkv_ln_b, gmlp_ln_g, gmlp_ln_b, gmlp_ws,
           gmlp_bs, w_out_a, w_out_b, w_out, router_w, router_bias, exp_w1, exp_w3, exp_w2, shared_w1,
           shared_w3, shared_w2, normf_g):
    bsz, seq, d = x.shape
    t = bsz * seq
    xf = x.reshape(t, d)
    assert ada_w.shape[0] == 1, "single-layer block only"
    l = 0
    mod3 = _mod(c, ada_w[l], ada_b[l]).reshape(bsz, 6, d)
    p_rwkv, p_uv, p_gate = _inproj(xf, mod3, norm1_g[l], w_in[l], seq)
    y_a = _rwkv(p_rwkv, bsz, seq, tshift_mu[l], rwkv_w0[l], rwkv_w_up[l], rwkv_a0[l], rwkv_a_up[l],
                rwkv_g_up[l], rwkv_k_k[l], rwkv_k_a[l], rwkv_r_k[l], rwkv_ln_g[l], rwkv_ln_b[l])
    x1, h2, h2p = _mix(p_uv, p_gate, y_a, xf, mod3, seq, gmlp_ln_g[l], gmlp_ln_b[l], gmlp_ws[l], gmlp_bs[l],
                  w_out_a[l], w_out_b[l], w_out[l], norm2_g[l])
    eidx, wts = _route(h2, router_w[l], router_bias[l])
    rank, counts = _rank(eidx)
    counts = counts[:, 0].astype(jnp.int32)
    pstarts, block_e, nvalid, n_rows = _dispatch_plan(counts, t)
    dest = _dest(eidx, rank, pstarts)
    x_sorted = _dispatch(h2p, dest, pstarts, counts, nvalid, n_rows)
    y_sorted = _expert(x_sorted, block_e, nvalid, exp_w1[l], exp_w3[l], exp_w2[l])
    y_gathered = _gather_expert_rows(y_sorted, dest)
    out = _combine(y_gathered, wts.T, h2, x1, mod3, seq, shared_w1[l], shared_w3[l], shared_w2[l], normf_g)
    return out.reshape(bsz, seq, d)
```

```python
import functools

import jax
import jax.numpy as jnp
from jax import lax
from jax.experimental import pallas as pl
from jax.experimental.pallas import tpu as pltpu
from jax.experimental.pallas import tpu_sc as plsc

F32 = jnp.float32
BF16 = jnp.bfloat16

RWKV_HEAD = 64
RWKV_HEADS = 8
RWKV_WIDTH = RWKV_HEAD * RWKV_HEADS
DECAY_LORA = 64
AAA_LORA = 64
GATE_LORA = 128
RWKV_COLS = 3 * RWKV_WIDTH + DECAY_LORA + AAA_LORA + GATE_LORA
GMLP_WIDTH = 512
GMLP_BLOCK = 128
GMLP_GROUPS = 8
GMLP_GROUP_DIM = GMLP_WIDTH // GMLP_GROUPS
N_EXPERTS = 256
TOP_K = 8
N_EXPERT_GROUPS = 8
GROUP_SIZE = N_EXPERTS // N_EXPERT_GROUPS
TOPK_GROUPS = 4
ROUTED_SCALE = 2.5
RMS_EPS = 1e-6
LN_EPS = 1e-5
GN_EPS = 64e-5

VMEM_LIMIT_BYTES = 56 * 1024 * 1024

CHUNK = 64
RWKV_TILE = 256
ROW_BLOCK = 512
INPROJ_TILE = 256
MIX_TILE = 256
ROUTE_TILE = 512
RANK_TILE = 512
COMBINE_TILE = 256
LANES = 128
SC_WINDOW = 128


def _params(*sem):
    return pltpu.CompilerParams(dimension_semantics=sem, vmem_limit_bytes=VMEM_LIMIT_BYTES)


def _mm(a, b):
    return jnp.dot(a.astype(BF16), b.astype(BF16), preferred_element_type=F32)


def _mm_nt(a, b):
    return lax.dot_general(a.astype(BF16), b.astype(BF16), (((1,), (1,)), ((), ())),
                           preferred_element_type=F32)


def _mm_tn(a, b):
    return lax.dot_general(a.astype(BF16), b.astype(BF16), (((0,), (0,)), ((), ())),
                           preferred_element_type=F32)


def _split3(a):
    hi = a.astype(BF16)
    r1 = a - hi.astype(F32)
    mid = r1.astype(BF16)
    lo = (r1 - mid.astype(F32)).astype(BF16)
    return hi, mid, lo


def _silu(x):
    return x * jax.nn.sigmoid(x)


def _rms(x):
    return x * lax.rsqrt(jnp.mean(x * x, axis=-1, keepdims=True) + RMS_EPS)


def _pack_halves(x):
    n = x.shape[1] // 2
    bits = lax.bitcast_convert_type(x.astype(BF16).astype(F32), jnp.uint32)
    return bits[:, :n] | (bits[:, n:] >> 16)


def _unpack_halves(u):
    hi = lax.bitcast_convert_type(u & jnp.uint32(0xFFFF0000), F32)
    lo = lax.bitcast_convert_type(u << 16, F32)
    return hi, lo


def _mod_kernel(c_ref, w_ref, b_ref, o_ref):
    s = _silu(c_ref[...])
    hi, mid, lo = _split3(s)
    whi, wmid, wlo = _split3(w_ref[...])
    dot = lambda p, q: jnp.dot(p, q, preferred_element_type=F32)
    acc = dot(hi, whi) + (dot(hi, wmid) + dot(mid, whi)) + (dot(hi, wlo) + dot(mid, wmid) + dot(lo, whi))
    o_ref[...] = acc + b_ref[...]


def _mod(c, ada_w, ada_b):
    bsz, d = c.shape
    n = ada_w.shape[1]
    tn = d
    return pl.pallas_call(
        _mod_kernel,
        grid=(n // tn,),
        in_specs=[pl.BlockSpec((bsz, d), lambda j: (0, 0)),
                  pl.BlockSpec((d, tn), lambda j: (0, j)),
                  pl.BlockSpec((1, tn), lambda j: (0, j))],
        out_specs=pl.BlockSpec((bsz, tn), lambda j: (0, j)),
        out_shape=jax.ShapeDtypeStruct((bsz, n), F32),
        compiler_params=_params("arbitrary"),
        name="mod",
    )(c, ada_w, ada_b.reshape(1, n))


def _inproj_kernel(x_ref, mod_ref, g_ref, wr_ref, wuv_ref, wg_ref, pr_ref, puv_ref, pg_ref):
    sh = mod_ref[0, 0:1, :]
    sc = mod_ref[0, 1:2, :]
    h = (_rms(x_ref[...]) * g_ref[...]) * (1.0 + sc) + sh
    hb = h.astype(BF16)
    pr_ref[...] = jnp.dot(hb, wr_ref[...], preferred_element_type=F32).astype(pr_ref.dtype)
    puv_ref[...] = jnp.dot(hb, wuv_ref[...], preferred_element_type=F32).astype(puv_ref.dtype)
    pg_ref[...] = jnp.dot(hb, wg_ref[...], preferred_element_type=F32).astype(pg_ref.dtype)


def _inproj(xf, mod3, norm1_g, w_in, seq):
    t, d = xf.shape
    tm = min(INPROJ_TILE, seq)
    per_seq = seq // tm
    wr = w_in[:, :RWKV_COLS].astype(BF16)
    wuv = w_in[:, RWKV_COLS:RWKV_COLS + 2 * GMLP_WIDTH].astype(BF16)
    wg = w_in[:, RWKV_COLS + 2 * GMLP_WIDTH:].astype(BF16)
    full = lambda a: pl.BlockSpec(a.shape, lambda i: (0,) * a.ndim)
    g = norm1_g.reshape(1, d)
    return pl.pallas_call(
        _inproj_kernel,
        grid=(t // tm,),
        in_specs=[pl.BlockSpec((tm, d), lambda i: (i, 0)),
                  pl.BlockSpec((1,) + mod3.shape[1:], lambda i: (i // per_seq, 0, 0)),
                  full(g), full(wr), full(wuv), full(wg)],
        out_specs=[pl.BlockSpec((tm, wr.shape[1]), lambda i: (i, 0)),
                   pl.BlockSpec((tm, wuv.shape[1]), lambda i: (i, 0)),
                   pl.BlockSpec((tm, wg.shape[1]), lambda i: (i, 0))],
        out_shape=[jax.ShapeDtypeStruct((t, wr.shape[1]), BF16),
                   jax.ShapeDtypeStruct((t, wuv.shape[1]), BF16),
                   jax.ShapeDtypeStruct((t, wg.shape[1]), BF16)],
        compiler_params=_params("arbitrary"),
        name="inproj",
    )(xf, mod3, g, wr, wuv, wg)


def _unit_lower_inverses(ns, row, col):
    c = ns[0].shape[0]
    eye = jnp.where(row == col, 1.0, 0.0).astype(F32)
    blk8 = (row // 8) == (col // 8)
    n8 = [jnp.where(blk8, n, 0.0) for n in ns]
    n8_2 = [_mm(a, a) for a in n8]
    n8_4 = [_mm(a, a) for a in n8_2]
    inv = [eye + a for a in n8]
    inv = [i + _mm(i, b) for i, b in zip(inv, n8_2)]
    inv = [i + _mm(i, b) for i, b in zip(inv, n8_4)]
    s = 8
    while s < c:
        sel = ((row // (2 * s)) == (col // (2 * s))) & ((row // s) != (col // s))
        low = [jnp.where(sel, n, 0.0) for n in ns]
        t1 = [_mm(l, i) for l, i in zip(low, inv)]
        inv = [i + _mm(i, t) for i, t in zip(inv, t1)]
        s *= 2
    return inv


def _rwkv_kernel(p_ref, mu_ref, w0_ref, wup_ref, a0_ref, aup_ref, gup_ref, kk_ref, ka_ref, rk_ref,
                 lng_ref, lnb_ref, bd_ref, y_ref, state_ref, carry_ref, ycat_ref, *, c):
    tm = p_ref.shape[0]
    nq = tm // c
    hd = RWKV_HEAD
    nh = RWKV_HEADS

    @pl.when(pl.program_id(1) == 0)
    def _():
        state_ref[...] = jnp.zeros_like(state_ref)
        carry_ref[...] = jnp.zeros_like(carry_ref)

    p = p_ref[...].astype(F32)
    rowc = lax.broadcasted_iota(jnp.int32, (tm, 1), 0)
    prev = jnp.where(rowc == 0, carry_ref[...], pltpu.roll(p, 1, axis=0))
    carry_ref[...] = p[tm - 1:tm, :]
    ps = p + (prev - p) * mu_ref[...]

    o = RWKV_WIDTH
    r = ps[:, 0:o]
    k = ps[:, o:2 * o]
    v = ps[:, 2 * o:3 * o]
    xw = ps[:, 3 * o:3 * o + DECAY_LORA]
    xa = ps[:, 3 * o + DECAY_LORA:3 * o + DECAY_LORA + AAA_LORA]
    xg = ps[:, 3 * o + DECAY_LORA + AAA_LORA:]

    z = -(w0_ref[...] + _mm(jnp.tanh(xw), wup_ref[...]))
    softplus = jnp.maximum(z, 0.0) + jnp.log1p(jnp.exp(-jnp.abs(z)))
    wlog = -softplus - 0.5
    ld = -jnp.exp(wlog)
    a = jax.nn.sigmoid(a0_ref[...] + _mm(xa, aup_ref[...]))
    g = _mm(jax.nn.sigmoid(xg), gup_ref[...])

    def head_sum(x):
        hi = x.astype(BF16)
        lo = (x - hi.astype(F32)).astype(BF16)
        bd = bd_ref[...]
        return jnp.dot(hi, bd, preferred_element_type=F32) + jnp.dot(lo, bd, preferred_element_type=F32)

    kkf = k * kk_ref[...]
    kkn = kkf * (1.0 / jnp.maximum(jnp.sqrt(head_sum(kkf * kkf)), 1e-12))
    k2 = k * (1.0 + (a - 1.0) * ka_ref[...])
    avec = -kkn
    bvec = kkn * a

    rowt = lax.broadcasted_iota(jnp.int32, (tm, tm), 0)
    colt = lax.broadcasted_iota(jnp.int32, (tm, tm), 1)
    tri = jnp.where((rowt >= colt) & ((rowt // c) == (colt // c)), 1.0, 0.0).astype(BF16)
    hi, mid, lo = _split3(ld)
    dot = lambda q: jnp.dot(tri, q, preferred_element_type=F32)
    cl = dot(hi) + dot(mid) + dot(lo)
    cl_end = jnp.concatenate(
        [jnp.broadcast_to(cl[(q + 1) * c - 1:(q + 1) * c, :], (c, o)) for q in range(nq)], axis=0)
    g_inv = jnp.exp(-cl)
    g_tail = jnp.exp(cl_end - cl)
    g_end = jnp.exp(cl_end)
    rt = (r * jnp.exp(cl)).astype(BF16)
    at = (avec * jnp.exp(cl - ld)).astype(BF16)
    kt = (k2 * g_inv).astype(BF16)
    bt = (bvec * g_inv).astype(BF16)
    kend = (k2 * g_tail).astype(BF16)
    bend = (bvec * g_tail).astype(BF16)
    vb = v.astype(BF16)

    row = lax.broadcasted_iota(jnp.int32, (c, c), 0)
    col = lax.broadcasted_iota(jnp.int32, (c, c), 1)
    strict = row > col
    incl = row >= col
    items = [(q, h) for q in range(nq) for h in range(nh)]
    blk = lambda x, q, h: x[q * c:(q + 1) * c, h * hd:(h + 1) * hd]

    lhs = [jnp.concatenate([blk(at, q, h), blk(rt, q, h)], axis=0) for q, h in items]
    mat_b = [_mm_nt(l, blk(bt, q, h)) for l, (q, h) in zip(lhs, items)]
    mat_k = [_mm_nt(l, blk(kt, q, h)) for l, (q, h) in zip(lhs, items)]
    a_ab = [jnp.where(strict, m[:c], 0.0) for m in mat_b]
    a_rb = [jnp.where(incl, m[c:], 0.0).astype(BF16) for m in mat_b]
    a_ak = [jnp.where(strict, m[:c], 0.0) for m in mat_k]
    a_rk = [jnp.where(incl, m[c:], 0.0) for m in mat_k]
    tinv = _unit_lower_inverses(a_ab, row, col)
    akv = [_mm(m, blk(vb, q, h)) for m, (q, h) in zip(a_ak, items)]
    y_loc = [_mm(m, blk(vb, q, h)) for m, (q, h) in zip(a_rk, items)]
    s_loc = [_mm_tn(blk(vb, q, h), blk(kend, q, h)) for q, h in items]
    wmat = [_mm(t, blk(at, q, h)).astype(BF16) for t, (q, h) in zip(tinv, items)]
    ut = [_mm(t, m) for t, m in zip(tinv, akv)]

    state = [state_ref[h] for h in range(nh)]
    for q in range(nq):
        base = q * nh
        sb = [s.astype(BF16) for s in state]
        u = [_mm_nt(wmat[base + h], sb[h]) + ut[base + h] for h in range(nh)]
        yh = [_mm_nt(blk(rt, q, h), sb[h]) + y_loc[base + h] for h in range(nh)]
        ub = [x.astype(BF16) for x in u]
        yh = [y0 + _mm(a_rb[base + h], ub[h]) for h, y0 in enumerate(yh)]
        state = [state[h] * g_end[q * c:q * c + 1, h * hd:(h + 1) * hd]
                 + _mm_tn(ub[h], blk(bend, q, h)) + s_loc[base + h] for h in range(nh)]
        for h in range(nh):
            ycat_ref[q * c:(q + 1) * c, h * hd:(h + 1) * hd] = yh[h]
    for h in range(nh):
        state_ref[h] = state[h]

    y = ycat_ref[...]
    dev = y - head_sum(y) * (1.0 / hd)
    rstd = lax.rsqrt(head_sum(dev * dev) * (1.0 / hd) + GN_EPS)
    bonus = head_sum(r * k2 * rk_ref[...])
    yn = dev * rstd * lng_ref[...] + lnb_ref[...]
    y_ref[...] = (yn + bonus * v) * g


def _rwkv(p_rwkv, bsz, seq, tshift_mu, w0, w_up, a0, a_up, g_up, k_k, k_a, r_k, ln_g, ln_b):
    t = p_rwkv.shape[0]
    c = min(CHUNK, seq)
    tm = min(RWKV_TILE, seq)
    nstep = seq // tm
    row = lambda a: a.reshape(1, -1)
    head_of = jnp.arange(RWKV_WIDTH, dtype=jnp.int32) // RWKV_HEAD
    block_diag = (head_of[:, None] == head_of[None, :]).astype(BF16)
    args = (row(tshift_mu), row(w0), w_up, row(a0), a_up, g_up, row(k_k), row(k_a), row(r_k),
            row(ln_g), row(ln_b), block_diag)
    full = lambda a: pl.BlockSpec(a.shape, lambda b, j: (0,) * a.ndim)
    return pl.pallas_call(
        functools.partial(_rwkv_kernel, c=c),
        grid=(bsz, nstep),
        in_specs=[pl.BlockSpec((tm, RWKV_COLS), lambda b, j: (b * nstep + j, 0))] + [full(a) for a in args],
        out_specs=pl.BlockSpec((tm, RWKV_WIDTH), lambda b, j: (b * nstep + j, 0)),
        out_shape=jax.ShapeDtypeStruct((t, RWKV_WIDTH), F32),
        scratch_shapes=[pltpu.VMEM((RWKV_HEADS, RWKV_HEAD, RWKV_HEAD), F32),
                        pltpu.VMEM((1, RWKV_COLS), F32),
                        pltpu.VMEM((tm, RWKV_WIDTH), F32)],
        compiler_params=_params("arbitrary", "arbitrary"),
        name="rwkv",
    )(p_rwkv, *args)


def _mix_kernel(puv_ref, pg_ref, ya_ref, x_ref, mod_ref, lng_ref, lnb_ref, ws_ref, bsm_ref,
                woa_ref, wob_ref, wo_ref, n2g_ref, x1_ref, h2_ref, h2p_ref, yb_ref):
    tm = x_ref.shape[0]
    gw = GMLP_WIDTH
    puv = puv_ref[...].astype(F32)
    u = jax.nn.gelu(puv[:, :gw])
    vg = jax.nn.gelu(puv[:, gw:])
    mu = jnp.mean(vg, axis=-1, keepdims=True)
    var = jnp.mean(jnp.square(vg - mu), axis=-1, keepdims=True)
    vln = (vg - mu) * lax.rsqrt(var + LN_EPS) * lng_ref[...] + lnb_ref[...]

    row = lax.broadcasted_iota(jnp.int32, (GMLP_BLOCK, GMLP_BLOCK), 0)
    col = lax.broadcasted_iota(jnp.int32, (GMLP_BLOCK, GMLP_BLOCK), 1)
    gd = GMLP_GROUP_DIM
    for gi in range(GMLP_GROUPS):
        wsm = jnp.where(row >= col, ws_ref[gi], 0.0).astype(BF16)
        for nb in range(tm // GMLP_BLOCK):
            rs = slice(nb * GMLP_BLOCK, (nb + 1) * GMLP_BLOCK)
            cs = slice(gi * gd, (gi + 1) * gd)
            sv = jnp.dot(wsm, vln[rs, cs].astype(BF16), preferred_element_type=F32) + bsm_ref[:, cs]
            yb_ref[rs, cs] = u[rs, cs] * sv

    pg = pg_ref[...].astype(F32)
    d = x_ref.shape[1]
    gate_a = jax.nn.sigmoid(pg[:, :d])
    gate_b = jax.nn.sigmoid(pg[:, d:])
    merged = gate_a * _mm(ya_ref[...], woa_ref[...]) + gate_b * _mm(yb_ref[...], wob_ref[...])
    g1 = mod_ref[0, 2:3, :]
    x1 = x_ref[...] + g1 * _mm(merged, wo_ref[...])
    x1_ref[...] = x1
    sh2 = mod_ref[0, 3:4, :]
    sc2 = mod_ref[0, 4:5, :]
    h2 = (_rms(x1) * n2g_ref[...]) * (1.0 + sc2) + sh2
    h2_ref[...] = h2
    packed = _pack_halves(h2)
    for j in range(h2p_ref.shape[0]):
        h2p_ref[j] = packed[:, j * LANES:(j + 1) * LANES]


def _mix(p_uv, p_gate, y_a, xf, mod3, seq, ln_g, ln_b, ws, bs, w_out_a, w_out_b, w_out, norm2_g):
    t, d = xf.shape
    tm = min(MIX_TILE, seq)
    per_seq = seq // tm
    row = lambda a: a.reshape(1, -1)
    bsm = jnp.repeat(bs.T, GMLP_GROUP_DIM, axis=1)
    consts = (row(ln_g), row(ln_b), ws, bsm, w_out_a.astype(BF16), w_out_b.astype(BF16),
              w_out.astype(BF16), row(norm2_g))
    full = lambda a: pl.BlockSpec(a.shape, lambda i: (0,) * a.ndim)
    tile = lambda a: pl.BlockSpec((tm, a.shape[1]), lambda i: (i, 0))
    return pl.pallas_call(
        _mix_kernel,
        grid=(t // tm,),
        in_specs=[tile(p_uv), tile(p_gate), tile(y_a), tile(xf),
                  pl.BlockSpec((1,) + mod3.shape[1:], lambda i: (i // per_seq, 0, 0))]
                 + [full(a) for a in consts],
        out_specs=[pl.BlockSpec((tm, d), lambda i: (i, 0)), pl.BlockSpec((tm, d), lambda i: (i, 0)),
                   pl.BlockSpec((d // 2 // LANES, tm, LANES), lambda i: (0, i, 0))],
        out_shape=[jax.ShapeDtypeStruct((t, d), F32), jax.ShapeDtypeStruct((t, d), F32),
                   jax.ShapeDtypeStruct((d // 2 // LANES, t, LANES), jnp.uint32)],
        scratch_shapes=[pltpu.VMEM((tm, GMLP_WIDTH), F32)],
        compiler_params=_params("arbitrary"),
        name="mix",
    )(p_uv, p_gate, y_a, xf, mod3, *consts)


def _first_argmax(vals, idx, big):
    m = jnp.max(vals, axis=0, keepdims=True)
    first = jnp.min(jnp.where(vals == m, idx, big), axis=0, keepdims=True)
    return m, first


def _route_kernel(h_ref, rwt_ref, bias_ref, eidx_ref, wts_ref):
    tm = h_ref.shape[0]
    ne = N_EXPERTS
    hhi, hmid, _ = _split3(h_ref[...])
    whi, wmid, _ = _split3(rwt_ref[...])
    nt = lambda p, q: lax.dot_general(p, q, (((1,), (1,)), ((), ())), preferred_element_type=F32)
    logits = nt(whi, hhi) + (nt(whi, hmid) + nt(wmid, hhi))
    scores = jax.nn.sigmoid(logits)
    sel = scores + bias_ref[...]
    neg = -jnp.inf

    gs = GROUP_SIZE
    gidx = lax.broadcasted_iota(jnp.int32, (gs, tm), 0)
    grp_rows = []
    for gi in range(N_EXPERT_GROUPS):
        blk = sel[gi * gs:(gi + 1) * gs, :]
        m1, i1 = _first_argmax(blk, gidx, gs)
        m2 = jnp.max(jnp.where(gidx == i1, neg, blk), axis=0, keepdims=True)
        grp_rows.append(m1 + m2)
    grp = jnp.concatenate(grp_rows, axis=0)

    ng = N_EXPERT_GROUPS
    giota = lax.broadcasted_iota(jnp.int32, (ng, tm), 0)
    gsel = jnp.zeros((ng, tm), jnp.bool_)
    work = grp
    for _ in range(TOPK_GROUPS):
        _, gi1 = _first_argmax(work, giota, ng)
        hit = giota == gi1
        gsel = gsel | hit
        work = jnp.where(hit, neg, work)
    gself = jnp.where(gsel, 1.0, 0.0)
    emask = jnp.concatenate(
        [jnp.broadcast_to(gself[gi:gi + 1, :], (gs, tm)) for gi in range(ng)], axis=0) > 0.5

    eiota = lax.broadcasted_iota(jnp.int32, (ne, tm), 0)
    work = jnp.where(emask, sel, neg)
    idx_rows, w_rows = [], []
    for _ in range(TOP_K):
        _, e1 = _first_argmax(work, eiota, ne)
        hit = eiota == e1
        idx_rows.append(e1)
        w_rows.append(jnp.sum(jnp.where(hit, scores, 0.0), axis=0, keepdims=True))
        work = jnp.where(hit, neg, work)
    eidx_ref[...] = jnp.concatenate(idx_rows, axis=0)
    w = jnp.concatenate(w_rows, axis=0)
    wts_ref[...] = w / jnp.sum(w, axis=0, keepdims=True) * ROUTED_SCALE


def _route(h2, router_w, router_bias):
    t, d = h2.shape
    tm = min(ROUTE_TILE, t)
    rwt = router_w.T
    bias = router_bias.reshape(N_EXPERTS, 1)
    return pl.pallas_call(
        _route_kernel,
        grid=(t // tm,),
        in_specs=[pl.BlockSpec((tm, d), lambda i: (i, 0)),
                  pl.BlockSpec(rwt.shape, lambda i: (0, 0)),
                  pl.BlockSpec(bias.shape, lambda i: (0, 0))],
        out_specs=[pl.BlockSpec((TOP_K, tm), lambda i: (0, i)), pl.BlockSpec((TOP_K, tm), lambda i: (0, i))],
        out_shape=[jax.ShapeDtypeStruct((TOP_K, t), jnp.int32), jax.ShapeDtypeStruct((TOP_K, t), F32)],
        compiler_params=_params("arbitrary"),
        name="route",
    )(h2, rwt, bias)


def _rank_kernel(e_ref, upper_ref, rank_ref, count_ref):
    tm = e_ref.shape[1]
    ne = N_EXPERTS

    @pl.when(pl.program_id(0) == 0)
    def _():
        count_ref[...] = jnp.zeros_like(count_ref)

    e = e_ref[...]
    eiota = lax.broadcasted_iota(jnp.int32, (ne, tm), 0)
    onehot = jnp.zeros((ne, tm), F32)
    for kk in range(TOP_K):
        onehot = onehot + jnp.where(eiota == e[kk:kk + 1, :], 1.0, 0.0)
    before = jnp.dot(onehot.astype(BF16), upper_ref[...], preferred_element_type=F32)
    base = count_ref[:, 0:1]
    tot = before + base
    rows = [jnp.sum(jnp.where(eiota == e[kk:kk + 1, :], tot, 0.0), axis=0, keepdims=True)
            for kk in range(TOP_K)]
    rank_ref[...] = jnp.concatenate(rows, axis=0).astype(jnp.int32)
    count_ref[...] = count_ref[...] + jnp.sum(onehot, axis=1, keepdims=True)


def _rank(eidx):
    k, t = eidx.shape
    tm = min(RANK_TILE, t)
    upper = jnp.triu(jnp.ones((tm, tm), F32), 1).astype(BF16)
    return pl.pallas_call(
        _rank_kernel,
        grid=(t // tm,),
        in_specs=[pl.BlockSpec((k, tm), lambda i: (0, i)), pl.BlockSpec((tm, tm), lambda i: (0, 0))],
        out_specs=[pl.BlockSpec((k, tm), lambda i: (0, i)), pl.BlockSpec((N_EXPERTS, 128), lambda i: (0, 0))],
        out_shape=[jax.ShapeDtypeStruct((k, t), jnp.int32), jax.ShapeDtypeStruct((N_EXPERTS, 128), F32)],
        compiler_params=_params("arbitrary"),
        name="rank",
    )(eidx, upper)


def _dest_kernel(e_ref, rank_ref, pst_ref, dest_ref):
    tm = e_ref.shape[1]
    e = e_ref[...]
    eiota = lax.broadcasted_iota(jnp.int32, (N_EXPERTS, tm), 0)
    pst = pst_ref[...]
    rows = [jnp.sum(jnp.where(eiota == e[kk:kk + 1, :], pst, 0.0), axis=0, keepdims=True)
            for kk in range(TOP_K)]
    dest_ref[...] = jnp.concatenate(rows, axis=0).astype(jnp.int32) + rank_ref[...]


def _dest(eidx, rank, pstarts):
    k, t = eidx.shape
    tm = min(RANK_TILE, t)
    pst = pstarts.astype(F32).reshape(N_EXPERTS, 1)
    return pl.pallas_call(
        _dest_kernel,
        grid=(t // tm,),
        in_specs=[pl.BlockSpec((k, tm), lambda i: (0, i)), pl.BlockSpec((k, tm), lambda i: (0, i)),
                  pl.BlockSpec((N_EXPERTS, 1), lambda i: (0, 0))],
        out_specs=pl.BlockSpec((k, tm), lambda i: (0, i)),
        out_shape=jax.ShapeDtypeStruct((k, t), jnp.int32),
        compiler_params=_params("arbitrary"),
        name="dest",
    )(eidx, rank, pst)


def _sc_scatter(rows, idx, n_out):
    n, lanes = rows.shape
    nstep, copies, w = idx.shape
    mesh = plsc.VectorSubcoreMesh(core_axis_name="core", subcore_axis_name="subcore")

    @functools.partial(pl.kernel, out_type=jax.ShapeDtypeStruct((n_out, lanes), rows.dtype), mesh=mesh,
                       scratch_types=[])
    def scatter_kernel(rows_hbm, idx_hbm, out_hbm):
        def body(rows_vmem, idx_vmem):
            for j in range(copies):
                pltpu.sync_copy(rows_vmem, out_hbm.at[idx_vmem.at[j]])

        pltpu.emit_pipeline(
            body,
            grid=(nstep,),
            in_specs=[pl.BlockSpec((w, lanes), index_map=lambda i: (i, 0)),
                      pl.BlockSpec((copies, w), index_map=lambda i: (i, 0))],
            out_specs=[],
            core_axis_name=("core", "subcore"),
            dimension_semantics=(pltpu.PARALLEL,),
        )(rows_hbm, idx_hbm)

    return scatter_kernel(rows, idx.reshape(nstep * copies, w))


def _dispatch(h2p, dest_tk, n_rows):
    nchunk, t, lanes = h2p.shape
    k = dest_tk.shape[0]
    w = SC_WINDOW
    idx = dest_tk.reshape(k, t // w, w).transpose(1, 0, 2)[None]
    idx = idx + (jnp.arange(nchunk, dtype=jnp.int32) * n_rows)[:, None, None, None]
    out = _sc_scatter(h2p.reshape(nchunk * t, lanes), idx.reshape(nchunk * (t // w), k, w), nchunk * n_rows)
    return out.reshape(nchunk, n_rows, lanes)


def _expert_kernel(be_ref, nv_ref, rows_ref, x_ref, w1_ref, w3_ref, w2_ref, y_ref, w1b, w3b, w2b):
    i = pl.program_id(0)
    nchunk, rb, _ = x_ref.shape
    half = nchunk * LANES

    @pl.when((i == 0) | (be_ref[i] != be_ref[jnp.maximum(i - 1, 0)]))
    def _():
        w1b[...] = w1_ref[0].astype(BF16)
        w3b[...] = w3_ref[0].astype(BF16)
        w2b[...] = w2_ref[0].astype(BF16)

    @pl.when(i < nv_ref[0])
    def _():
        live = lax.broadcasted_iota(jnp.int32, (rb, 1), 0) < rows_ref[i]
        x = jnp.concatenate([x_ref[j] for j in range(nchunk)], axis=1)
        hi, lo = _unpack_halves(jnp.where(live, x, jnp.uint32(0)))
        xa = hi.astype(BF16)
        xb = lo.astype(BF16)
        dot = lambda p, q: jnp.dot(p, q, preferred_element_type=F32)
        h1 = dot(xa, w1b[:half, :]) + dot(xb, w1b[half:, :])
        h3 = dot(xa, w3b[:half, :]) + dot(xb, w3b[half:, :])
        hid = _silu(h1) * h3
        packed = _pack_halves(dot(hid.astype(BF16), w2b[...]))
        for j in range(y_ref.shape[0]):
            y_ref[j] = packed[:, j * LANES:(j + 1) * LANES]

    @pl.when(i >= nv_ref[0])
    def _():
        y_ref[...] = jnp.zeros_like(y_ref)


def _expert(x_sorted, block_e, nvalid, block_rows, w1, w3, w2):
    nchunk, n_rows, lanes = x_sorted.shape
    half = nchunk * lanes
    rb = ROW_BLOCK
    nblk = n_rows // rb
    d, f = w1.shape[1], w1.shape[2]
    grid_spec = pltpu.PrefetchScalarGridSpec(
        num_scalar_prefetch=3,
        grid=(nblk,),
        in_specs=[
            pl.BlockSpec((nchunk, rb, lanes), lambda i, be, nv, br: (0, jnp.minimum(i, nv[0] - 1), 0)),
            pl.BlockSpec((1, d, f), lambda i, be, nv, br: (be[i], 0, 0)),
            pl.BlockSpec((1, d, f), lambda i, be, nv, br: (be[i], 0, 0)),
            pl.BlockSpec((1, f, d), lambda i, be, nv, br: (be[i], 0, 0)),
        ],
        out_specs=pl.BlockSpec((nchunk, rb, lanes), lambda i, be, nv, br: (0, i, 0)),
        scratch_shapes=[pltpu.VMEM((d, f), BF16), pltpu.VMEM((d, f), BF16), pltpu.VMEM((f, d), BF16)],
    )
    return pl.pallas_call(
        _expert_kernel,
        grid_spec=grid_spec,
        out_shape=jax.ShapeDtypeStruct((half // LANES, n_rows, LANES), jnp.uint32),
        compiler_params=_params("arbitrary"),
        name="expert",
    )(block_e, nvalid, block_rows, x_sorted, w1, w3, w2)


def _sc_gather(table, idx):
    n = idx.shape[0]
    mesh = plsc.VectorSubcoreMesh(core_axis_name="core", subcore_axis_name="subcore")

    @functools.partial(pl.kernel, out_type=jax.ShapeDtypeStruct((n, table.shape[1]), table.dtype), mesh=mesh)
    def gather_kernel(table_hbm, idx_hbm, out_hbm):
        def body(idx_vmem, out_vmem):
            pltpu.sync_copy(table_hbm.at[idx_vmem.at[0]], out_vmem)

        pltpu.emit_pipeline(
            body,
            grid=(n // SC_WINDOW,),
            in_specs=[pl.BlockSpec((1, SC_WINDOW), index_map=lambda i: (0, i))],
            out_specs=[pl.BlockSpec((SC_WINDOW, table.shape[1]), index_map=lambda i: (i, 0))],
            core_axis_name=("core", "subcore"),
            dimension_semantics=(pltpu.PARALLEL,),
        )(idx_hbm, out_hbm)

    return gather_kernel(table, idx.reshape(1, n))


def _gather_expert_rows(y_sorted, dest_tk):
    nchunk, n_rows, lanes = y_sorted.shape
    k, t = dest_tk.shape
    idx = dest_tk[None, :, :] + (jnp.arange(nchunk, dtype=jnp.int32) * n_rows)[:, None, None]
    rows = _sc_gather(y_sorted.reshape(nchunk * n_rows, lanes), idx.reshape(-1))
    return rows.reshape(nchunk, k, t, lanes)


def _combine_kernel(yg_ref, wts_ref, h_ref, x1_ref, mod_ref, sw1_ref, sw3_ref, sw2_ref, nfg_ref, o_ref):
    nchunk = yg_ref.shape[0]
    w = wts_ref[...]
    acc_hi = [None] * nchunk
    acc_lo = [None] * nchunk
    for kk in range(TOP_K):
        wk = w[:, kk:kk + 1]
        for j in range(nchunk):
            hi, lo = _unpack_halves(yg_ref[j, kk])
            acc_hi[j] = hi * wk if kk == 0 else acc_hi[j] + hi * wk
            acc_lo[j] = lo * wk if kk == 0 else acc_lo[j] + lo * wk
    routed = jnp.concatenate(acc_hi + acc_lo, axis=1)
    hb = h_ref[...].astype(BF16)
    hid = _silu(jnp.dot(hb, sw1_ref[...], preferred_element_type=F32)) * jnp.dot(
        hb, sw3_ref[...], preferred_element_type=F32)
    shared = jnp.dot(hid.astype(BF16), sw2_ref[...], preferred_element_type=F32)
    g2 = mod_ref[0, 5:6, :]
    x2 = x1_ref[...] + g2 * (routed + shared)
    o_ref[...] = _rms(x2) * nfg_ref[...]


def _combine(y_gathered, wts_tk, h2, x1, mod3, seq, sw1, sw3, sw2, normf_g):
    t, d = h2.shape
    tm = min(COMBINE_TILE, seq)
    per_seq = seq // tm
    nchunk, k, _, lanes = y_gathered.shape
    consts = (sw1.astype(BF16), sw3.astype(BF16), sw2.astype(BF16), normf_g.reshape(1, d))
    full = lambda a: pl.BlockSpec(a.shape, lambda i: (0,) * a.ndim)
    tile = lambda a: pl.BlockSpec((tm, a.shape[1]), lambda i: (i, 0))
    return pl.pallas_call(
        _combine_kernel,
        grid=(t // tm,),
        in_specs=[pl.BlockSpec((nchunk, k, tm, lanes), lambda i: (0, 0, i, 0)),
                  tile(wts_tk), tile(h2), tile(x1),
                  pl.BlockSpec((1,) + mod3.shape[1:], lambda i: (i // per_seq, 0, 0))]
                 + [full(a) for a in consts],
        out_specs=pl.BlockSpec((tm, d), lambda i: (i, 0)),
        out_shape=jax.ShapeDtypeStruct((t, d), F32),
        compiler_params=_params("arbitrary"),
        name="combine",
    )(y_gathered, wts_tk, h2, x1, mod3, *consts)


def _dispatch_plan(counts, t):
    rb = ROW_BLOCK
    padded = (counts + rb - 1) // rb * rb
    pends = jnp.cumsum(padded)
    pstarts = pends - padded
    n_rows = t * TOP_K + N_EXPERTS * rb
    block_start = jnp.arange(n_rows // rb, dtype=jnp.int32) * rb
    block_e = jnp.sum((pends[None, :] <= block_start[:, None]).astype(jnp.int32), axis=1)
    block_e = jnp.minimum(block_e, N_EXPERTS - 1)
    block_rows = jnp.clip(pstarts[block_e] + counts[block_e] - block_start, 0, rb).astype(jnp.int32)
    nvalid = (pends[-1:] // rb).astype(jnp.int32)
    return pstarts, block_e, block_rows, nvalid, n_rows


def kernel(x, c, ada_w, ada_b, norm1_g, norm2_g, w_in, tshift_mu, rwkv_w0, rwkv_w_up, rwkv_a0, rwkv_a_up,
           rwkv_g_up, rwkv_k_k, rwkv_k_a, rwkv_r_k, rwkv_ln_g, rwkv_ln_b, gmlp_ln_g, gmlp_ln_b, gmlp_ws,
           gmlp_bs, w_out_a, w_out_b, w_out, router_w, router_bias, exp_w1, exp_w3, exp_w2, shared_w1,
           shared_w3, shared_w2, normf_g):
    bsz, seq, d = x.shape
    t = bsz * seq
    xf = x.reshape(t, d)
    assert ada_w.shape[0] == 1, "single-layer block only"
    l = 0
    mod3 = _mod(c, ada_w[l], ada_b[l]).reshape(bsz, 6, d)
    p_rwkv, p_uv, p_gate = _inproj(xf, mod3, norm1_g[l], w_in[l], seq)
    y_a = _rwkv(p_rwkv, bsz, seq, tshift_mu[l], rwkv_w0[l], rwkv_w_up[l], rwkv_a0[l], rwkv_a_up[l],
                rwkv_g_up[l], rwkv_k_k[l], rwkv_k_a[l], rwkv_r_k[l], rwkv_ln_g[l], rwkv_ln_b[l])
    x1, h2, h2p = _mix(p_uv, p_gate, y_a, xf, mod3, seq, gmlp_ln_g[l], gmlp_ln_b[l], gmlp_ws[l], gmlp_bs[l],
                  w_out_a[l], w_out_b[l], w_out[l], norm2_g[l])
    eidx, wts = _route(h2, router_w[l], router_bias[l])
    rank, counts = _rank(eidx)
    counts = counts[:, 0].astype(jnp.int32)
    pstarts, block_e, block_rows, nvalid, n_rows = _dispatch_plan(counts, t)
    dest = _dest(eidx, rank, pstarts)
    x_sorted = _dispatch(h2p, dest, n_rows)
    y_sorted = _expert(x_sorted, block_e, nvalid, block_rows, exp_w1[l], exp_w3[l], exp_w2[l])
    y_gathered = _gather_expert_rows(y_sorted, dest)
    out = _combine(y_gathered, wts.T, h2, x1, mod3, seq, shared_w1[l], shared_w3[l], shared_w2[l], normf_g)
    return out.reshape(bsz, seq, d)
```

```python
import functools

import jax
import jax.numpy as jnp
from jax import lax
from jax.experimental import pallas as pl
from jax.experimental.pallas import tpu as pltpu
from jax.experimental.pallas import tpu_sc as plsc

F32 = jnp.float32
BF16 = jnp.bfloat16

RWKV_HEAD = 64
RWKV_HEADS = 8
RWKV_WIDTH = RWKV_HEAD * RWKV_HEADS
DECAY_LORA = 64
AAA_LORA = 64
GATE_LORA = 128
RWKV_COLS = 3 * RWKV_WIDTH + DECAY_LORA + AAA_LORA + GATE_LORA
GMLP_WIDTH = 512
GMLP_BLOCK = 128
GMLP_GROUPS = 8
GMLP_GROUP_DIM = GMLP_WIDTH // GMLP_GROUPS
N_EXPERTS = 256
TOP_K = 8
N_EXPERT_GROUPS = 8
GROUP_SIZE = N_EXPERTS // N_EXPERT_GROUPS
TOPK_GROUPS = 4
ROUTED_SCALE = 2.5
RMS_EPS = 1e-6
LN_EPS = 1e-5
GN_EPS = 64e-5

VMEM_LIMIT_BYTES = 56 * 1024 * 1024

CHUNK = 64
RWKV_TILE = 256
ROW_BLOCK = 256
INPROJ_TILE = 256
MIX_TILE = 256
ROUTE_TILE = 512
RANK_TILE = 512
COMBINE_TILE = 256
LANES = 128
SC_WINDOW = 128


def _params(*sem):
    return pltpu.CompilerParams(dimension_semantics=sem, vmem_limit_bytes=VMEM_LIMIT_BYTES)


def _mm(a, b):
    return jnp.dot(a.astype(BF16), b.astype(BF16), preferred_element_type=F32)


def _mm_nt(a, b):
    return lax.dot_general(a.astype(BF16), b.astype(BF16), (((1,), (1,)), ((), ())),
                           preferred_element_type=F32)


def _mm_tn(a, b):
    return lax.dot_general(a.astype(BF16), b.astype(BF16), (((0,), (0,)), ((), ())),
                           preferred_element_type=F32)


def _split3(a):
    hi = a.astype(BF16)
    r1 = a - hi.astype(F32)
    mid = r1.astype(BF16)
    lo = (r1 - mid.astype(F32)).astype(BF16)
    return hi, mid, lo


def _silu(x):
    return x * jax.nn.sigmoid(x)


def _rms(x):
    return x * lax.rsqrt(jnp.mean(x * x, axis=-1, keepdims=True) + RMS_EPS)


def _pack_halves(x):
    n = x.shape[1] // 2
    bits = lax.bitcast_convert_type(x.astype(BF16).astype(F32), jnp.uint32)
    return bits[:, :n] | (bits[:, n:] >> 16)


def _unpack_halves(u):
    hi = lax.bitcast_convert_type(u & jnp.uint32(0xFFFF0000), F32)
    lo = lax.bitcast_convert_type(u << 16, F32)
    return hi, lo


def _mod_kernel(c_ref, w_ref, b_ref, o_ref):
    s = _silu(c_ref[...])
    hi, mid, lo = _split3(s)
    whi, wmid, wlo = _split3(w_ref[...])
    dot = lambda p, q: jnp.dot(p, q, preferred_element_type=F32)
    acc = dot(hi, whi) + (dot(hi, wmid) + dot(mid, whi)) + (dot(hi, wlo) + dot(mid, wmid) + dot(lo, whi))
    o_ref[...] = acc + b_ref[...]


def _mod(c, ada_w, ada_b):
    bsz, d = c.shape
    n = ada_w.shape[1]
    tn = d
    return pl.pallas_call(
        _mod_kernel,
        grid=(n // tn,),
        in_specs=[pl.BlockSpec((bsz, d), lambda j: (0, 0)),
                  pl.BlockSpec((d, tn), lambda j: (0, j)),
                  pl.BlockSpec((1, tn), lambda j: (0, j))],
        out_specs=pl.BlockSpec((bsz, tn), lambda j: (0, j)),
        out_shape=jax.ShapeDtypeStruct((bsz, n), F32),
        compiler_params=_params("arbitrary"),
        name="mod",
    )(c, ada_w, ada_b.reshape(1, n))


def _inproj_kernel(x_ref, mod_ref, g_ref, wr_ref, wuv_ref, wg_ref, pr_ref, puv_ref, pg_ref):
    sh = mod_ref[0, 0:1, :]
    sc = mod_ref[0, 1:2, :]
    h = (_rms(x_ref[...]) * g_ref[...]) * (1.0 + sc) + sh
    hb = h.astype(BF16)
    pr_ref[...] = jnp.dot(hb, wr_ref[...], preferred_element_type=F32).astype(pr_ref.dtype)
    puv_ref[...] = jnp.dot(hb, wuv_ref[...], preferred_element_type=F32).astype(puv_ref.dtype)
    pg_ref[...] = jnp.dot(hb, wg_ref[...], preferred_element_type=F32).astype(pg_ref.dtype)


def _inproj(xf, mod3, norm1_g, w_in, seq):
    t, d = xf.shape
    tm = min(INPROJ_TILE, seq)
    per_seq = seq // tm
    wr = w_in[:, :RWKV_COLS].astype(BF16)
    wuv = w_in[:, RWKV_COLS:RWKV_COLS + 2 * GMLP_WIDTH].astype(BF16)
    wg = w_in[:, RWKV_COLS + 2 * GMLP_WIDTH:].astype(BF16)
    full = lambda a: pl.BlockSpec(a.shape, lambda i: (0,) * a.ndim)
    g = norm1_g.reshape(1, d)
    return pl.pallas_call(
        _inproj_kernel,
        grid=(t // tm,),
        in_specs=[pl.BlockSpec((tm, d), lambda i: (i, 0)),
                  pl.BlockSpec((1,) + mod3.shape[1:], lambda i: (i // per_seq, 0, 0)),
                  full(g), full(wr), full(wuv), full(wg)],
        out_specs=[pl.BlockSpec((tm, wr.shape[1]), lambda i: (i, 0)),
                   pl.BlockSpec((tm, wuv.shape[1]), lambda i: (i, 0)),
                   pl.BlockSpec((tm, wg.shape[1]), lambda i: (i, 0))],
        out_shape=[jax.ShapeDtypeStruct((t, wr.shape[1]), BF16),
                   jax.ShapeDtypeStruct((t, wuv.shape[1]), BF16),
                   jax.ShapeDtypeStruct((t, wg.shape[1]), BF16)],
        compiler_params=_params("arbitrary"),
        name="inproj",
    )(xf, mod3, g, wr, wuv, wg)


def _unit_lower_inverses(ns, row, col):
    c = ns[0].shape[0]
    eye = jnp.where(row == col, 1.0, 0.0).astype(F32)
    blk8 = (row // 8) == (col // 8)
    n8 = [jnp.where(blk8, n, 0.0) for n in ns]
    n8_2 = [_mm(a, a) for a in n8]
    n8_4 = [_mm(a, a) for a in n8_2]
    inv = [eye + a for a in n8]
    inv = [i + _mm(i, b) for i, b in zip(inv, n8_2)]
    inv = [i + _mm(i, b) for i, b in zip(inv, n8_4)]
    s = 8
    while s < c:
        sel = ((row // (2 * s)) == (col // (2 * s))) & ((row // s) != (col // s))
        low = [jnp.where(sel, n, 0.0) for n in ns]
        t1 = [_mm(l, i) for l, i in zip(low, inv)]
        inv = [i + _mm(i, t) for i, t in zip(inv, t1)]
        s *= 2
    return inv


def _rwkv_kernel(p_ref, mu_ref, w0_ref, wup_ref, a0_ref, aup_ref, gup_ref, kk_ref, ka_ref, rk_ref,
                 lng_ref, lnb_ref, bd_ref, y_ref, state_ref, carry_ref, ycat_ref, *, c):
    tm = p_ref.shape[0]
    nq = tm // c
    hd = RWKV_HEAD
    nh = RWKV_HEADS

    @pl.when(pl.program_id(1) == 0)
    def _():
        state_ref[...] = jnp.zeros_like(state_ref)
        carry_ref[...] = jnp.zeros_like(carry_ref)

    p = p_ref[...].astype(F32)
    rowc = lax.broadcasted_iota(jnp.int32, (tm, 1), 0)
    prev = jnp.where(rowc == 0, carry_ref[...], pltpu.roll(p, 1, axis=0))
    carry_ref[...] = p[tm - 1:tm, :]
    ps = p + (prev - p) * mu_ref[...]

    o = RWKV_WIDTH
    r = ps[:, 0:o]
    k = ps[:, o:2 * o]
    v = ps[:, 2 * o:3 * o]
    xw = ps[:, 3 * o:3 * o + DECAY_LORA]
    xa = ps[:, 3 * o + DECAY_LORA:3 * o + DECAY_LORA + AAA_LORA]
    xg = ps[:, 3 * o + DECAY_LORA + AAA_LORA:]

    z = -(w0_ref[...] + _mm(jnp.tanh(xw), wup_ref[...]))
    softplus = jnp.maximum(z, 0.0) + jnp.log1p(jnp.exp(-jnp.abs(z)))
    wlog = -softplus - 0.5
    ld = -jnp.exp(wlog)
    a = jax.nn.sigmoid(a0_ref[...] + _mm(xa, aup_ref[...]))
    g = _mm(jax.nn.sigmoid(xg), gup_ref[...])

    def head_sum(x):
        hi = x.astype(BF16)
        lo = (x - hi.astype(F32)).astype(BF16)
        bd = bd_ref[...]
        return jnp.dot(hi, bd, preferred_element_type=F32) + jnp.dot(lo, bd, preferred_element_type=F32)

    kkf = k * kk_ref[...]
    kkn = kkf * (1.0 / jnp.maximum(jnp.sqrt(head_sum(kkf * kkf)), 1e-12))
    k2 = k * (1.0 + (a - 1.0) * ka_ref[...])
    avec = -kkn
    bvec = kkn * a

    rowt = lax.broadcasted_iota(jnp.int32, (tm, tm), 0)
    colt = lax.broadcasted_iota(jnp.int32, (tm, tm), 1)
    tri = jnp.where((rowt >= colt) & ((rowt // c) == (colt // c)), 1.0, 0.0).astype(BF16)
    hi, mid, lo = _split3(ld)
    dot = lambda q: jnp.dot(tri, q, preferred_element_type=F32)
    cl = dot(hi) + dot(mid) + dot(lo)
    cl_end = jnp.concatenate(
        [jnp.broadcast_to(cl[(q + 1) * c - 1:(q + 1) * c, :], (c, o)) for q in range(nq)], axis=0)
    g_inv = jnp.exp(-cl)
    g_tail = jnp.exp(cl_end - cl)
    g_end = jnp.exp(cl_end)
    rt = (r * jnp.exp(cl)).astype(BF16)
    at = (avec * jnp.exp(cl - ld)).astype(BF16)
    kt = (k2 * g_inv).astype(BF16)
    bt = (bvec * g_inv).astype(BF16)
    kend = (k2 * g_tail).astype(BF16)
    bend = (bvec * g_tail).astype(BF16)
    vb = v.astype(BF16)

    row = lax.broadcasted_iota(jnp.int32, (c, c), 0)
    col = lax.broadcasted_iota(jnp.int32, (c, c), 1)
    strict = row > col
    incl = row >= col
    items = [(q, h) for q in range(nq) for h in range(nh)]
    blk = lambda x, q, h: x[q * c:(q + 1) * c, h * hd:(h + 1) * hd]

    lhs = [jnp.concatenate([blk(at, q, h), blk(rt, q, h)], axis=0) for q, h in items]
    mat_b = [_mm_nt(l, blk(bt, q, h)) for l, (q, h) in zip(lhs, items)]
    mat_k = [_mm_nt(l, blk(kt, q, h)) for l, (q, h) in zip(lhs, items)]
    a_ab = [jnp.where(strict, m[:c], 0.0) for m in mat_b]
    a_rb = [jnp.where(incl, m[c:], 0.0).astype(BF16) for m in mat_b]
    a_ak = [jnp.where(strict, m[:c], 0.0) for m in mat_k]
    a_rk = [jnp.where(incl, m[c:], 0.0) for m in mat_k]
    tinv = _unit_lower_inverses(a_ab, row, col)
    akv = [_mm(m, blk(vb, q, h)) for m, (q, h) in zip(a_ak, items)]
    y_loc = [_mm(m, blk(vb, q, h)) for m, (q, h) in zip(a_rk, items)]
    s_loc = [_mm_tn(blk(vb, q, h), blk(kend, q, h)) for q, h in items]
    wmat = [_mm(t, blk(at, q, h)).astype(BF16) for t, (q, h) in zip(tinv, items)]
    ut = [_mm(t, m) for t, m in zip(tinv, akv)]

    state = [state_ref[h] for h in range(nh)]
    for q in range(nq):
        base = q * nh
        sb = [s.astype(BF16) for s in state]
        u = [_mm_nt(wmat[base + h], sb[h]) + ut[base + h] for h in range(nh)]
        yh = [_mm_nt(blk(rt, q, h), sb[h]) + y_loc[base + h] for h in range(nh)]
        ub = [x.astype(BF16) for x in u]
        yh = [y0 + _mm(a_rb[base + h], ub[h]) for h, y0 in enumerate(yh)]
        state = [state[h] * g_end[q * c:q * c + 1, h * hd:(h + 1) * hd]
                 + _mm_tn(ub[h], blk(bend, q, h)) + s_loc[base + h] for h in range(nh)]
        for h in range(nh):
            ycat_ref[q * c:(q + 1) * c, h * hd:(h + 1) * hd] = yh[h]
    for h in range(nh):
        state_ref[h] = state[h]

    y = ycat_ref[...]
    dev = y - head_sum(y) * (1.0 / hd)
    rstd = lax.rsqrt(head_sum(dev * dev) * (1.0 / hd) + GN_EPS)
    bonus = head_sum(r * k2 * rk_ref[...])
    yn = dev * rstd * lng_ref[...] + lnb_ref[...]
    y_ref[...] = (yn + bonus * v) * g


def _rwkv(p_rwkv, bsz, seq, tshift_mu, w0, w_up, a0, a_up, g_up, k_k, k_a, r_k, ln_g, ln_b):
    t = p_rwkv.shape[0]
    c = min(CHUNK, seq)
    tm = min(RWKV_TILE, seq)
    nstep = seq // tm
    row = lambda a: a.reshape(1, -1)
    head_of = jnp.arange(RWKV_WIDTH, dtype=jnp.int32) // RWKV_HEAD
    block_diag = (head_of[:, None] == head_of[None, :]).astype(BF16)
    args = (row(tshift_mu), row(w0), w_up, row(a0), a_up, g_up, row(k_k), row(k_a), row(r_k),
            row(ln_g), row(ln_b), block_diag)
    full = lambda a: pl.BlockSpec(a.shape, lambda b, j: (0,) * a.ndim)
    return pl.pallas_call(
        functools.partial(_rwkv_kernel, c=c),
        grid=(bsz, nstep),
        in_specs=[pl.BlockSpec((tm, RWKV_COLS), lambda b, j: (b * nstep + j, 0))] + [full(a) for a in args],
        out_specs=pl.BlockSpec((tm, RWKV_WIDTH), lambda b, j: (b * nstep + j, 0)),
        out_shape=jax.ShapeDtypeStruct((t, RWKV_WIDTH), F32),
        scratch_shapes=[pltpu.VMEM((RWKV_HEADS, RWKV_HEAD, RWKV_HEAD), F32),
                        pltpu.VMEM((1, RWKV_COLS), F32),
                        pltpu.VMEM((tm, RWKV_WIDTH), F32)],
        compiler_params=_params("arbitrary", "arbitrary"),
        name="rwkv",
    )(p_rwkv, *args)


def _mix_kernel(puv_ref, pg_ref, ya_ref, x_ref, mod_ref, lng_ref, lnb_ref, ws_ref, bsm_ref,
                woa_ref, wob_ref, wo_ref, n2g_ref, x1_ref, h2_ref, h2p_ref, yb_ref):
    tm = x_ref.shape[0]
    gw = GMLP_WIDTH
    puv = puv_ref[...].astype(F32)
    u = jax.nn.gelu(puv[:, :gw])
    vg = jax.nn.gelu(puv[:, gw:])
    mu = jnp.mean(vg, axis=-1, keepdims=True)
    var = jnp.mean(jnp.square(vg - mu), axis=-1, keepdims=True)
    vln = (vg - mu) * lax.rsqrt(var + LN_EPS) * lng_ref[...] + lnb_ref[...]

    row = lax.broadcasted_iota(jnp.int32, (GMLP_BLOCK, GMLP_BLOCK), 0)
    col = lax.broadcasted_iota(jnp.int32, (GMLP_BLOCK, GMLP_BLOCK), 1)
    gd = GMLP_GROUP_DIM
    for gi in range(GMLP_GROUPS):
        wsm = jnp.where(row >= col, ws_ref[gi], 0.0).astype(BF16)
        for nb in range(tm // GMLP_BLOCK):
            rs = slice(nb * GMLP_BLOCK, (nb + 1) * GMLP_BLOCK)
            cs = slice(gi * gd, (gi + 1) * gd)
            sv = jnp.dot(wsm, vln[rs, cs].astype(BF16), preferred_element_type=F32) + bsm_ref[:, cs]
            yb_ref[rs, cs] = u[rs, cs] * sv

    pg = pg_ref[...].astype(F32)
    d = x_ref.shape[1]
    gate_a = jax.nn.sigmoid(pg[:, :d])
    gate_b = jax.nn.sigmoid(pg[:, d:])
    merged = gate_a * _mm(ya_ref[...], woa_ref[...]) + gate_b * _mm(yb_ref[...], wob_ref[...])
    g1 = mod_ref[0, 2:3, :]
    x1 = x_ref[...] + g1 * _mm(merged, wo_ref[...])
    x1_ref[...] = x1
    sh2 = mod_ref[0, 3:4, :]
    sc2 = mod_ref[0, 4:5, :]
    h2 = (_rms(x1) * n2g_ref[...]) * (1.0 + sc2) + sh2
    h2_ref[...] = h2
    packed = _pack_halves(h2)
    for j in range(h2p_ref.shape[0]):
        h2p_ref[j] = packed[:, j * LANES:(j + 1) * LANES]


def _mix(p_uv, p_gate, y_a, xf, mod3, seq, ln_g, ln_b, ws, bs, w_out_a, w_out_b, w_out, norm2_g):
    t, d = xf.shape
    tm = min(MIX_TILE, seq)
    per_seq = seq // tm
    row = lambda a: a.reshape(1, -1)
    bsm = jnp.repeat(bs.T, GMLP_GROUP_DIM, axis=1)
    consts = (row(ln_g), row(ln_b), ws, bsm, w_out_a.astype(BF16), w_out_b.astype(BF16),
              w_out.astype(BF16), row(norm2_g))
    full = lambda a: pl.BlockSpec(a.shape, lambda i: (0,) * a.ndim)
    tile = lambda a: pl.BlockSpec((tm, a.shape[1]), lambda i: (i, 0))
    return pl.pallas_call(
        _mix_kernel,
        grid=(t // tm,),
        in_specs=[tile(p_uv), tile(p_gate), tile(y_a), tile(xf),
                  pl.BlockSpec((1,) + mod3.shape[1:], lambda i: (i // per_seq, 0, 0))]
                 + [full(a) for a in consts],
        out_specs=[pl.BlockSpec((tm, d), lambda i: (i, 0)), pl.BlockSpec((tm, d), lambda i: (i, 0)),
                   pl.BlockSpec((d // 2 // LANES, tm, LANES), lambda i: (0, i, 0))],
        out_shape=[jax.ShapeDtypeStruct((t, d), F32), jax.ShapeDtypeStruct((t, d), F32),
                   jax.ShapeDtypeStruct((d // 2 // LANES, t, LANES), jnp.uint32)],
        scratch_shapes=[pltpu.VMEM((tm, GMLP_WIDTH), F32)],
        compiler_params=_params("arbitrary"),
        name="mix",
    )(p_uv, p_gate, y_a, xf, mod3, *consts)


def _first_argmax(vals, idx, big):
    m = jnp.max(vals, axis=0, keepdims=True)
    first = jnp.min(jnp.where(vals == m, idx, big), axis=0, keepdims=True)
    return m, first


def _route_kernel(h_ref, rwt_ref, bias_ref, eidx_ref, wts_ref):
    tm = h_ref.shape[0]
    ne = N_EXPERTS
    hhi, hmid, _ = _split3(h_ref[...])
    whi, wmid, _ = _split3(rwt_ref[...])
    nt = lambda p, q: lax.dot_general(p, q, (((1,), (1,)), ((), ())), preferred_element_type=F32)
    logits = nt(whi, hhi) + (nt(whi, hmid) + nt(wmid, hhi))
    scores = jax.nn.sigmoid(logits)
    sel = scores + bias_ref[...]
    neg = -jnp.inf

    gs = GROUP_SIZE
    gidx = lax.broadcasted_iota(jnp.int32, (gs, tm), 0)
    grp_rows = []
    for gi in range(N_EXPERT_GROUPS):
        blk = sel[gi * gs:(gi + 1) * gs, :]
        m1, i1 = _first_argmax(blk, gidx, gs)
        m2 = jnp.max(jnp.where(gidx == i1, neg, blk), axis=0, keepdims=True)
        grp_rows.append(m1 + m2)
    grp = jnp.concatenate(grp_rows, axis=0)

    ng = N_EXPERT_GROUPS
    giota = lax.broadcasted_iota(jnp.int32, (ng, tm), 0)
    gsel = jnp.zeros((ng, tm), jnp.bool_)
    work = grp
    for _ in range(TOPK_GROUPS):
        _, gi1 = _first_argmax(work, giota, ng)
        hit = giota == gi1
        gsel = gsel | hit
        work = jnp.where(hit, neg, work)
    gself = jnp.where(gsel, 1.0, 0.0)
    emask = jnp.concatenate(
        [jnp.broadcast_to(gself[gi:gi + 1, :], (gs, tm)) for gi in range(ng)], axis=0) > 0.5

    eiota = lax.broadcasted_iota(jnp.int32, (ne, tm), 0)
    work = jnp.where(emask, sel, neg)
    idx_rows, w_rows = [], []
    for _ in range(TOP_K):
        _, e1 = _first_argmax(work, eiota, ne)
        hit = eiota == e1
        idx_rows.append(e1)
        w_rows.append(jnp.sum(jnp.where(hit, scores, 0.0), axis=0, keepdims=True))
        work = jnp.where(hit, neg, work)
    eidx_ref[...] = jnp.concatenate(idx_rows, axis=0)
    w = jnp.concatenate(w_rows, axis=0)
    wts_ref[...] = w / jnp.sum(w, axis=0, keepdims=True) * ROUTED_SCALE


def _route(h2, router_w, router_bias):
    t, d = h2.shape
    tm = min(ROUTE_TILE, t)
    rwt = router_w.T
    bias = router_bias.reshape(N_EXPERTS, 1)
    return pl.pallas_call(
        _route_kernel,
        grid=(t // tm,),
        in_specs=[pl.BlockSpec((tm, d), lambda i: (i, 0)),
                  pl.BlockSpec(rwt.shape, lambda i: (0, 0)),
                  pl.BlockSpec(bias.shape, lambda i: (0, 0))],
        out_specs=[pl.BlockSpec((TOP_K, tm), lambda i: (0, i)), pl.BlockSpec((TOP_K, tm), lambda i: (0, i))],
        out_shape=[jax.ShapeDtypeStruct((TOP_K, t), jnp.int32), jax.ShapeDtypeStruct((TOP_K, t), F32)],
        compiler_params=_params("arbitrary"),
        name="route",
    )(h2, rwt, bias)


def _rank_kernel(e_ref, upper_ref, rank_ref, count_ref):
    tm = e_ref.shape[1]
    ne = N_EXPERTS

    @pl.when(pl.program_id(0) == 0)
    def _():
        count_ref[...] = jnp.zeros_like(count_ref)

    e = e_ref[...]
    eiota = lax.broadcasted_iota(jnp.int32, (ne, tm), 0)
    onehot = jnp.zeros((ne, tm), F32)
    for kk in range(TOP_K):
        onehot = onehot + jnp.where(eiota == e[kk:kk + 1, :], 1.0, 0.0)
    before = jnp.dot(onehot.astype(BF16), upper_ref[...], preferred_element_type=F32)
    base = count_ref[:, 0:1]
    tot = before + base
    rows = [jnp.sum(jnp.where(eiota == e[kk:kk + 1, :], tot, 0.0), axis=0, keepdims=True)
            for kk in range(TOP_K)]
    rank_ref[...] = jnp.concatenate(rows, axis=0).astype(jnp.int32)
    count_ref[...] = count_ref[...] + jnp.sum(onehot, axis=1, keepdims=True)


def _rank(eidx):
    k, t = eidx.shape
    tm = min(RANK_TILE, t)
    upper = jnp.triu(jnp.ones((tm, tm), F32), 1).astype(BF16)
    return pl.pallas_call(
        _rank_kernel,
        grid=(t // tm,),
        in_specs=[pl.BlockSpec((k, tm), lambda i: (0, i)), pl.BlockSpec((tm, tm), lambda i: (0, 0))],
        out_specs=[pl.BlockSpec((k, tm), lambda i: (0, i)), pl.BlockSpec((N_EXPERTS, 128), lambda i: (0, 0))],
        out_shape=[jax.ShapeDtypeStruct((k, t), jnp.int32), jax.ShapeDtypeStruct((N_EXPERTS, 128), F32)],
        compiler_params=_params("arbitrary"),
        name="rank",
    )(eidx, upper)


def _dest_kernel(e_ref, rank_ref, pst_ref, dest_ref):
    tm = e_ref.shape[1]
    e = e_ref[...]
    eiota = lax.broadcasted_iota(jnp.int32, (N_EXPERTS, tm), 0)
    pst = pst_ref[...]
    rows = [jnp.sum(jnp.where(eiota == e[kk:kk + 1, :], pst, 0.0), axis=0, keepdims=True)
            for kk in range(TOP_K)]
    dest_ref[...] = jnp.concatenate(rows, axis=0).astype(jnp.int32) + rank_ref[...]


def _dest(eidx, rank, pstarts):
    k, t = eidx.shape
    tm = min(RANK_TILE, t)
    pst = pstarts.astype(F32).reshape(N_EXPERTS, 1)
    return pl.pallas_call(
        _dest_kernel,
        grid=(t // tm,),
        in_specs=[pl.BlockSpec((k, tm), lambda i: (0, i)), pl.BlockSpec((k, tm), lambda i: (0, i)),
                  pl.BlockSpec((N_EXPERTS, 1), lambda i: (0, 0))],
        out_specs=pl.BlockSpec((k, tm), lambda i: (0, i)),
        out_shape=jax.ShapeDtypeStruct((k, t), jnp.int32),
        compiler_params=_params("arbitrary"),
        name="dest",
    )(eidx, rank, pst)


def _sc_scatter(rows, idx, n_out):
    n, lanes = rows.shape
    nstep, copies, w = idx.shape
    mesh = plsc.VectorSubcoreMesh(core_axis_name="core", subcore_axis_name="subcore")

    @functools.partial(pl.kernel, out_type=jax.ShapeDtypeStruct((n_out, lanes), rows.dtype), mesh=mesh,
                       scratch_types=[])
    def scatter_kernel(rows_hbm, idx_hbm, out_hbm):
        def body(rows_vmem, idx_vmem):
            for j in range(copies):
                pltpu.sync_copy(rows_vmem, out_hbm.at[idx_vmem.at[j]])

        pltpu.emit_pipeline(
            body,
            grid=(nstep,),
            in_specs=[pl.BlockSpec((w, lanes), index_map=lambda i: (i, 0)),
                      pl.BlockSpec((copies, w), index_map=lambda i: (i, 0))],
            out_specs=[],
            core_axis_name=("core", "subcore"),
            dimension_semantics=(pltpu.PARALLEL,),
        )(rows_hbm, idx_hbm)

    return scatter_kernel(rows, idx.reshape(nstep * copies, w))


def _dispatch(h2p, dest_tk, n_rows):
    nchunk, t, lanes = h2p.shape
    k = dest_tk.shape[0]
    w = SC_WINDOW
    idx = dest_tk.reshape(k, t // w, w).transpose(1, 0, 2)[None]
    idx = idx + (jnp.arange(nchunk, dtype=jnp.int32) * n_rows)[:, None, None, None]
    out = _sc_scatter(h2p.reshape(nchunk * t, lanes), idx.reshape(nchunk * (t // w), k, w), nchunk * n_rows)
    return out.reshape(nchunk, n_rows, lanes)


def _expert_kernel(first_ref, nblk_ref, nv_ref, rows_ref, x_hbm, w1_ref, w3_ref, w2_ref, y_hbm,
                   xbuf, ybuf, xsem, ysem, w1b, w3b, w2b):
    e = pl.program_id(0)
    nchunk, rb = xbuf.shape[1], xbuf.shape[2]
    half = nchunk * LANES
    total = nv_ref[0]

    def x_copy(g, slot):
        return pltpu.make_async_copy(x_hbm.at[:, pl.ds(g * rb, rb), :], xbuf.at[slot], xsem.at[slot])

    def y_copy(g, slot):
        return pltpu.make_async_copy(ybuf.at[slot], y_hbm.at[:, pl.ds(g * rb, rb), :], ysem.at[slot])

    @pl.when(e == 0)
    def _():
        x_copy(0, 0).start()

    @pl.when(nblk_ref[e] > 0)
    def _():
        w1b[...] = w1_ref[0].astype(BF16)
        w3b[...] = w3_ref[0].astype(BF16)
        w2b[...] = w2_ref[0].astype(BF16)

    def block(j, carry):
        g = first_ref[e] + j
        slot = g % 2
        x_copy(g, slot).wait()

        @pl.when(g + 1 < total)
        def _():
            x_copy(g + 1, 1 - slot).start()

        live = lax.broadcasted_iota(jnp.int32, (rb, 1), 0) < rows_ref[g]
        x = jnp.concatenate([xbuf[slot, c] for c in range(nchunk)], axis=1)
        hi, lo = _unpack_halves(jnp.where(live, x, jnp.uint32(0)))
        xa = hi.astype(BF16)
        xb = lo.astype(BF16)
        dot = lambda p, q: jnp.dot(p, q, preferred_element_type=F32)
        h1 = dot(xa, w1b[:half, :]) + dot(xb, w1b[half:, :])
        h3 = dot(xa, w3b[:half, :]) + dot(xb, w3b[half:, :])
        hid = _silu(h1) * h3
        packed = _pack_halves(dot(hid.astype(BF16), w2b[...]))

        @pl.when(g >= 2)
        def _():
            y_copy(g - 2, slot).wait()

        for c in range(nchunk):
            ybuf[slot, c] = packed[:, c * LANES:(c + 1) * LANES]
        y_copy(g, slot).start()
        return carry

    lax.fori_loop(0, nblk_ref[e], block, 0)

    @pl.when(e == pl.num_programs(0) - 1)
    def _():
        for back in (2, 1):
            @pl.when(total >= back)
            def _():
                y_copy(total - back, (total - back) % 2).wait()


def _expert(x_sorted, first_blk, nblk_e, nvalid, block_rows, w1, w3, w2):
    nchunk, n_rows, lanes = x_sorted.shape
    rb = ROW_BLOCK
    ne, d, f = w1.shape
    grid_spec = pltpu.PrefetchScalarGridSpec(
        num_scalar_prefetch=4,
        grid=(ne,),
        in_specs=[
            pl.BlockSpec(memory_space=pl.ANY),
            pl.BlockSpec((1, d, f), lambda e, *_: (e, 0, 0)),
            pl.BlockSpec((1, d, f), lambda e, *_: (e, 0, 0)),
            pl.BlockSpec((1, f, d), lambda e, *_: (e, 0, 0)),
        ],
        out_specs=pl.BlockSpec(memory_space=pl.ANY),
        scratch_shapes=[pltpu.VMEM((2, nchunk, rb, lanes), jnp.uint32),
                        pltpu.VMEM((2, nchunk, rb, lanes), jnp.uint32),
                        pltpu.SemaphoreType.DMA((2,)), pltpu.SemaphoreType.DMA((2,)),
                        pltpu.VMEM((d, f), BF16), pltpu.VMEM((d, f), BF16), pltpu.VMEM((f, d), BF16)],
    )
    return pl.pallas_call(
        _expert_kernel,
        grid_spec=grid_spec,
        out_shape=jax.ShapeDtypeStruct((nchunk, n_rows, lanes), jnp.uint32),
        compiler_params=_params("arbitrary"),
        name="expert",
    )(first_blk, nblk_e, nvalid, block_rows, x_sorted, w1, w3, w2)


def _sc_gather(table, idx):
    n = idx.shape[0]
    mesh = plsc.VectorSubcoreMesh(core_axis_name="core", subcore_axis_name="subcore")

    @functools.partial(pl.kernel, out_type=jax.ShapeDtypeStruct((n, table.shape[1]), table.dtype), mesh=mesh)
    def gather_kernel(table_hbm, idx_hbm, out_hbm):
        def body(idx_vmem, out_vmem):
            pltpu.sync_copy(table_hbm.at[idx_vmem.at[0]], out_vmem)

        pltpu.emit_pipeline(
            body,
            grid=(n // SC_WINDOW,),
            in_specs=[pl.BlockSpec((1, SC_WINDOW), index_map=lambda i: (0, i))],
            out_specs=[pl.BlockSpec((SC_WINDOW, table.shape[1]), index_map=lambda i: (i, 0))],
            core_axis_name=("core", "subcore"),
            dimension_semantics=(pltpu.PARALLEL,),
        )(idx_hbm, out_hbm)

    return gather_kernel(table, idx.reshape(1, n))


def _gather_expert_rows(y_sorted, dest_tk):
    nchunk, n_rows, lanes = y_sorted.shape
    k, t = dest_tk.shape
    idx = dest_tk[None, :, :] + (jnp.arange(nchunk, dtype=jnp.int32) * n_rows)[:, None, None]
    rows = _sc_gather(y_sorted.reshape(nchunk * n_rows, lanes), idx.reshape(-1))
    return rows.reshape(nchunk, k, t, lanes)


def _combine_kernel(yg_ref, wts_ref, h_ref, x1_ref, mod_ref, sw1_ref, sw3_ref, sw2_ref, nfg_ref, o_ref):
    nchunk = yg_ref.shape[0]
    w = wts_ref[...]
    acc_hi = [None] * nchunk
    acc_lo = [None] * nchunk
    for kk in range(TOP_K):
        wk = w[:, kk:kk + 1]
        for j in range(nchunk):
            hi, lo = _unpack_halves(yg_ref[j, kk])
            acc_hi[j] = hi * wk if kk == 0 else acc_hi[j] + hi * wk
            acc_lo[j] = lo * wk if kk == 0 else acc_lo[j] + lo * wk
    routed = jnp.concatenate(acc_hi + acc_lo, axis=1)
    hb = h_ref[...].astype(BF16)
    hid = _silu(jnp.dot(hb, sw1_ref[...], preferred_element_type=F32)) * jnp.dot(
        hb, sw3_ref[...], preferred_element_type=F32)
    shared = jnp.dot(hid.astype(BF16), sw2_ref[...], preferred_element_type=F32)
    g2 = mod_ref[0, 5:6, :]
    x2 = x1_ref[...] + g2 * (routed + shared)
    o_ref[...] = _rms(x2) * nfg_ref[...]


def _combine(y_gathered, wts_tk, h2, x1, mod3, seq, sw1, sw3, sw2, normf_g):
    t, d = h2.shape
    tm = min(COMBINE_TILE, seq)
    per_seq = seq // tm
    nchunk, k, _, lanes = y_gathered.shape
    consts = (sw1.astype(BF16), sw3.astype(BF16), sw2.astype(BF16), normf_g.reshape(1, d))
    full = lambda a: pl.BlockSpec(a.shape, lambda i: (0,) * a.ndim)
    tile = lambda a: pl.BlockSpec((tm, a.shape[1]), lambda i: (i, 0))
    return pl.pallas_call(
        _combine_kernel,
        grid=(t // tm,),
        in_specs=[pl.BlockSpec((nchunk, k, tm, lanes), lambda i: (0, 0, i, 0)),
                  tile(wts_tk), tile(h2), tile(x1),
                  pl.BlockSpec((1,) + mod3.shape[1:], lambda i: (i // per_seq, 0, 0))]
                 + [full(a) for a in consts],
        out_specs=pl.BlockSpec((tm, d), lambda i: (i, 0)),
        out_shape=jax.ShapeDtypeStruct((t, d), F32),
        compiler_params=_params("arbitrary"),
        name="combine",
    )(y_gathered, wts_tk, h2, x1, mod3, *consts)


def _dispatch_plan(counts, t):
    rb = ROW_BLOCK
    padded = (counts + rb - 1) // rb * rb
    pends = jnp.cumsum(padded)
    pstarts = pends - padded
    n_rows = t * TOP_K + N_EXPERTS * rb
    block_start = jnp.arange(n_rows // rb, dtype=jnp.int32) * rb
    block_e = jnp.sum((pends[None, :] <= block_start[:, None]).astype(jnp.int32), axis=1)
    block_e = jnp.minimum(block_e, N_EXPERTS - 1)
    block_rows = jnp.clip(pstarts[block_e] + counts[block_e] - block_start, 0, rb).astype(jnp.int32)
    nvalid = (pends[-1:] // rb).astype(jnp.int32)
    return pstarts, pstarts // rb, padded // rb, block_rows, nvalid, n_rows


def kernel(x, c, ada_w, ada_b, norm1_g, norm2_g, w_in, tshift_mu, rwkv_w0, rwkv_w_up, rwkv_a0, rwkv_a_up,
           rwkv_g_up, rwkv_k_k, rwkv_k_a, rwkv_r_k, rwkv_ln_g, rwkv_ln_b, gmlp_ln_g, gmlp_ln_b, gmlp_ws,
           gmlp_bs, w_out_a, w_out_b, w_out, router_w, router_bias, exp_w1, exp_w3, exp_w2, shared_w1,
           shared_w3, shared_w2, normf_g):
    bsz, seq, d = x.shape
    t = bsz * seq
    xf = x.reshape(t, d)
    assert ada_w.shape[0] == 1, "single-layer block only"
    l = 0
    mod3 = _mod(c, ada_w[l], ada_b[l]).reshape(bsz, 6, d)
    p_rwkv, p_uv, p_gate = _inproj(xf, mod3, norm1_g[l], w_in[l], seq)
    y_a = _rwkv(p_rwkv, bsz, seq, tshift_mu[l], rwkv_w0[l], rwkv_w_up[l], rwkv_a0[l], rwkv_a_up[l],
                rwkv_g_up[l], rwkv_k_k[l], rwkv_k_a[l], rwkv_r_k[l], rwkv_ln_g[l], rwkv_ln_b[l])
    x1, h2, h2p = _mix(p_uv, p_gate, y_a, xf, mod3, seq, gmlp_ln_g[l], gmlp_ln_b[l], gmlp_ws[l], gmlp_bs[l],
                  w_out_a[l], w_out_b[l], w_out[l], norm2_g[l])
    eidx, wts = _route(h2, router_w[l], router_bias[l])
    rank, counts = _rank(eidx)
    counts = counts[:, 0].astype(jnp.int32)
    pstarts, first_blk, nblk_e, block_rows, nvalid, n_rows = _dispatch_plan(counts, t)
    dest = _dest(eidx, rank, pstarts)
    x_sorted = _dispatch(h2p, dest, n_rows)
    y_sorted = _expert(x_sorted, first_blk, nblk_e, nvalid, block_rows, exp_w1[l], exp_w3[l], exp_w2[l])
    y_gathered = _gather_expert_rows(y_sorted, dest)
    out = _combine(y_gathered, wts.T, h2, x1, mod3, seq, shared_w1[l], shared_w3[l], shared_w2[l], normf_g)
    return out.reshape(bsz, seq, d)
```

```python
import functools

import jax
import jax.numpy as jnp
from jax import lax
from jax.experimental import pallas as pl
from jax.experimental.pallas import tpu as pltpu
from jax.experimental.pallas import tpu_sc as plsc

F32 = jnp.float32
BF16 = jnp.bfloat16

RWKV_HEAD = 64
RWKV_HEADS = 8
RWKV_WIDTH = RWKV_HEAD * RWKV_HEADS
DECAY_LORA = 64
AAA_LORA = 64
GATE_LORA = 128
RWKV_COLS = 3 * RWKV_WIDTH + DECAY_LORA + AAA_LORA + GATE_LORA
GMLP_WIDTH = 512
GMLP_BLOCK = 128
GMLP_GROUPS = 8
GMLP_GROUP_DIM = GMLP_WIDTH // GMLP_GROUPS
N_EXPERTS = 256
TOP_K = 8
N_EXPERT_GROUPS = 8
GROUP_SIZE = N_EXPERTS // N_EXPERT_GROUPS
TOPK_GROUPS = 4
ROUTED_SCALE = 2.5
RMS_EPS = 1e-6
LN_EPS = 1e-5
GN_EPS = 64e-5

VMEM_LIMIT_BYTES = 56 * 1024 * 1024

CHUNK = 64
RWKV_TILE = 256
ROW_BLOCK = 256
EXPERT_RING = 4
INPROJ_TILE = 256
MIX_TILE = 256
ROUTE_TILE = 512
RANK_TILE = 512
COMBINE_TILE = 256
LANES = 128
SC_WINDOW = 128


def _params(*sem):
    return pltpu.CompilerParams(dimension_semantics=sem, vmem_limit_bytes=VMEM_LIMIT_BYTES)


def _mm(a, b):
    return jnp.dot(a.astype(BF16), b.astype(BF16), preferred_element_type=F32)


def _mm_nt(a, b):
    return lax.dot_general(a.astype(BF16), b.astype(BF16), (((1,), (1,)), ((), ())),
                           preferred_element_type=F32)


def _mm_tn(a, b):
    return lax.dot_general(a.astype(BF16), b.astype(BF16), (((0,), (0,)), ((), ())),
                           preferred_element_type=F32)


def _split3(a):
    hi = a.astype(BF16)
    r1 = a - hi.astype(F32)
    mid = r1.astype(BF16)
    lo = (r1 - mid.astype(F32)).astype(BF16)
    return hi, mid, lo


def _silu(x):
    return x * jax.nn.sigmoid(x)


def _rms(x):
    return x * lax.rsqrt(jnp.mean(x * x, axis=-1, keepdims=True) + RMS_EPS)


def _pack_halves(x):
    n = x.shape[1] // 2
    bits = lax.bitcast_convert_type(x.astype(BF16).astype(F32), jnp.uint32)
    return bits[:, :n] | (bits[:, n:] >> 16)


def _unpack_halves(u):
    hi = lax.bitcast_convert_type(u & jnp.uint32(0xFFFF0000), F32)
    lo = lax.bitcast_convert_type(u << 16, F32)
    return hi, lo


def _mod_kernel(c_ref, w_ref, b_ref, o_ref):
    s = _silu(c_ref[...])
    hi, mid, lo = _split3(s)
    whi, wmid, wlo = _split3(w_ref[...])
    dot = lambda p, q: jnp.dot(p, q, preferred_element_type=F32)
    acc = dot(hi, whi) + (dot(hi, wmid) + dot(mid, whi)) + (dot(hi, wlo) + dot(mid, wmid) + dot(lo, whi))
    o_ref[...] = acc + b_ref[...]


def _mod(c, ada_w, ada_b):
    bsz, d = c.shape
    n = ada_w.shape[1]
    tn = d
    return pl.pallas_call(
        _mod_kernel,
        grid=(n // tn,),
        in_specs=[pl.BlockSpec((bsz, d), lambda j: (0, 0)),
                  pl.BlockSpec((d, tn), lambda j: (0, j)),
                  pl.BlockSpec((1, tn), lambda j: (0, j))],
        out_specs=pl.BlockSpec((bsz, tn), lambda j: (0, j)),
        out_shape=jax.ShapeDtypeStruct((bsz, n), F32),
        compiler_params=_params("arbitrary"),
        name="mod",
    )(c, ada_w, ada_b.reshape(1, n))


def _inproj_kernel(x_ref, mod_ref, g_ref, wr_ref, wuv_ref, wg_ref, pr_ref, puv_ref, pg_ref):
    sh = mod_ref[0, 0:1, :]
    sc = mod_ref[0, 1:2, :]
    h = (_rms(x_ref[...]) * g_ref[...]) * (1.0 + sc) + sh
    hb = h.astype(BF16)
    pr_ref[...] = jnp.dot(hb, wr_ref[...], preferred_element_type=F32).astype(pr_ref.dtype)
    puv_ref[...] = jnp.dot(hb, wuv_ref[...], preferred_element_type=F32).astype(puv_ref.dtype)
    pg_ref[...] = jnp.dot(hb, wg_ref[...], preferred_element_type=F32).astype(pg_ref.dtype)


def _inproj(xf, mod3, norm1_g, w_in, seq):
    t, d = xf.shape
    tm = min(INPROJ_TILE, seq)
    per_seq = seq // tm
    wr = w_in[:, :RWKV_COLS].astype(BF16)
    wuv = w_in[:, RWKV_COLS:RWKV_COLS + 2 * GMLP_WIDTH].astype(BF16)
    wg = w_in[:, RWKV_COLS + 2 * GMLP_WIDTH:].astype(BF16)
    full = lambda a: pl.BlockSpec(a.shape, lambda i: (0,) * a.ndim)
    g = norm1_g.reshape(1, d)
    return pl.pallas_call(
        _inproj_kernel,
        grid=(t // tm,),
        in_specs=[pl.BlockSpec((tm, d), lambda i: (i, 0)),
                  pl.BlockSpec((1,) + mod3.shape[1:], lambda i: (i // per_seq, 0, 0)),
                  full(g), full(wr), full(wuv), full(wg)],
        out_specs=[pl.BlockSpec((tm, wr.shape[1]), lambda i: (i, 0)),
                   pl.BlockSpec((tm, wuv.shape[1]), lambda i: (i, 0)),
                   pl.BlockSpec((tm, wg.shape[1]), lambda i: (i, 0))],
        out_shape=[jax.ShapeDtypeStruct((t, wr.shape[1]), BF16),
                   jax.ShapeDtypeStruct((t, wuv.shape[1]), BF16),
                   jax.ShapeDtypeStruct((t, wg.shape[1]), BF16)],
        compiler_params=_params("arbitrary"),
        name="inproj",
    )(xf, mod3, g, wr, wuv, wg)


def _unit_lower_inverses(ns, row, col):
    c = ns[0].shape[0]
    eye = jnp.where(row == col, 1.0, 0.0).astype(F32)
    blk8 = (row // 8) == (col // 8)
    n8 = [jnp.where(blk8, n, 0.0) for n in ns]
    n8_2 = [_mm(a, a) for a in n8]
    n8_4 = [_mm(a, a) for a in n8_2]
    inv = [eye + a for a in n8]
    inv = [i + _mm(i, b) for i, b in zip(inv, n8_2)]
    inv = [i + _mm(i, b) for i, b in zip(inv, n8_4)]
    s = 8
    while s < c:
        sel = ((row // (2 * s)) == (col // (2 * s))) & ((row // s) != (col // s))
        low = [jnp.where(sel, n, 0.0) for n in ns]
        t1 = [_mm(l, i) for l, i in zip(low, inv)]
        inv = [i + _mm(i, t) for i, t in zip(inv, t1)]
        s *= 2
    return inv


def _rwkv_kernel(p_ref, mu_ref, w0_ref, wup_ref, a0_ref, aup_ref, gup_ref, kk_ref, ka_ref, rk_ref,
                 lng_ref, lnb_ref, bd_ref, y_ref, state_ref, carry_ref, ycat_ref, *, c):
    tm = p_ref.shape[0]
    nq = tm // c
    hd = RWKV_HEAD
    nh = RWKV_HEADS

    @pl.when(pl.program_id(1) == 0)
    def _():
        state_ref[...] = jnp.zeros_like(state_ref)
        carry_ref[...] = jnp.zeros_like(carry_ref)

    p = p_ref[...].astype(F32)
    rowc = lax.broadcasted_iota(jnp.int32, (tm, 1), 0)
    prev = jnp.where(rowc == 0, carry_ref[...], pltpu.roll(p, 1, axis=0))
    carry_ref[...] = p[tm - 1:tm, :]
    ps = p + (prev - p) * mu_ref[...]

    o = RWKV_WIDTH
    r = ps[:, 0:o]
    k = ps[:, o:2 * o]
    v = ps[:, 2 * o:3 * o]
    xw = ps[:, 3 * o:3 * o + DECAY_LORA]
    xa = ps[:, 3 * o + DECAY_LORA:3 * o + DECAY_LORA + AAA_LORA]
    xg = ps[:, 3 * o + DECAY_LORA + AAA_LORA:]

    z = -(w0_ref[...] + _mm(jnp.tanh(xw), wup_ref[...]))
    softplus = jnp.maximum(z, 0.0) + jnp.log1p(jnp.exp(-jnp.abs(z)))
    wlog = -softplus - 0.5
    ld = -jnp.exp(wlog)
    a = jax.nn.sigmoid(a0_ref[...] + _mm(xa, aup_ref[...]))
    g = _mm(jax.nn.sigmoid(xg), gup_ref[...])

    def head_sum(x):
        hi = x.astype(BF16)
        lo = (x - hi.astype(F32)).astype(BF16)
        bd = bd_ref[...]
        return jnp.dot(hi, bd, preferred_element_type=F32) + jnp.dot(lo, bd, preferred_element_type=F32)

    kkf = k * kk_ref[...]
    kkn = kkf * (1.0 / jnp.maximum(jnp.sqrt(head_sum(kkf * kkf)), 1e-12))
    k2 = k * (1.0 + (a - 1.0) * ka_ref[...])
    avec = -kkn
    bvec = kkn * a

    rowt = lax.broadcasted_iota(jnp.int32, (tm, tm), 0)
    colt = lax.broadcasted_iota(jnp.int32, (tm, tm), 1)
    tri = jnp.where((rowt >= colt) & ((rowt // c) == (colt // c)), 1.0, 0.0).astype(BF16)
    hi, mid, lo = _split3(ld)
    dot = lambda q: jnp.dot(tri, q, preferred_element_type=F32)
    cl = dot(hi) + dot(mid) + dot(lo)
    cl_end = jnp.concatenate(
        [jnp.broadcast_to(cl[(q + 1) * c - 1:(q + 1) * c, :], (c, o)) for q in range(nq)], axis=0)
    g_inv = jnp.exp(-cl)
    g_tail = jnp.exp(cl_end - cl)
    g_end = jnp.exp(cl_end)
    rt = (r * jnp.exp(cl)).astype(BF16)
    at = (avec * jnp.exp(cl - ld)).astype(BF16)
    kt = (k2 * g_inv).astype(BF16)
    bt = (bvec * g_inv).astype(BF16)
    kend = (k2 * g_tail).astype(BF16)
    bend = (bvec * g_tail).astype(BF16)
    vb = v.astype(BF16)

    row = lax.broadcasted_iota(jnp.int32, (c, c), 0)
    col = lax.broadcasted_iota(jnp.int32, (c, c), 1)
    strict = row > col
    incl = row >= col
    items = [(q, h) for q in range(nq) for h in range(nh)]
    blk = lambda x, q, h: x[q * c:(q + 1) * c, h * hd:(h + 1) * hd]

    lhs = [jnp.concatenate([blk(at, q, h), blk(rt, q, h)], axis=0) for q, h in items]
    mat_b = [_mm_nt(l, blk(bt, q, h)) for l, (q, h) in zip(lhs, items)]
    mat_k = [_mm_nt(l, blk(kt, q, h)) for l, (q, h) in zip(lhs, items)]
    a_ab = [jnp.where(strict, m[:c], 0.0) for m in mat_b]
    a_rb = [jnp.where(incl, m[c:], 0.0).astype(BF16) for m in mat_b]
    a_ak = [jnp.where(strict, m[:c], 0.0) for m in mat_k]
    a_rk = [jnp.where(incl, m[c:], 0.0) for m in mat_k]
    tinv = _unit_lower_inverses(a_ab, row, col)
    akv = [_mm(m, blk(vb, q, h)) for m, (q, h) in zip(a_ak, items)]
    y_loc = [_mm(m, blk(vb, q, h)) for m, (q, h) in zip(a_rk, items)]
    s_loc = [_mm_tn(blk(vb, q, h), blk(kend, q, h)) for q, h in items]
    wmat = [_mm(t, blk(at, q, h)).astype(BF16) for t, (q, h) in zip(tinv, items)]
    ut = [_mm(t, m) for t, m in zip(tinv, akv)]

    state = [state_ref[h] for h in range(nh)]
    for q in range(nq):
        base = q * nh
        sb = [s.astype(BF16) for s in state]
        u = [_mm_nt(wmat[base + h], sb[h]) + ut[base + h] for h in range(nh)]
        yh = [_mm_nt(blk(rt, q, h), sb[h]) + y_loc[base + h] for h in range(nh)]
        ub = [x.astype(BF16) for x in u]
        yh = [y0 + _mm(a_rb[base + h], ub[h]) for h, y0 in enumerate(yh)]
        state = [state[h] * g_end[q * c:q * c + 1, h * hd:(h + 1) * hd]
                 + _mm_tn(ub[h], blk(bend, q, h)) + s_loc[base + h] for h in range(nh)]
        for h in range(nh):
            ycat_ref[q * c:(q + 1) * c, h * hd:(h + 1) * hd] = yh[h]
    for h in range(nh):
        state_ref[h] = state[h]

    y = ycat_ref[...]
    dev = y - head_sum(y) * (1.0 / hd)
    rstd = lax.rsqrt(head_sum(dev * dev) * (1.0 / hd) + GN_EPS)
    bonus = head_sum(r * k2 * rk_ref[...])
    yn = dev * rstd * lng_ref[...] + lnb_ref[...]
    y_ref[...] = (yn + bonus * v) * g


def _rwkv(p_rwkv, bsz, seq, tshift_mu, w0, w_up, a0, a_up, g_up, k_k, k_a, r_k, ln_g, ln_b):
    t = p_rwkv.shape[0]
    c = min(CHUNK, seq)
    tm = min(RWKV_TILE, seq)
    nstep = seq // tm
    row = lambda a: a.reshape(1, -1)
    head_of = jnp.arange(RWKV_WIDTH, dtype=jnp.int32) // RWKV_HEAD
    block_diag = (head_of[:, None] == head_of[None, :]).astype(BF16)
    args = (row(tshift_mu), row(w0), w_up, row(a0), a_up, g_up, row(k_k), row(k_a), row(r_k),
            row(ln_g), row(ln_b), block_diag)
    full = lambda a: pl.BlockSpec(a.shape, lambda b, j: (0,) * a.ndim)
    return pl.pallas_call(
        functools.partial(_rwkv_kernel, c=c),
        grid=(bsz, nstep),
        in_specs=[pl.BlockSpec((tm, RWKV_COLS), lambda b, j: (b * nstep + j, 0))] + [full(a) for a in args],
        out_specs=pl.BlockSpec((tm, RWKV_WIDTH), lambda b, j: (b * nstep + j, 0)),
        out_shape=jax.ShapeDtypeStruct((t, RWKV_WIDTH), F32),
        scratch_shapes=[pltpu.VMEM((RWKV_HEADS, RWKV_HEAD, RWKV_HEAD), F32),
                        pltpu.VMEM((1, RWKV_COLS), F32),
                        pltpu.VMEM((tm, RWKV_WIDTH), F32)],
        compiler_params=_params("arbitrary", "arbitrary"),
        name="rwkv",
    )(p_rwkv, *args)


def _mix_kernel(puv_ref, pg_ref, ya_ref, x_ref, mod_ref, lng_ref, lnb_ref, ws_ref, bsm_ref,
                woa_ref, wob_ref, wo_ref, n2g_ref, x1_ref, h2_ref, h2p_ref, yb_ref):
    tm = x_ref.shape[0]
    gw = GMLP_WIDTH
    puv = puv_ref[...].astype(F32)
    u = jax.nn.gelu(puv[:, :gw])
    vg = jax.nn.gelu(puv[:, gw:])
    mu = jnp.mean(vg, axis=-1, keepdims=True)
    var = jnp.mean(jnp.square(vg - mu), axis=-1, keepdims=True)
    vln = (vg - mu) * lax.rsqrt(var + LN_EPS) * lng_ref[...] + lnb_ref[...]

    row = lax.broadcasted_iota(jnp.int32, (GMLP_BLOCK, GMLP_BLOCK), 0)
    col = lax.broadcasted_iota(jnp.int32, (GMLP_BLOCK, GMLP_BLOCK), 1)
    gd = GMLP_GROUP_DIM
    for gi in range(GMLP_GROUPS):
        wsm = jnp.where(row >= col, ws_ref[gi], 0.0).astype(BF16)
        for nb in range(tm // GMLP_BLOCK):
            rs = slice(nb * GMLP_BLOCK, (nb + 1) * GMLP_BLOCK)
            cs = slice(gi * gd, (gi + 1) * gd)
            sv = jnp.dot(wsm, vln[rs, cs].astype(BF16), preferred_element_type=F32) + bsm_ref[:, cs]
            yb_ref[rs, cs] = u[rs, cs] * sv

    pg = pg_ref[...].astype(F32)
    d = x_ref.shape[1]
    gate_a = jax.nn.sigmoid(pg[:, :d])
    gate_b = jax.nn.sigmoid(pg[:, d:])
    merged = gate_a * _mm(ya_ref[...], woa_ref[...]) + gate_b * _mm(yb_ref[...], wob_ref[...])
    g1 = mod_ref[0, 2:3, :]
    x1 = x_ref[...] + g1 * _mm(merged, wo_ref[...])
    x1_ref[...] = x1
    sh2 = mod_ref[0, 3:4, :]
    sc2 = mod_ref[0, 4:5, :]
    h2 = (_rms(x1) * n2g_ref[...]) * (1.0 + sc2) + sh2
    h2_ref[...] = h2
    packed = _pack_halves(h2)
    for j in range(h2p_ref.shape[0]):
        h2p_ref[j] = packed[:, j * LANES:(j + 1) * LANES]


def _mix(p_uv, p_gate, y_a, xf, mod3, seq, ln_g, ln_b, ws, bs, w_out_a, w_out_b, w_out, norm2_g):
    t, d = xf.shape
    tm = min(MIX_TILE, seq)
    per_seq = seq // tm
    row = lambda a: a.reshape(1, -1)
    bsm = jnp.repeat(bs.T, GMLP_GROUP_DIM, axis=1)
    consts = (row(ln_g), row(ln_b), ws, bsm, w_out_a.astype(BF16), w_out_b.astype(BF16),
              w_out.astype(BF16), row(norm2_g))
    full = lambda a: pl.BlockSpec(a.shape, lambda i: (0,) * a.ndim)
    tile = lambda a: pl.BlockSpec((tm, a.shape[1]), lambda i: (i, 0))
    return pl.pallas_call(
        _mix_kernel,
        grid=(t // tm,),
        in_specs=[tile(p_uv), tile(p_gate), tile(y_a), tile(xf),
                  pl.BlockSpec((1,) + mod3.shape[1:], lambda i: (i // per_seq, 0, 0))]
                 + [full(a) for a in consts],
        out_specs=[pl.BlockSpec((tm, d), lambda i: (i, 0)), pl.BlockSpec((tm, d), lambda i: (i, 0)),
                   pl.BlockSpec((d // 2 // LANES, tm, LANES), lambda i: (0, i, 0))],
        out_shape=[jax.ShapeDtypeStruct((t, d), F32), jax.ShapeDtypeStruct((t, d), F32),
                   jax.ShapeDtypeStruct((d // 2 // LANES, t, LANES), jnp.uint32)],
        scratch_shapes=[pltpu.VMEM((tm, GMLP_WIDTH), F32)],
        compiler_params=_params("arbitrary"),
        name="mix",
    )(p_uv, p_gate, y_a, xf, mod3, *consts)


def _first_argmax(vals, idx, big):
    m = jnp.max(vals, axis=0, keepdims=True)
    first = jnp.min(jnp.where(vals == m, idx, big), axis=0, keepdims=True)
    return m, first


def _route_kernel(h_ref, rwt_ref, bias_ref, eidx_ref, wts_ref):
    tm = h_ref.shape[0]
    ne = N_EXPERTS
    hhi, hmid, _ = _split3(h_ref[...])
    whi, wmid, _ = _split3(rwt_ref[...])
    nt = lambda p, q: lax.dot_general(p, q, (((1,), (1,)), ((), ())), preferred_element_type=F32)
    logits = nt(whi, hhi) + (nt(whi, hmid) + nt(wmid, hhi))
    scores = jax.nn.sigmoid(logits)
    sel = scores + bias_ref[...]
    neg = -jnp.inf

    gs = GROUP_SIZE
    gidx = lax.broadcasted_iota(jnp.int32, (gs, tm), 0)
    grp_rows = []
    for gi in range(N_EXPERT_GROUPS):
        blk = sel[gi * gs:(gi + 1) * gs, :]
        m1, i1 = _first_argmax(blk, gidx, gs)
        m2 = jnp.max(jnp.where(gidx == i1, neg, blk), axis=0, keepdims=True)
        grp_rows.append(m1 + m2)
    grp = jnp.concatenate(grp_rows, axis=0)

    ng = N_EXPERT_GROUPS
    giota = lax.broadcasted_iota(jnp.int32, (ng, tm), 0)
    gsel = jnp.zeros((ng, tm), jnp.bool_)
    work = grp
    for _ in range(TOPK_GROUPS):
        _, gi1 = _first_argmax(work, giota, ng)
        hit = giota == gi1
        gsel = gsel | hit
        work = jnp.where(hit, neg, work)
    gself = jnp.where(gsel, 1.0, 0.0)
    emask = jnp.concatenate(
        [jnp.broadcast_to(gself[gi:gi + 1, :], (gs, tm)) for gi in range(ng)], axis=0) > 0.5

    eiota = lax.broadcasted_iota(jnp.int32, (ne, tm), 0)
    work = jnp.where(emask, sel, neg)
    idx_rows, w_rows = [], []
    for _ in range(TOP_K):
        _, e1 = _first_argmax(work, eiota, ne)
        hit = eiota == e1
        idx_rows.append(e1)
        w_rows.append(jnp.sum(jnp.where(hit, scores, 0.0), axis=0, keepdims=True))
        work = jnp.where(hit, neg, work)
    eidx_ref[...] = jnp.concatenate(idx_rows, axis=0)
    w = jnp.concatenate(w_rows, axis=0)
    wts_ref[...] = w / jnp.sum(w, axis=0, keepdims=True) * ROUTED_SCALE


def _route(h2, router_w, router_bias):
    t, d = h2.shape
    tm = min(ROUTE_TILE, t)
    rwt = router_w.T
    bias = router_bias.reshape(N_EXPERTS, 1)
    return pl.pallas_call(
        _route_kernel,
        grid=(t // tm,),
        in_specs=[pl.BlockSpec((tm, d), lambda i: (i, 0)),
                  pl.BlockSpec(rwt.shape, lambda i: (0, 0)),
                  pl.BlockSpec(bias.shape, lambda i: (0, 0))],
        out_specs=[pl.BlockSpec((TOP_K, tm), lambda i: (0, i)), pl.BlockSpec((TOP_K, tm), lambda i: (0, i))],
        out_shape=[jax.ShapeDtypeStruct((TOP_K, t), jnp.int32), jax.ShapeDtypeStruct((TOP_K, t), F32)],
        compiler_params=_params("arbitrary"),
        name="route",
    )(h2, rwt, bias)


def _rank_kernel(e_ref, upper_ref, rank_ref, count_ref):
    tm = e_ref.shape[1]
    ne = N_EXPERTS

    @pl.when(pl.program_id(0) == 0)
    def _():
        count_ref[...] = jnp.zeros_like(count_ref)

    e = e_ref[...]
    eiota = lax.broadcasted_iota(jnp.int32, (ne, tm), 0)
    onehot = jnp.zeros((ne, tm), F32)
    for kk in range(TOP_K):
        onehot = onehot + jnp.where(eiota == e[kk:kk + 1, :], 1.0, 0.0)
    before = jnp.dot(onehot.astype(BF16), upper_ref[...], preferred_element_type=F32)
    base = count_ref[:, 0:1]
    tot = before + base
    rows = [jnp.sum(jnp.where(eiota == e[kk:kk + 1, :], tot, 0.0), axis=0, keepdims=True)
            for kk in range(TOP_K)]
    rank_ref[...] = jnp.concatenate(rows, axis=0).astype(jnp.int32)
    count_ref[...] = count_ref[...] + jnp.sum(onehot, axis=1, keepdims=True)


def _rank(eidx):
    k, t = eidx.shape
    tm = min(RANK_TILE, t)
    upper = jnp.triu(jnp.ones((tm, tm), F32), 1).astype(BF16)
    return pl.pallas_call(
        _rank_kernel,
        grid=(t // tm,),
        in_specs=[pl.BlockSpec((k, tm), lambda i: (0, i)), pl.BlockSpec((tm, tm), lambda i: (0, 0))],
        out_specs=[pl.BlockSpec((k, tm), lambda i: (0, i)), pl.BlockSpec((N_EXPERTS, 128), lambda i: (0, 0))],
        out_shape=[jax.ShapeDtypeStruct((k, t), jnp.int32), jax.ShapeDtypeStruct((N_EXPERTS, 128), F32)],
        compiler_params=_params("arbitrary"),
        name="rank",
    )(eidx, upper)


def _dest_kernel(e_ref, rank_ref, pst_ref, dest_ref):
    tm = e_ref.shape[1]
    e = e_ref[...]
    eiota = lax.broadcasted_iota(jnp.int32, (N_EXPERTS, tm), 0)
    pst = pst_ref[...]
    rows = [jnp.sum(jnp.where(eiota == e[kk:kk + 1, :], pst, 0.0), axis=0, keepdims=True)
            for kk in range(TOP_K)]
    dest_ref[...] = jnp.concatenate(rows, axis=0).astype(jnp.int32) + rank_ref[...]


def _dest(eidx, rank, pstarts):
    k, t = eidx.shape
    tm = min(RANK_TILE, t)
    pst = pstarts.astype(F32).reshape(N_EXPERTS, 1)
    return pl.pallas_call(
        _dest_kernel,
        grid=(t // tm,),
        in_specs=[pl.BlockSpec((k, tm), lambda i: (0, i)), pl.BlockSpec((k, tm), lambda i: (0, i)),
                  pl.BlockSpec((N_EXPERTS, 1), lambda i: (0, 0))],
        out_specs=pl.BlockSpec((k, tm), lambda i: (0, i)),
        out_shape=jax.ShapeDtypeStruct((k, t), jnp.int32),
        compiler_params=_params("arbitrary"),
        name="dest",
    )(eidx, rank, pst)


def _sc_scatter(rows, idx, n_out):
    n, lanes = rows.shape
    nstep, copies, w = idx.shape
    mesh = plsc.VectorSubcoreMesh(core_axis_name="core", subcore_axis_name="subcore")

    @functools.partial(pl.kernel, out_type=jax.ShapeDtypeStruct((n_out, lanes), rows.dtype), mesh=mesh,
                       scratch_types=[])
    def scatter_kernel(rows_hbm, idx_hbm, out_hbm):
        def body(rows_vmem, idx_vmem):
            for j in range(copies):
                pltpu.sync_copy(rows_vmem, out_hbm.at[idx_vmem.at[j]])

        pltpu.emit_pipeline(
            body,
            grid=(nstep,),
            in_specs=[pl.BlockSpec((w, lanes), index_map=lambda i: (i, 0)),
                      pl.BlockSpec((copies, w), index_map=lambda i: (i, 0))],
            out_specs=[],
            core_axis_name=("core", "subcore"),
            dimension_semantics=(pltpu.PARALLEL,),
        )(rows_hbm, idx_hbm)

    return scatter_kernel(rows, idx.reshape(nstep * copies, w))


def _dispatch(h2p, dest_tk, n_rows):
    nchunk, t, lanes = h2p.shape
    k = dest_tk.shape[0]
    w = SC_WINDOW
    idx = dest_tk.reshape(k, t // w, w).transpose(1, 0, 2)[None]
    idx = idx + (jnp.arange(nchunk, dtype=jnp.int32) * n_rows)[:, None, None, None]
    out = _sc_scatter(h2p.reshape(nchunk * t, lanes), idx.reshape(nchunk * (t // w), k, w), nchunk * n_rows)
    return out.reshape(nchunk, n_rows, lanes)


def _expert_kernel(first_ref, nblk_ref, nv_ref, rows_ref, x_hbm, w1_ref, w3_ref, w2_ref, y_hbm,
                   xbuf, ybuf, xsem, ysem, w1b, w3b, w2b):
    e = pl.program_id(0)
    ring, nchunk, rb = xbuf.shape[0], xbuf.shape[1], xbuf.shape[2]
    half = nchunk * LANES
    total = nv_ref[0]

    def x_copy(g, slot):
        return pltpu.make_async_copy(x_hbm.at[:, pl.ds(g * rb, rb), :], xbuf.at[slot], xsem.at[slot])

    def y_copy(g, slot):
        return pltpu.make_async_copy(ybuf.at[slot], y_hbm.at[:, pl.ds(g * rb, rb), :], ysem.at[slot])

    @pl.when(e == 0)
    def _():
        for g0 in range(ring - 1):
            @pl.when(g0 < total)
            def _():
                x_copy(g0, g0).start()

    @pl.when(nblk_ref[e] > 0)
    def _():
        w1b[...] = w1_ref[0].astype(BF16)
        w3b[...] = w3_ref[0].astype(BF16)
        w2b[...] = w2_ref[0].astype(BF16)

    def block(j, carry):
        g = first_ref[e] + j
        slot = g % ring
        x_copy(g, slot).wait()

        @pl.when(g + ring - 1 < total)
        def _():
            x_copy(g + ring - 1, (g + ring - 1) % ring).start()

        live = lax.broadcasted_iota(jnp.int32, (rb, 1), 0) < rows_ref[g]
        x = jnp.concatenate([xbuf[slot, c] for c in range(nchunk)], axis=1)
        hi, lo = _unpack_halves(jnp.where(live, x, jnp.uint32(0)))
        xa = hi.astype(BF16)
        xb = lo.astype(BF16)
        dot = lambda p, q: jnp.dot(p, q, preferred_element_type=F32)
        h1 = dot(xa, w1b[:half, :]) + dot(xb, w1b[half:, :])
        h3 = dot(xa, w3b[:half, :]) + dot(xb, w3b[half:, :])
        hid = _silu(h1) * h3
        packed = _pack_halves(dot(hid.astype(BF16), w2b[...]))

        @pl.when(g >= ring)
        def _():
            y_copy(g - ring, slot).wait()

        for c in range(nchunk):
            ybuf[slot, c] = packed[:, c * LANES:(c + 1) * LANES]
        y_copy(g, slot).start()
        return carry

    lax.fori_loop(0, nblk_ref[e], block, 0)

    @pl.when(e == pl.num_programs(0) - 1)
    def _():
        for back in range(ring, 0, -1):
            @pl.when(total >= back)
            def _():
                y_copy(total - back, (total - back) % ring).wait()


def _expert(x_sorted, first_blk, nblk_e, nvalid, block_rows, w1, w3, w2):
    nchunk, n_rows, lanes = x_sorted.shape
    rb = ROW_BLOCK
    ne, d, f = w1.shape
    grid_spec = pltpu.PrefetchScalarGridSpec(
        num_scalar_prefetch=4,
        grid=(ne,),
        in_specs=[
            pl.BlockSpec(memory_space=pl.ANY),
            pl.BlockSpec((1, d, f), lambda e, *_: (e, 0, 0)),
            pl.BlockSpec((1, d, f), lambda e, *_: (e, 0, 0)),
            pl.BlockSpec((1, f, d), lambda e, *_: (e, 0, 0)),
        ],
        out_specs=pl.BlockSpec(memory_space=pl.ANY),
        scratch_shapes=[pltpu.VMEM((EXPERT_RING, nchunk, rb, lanes), jnp.uint32),
                        pltpu.VMEM((EXPERT_RING, nchunk, rb, lanes), jnp.uint32),
                        pltpu.SemaphoreType.DMA((EXPERT_RING,)), pltpu.SemaphoreType.DMA((EXPERT_RING,)),
                        pltpu.VMEM((d, f), BF16), pltpu.VMEM((d, f), BF16), pltpu.VMEM((f, d), BF16)],
    )
    return pl.pallas_call(
        _expert_kernel,
        grid_spec=grid_spec,
        out_shape=jax.ShapeDtypeStruct((nchunk, n_rows, lanes), jnp.uint32),
        compiler_params=_params("arbitrary"),
        name="expert",
    )(first_blk, nblk_e, nvalid, block_rows, x_sorted, w1, w3, w2)


def _sc_gather(table, idx):
    n = idx.shape[0]
    mesh = plsc.VectorSubcoreMesh(core_axis_name="core", subcore_axis_name="subcore")

    @functools.partial(pl.kernel, out_type=jax.ShapeDtypeStruct((n, table.shape[1]), table.dtype), mesh=mesh)
    def gather_kernel(table_hbm, idx_hbm, out_hbm):
        def body(idx_vmem, out_vmem):
            pltpu.sync_copy(table_hbm.at[idx_vmem.at[0]], out_vmem)

        pltpu.emit_pipeline(
            body,
            grid=(n // SC_WINDOW,),
            in_specs=[pl.BlockSpec((1, SC_WINDOW), index_map=lambda i: (0, i))],
            out_specs=[pl.BlockSpec((SC_WINDOW, table.shape[1]), index_map=lambda i: (i, 0))],
            core_axis_name=("core", "subcore"),
            dimension_semantics=(pltpu.PARALLEL,),
        )(idx_hbm, out_hbm)

    return gather_kernel(table, idx.reshape(1, n))


def _gather_expert_rows(y_sorted, dest_tk):
    nchunk, n_rows, lanes = y_sorted.shape
    k, t = dest_tk.shape
    idx = dest_tk[None, :, :] + (jnp.arange(nchunk, dtype=jnp.int32) * n_rows)[:, None, None]
    rows = _sc_gather(y_sorted.reshape(nchunk * n_rows, lanes), idx.reshape(-1))
    return rows.reshape(nchunk, k, t, lanes)


def _combine_kernel(yg_ref, wts_ref, h_ref, x1_ref, mod_ref, sw1_ref, sw3_ref, sw2_ref, nfg_ref, o_ref):
    nchunk = yg_ref.shape[0]
    w = wts_ref[...]
    acc_hi = [None] * nchunk
    acc_lo = [None] * nchunk
    for kk in range(TOP_K):
        wk = w[:, kk:kk + 1]
        for j in range(nchunk):
            hi, lo = _unpack_halves(yg_ref[j, kk])
            acc_hi[j] = hi * wk if kk == 0 else acc_hi[j] + hi * wk
            acc_lo[j] = lo * wk if kk == 0 else acc_lo[j] + lo * wk
    routed = jnp.concatenate(acc_hi + acc_lo, axis=1)
    hb = h_ref[...].astype(BF16)
    hid = _silu(jnp.dot(hb, sw1_ref[...], preferred_element_type=F32)) * jnp.dot(
        hb, sw3_ref[...], preferred_element_type=F32)
    shared = jnp.dot(hid.astype(BF16), sw2_ref[...], preferred_element_type=F32)
    g2 = mod_ref[0, 5:6, :]
    x2 = x1_ref[...] + g2 * (routed + shared)
    o_ref[...] = _rms(x2) * nfg_ref[...]


def _combine(y_gathered, wts_tk, h2, x1, mod3, seq, sw1, sw3, sw2, normf_g):
    t, d = h2.shape
    tm = min(COMBINE_TILE, seq)
    per_seq = seq // tm
    nchunk, k, _, lanes = y_gathered.shape
    consts = (sw1.astype(BF16), sw3.astype(BF16), sw2.astype(BF16), normf_g.reshape(1, d))
    full = lambda a: pl.BlockSpec(a.shape, lambda i: (0,) * a.ndim)
    tile = lambda a: pl.BlockSpec((tm, a.shape[1]), lambda i: (i, 0))
    return pl.pallas_call(
        _combine_kernel,
        grid=(t // tm,),
        in_specs=[pl.BlockSpec((nchunk, k, tm, lanes), lambda i: (0, 0, i, 0)),
                  tile(wts_tk), tile(h2), tile(x1),
                  pl.BlockSpec((1,) + mod3.shape[1:], lambda i: (i // per_seq, 0, 0))]
                 + [full(a) for a in consts],
        out_specs=pl.BlockSpec((tm, d), lambda i: (i, 0)),
        out_shape=jax.ShapeDtypeStruct((t, d), F32),
        compiler_params=_params("arbitrary"),
        name="combine",
    )(y_gathered, wts_tk, h2, x1, mod3, *consts)


def _dispatch_plan(counts, t):
    rb = ROW_BLOCK
    padded = (counts + rb - 1) // rb * rb
    pends = jnp.cumsum(padded)
    pstarts = pends - padded
    n_rows = t * TOP_K + N_EXPERTS * rb
    block_start = jnp.arange(n_rows // rb, dtype=jnp.int32) * rb
    block_e = jnp.sum((pends[None, :] <= block_start[:, None]).astype(jnp.int32), axis=1)
    block_e = jnp.minimum(block_e, N_EXPERTS - 1)
    block_rows = jnp.clip(pstarts[block_e] + counts[block_e] - block_start, 0, rb).astype(jnp.int32)
    nvalid = (pends[-1:] // rb).astype(jnp.int32)
    return pstarts, pstarts // rb, padded // rb, block_rows, nvalid, n_rows


def kernel(x, c, ada_w, ada_b, norm1_g, norm2_g, w_in, tshift_mu, rwkv_w0, rwkv_w_up, rwkv_a0, rwkv_a_up,
           rwkv_g_up, rwkv_k_k, rwkv_k_a, rwkv_r_k, rwkv_ln_g, rwkv_ln_b, gmlp_ln_g, gmlp_ln_b, gmlp_ws,
           gmlp_bs, w_out_a, w_out_b, w_out, router_w, router_bias, exp_w1, exp_w3, exp_w2, shared_w1,
           shared_w3, shared_w2, normf_g):
    bsz, seq, d = x.shape
    t = bsz * seq
    xf = x.reshape(t, d)
    assert ada_w.shape[0] == 1, "single-layer block only"
    l = 0
    mod3 = _mod(c, ada_w[l], ada_b[l]).reshape(bsz, 6, d)
    p_rwkv, p_uv, p_gate = _inproj(xf, mod3, norm1_g[l], w_in[l], seq)
    y_a = _rwkv(p_rwkv, bsz, seq, tshift_mu[l], rwkv_w0[l], rwkv_w_up[l], rwkv_a0[l], rwkv_a_up[l],
                rwkv_g_up[l], rwkv_k_k[l], rwkv_k_a[l], rwkv_r_k[l], rwkv_ln_g[l], rwkv_ln_b[l])
    x1, h2, h2p = _mix(p_uv, p_gate, y_a, xf, mod3, seq, gmlp_ln_g[l], gmlp_ln_b[l], gmlp_ws[l], gmlp_bs[l],
                  w_out_a[l], w_out_b[l], w_out[l], norm2_g[l])
    eidx, wts = _route(h2, router_w[l], router_bias[l])
    rank, counts = _rank(eidx)
    counts = counts[:, 0].astype(jnp.int32)
    pstarts, first_blk, nblk_e, block_rows, nvalid, n_rows = _dispatch_plan(counts, t)
    dest = _dest(eidx, rank, pstarts)
    x_sorted = _dispatch(h2p, dest, n_rows)
    y_sorted = _expert(x_sorted, first_blk, nblk_e, nvalid, block_rows, exp_w1[l], exp_w3[l], exp_w2[l])
    y_gathered = _gather_expert_rows(y_sorted, dest)
    out = _combine(y_gathered, wts.T, h2, x1, mod3, seq, shared_w1[l], shared_w3[l], shared_w2[l], normf_g)
    return out.reshape(bsz, seq, d)
```

```python
import functools

import jax
import jax.numpy as jnp
from jax import lax
from jax.experimental import pallas as pl
from jax.experimental.pallas import tpu as pltpu
from jax.experimental.pallas import tpu_sc as plsc

F32 = jnp.float32
BF16 = jnp.bfloat16

RWKV_HEAD = 64
RWKV_HEADS = 8
RWKV_WIDTH = RWKV_HEAD * RWKV_HEADS
DECAY_LORA = 64
AAA_LORA = 64
GATE_LORA = 128
RWKV_COLS = 3 * RWKV_WIDTH + DECAY_LORA + AAA_LORA + GATE_LORA
GMLP_WIDTH = 512
GMLP_BLOCK = 128
GMLP_GROUPS = 8
GMLP_GROUP_DIM = GMLP_WIDTH // GMLP_GROUPS
N_EXPERTS = 256
TOP_K = 8
N_EXPERT_GROUPS = 8
GROUP_SIZE = N_EXPERTS // N_EXPERT_GROUPS
TOPK_GROUPS = 4
ROUTED_SCALE = 2.5
RMS_EPS = 1e-6
LN_EPS = 1e-5
GN_EPS = 64e-5

VMEM_LIMIT_BYTES = 56 * 1024 * 1024

CHUNK = 64
RWKV_TILE = 256
ROW_BLOCK = 256
EXPERT_RING = 4
INPROJ_TILE = 256
MIX_TILE = 256
ROUTE_TILE = 512
RANK_TILE = 512
COMBINE_TILE = 256
LANES = 128
SC_WINDOW = 128
SC_GATHERS_PER_STEP = 2


def _params(*sem):
    return pltpu.CompilerParams(dimension_semantics=sem, vmem_limit_bytes=VMEM_LIMIT_BYTES)


def _mm(a, b):
    return jnp.dot(a.astype(BF16), b.astype(BF16), preferred_element_type=F32)


def _mm_nt(a, b):
    return lax.dot_general(a.astype(BF16), b.astype(BF16), (((1,), (1,)), ((), ())),
                           preferred_element_type=F32)


def _mm_tn(a, b):
    return lax.dot_general(a.astype(BF16), b.astype(BF16), (((0,), (0,)), ((), ())),
                           preferred_element_type=F32)


def _split3(a):
    hi = a.astype(BF16)
    r1 = a - hi.astype(F32)
    mid = r1.astype(BF16)
    lo = (r1 - mid.astype(F32)).astype(BF16)
    return hi, mid, lo


def _silu(x):
    return x * jax.nn.sigmoid(x)


def _rms(x):
    return x * lax.rsqrt(jnp.mean(x * x, axis=-1, keepdims=True) + RMS_EPS)


def _pack_halves(x):
    n = x.shape[1] // 2
    bits = lax.bitcast_convert_type(x.astype(BF16).astype(F32), jnp.uint32)
    return bits[:, :n] | (bits[:, n:] >> 16)


def _unpack_halves(u):
    hi = lax.bitcast_convert_type(u & jnp.uint32(0xFFFF0000), F32)
    lo = lax.bitcast_convert_type(u << 16, F32)
    return hi, lo


def _mod_kernel(c_ref, w_ref, b_ref, o_ref):
    s = _silu(c_ref[...])
    hi, mid, lo = _split3(s)
    whi, wmid, wlo = _split3(w_ref[...])
    dot = lambda p, q: jnp.dot(p, q, preferred_element_type=F32)
    acc = dot(hi, whi) + (dot(hi, wmid) + dot(mid, whi)) + (dot(hi, wlo) + dot(mid, wmid) + dot(lo, whi))
    o_ref[...] = acc + b_ref[...]


def _mod(c, ada_w, ada_b):
    bsz, d = c.shape
    n = ada_w.shape[1]
    tn = d
    return pl.pallas_call(
        _mod_kernel,
        grid=(n // tn,),
        in_specs=[pl.BlockSpec((bsz, d), lambda j: (0, 0)),
                  pl.BlockSpec((d, tn), lambda j: (0, j)),
                  pl.BlockSpec((1, tn), lambda j: (0, j))],
        out_specs=pl.BlockSpec((bsz, tn), lambda j: (0, j)),
        out_shape=jax.ShapeDtypeStruct((bsz, n), F32),
        compiler_params=_params("arbitrary"),
        name="mod",
    )(c, ada_w, ada_b.reshape(1, n))


def _inproj_kernel(x_ref, mod_ref, g_ref, wr_ref, wuv_ref, wg_ref, pr_ref, puv_ref, pg_ref):
    sh = mod_ref[0, 0:1, :]
    sc = mod_ref[0, 1:2, :]
    h = (_rms(x_ref[...]) * g_ref[...]) * (1.0 + sc) + sh
    hb = h.astype(BF16)
    pr_ref[...] = jnp.dot(hb, wr_ref[...], preferred_element_type=F32).astype(pr_ref.dtype)
    puv_ref[...] = jnp.dot(hb, wuv_ref[...], preferred_element_type=F32).astype(puv_ref.dtype)
    pg_ref[...] = jnp.dot(hb, wg_ref[...], preferred_element_type=F32).astype(pg_ref.dtype)


def _inproj(xf, mod3, norm1_g, w_in, seq):
    t, d = xf.shape
    tm = min(INPROJ_TILE, seq)
    per_seq = seq // tm
    wr = w_in[:, :RWKV_COLS].astype(BF16)
    wuv = w_in[:, RWKV_COLS:RWKV_COLS + 2 * GMLP_WIDTH].astype(BF16)
    wg = w_in[:, RWKV_COLS + 2 * GMLP_WIDTH:].astype(BF16)
    full = lambda a: pl.BlockSpec(a.shape, lambda i: (0,) * a.ndim)
    g = norm1_g.reshape(1, d)
    return pl.pallas_call(
        _inproj_kernel,
        grid=(t // tm,),
        in_specs=[pl.BlockSpec((tm, d), lambda i: (i, 0)),
                  pl.BlockSpec((1,) + mod3.shape[1:], lambda i: (i // per_seq, 0, 0)),
                  full(g), full(wr), full(wuv), full(wg)],
        out_specs=[pl.BlockSpec((tm, wr.shape[1]), lambda i: (i, 0)),
                   pl.BlockSpec((tm, wuv.shape[1]), lambda i: (i, 0)),
                   pl.BlockSpec((tm, wg.shape[1]), lambda i: (i, 0))],
        out_shape=[jax.ShapeDtypeStruct((t, wr.shape[1]), BF16),
                   jax.ShapeDtypeStruct((t, wuv.shape[1]), BF16),
                   jax.ShapeDtypeStruct((t, wg.shape[1]), BF16)],
        compiler_params=_params("arbitrary"),
        name="inproj",
    )(xf, mod3, g, wr, wuv, wg)


def _unit_lower_inverses(ns, row, col):
    c = ns[0].shape[0]
    eye = jnp.where(row == col, 1.0, 0.0).astype(F32)
    blk8 = (row // 8) == (col // 8)
    n8 = [jnp.where(blk8, n, 0.0) for n in ns]
    n8_2 = [_mm(a, a) for a in n8]
    n8_4 = [_mm(a, a) for a in n8_2]
    inv = [eye + a for a in n8]
    inv = [i + _mm(i, b) for i, b in zip(inv, n8_2)]
    inv = [i + _mm(i, b) for i, b in zip(inv, n8_4)]
    s = 8
    while s < c:
        sel = ((row // (2 * s)) == (col // (2 * s))) & ((row // s) != (col // s))
        low = [jnp.where(sel, n, 0.0) for n in ns]
        t1 = [_mm(l, i) for l, i in zip(low, inv)]
        inv = [i + _mm(i, t) for i, t in zip(inv, t1)]
        s *= 2
    return inv


def _rwkv_kernel(p_ref, mu_ref, w0_ref, wup_ref, a0_ref, aup_ref, gup_ref, kk_ref, ka_ref, rk_ref,
                 lng_ref, lnb_ref, bd_ref, y_ref, state_ref, carry_ref, ycat_ref, *, c):
    tm = p_ref.shape[0]
    nq = tm // c
    hd = RWKV_HEAD
    nh = RWKV_HEADS

    @pl.when(pl.program_id(1) == 0)
    def _():
        state_ref[...] = jnp.zeros_like(state_ref)
        carry_ref[...] = jnp.zeros_like(carry_ref)

    p = p_ref[...].astype(F32)
    rowc = lax.broadcasted_iota(jnp.int32, (tm, 1), 0)
    prev = jnp.where(rowc == 0, carry_ref[...], pltpu.roll(p, 1, axis=0))
    carry_ref[...] = p[tm - 1:tm, :]
    ps = p + (prev - p) * mu_ref[...]

    o = RWKV_WIDTH
    r = ps[:, 0:o]
    k = ps[:, o:2 * o]
    v = ps[:, 2 * o:3 * o]
    xw = ps[:, 3 * o:3 * o + DECAY_LORA]
    xa = ps[:, 3 * o + DECAY_LORA:3 * o + DECAY_LORA + AAA_LORA]
    xg = ps[:, 3 * o + DECAY_LORA + AAA_LORA:]

    z = -(w0_ref[...] + _mm(jnp.tanh(xw), wup_ref[...]))
    softplus = jnp.maximum(z, 0.0) + jnp.log1p(jnp.exp(-jnp.abs(z)))
    wlog = -softplus - 0.5
    ld = -jnp.exp(wlog)
    a = jax.nn.sigmoid(a0_ref[...] + _mm(xa, aup_ref[...]))
    g = _mm(jax.nn.sigmoid(xg), gup_ref[...])

    def head_sum(x):
        hi = x.astype(BF16)
        lo = (x - hi.astype(F32)).astype(BF16)
        bd = bd_ref[...]
        return jnp.dot(hi, bd, preferred_element_type=F32) + jnp.dot(lo, bd, preferred_element_type=F32)

    kkf = k * kk_ref[...]
    kkn = kkf * (1.0 / jnp.maximum(jnp.sqrt(head_sum(kkf * kkf)), 1e-12))
    k2 = k * (1.0 + (a - 1.0) * ka_ref[...])
    avec = -kkn
    bvec = kkn * a

    rowt = lax.broadcasted_iota(jnp.int32, (tm, tm), 0)
    colt = lax.broadcasted_iota(jnp.int32, (tm, tm), 1)
    tri = jnp.where((rowt >= colt) & ((rowt // c) == (colt // c)), 1.0, 0.0).astype(BF16)
    hi, mid, lo = _split3(ld)
    dot = lambda q: jnp.dot(tri, q, preferred_element_type=F32)
    cl = dot(hi) + dot(mid) + dot(lo)
    cl_end = jnp.concatenate(
        [jnp.broadcast_to(cl[(q + 1) * c - 1:(q + 1) * c, :], (c, o)) for q in range(nq)], axis=0)
    g_inv = jnp.exp(-cl)
    g_tail = jnp.exp(cl_end - cl)
    g_end = jnp.exp(cl_end)
    rt = (r * jnp.exp(cl)).astype(BF16)
    at = (avec * jnp.exp(cl - ld)).astype(BF16)
    kt = (k2 * g_inv).astype(BF16)
    bt = (bvec * g_inv).astype(BF16)
    kend = (k2 * g_tail).astype(BF16)
    bend = (bvec * g_tail).astype(BF16)
    vb = v.astype(BF16)

    row = lax.broadcasted_iota(jnp.int32, (c, c), 0)
    col = lax.broadcasted_iota(jnp.int32, (c, c), 1)
    strict = row > col
    incl = row >= col
    items = [(q, h) for q in range(nq) for h in range(nh)]
    blk = lambda x, q, h: x[q * c:(q + 1) * c, h * hd:(h + 1) * hd]

    lhs = [jnp.concatenate([blk(at, q, h), blk(rt, q, h)], axis=0) for q, h in items]
    mat_b = [_mm_nt(l, blk(bt, q, h)) for l, (q, h) in zip(lhs, items)]
    mat_k = [_mm_nt(l, blk(kt, q, h)) for l, (q, h) in zip(lhs, items)]
    a_ab = [jnp.where(strict, m[:c], 0.0) for m in mat_b]
    a_rb = [jnp.where(incl, m[c:], 0.0).astype(BF16) for m in mat_b]
    a_ak = [jnp.where(strict, m[:c], 0.0) for m in mat_k]
    a_rk = [jnp.where(incl, m[c:], 0.0) for m in mat_k]
    tinv = _unit_lower_inverses(a_ab, row, col)
    akv = [_mm(m, blk(vb, q, h)) for m, (q, h) in zip(a_ak, items)]
    y_loc = [_mm(m, blk(vb, q, h)) for m, (q, h) in zip(a_rk, items)]
    s_loc = [_mm_tn(blk(vb, q, h), blk(kend, q, h)) for q, h in items]
    wmat = [_mm(t, blk(at, q, h)).astype(BF16) for t, (q, h) in zip(tinv, items)]
    ut = [_mm(t, m) for t, m in zip(tinv, akv)]

    state = [state_ref[h] for h in range(nh)]
    for q in range(nq):
        base = q * nh
        sb = [s.astype(BF16) for s in state]
        u = [_mm_nt(wmat[base + h], sb[h]) + ut[base + h] for h in range(nh)]
        yh = [_mm_nt(blk(rt, q, h), sb[h]) + y_loc[base + h] for h in range(nh)]
        ub = [x.astype(BF16) for x in u]
        yh = [y0 + _mm(a_rb[base + h], ub[h]) for h, y0 in enumerate(yh)]
        state = [state[h] * g_end[q * c:q * c + 1, h * hd:(h + 1) * hd]
                 + _mm_tn(ub[h], blk(bend, q, h)) + s_loc[base + h] for h in range(nh)]
        for h in range(nh):
            ycat_ref[q * c:(q + 1) * c, h * hd:(h + 1) * hd] = yh[h]
    for h in range(nh):
        state_ref[h] = state[h]

    y = ycat_ref[...]
    dev = y - head_sum(y) * (1.0 / hd)
    rstd = lax.rsqrt(head_sum(dev * dev) * (1.0 / hd) + GN_EPS)
    bonus = head_sum(r * k2 * rk_ref[...])
    yn = dev * rstd * lng_ref[...] + lnb_ref[...]
    y_ref[...] = (yn + bonus * v) * g


def _rwkv(p_rwkv, bsz, seq, tshift_mu, w0, w_up, a0, a_up, g_up, k_k, k_a, r_k, ln_g, ln_b):
    t = p_rwkv.shape[0]
    c = min(CHUNK, seq)
    tm = min(RWKV_TILE, seq)
    nstep = seq // tm
    row = lambda a: a.reshape(1, -1)
    head_of = jnp.arange(RWKV_WIDTH, dtype=jnp.int32) // RWKV_HEAD
    block_diag = (head_of[:, None] == head_of[None, :]).astype(BF16)
    args = (row(tshift_mu), row(w0), w_up, row(a0), a_up, g_up, row(k_k), row(k_a), row(r_k),
            row(ln_g), row(ln_b), block_diag)
    full = lambda a: pl.BlockSpec(a.shape, lambda b, j: (0,) * a.ndim)
    return pl.pallas_call(
        functools.partial(_rwkv_kernel, c=c),
        grid=(bsz, nstep),
        in_specs=[pl.BlockSpec((tm, RWKV_COLS), lambda b, j: (b * nstep + j, 0))] + [full(a) for a in args],
        out_specs=pl.BlockSpec((tm, RWKV_WIDTH), lambda b, j: (b * nstep + j, 0)),
        out_shape=jax.ShapeDtypeStruct((t, RWKV_WIDTH), F32),
        scratch_shapes=[pltpu.VMEM((RWKV_HEADS, RWKV_HEAD, RWKV_HEAD), F32),
                        pltpu.VMEM((1, RWKV_COLS), F32),
                        pltpu.VMEM((tm, RWKV_WIDTH), F32)],
        compiler_params=_params("arbitrary", "arbitrary"),
        name="rwkv",
    )(p_rwkv, *args)


def _mix_kernel(puv_ref, pg_ref, ya_ref, x_ref, mod_ref, lng_ref, lnb_ref, ws_ref, bsm_ref,
                woa_ref, wob_ref, wo_ref, n2g_ref, x1_ref, h2_ref, h2p_ref, yb_ref):
    tm = x_ref.shape[0]
    gw = GMLP_WIDTH
    puv = puv_ref[...].astype(F32)
    u = jax.nn.gelu(puv[:, :gw])
    vg = jax.nn.gelu(puv[:, gw:])
    mu = jnp.mean(vg, axis=-1, keepdims=True)
    var = jnp.mean(jnp.square(vg - mu), axis=-1, keepdims=True)
    vln = (vg - mu) * lax.rsqrt(var + LN_EPS) * lng_ref[...] + lnb_ref[...]

    row = lax.broadcasted_iota(jnp.int32, (GMLP_BLOCK, GMLP_BLOCK), 0)
    col = lax.broadcasted_iota(jnp.int32, (GMLP_BLOCK, GMLP_BLOCK), 1)
    gd = GMLP_GROUP_DIM
    for gi in range(GMLP_GROUPS):
        wsm = jnp.where(row >= col, ws_ref[gi], 0.0).astype(BF16)
        for nb in range(tm // GMLP_BLOCK):
            rs = slice(nb * GMLP_BLOCK, (nb + 1) * GMLP_BLOCK)
            cs = slice(gi * gd, (gi + 1) * gd)
            sv = jnp.dot(wsm, vln[rs, cs].astype(BF16), preferred_element_type=F32) + bsm_ref[:, cs]
            yb_ref[rs, cs] = u[rs, cs] * sv

    pg = pg_ref[...].astype(F32)
    d = x_ref.shape[1]
    gate_a = jax.nn.sigmoid(pg[:, :d])
    gate_b = jax.nn.sigmoid(pg[:, d:])
    merged = gate_a * _mm(ya_ref[...], woa_ref[...]) + gate_b * _mm(yb_ref[...], wob_ref[...])
    g1 = mod_ref[0, 2:3, :]
    x1 = x_ref[...] + g1 * _mm(merged, wo_ref[...])
    x1_ref[...] = x1
    sh2 = mod_ref[0, 3:4, :]
    sc2 = mod_ref[0, 4:5, :]
    h2 = (_rms(x1) * n2g_ref[...]) * (1.0 + sc2) + sh2
    h2_ref[...] = h2
    packed = _pack_halves(h2)
    for j in range(h2p_ref.shape[0]):
        h2p_ref[j] = packed[:, j * LANES:(j + 1) * LANES]


def _mix(p_uv, p_gate, y_a, xf, mod3, seq, ln_g, ln_b, ws, bs, w_out_a, w_out_b, w_out, norm2_g):
    t, d = xf.shape
    tm = min(MIX_TILE, seq)
    per_seq = seq // tm
    row = lambda a: a.reshape(1, -1)
    bsm = jnp.repeat(bs.T, GMLP_GROUP_DIM, axis=1)
    consts = (row(ln_g), row(ln_b), ws, bsm, w_out_a.astype(BF16), w_out_b.astype(BF16),
              w_out.astype(BF16), row(norm2_g))
    full = lambda a: pl.BlockSpec(a.shape, lambda i: (0,) * a.ndim)
    tile = lambda a: pl.BlockSpec((tm, a.shape[1]), lambda i: (i, 0))
    return pl.pallas_call(
        _mix_kernel,
        grid=(t // tm,),
        in_specs=[tile(p_uv), tile(p_gate), tile(y_a), tile(xf),
                  pl.BlockSpec((1,) + mod3.shape[1:], lambda i: (i // per_seq, 0, 0))]
                 + [full(a) for a in consts],
        out_specs=[pl.BlockSpec((tm, d), lambda i: (i, 0)), pl.BlockSpec((tm, d), lambda i: (i, 0)),
                   pl.BlockSpec((d // 2 // LANES, tm, LANES), lambda i: (0, i, 0))],
        out_shape=[jax.ShapeDtypeStruct((t, d), F32), jax.ShapeDtypeStruct((t, d), F32),
                   jax.ShapeDtypeStruct((d // 2 // LANES, t, LANES), jnp.uint32)],
        scratch_shapes=[pltpu.VMEM((tm, GMLP_WIDTH), F32)],
        compiler_params=_params("arbitrary"),
        name="mix",
    )(p_uv, p_gate, y_a, xf, mod3, *consts)


def _first_argmax(vals, idx, big):
    m = jnp.max(vals, axis=0, keepdims=True)
    first = jnp.min(jnp.where(vals == m, idx, big), axis=0, keepdims=True)
    return m, first


def _route_kernel(h_ref, rwt_ref, bias_ref, eidx_ref, wts_ref):
    tm = h_ref.shape[0]
    ne = N_EXPERTS
    hhi, hmid, _ = _split3(h_ref[...])
    whi, wmid, _ = _split3(rwt_ref[...])
    nt = lambda p, q: lax.dot_general(p, q, (((1,), (1,)), ((), ())), preferred_element_type=F32)
    logits = nt(whi, hhi) + (nt(whi, hmid) + nt(wmid, hhi))
    scores = jax.nn.sigmoid(logits)
    sel = scores + bias_ref[...]
    neg = -jnp.inf

    gs = GROUP_SIZE
    gidx = lax.broadcasted_iota(jnp.int32, (gs, tm), 0)
    grp_rows = []
    for gi in range(N_EXPERT_GROUPS):
        blk = sel[gi * gs:(gi + 1) * gs, :]
        m1, i1 = _first_argmax(blk, gidx, gs)
        m2 = jnp.max(jnp.where(gidx == i1, neg, blk), axis=0, keepdims=True)
        grp_rows.append(m1 + m2)
    grp = jnp.concatenate(grp_rows, axis=0)

    ng = N_EXPERT_GROUPS
    giota = lax.broadcasted_iota(jnp.int32, (ng, tm), 0)
    gsel = jnp.zeros((ng, tm), jnp.bool_)
    work = grp
    for _ in range(TOPK_GROUPS):
        _, gi1 = _first_argmax(work, giota, ng)
        hit = giota == gi1
        gsel = gsel | hit
        work = jnp.where(hit, neg, work)
    gself = jnp.where(gsel, 1.0, 0.0)
    emask = jnp.concatenate(
        [jnp.broadcast_to(gself[gi:gi + 1, :], (gs, tm)) for gi in range(ng)], axis=0) > 0.5

    eiota = lax.broadcasted_iota(jnp.int32, (ne, tm), 0)
    work = jnp.where(emask, sel, neg)
    idx_rows, w_rows = [], []
    for _ in range(TOP_K):
        _, e1 = _first_argmax(work, eiota, ne)
        hit = eiota == e1
        idx_rows.append(e1)
        w_rows.append(jnp.sum(jnp.where(hit, scores, 0.0), axis=0, keepdims=True))
        work = jnp.where(hit, neg, work)
    eidx_ref[...] = jnp.concatenate(idx_rows, axis=0)
    w = jnp.concatenate(w_rows, axis=0)
    wts_ref[...] = w / jnp.sum(w, axis=0, keepdims=True) * ROUTED_SCALE


def _route(h2, router_w, router_bias):
    t, d = h2.shape
    tm = min(ROUTE_TILE, t)
    rwt = router_w.T
    bias = router_bias.reshape(N_EXPERTS, 1)
    return pl.pallas_call(
        _route_kernel,
        grid=(t // tm,),
        in_specs=[pl.BlockSpec((tm, d), lambda i: (i, 0)),
                  pl.BlockSpec(rwt.shape, lambda i: (0, 0)),
                  pl.BlockSpec(bias.shape, lambda i: (0, 0))],
        out_specs=[pl.BlockSpec((TOP_K, tm), lambda i: (0, i)), pl.BlockSpec((TOP_K, tm), lambda i: (0, i))],
        out_shape=[jax.ShapeDtypeStruct((TOP_K, t), jnp.int32), jax.ShapeDtypeStruct((TOP_K, t), F32)],
        compiler_params=_params("arbitrary"),
        name="route",
    )(h2, rwt, bias)


def _rank_kernel(e_ref, upper_ref, rank_ref, count_ref):
    tm = e_ref.shape[1]
    ne = N_EXPERTS

    @pl.when(pl.program_id(0) == 0)
    def _():
        count_ref[...] = jnp.zeros_like(count_ref)

    e = e_ref[...]
    eiota = lax.broadcasted_iota(jnp.int32, (ne, tm), 0)
    onehot = jnp.zeros((ne, tm), F32)
    for kk in range(TOP_K):
        onehot = onehot + jnp.where(eiota == e[kk:kk + 1, :], 1.0, 0.0)
    before = jnp.dot(onehot.astype(BF16), upper_ref[...], preferred_element_type=F32)
    base = count_ref[:, 0:1]
    tot = before + base
    rows = [jnp.sum(jnp.where(eiota == e[kk:kk + 1, :], tot, 0.0), axis=0, keepdims=True)
            for kk in range(TOP_K)]
    rank_ref[...] = jnp.concatenate(rows, axis=0).astype(jnp.int32)
    count_ref[...] = count_ref[...] + jnp.sum(onehot, axis=1, keepdims=True)


def _rank(eidx):
    k, t = eidx.shape
    tm = min(RANK_TILE, t)
    upper = jnp.triu(jnp.ones((tm, tm), F32), 1).astype(BF16)
    return pl.pallas_call(
        _rank_kernel,
        grid=(t // tm,),
        in_specs=[pl.BlockSpec((k, tm), lambda i: (0, i)), pl.BlockSpec((tm, tm), lambda i: (0, 0))],
        out_specs=[pl.BlockSpec((k, tm), lambda i: (0, i)), pl.BlockSpec((N_EXPERTS, 128), lambda i: (0, 0))],
        out_shape=[jax.ShapeDtypeStruct((k, t), jnp.int32), jax.ShapeDtypeStruct((N_EXPERTS, 128), F32)],
        compiler_params=_params("arbitrary"),
        name="rank",
    )(eidx, upper)


def _dest_kernel(e_ref, rank_ref, pst_ref, dest_ref):
    tm = e_ref.shape[1]
    e = e_ref[...]
    eiota = lax.broadcasted_iota(jnp.int32, (N_EXPERTS, tm), 0)
    pst = pst_ref[...]
    rows = [jnp.sum(jnp.where(eiota == e[kk:kk + 1, :], pst, 0.0), axis=0, keepdims=True)
            for kk in range(TOP_K)]
    dest_ref[...] = jnp.concatenate(rows, axis=0).astype(jnp.int32) + rank_ref[...]


def _dest(eidx, rank, pstarts):
    k, t = eidx.shape
    tm = min(RANK_TILE, t)
    pst = pstarts.astype(F32).reshape(N_EXPERTS, 1)
    return pl.pallas_call(
        _dest_kernel,
        grid=(t // tm,),
        in_specs=[pl.BlockSpec((k, tm), lambda i: (0, i)), pl.BlockSpec((k, tm), lambda i: (0, i)),
                  pl.BlockSpec((N_EXPERTS, 1), lambda i: (0, 0))],
        out_specs=pl.BlockSpec((k, tm), lambda i: (0, i)),
        out_shape=jax.ShapeDtypeStruct((k, t), jnp.int32),
        compiler_params=_params("arbitrary"),
        name="dest",
    )(eidx, rank, pst)


def _sc_scatter(rows, idx, n_out):
    n, lanes = rows.shape
    nstep, copies, w = idx.shape
    mesh = plsc.VectorSubcoreMesh(core_axis_name="core", subcore_axis_name="subcore")

    @functools.partial(pl.kernel, out_type=jax.ShapeDtypeStruct((n_out, lanes), rows.dtype), mesh=mesh,
                       scratch_types=[pltpu.SemaphoreType.DMA])
    def scatter_kernel(rows_hbm, idx_hbm, out_hbm, sem):
        def body(rows_vmem, idx_vmem):
            cps = [pltpu.make_async_copy(rows_vmem, out_hbm.at[idx_vmem.at[j]], sem) for j in range(copies)]
            for cp in cps:
                cp.start()
            for cp in cps:
                cp.wait()

        pltpu.emit_pipeline(
            body,
            grid=(nstep,),
            in_specs=[pl.BlockSpec((w, lanes), index_map=lambda i: (i, 0)),
                      pl.BlockSpec((copies, w), index_map=lambda i: (i, 0))],
            out_specs=[],
            core_axis_name=("core", "subcore"),
            dimension_semantics=(pltpu.PARALLEL,),
        )(rows_hbm, idx_hbm)

    return scatter_kernel(rows, idx.reshape(nstep * copies, w))


def _dispatch(h2p, dest_tk, n_rows):
    nchunk, t, lanes = h2p.shape
    k = dest_tk.shape[0]
    w = SC_WINDOW
    idx = dest_tk.reshape(k, t // w, w).transpose(1, 0, 2)[None]
    idx = idx + (jnp.arange(nchunk, dtype=jnp.int32) * n_rows)[:, None, None, None]
    out = _sc_scatter(h2p.reshape(nchunk * t, lanes), idx.reshape(nchunk * (t // w), k, w), nchunk * n_rows)
    return out.reshape(nchunk, n_rows, lanes)


def _expert_kernel(first_ref, nblk_ref, nv_ref, rows_ref, x_hbm, w1_ref, w3_ref, w2_ref, y_hbm,
                   xbuf, ybuf, xsem, ysem, w1b, w3b, w2b):
    e = pl.program_id(0)
    ring, nchunk, rb = xbuf.shape[0], xbuf.shape[1], xbuf.shape[2]
    half = nchunk * LANES
    total = nv_ref[0]

    def x_copy(g, slot):
        return pltpu.make_async_copy(x_hbm.at[:, pl.ds(g * rb, rb), :], xbuf.at[slot], xsem.at[slot])

    def y_copy(g, slot):
        return pltpu.make_async_copy(ybuf.at[slot], y_hbm.at[:, pl.ds(g * rb, rb), :], ysem.at[slot])

    @pl.when(e == 0)
    def _():
        for g0 in range(ring - 1):
            @pl.when(g0 < total)
            def _():
                x_copy(g0, g0).start()

    @pl.when(nblk_ref[e] > 0)
    def _():
        w1b[...] = w1_ref[0].astype(BF16)
        w3b[...] = w3_ref[0].astype(BF16)
        w2b[...] = w2_ref[0].astype(BF16)

    def block(j, carry):
        g = first_ref[e] + j
        slot = g % ring
        x_copy(g, slot).wait()

        @pl.when(g + ring - 1 < total)
        def _():
            x_copy(g + ring - 1, (g + ring - 1) % ring).start()

        live = lax.broadcasted_iota(jnp.int32, (rb, 1), 0) < rows_ref[g]
        x = jnp.concatenate([xbuf[slot, c] for c in range(nchunk)], axis=1)
        hi, lo = _unpack_halves(jnp.where(live, x, jnp.uint32(0)))
        xa = hi.astype(BF16)
        xb = lo.astype(BF16)
        dot = lambda p, q: jnp.dot(p, q, preferred_element_type=F32)
        h1 = dot(xa, w1b[:half, :]) + dot(xb, w1b[half:, :])
        h3 = dot(xa, w3b[:half, :]) + dot(xb, w3b[half:, :])
        hid = _silu(h1) * h3
        packed = _pack_halves(dot(hid.astype(BF16), w2b[...]))

        @pl.when(g >= ring)
        def _():
            y_copy(g - ring, slot).wait()

        for c in range(nchunk):
            ybuf[slot, c] = packed[:, c * LANES:(c + 1) * LANES]
        y_copy(g, slot).start()
        return carry

    lax.fori_loop(0, nblk_ref[e], block, 0)

    @pl.when(e == pl.num_programs(0) - 1)
    def _():
        for back in range(ring, 0, -1):
            @pl.when(total >= back)
            def _():
                y_copy(total - back, (total - back) % ring).wait()


def _expert(x_sorted, first_blk, nblk_e, nvalid, block_rows, w1, w3, w2):
    nchunk, n_rows, lanes = x_sorted.shape
    rb = ROW_BLOCK
    ne, d, f = w1.shape
    grid_spec = pltpu.PrefetchScalarGridSpec(
        num_scalar_prefetch=4,
        grid=(ne,),
        in_specs=[
            pl.BlockSpec(memory_space=pl.ANY),
            pl.BlockSpec((1, d, f), lambda e, *_: (e, 0, 0)),
            pl.BlockSpec((1, d, f), lambda e, *_: (e, 0, 0)),
            pl.BlockSpec((1, f, d), lambda e, *_: (e, 0, 0)),
        ],
        out_specs=pl.BlockSpec(memory_space=pl.ANY),
        scratch_shapes=[pltpu.VMEM((EXPERT_RING, nchunk, rb, lanes), jnp.uint32),
                        pltpu.VMEM((EXPERT_RING, nchunk, rb, lanes), jnp.uint32),
                        pltpu.SemaphoreType.DMA((EXPERT_RING,)), pltpu.SemaphoreType.DMA((EXPERT_RING,)),
                        pltpu.VMEM((d, f), BF16), pltpu.VMEM((d, f), BF16), pltpu.VMEM((f, d), BF16)],
    )
    return pl.pallas_call(
        _expert_kernel,
        grid_spec=grid_spec,
        out_shape=jax.ShapeDtypeStruct((nchunk, n_rows, lanes), jnp.uint32),
        compiler_params=_params("arbitrary"),
        name="expert",
    )(first_blk, nblk_e, nvalid, block_rows, x_sorted, w1, w3, w2)


def _sc_gather(table, idx):
    n = idx.shape[0]
    w = SC_WINDOW
    group = SC_GATHERS_PER_STEP
    mesh = plsc.VectorSubcoreMesh(core_axis_name="core", subcore_axis_name="subcore")

    @functools.partial(pl.kernel, out_type=jax.ShapeDtypeStruct((n, table.shape[1]), table.dtype), mesh=mesh,
                       scratch_types=[pltpu.SemaphoreType.DMA])
    def gather_kernel(table_hbm, idx_hbm, out_hbm, sem):
        def body(idx_vmem, out_vmem):
            copies = [pltpu.make_async_copy(table_hbm.at[idx_vmem.at[j]], out_vmem.at[pl.ds(j * w, w)], sem)
                      for j in range(group)]
            for cp in copies:
                cp.start()
            for cp in copies:
                cp.wait()

        pltpu.emit_pipeline(
            body,
            grid=(n // (group * w),),
            in_specs=[pl.BlockSpec((group, w), index_map=lambda i: (i, 0))],
            out_specs=[pl.BlockSpec((group * w, table.shape[1]), index_map=lambda i: (i, 0))],
            core_axis_name=("core", "subcore"),
            dimension_semantics=(pltpu.PARALLEL,),
        )(idx_hbm, out_hbm)

    return gather_kernel(table, idx.reshape(n // w, w))


def _gather_expert_rows(y_sorted, dest_tk):
    nchunk, n_rows, lanes = y_sorted.shape
    k, t = dest_tk.shape
    idx = dest_tk[None, :, :] + (jnp.arange(nchunk, dtype=jnp.int32) * n_rows)[:, None, None]
    rows = _sc_gather(y_sorted.reshape(nchunk * n_rows, lanes), idx.reshape(-1))
    return rows.reshape(nchunk, k, t, lanes)


def _combine_kernel(yg_ref, wts_ref, h_ref, x1_ref, mod_ref, sw1_ref, sw3_ref, sw2_ref, nfg_ref, o_ref):
    nchunk = yg_ref.shape[0]
    w = wts_ref[...]
    acc_hi = [None] * nchunk
    acc_lo = [None] * nchunk
    for kk in range(TOP_K):
        wk = w[:, kk:kk + 1]
        for j in range(nchunk):
            hi, lo = _unpack_halves(yg_ref[j, kk])
            acc_hi[j] = hi * wk if kk == 0 else acc_hi[j] + hi * wk
            acc_lo[j] = lo * wk if kk == 0 else acc_lo[j] + lo * wk
    routed = jnp.concatenate(acc_hi + acc_lo, axis=1)
    hb = h_ref[...].astype(BF16)
    hid = _silu(jnp.dot(hb, sw1_ref[...], preferred_element_type=F32)) * jnp.dot(
        hb, sw3_ref[...], preferred_element_type=F32)
    shared = jnp.dot(hid.astype(BF16), sw2_ref[...], preferred_element_type=F32)
    g2 = mod_ref[0, 5:6, :]
    x2 = x1_ref[...] + g2 * (routed + shared)
    o_ref[...] = _rms(x2) * nfg_ref[...]


def _combine(y_gathered, wts_tk, h2, x1, mod3, seq, sw1, sw3, sw2, normf_g):
    t, d = h2.shape
    tm = min(COMBINE_TILE, seq)
    per_seq = seq // tm
    nchunk, k, _, lanes = y_gathered.shape
    consts = (sw1.astype(BF16), sw3.astype(BF16), sw2.astype(BF16), normf_g.reshape(1, d))
    full = lambda a: pl.BlockSpec(a.shape, lambda i: (0,) * a.ndim)
    tile = lambda a: pl.BlockSpec((tm, a.shape[1]), lambda i: (i, 0))
    return pl.pallas_call(
        _combine_kernel,
        grid=(t // tm,),
        in_specs=[pl.BlockSpec((nchunk, k, tm, lanes), lambda i: (0, 0, i, 0)),
                  tile(wts_tk), tile(h2), tile(x1),
                  pl.BlockSpec((1,) + mod3.shape[1:], lambda i: (i // per_seq, 0, 0))]
                 + [full(a) for a in consts],
        out_specs=pl.BlockSpec((tm, d), lambda i: (i, 0)),
        out_shape=jax.ShapeDtypeStruct((t, d), F32),
        compiler_params=_params("arbitrary"),
        name="combine",
    )(y_gathered, wts_tk, h2, x1, mod3, *consts)


def _dispatch_plan(counts, t):
    rb = ROW_BLOCK
    padded = (counts + rb - 1) // rb * rb
    pends = jnp.cumsum(padded)
    pstarts = pends - padded
    n_rows = t * TOP_K + N_EXPERTS * rb
    block_start = jnp.arange(n_rows // rb, dtype=jnp.int32) * rb
    block_e = jnp.sum((pends[None, :] <= block_start[:, None]).astype(jnp.int32), axis=1)
    block_e = jnp.minimum(block_e, N_EXPERTS - 1)
    block_rows = jnp.clip(pstarts[block_e] + counts[block_e] - block_start, 0, rb).astype(jnp.int32)
    nvalid = (pends[-1:] // rb).astype(jnp.int32)
    return pstarts, pstarts // rb, padded // rb, block_rows, nvalid, n_rows


def kernel(x, c, ada_w, ada_b, norm1_g, norm2_g, w_in, tshift_mu, rwkv_w0, rwkv_w_up, rwkv_a0, rwkv_a_up,
           rwkv_g_up, rwkv_k_k, rwkv_k_a, rwkv_r_k, rwkv_ln_g, rwkv_ln_b, gmlp_ln_g, gmlp_ln_b, gmlp_ws,
           gmlp_bs, w_out_a, w_out_b, w_out, router_w, router_bias, exp_w1, exp_w3, exp_w2, shared_w1,
           shared_w3, shared_w2, normf_g):
    bsz, seq, d = x.shape
    t = bsz * seq
    xf = x.reshape(t, d)
    assert ada_w.shape[0] == 1, "single-layer block only"
    l = 0
    mod3 = _mod(c, ada_w[l], ada_b[l]).reshape(bsz, 6, d)
    p_rwkv, p_uv, p_gate = _inproj(xf, mod3, norm1_g[l], w_in[l], seq)
    y_a = _rwkv(p_rwkv, bsz, seq, tshift_mu[l], rwkv_w0[l], rwkv_w_up[l], rwkv_a0[l], rwkv_a_up[l],
                rwkv_g_up[l], rwkv_k_k[l], rwkv_k_a[l], rwkv_r_k[l], rwkv_ln_g[l], rwkv_ln_b[l])
    x1, h2, h2p = _mix(p_uv, p_gate, y_a, xf, mod3, seq, gmlp_ln_g[l], gmlp_ln_b[l], gmlp_ws[l], gmlp_bs[l],
                  w_out_a[l], w_out_b[l], w_out[l], norm2_g[l])
    eidx, wts = _route(h2, router_w[l], router_bias[l])
    rank, counts = _rank(eidx)
    counts = counts[:, 0].astype(jnp.int32)
    pstarts, first_blk, nblk_e, block_rows, nvalid, n_rows = _dispatch_plan(counts, t)
    dest = _dest(eidx, rank, pstarts)
    x_sorted = _dispatch(h2p, dest, n_rows)
    y_sorted = _expert(x_sorted, first_blk, nblk_e, nvalid, block_rows, exp_w1[l], exp_w3[l], exp_w2[l])
    y_gathered = _gather_expert_rows(y_sorted, dest)
    out = _combine(y_gathered, wts.T, h2, x1, mod3, seq, shared_w1[l], shared_w3[l], shared_w2[l], normf_g)
    return out.reshape(bsz, seq, d)
```

```python
import functools

import jax
import jax.numpy as jnp
from jax import lax
from jax.experimental import pallas as pl
from jax.experimental.pallas import tpu as pltpu
from jax.experimental.pallas import tpu_sc as plsc

F32 = jnp.float32
BF16 = jnp.bfloat16

RWKV_HEAD = 64
RWKV_HEADS = 8
RWKV_WIDTH = RWKV_HEAD * RWKV_HEADS
DECAY_LORA = 64
AAA_LORA = 64
GATE_LORA = 128
RWKV_COLS = 3 * RWKV_WIDTH + DECAY_LORA + AAA_LORA + GATE_LORA
GMLP_WIDTH = 512
GMLP_BLOCK = 128
GMLP_GROUPS = 8
GMLP_GROUP_DIM = GMLP_WIDTH // GMLP_GROUPS
N_EXPERTS = 256
TOP_K = 8
N_EXPERT_GROUPS = 8
GROUP_SIZE = N_EXPERTS // N_EXPERT_GROUPS
TOPK_GROUPS = 4
ROUTED_SCALE = 2.5
RMS_EPS = 1e-6
LN_EPS = 1e-5
GN_EPS = 64e-5

VMEM_LIMIT_BYTES = 56 * 1024 * 1024

CHUNK = 64
RWKV_TILE = 256
ROW_BLOCK = 256
EXPERT_RING = 4
INPROJ_TILE = 256
MIX_TILE = 256
ROUTE_TILE = 512
RANK_TILE = 512
COMBINE_TILE = 256
COMBINE_PARTS = 4
LANES = 128
SC_WINDOW = 128
SC_GATHERS_PER_STEP = 2


def _params(*sem):
    return pltpu.CompilerParams(dimension_semantics=sem, vmem_limit_bytes=VMEM_LIMIT_BYTES)


def _mm(a, b):
    return jnp.dot(a.astype(BF16), b.astype(BF16), preferred_element_type=F32)


def _mm_nt(a, b):
    return lax.dot_general(a.astype(BF16), b.astype(BF16), (((1,), (1,)), ((), ())),
                           preferred_element_type=F32)


def _mm_tn(a, b):
    return lax.dot_general(a.astype(BF16), b.astype(BF16), (((0,), (0,)), ((), ())),
                           preferred_element_type=F32)


def _split3(a):
    hi = a.astype(BF16)
    r1 = a - hi.astype(F32)
    mid = r1.astype(BF16)
    lo = (r1 - mid.astype(F32)).astype(BF16)
    return hi, mid, lo


def _silu(x):
    return x * jax.nn.sigmoid(x)


def _rms(x):
    return x * lax.rsqrt(jnp.mean(x * x, axis=-1, keepdims=True) + RMS_EPS)


def _pack_halves(x):
    n = x.shape[1] // 2
    bits = lax.bitcast_convert_type(x.astype(BF16).astype(F32), jnp.uint32)
    return bits[:, :n] | (bits[:, n:] >> 16)


def _unpack_halves(u):
    hi = lax.bitcast_convert_type(u & jnp.uint32(0xFFFF0000), F32)
    lo = lax.bitcast_convert_type(u << 16, F32)
    return hi, lo


def _mod_kernel(c_ref, w_ref, b_ref, o_ref):
    s = _silu(c_ref[...])
    hi, mid, lo = _split3(s)
    whi, wmid, wlo = _split3(w_ref[...])
    dot = lambda p, q: jnp.dot(p, q, preferred_element_type=F32)
    acc = dot(hi, whi) + (dot(hi, wmid) + dot(mid, whi)) + (dot(hi, wlo) + dot(mid, wmid) + dot(lo, whi))
    o_ref[...] = acc + b_ref[...]


def _mod(c, ada_w, ada_b):
    bsz, d = c.shape
    n = ada_w.shape[1]
    tn = d
    return pl.pallas_call(
        _mod_kernel,
        grid=(n // tn,),
        in_specs=[pl.BlockSpec((bsz, d), lambda j: (0, 0)),
                  pl.BlockSpec((d, tn), lambda j: (0, j)),
                  pl.BlockSpec((1, tn), lambda j: (0, j))],
        out_specs=pl.BlockSpec((bsz, tn), lambda j: (0, j)),
        out_shape=jax.ShapeDtypeStruct((bsz, n), F32),
        compiler_params=_params("arbitrary"),
        name="mod",
    )(c, ada_w, ada_b.reshape(1, n))


def _inproj_kernel(x_ref, mod_ref, g_ref, wr_ref, wuv_ref, wg_ref, pr_ref, puv_ref, pg_ref):
    sh = mod_ref[0, 0:1, :]
    sc = mod_ref[0, 1:2, :]
    h = (_rms(x_ref[...]) * g_ref[...]) * (1.0 + sc) + sh
    hb = h.astype(BF16)
    pr_ref[...] = jnp.dot(hb, wr_ref[...], preferred_element_type=F32).astype(pr_ref.dtype)
    puv_ref[...] = jnp.dot(hb, wuv_ref[...], preferred_element_type=F32).astype(puv_ref.dtype)
    pg_ref[...] = jnp.dot(hb, wg_ref[...], preferred_element_type=F32).astype(pg_ref.dtype)


def _inproj(xf, mod3, norm1_g, w_in, seq):
    t, d = xf.shape
    tm = min(INPROJ_TILE, seq)
    per_seq = seq // tm
    wr = w_in[:, :RWKV_COLS].astype(BF16)
    wuv = w_in[:, RWKV_COLS:RWKV_COLS + 2 * GMLP_WIDTH].astype(BF16)
    wg = w_in[:, RWKV_COLS + 2 * GMLP_WIDTH:].astype(BF16)
    full = lambda a: pl.BlockSpec(a.shape, lambda i: (0,) * a.ndim)
    g = norm1_g.reshape(1, d)
    return pl.pallas_call(
        _inproj_kernel,
        grid=(t // tm,),
        in_specs=[pl.BlockSpec((tm, d), lambda i: (i, 0)),
                  pl.BlockSpec((1,) + mod3.shape[1:], lambda i: (i // per_seq, 0, 0)),
                  full(g), full(wr), full(wuv), full(wg)],
        out_specs=[pl.BlockSpec((tm, wr.shape[1]), lambda i: (i, 0)),
                   pl.BlockSpec((tm, wuv.shape[1]), lambda i: (i, 0)),
                   pl.BlockSpec((tm, wg.shape[1]), lambda i: (i, 0))],
        out_shape=[jax.ShapeDtypeStruct((t, wr.shape[1]), BF16),
                   jax.ShapeDtypeStruct((t, wuv.shape[1]), BF16),
                   jax.ShapeDtypeStruct((t, wg.shape[1]), BF16)],
        compiler_params=_params("arbitrary"),
        name="inproj",
    )(xf, mod3, g, wr, wuv, wg)


def _unit_lower_inverses(ns, row, col):
    c = ns[0].shape[0]
    eye = jnp.where(row == col, 1.0, 0.0).astype(F32)
    blk8 = (row // 8) == (col // 8)
    n8 = [jnp.where(blk8, n, 0.0) for n in ns]
    n8_2 = [_mm(a, a) for a in n8]
    n8_4 = [_mm(a, a) for a in n8_2]
    inv = [eye + a for a in n8]
    inv = [i + _mm(i, b) for i, b in zip(inv, n8_2)]
    inv = [i + _mm(i, b) for i, b in zip(inv, n8_4)]
    s = 8
    while s < c:
        sel = ((row // (2 * s)) == (col // (2 * s))) & ((row // s) != (col // s))
        low = [jnp.where(sel, n, 0.0) for n in ns]
        t1 = [_mm(l, i) for l, i in zip(low, inv)]
        inv = [i + _mm(i, t) for i, t in zip(inv, t1)]
        s *= 2
    return inv


def _rwkv_kernel(p_ref, mu_ref, w0_ref, wup_ref, a0_ref, aup_ref, gup_ref, kk_ref, ka_ref, rk_ref,
                 lng_ref, lnb_ref, bd_ref, y_ref, state_ref, carry_ref, ycat_ref, *, c):
    tm = p_ref.shape[0]
    nq = tm // c
    hd = RWKV_HEAD
    nh = RWKV_HEADS

    @pl.when(pl.program_id(1) == 0)
    def _():
        state_ref[...] = jnp.zeros_like(state_ref)
        carry_ref[...] = jnp.zeros_like(carry_ref)

    p = p_ref[...].astype(F32)
    rowc = lax.broadcasted_iota(jnp.int32, (tm, 1), 0)
    prev = jnp.where(rowc == 0, carry_ref[...], pltpu.roll(p, 1, axis=0))
    carry_ref[...] = p[tm - 1:tm, :]
    ps = p + (prev - p) * mu_ref[...]

    o = RWKV_WIDTH
    r = ps[:, 0:o]
    k = ps[:, o:2 * o]
    v = ps[:, 2 * o:3 * o]
    xw = ps[:, 3 * o:3 * o + DECAY_LORA]
    xa = ps[:, 3 * o + DECAY_LORA:3 * o + DECAY_LORA + AAA_LORA]
    xg = ps[:, 3 * o + DECAY_LORA + AAA_LORA:]

    z = -(w0_ref[...] + _mm(jnp.tanh(xw), wup_ref[...]))
    softplus = jnp.maximum(z, 0.0) + jnp.log1p(jnp.exp(-jnp.abs(z)))
    wlog = -softplus - 0.5
    ld = -jnp.exp(wlog)
    a = jax.nn.sigmoid(a0_ref[...] + _mm(xa, aup_ref[...]))
    g = _mm(jax.nn.sigmoid(xg), gup_ref[...])

    def head_sum(x):
        hi = x.astype(BF16)
        lo = (x - hi.astype(F32)).astype(BF16)
        bd = bd_ref[...]
        return jnp.dot(hi, bd, preferred_element_type=F32) + jnp.dot(lo, bd, preferred_element_type=F32)

    kkf = k * kk_ref[...]
    kkn = kkf * (1.0 / jnp.maximum(jnp.sqrt(head_sum(kkf * kkf)), 1e-12))
    k2 = k * (1.0 + (a - 1.0) * ka_ref[...])
    avec = -kkn
    bvec = kkn * a

    rowt = lax.broadcasted_iota(jnp.int32, (tm, tm), 0)
    colt = lax.broadcasted_iota(jnp.int32, (tm, tm), 1)
    tri = jnp.where((rowt >= colt) & ((rowt // c) == (colt // c)), 1.0, 0.0).astype(BF16)
    hi, mid, lo = _split3(ld)
    dot = lambda q: jnp.dot(tri, q, preferred_element_type=F32)
    cl = dot(hi) + dot(mid) + dot(lo)
    cl_end = jnp.concatenate(
        [jnp.broadcast_to(cl[(q + 1) * c - 1:(q + 1) * c, :], (c, o)) for q in range(nq)], axis=0)
    g_inv = jnp.exp(-cl)
    g_tail = jnp.exp(cl_end - cl)
    g_end = jnp.exp(cl_end)
    rt = (r * jnp.exp(cl)).astype(BF16)
    at = (avec * jnp.exp(cl - ld)).astype(BF16)
    kt = (k2 * g_inv).astype(BF16)
    bt = (bvec * g_inv).astype(BF16)
    kend = (k2 * g_tail).astype(BF16)
    bend = (bvec * g_tail).astype(BF16)
    vb = v.astype(BF16)

    row = lax.broadcasted_iota(jnp.int32, (c, c), 0)
    col = lax.broadcasted_iota(jnp.int32, (c, c), 1)
    strict = row > col
    incl = row >= col
    items = [(q, h) for q in range(nq) for h in range(nh)]
    blk = lambda x, q, h: x[q * c:(q + 1) * c, h * hd:(h + 1) * hd]

    lhs = [jnp.concatenate([blk(at, q, h), blk(rt, q, h)], axis=0) for q, h in items]
    mat_b = [_mm_nt(l, blk(bt, q, h)) for l, (q, h) in zip(lhs, items)]
    mat_k = [_mm_nt(l, blk(kt, q, h)) for l, (q, h) in zip(lhs, items)]
    a_ab = [jnp.where(strict, m[:c], 0.0) for m in mat_b]
    a_rb = [jnp.where(incl, m[c:], 0.0).astype(BF16) for m in mat_b]
    a_ak = [jnp.where(strict, m[:c], 0.0) for m in mat_k]
    a_rk = [jnp.where(incl, m[c:], 0.0) for m in mat_k]
    tinv = _unit_lower_inverses(a_ab, row, col)
    akv = [_mm(m, blk(vb, q, h)) for m, (q, h) in zip(a_ak, items)]
    y_loc = [_mm(m, blk(vb, q, h)) for m, (q, h) in zip(a_rk, items)]
    s_loc = [_mm_tn(blk(vb, q, h), blk(kend, q, h)) for q, h in items]
    wmat = [_mm(t, blk(at, q, h)).astype(BF16) for t, (q, h) in zip(tinv, items)]
    ut = [_mm(t, m) for t, m in zip(tinv, akv)]

    state = [state_ref[h] for h in range(nh)]
    for q in range(nq):
        base = q * nh
        sb = [s.astype(BF16) for s in state]
        u = [_mm_nt(wmat[base + h], sb[h]) + ut[base + h] for h in range(nh)]
        yh = [_mm_nt(blk(rt, q, h), sb[h]) + y_loc[base + h] for h in range(nh)]
        ub = [x.astype(BF16) for x in u]
        yh = [y0 + _mm(a_rb[base + h], ub[h]) for h, y0 in enumerate(yh)]
        state = [state[h] * g_end[q * c:q * c + 1, h * hd:(h + 1) * hd]
                 + _mm_tn(ub[h], blk(bend, q, h)) + s_loc[base + h] for h in range(nh)]
        for h in range(nh):
            ycat_ref[q * c:(q + 1) * c, h * hd:(h + 1) * hd] = yh[h]
    for h in range(nh):
        state_ref[h] = state[h]

    y = ycat_ref[...]
    dev = y - head_sum(y) * (1.0 / hd)
    rstd = lax.rsqrt(head_sum(dev * dev) * (1.0 / hd) + GN_EPS)
    bonus = head_sum(r * k2 * rk_ref[...])
    yn = dev * rstd * lng_ref[...] + lnb_ref[...]
    y_ref[...] = (yn + bonus * v) * g


def _rwkv(p_rwkv, bsz, seq, tshift_mu, w0, w_up, a0, a_up, g_up, k_k, k_a, r_k, ln_g, ln_b):
    t = p_rwkv.shape[0]
    c = min(CHUNK, seq)
    tm = min(RWKV_TILE, seq)
    nstep = seq // tm
    row = lambda a: a.reshape(1, -1)
    head_of = jnp.arange(RWKV_WIDTH, dtype=jnp.int32) // RWKV_HEAD
    block_diag = (head_of[:, None] == head_of[None, :]).astype(BF16)
    args = (row(tshift_mu), row(w0), w_up, row(a0), a_up, g_up, row(k_k), row(k_a), row(r_k),
            row(ln_g), row(ln_b), block_diag)
    full = lambda a: pl.BlockSpec(a.shape, lambda b, j: (0,) * a.ndim)
    return pl.pallas_call(
        functools.partial(_rwkv_kernel, c=c),
        grid=(bsz, nstep),
        in_specs=[pl.BlockSpec((tm, RWKV_COLS), lambda b, j: (b * nstep + j, 0))] + [full(a) for a in args],
        out_specs=pl.BlockSpec((tm, RWKV_WIDTH), lambda b, j: (b * nstep + j, 0)),
        out_shape=jax.ShapeDtypeStruct((t, RWKV_WIDTH), F32),
        scratch_shapes=[pltpu.VMEM((RWKV_HEADS, RWKV_HEAD, RWKV_HEAD), F32),
                        pltpu.VMEM((1, RWKV_COLS), F32),
                        pltpu.VMEM((tm, RWKV_WIDTH), F32)],
        compiler_params=_params("arbitrary", "arbitrary"),
        name="rwkv",
    )(p_rwkv, *args)


def _mix_kernel(puv_ref, pg_ref, ya_ref, x_ref, mod_ref, lng_ref, lnb_ref, ws_ref, bsm_ref,
                woa_ref, wob_ref, wo_ref, n2g_ref, x1_ref, h2_ref, h2p_ref, yb_ref):
    tm = x_ref.shape[0]
    gw = GMLP_WIDTH
    puv = puv_ref[...].astype(F32)
    u = jax.nn.gelu(puv[:, :gw])
    vg = jax.nn.gelu(puv[:, gw:])
    mu = jnp.mean(vg, axis=-1, keepdims=True)
    var = jnp.mean(jnp.square(vg - mu), axis=-1, keepdims=True)
    vln = (vg - mu) * lax.rsqrt(var + LN_EPS) * lng_ref[...] + lnb_ref[...]

    row = lax.broadcasted_iota(jnp.int32, (GMLP_BLOCK, GMLP_BLOCK), 0)
    col = lax.broadcasted_iota(jnp.int32, (GMLP_BLOCK, GMLP_BLOCK), 1)
    gd = GMLP_GROUP_DIM
    for gi in range(GMLP_GROUPS):
        wsm = jnp.where(row >= col, ws_ref[gi], 0.0).astype(BF16)
        for nb in range(tm // GMLP_BLOCK):
            rs = slice(nb * GMLP_BLOCK, (nb + 1) * GMLP_BLOCK)
            cs = slice(gi * gd, (gi + 1) * gd)
            sv = jnp.dot(wsm, vln[rs, cs].astype(BF16), preferred_element_type=F32) + bsm_ref[:, cs]
            yb_ref[rs, cs] = u[rs, cs] * sv

    pg = pg_ref[...].astype(F32)
    d = x_ref.shape[1]
    gate_a = jax.nn.sigmoid(pg[:, :d])
    gate_b = jax.nn.sigmoid(pg[:, d:])
    merged = gate_a * _mm(ya_ref[...], woa_ref[...]) + gate_b * _mm(yb_ref[...], wob_ref[...])
    g1 = mod_ref[0, 2:3, :]
    x1 = x_ref[...] + g1 * _mm(merged, wo_ref[...])
    x1_ref[...] = x1
    sh2 = mod_ref[0, 3:4, :]
    sc2 = mod_ref[0, 4:5, :]
    h2 = (_rms(x1) * n2g_ref[...]) * (1.0 + sc2) + sh2
    h2_ref[...] = h2
    packed = _pack_halves(h2)
    for j in range(h2p_ref.shape[0]):
        h2p_ref[j] = packed[:, j * LANES:(j + 1) * LANES]


def _mix(p_uv, p_gate, y_a, xf, mod3, seq, ln_g, ln_b, ws, bs, w_out_a, w_out_b, w_out, norm2_g):
    t, d = xf.shape
    tm = min(MIX_TILE, seq)
    per_seq = seq // tm
    row = lambda a: a.reshape(1, -1)
    bsm = jnp.repeat(bs.T, GMLP_GROUP_DIM, axis=1)
    consts = (row(ln_g), row(ln_b), ws, bsm, w_out_a.astype(BF16), w_out_b.astype(BF16),
              w_out.astype(BF16), row(norm2_g))
    full = lambda a: pl.BlockSpec(a.shape, lambda i: (0,) * a.ndim)
    tile = lambda a: pl.BlockSpec((tm, a.shape[1]), lambda i: (i, 0))
    return pl.pallas_call(
        _mix_kernel,
        grid=(t // tm,),
        in_specs=[tile(p_uv), tile(p_gate), tile(y_a), tile(xf),
                  pl.BlockSpec((1,) + mod3.shape[1:], lambda i: (i // per_seq, 0, 0))]
                 + [full(a) for a in consts],
        out_specs=[pl.BlockSpec((tm, d), lambda i: (i, 0)), pl.BlockSpec((tm, d), lambda i: (i, 0)),
                   pl.BlockSpec((d // 2 // LANES, tm, LANES), lambda i: (0, i, 0))],
        out_shape=[jax.ShapeDtypeStruct((t, d), F32), jax.ShapeDtypeStruct((t, d), F32),
                   jax.ShapeDtypeStruct((d // 2 // LANES, t, LANES), jnp.uint32)],
        scratch_shapes=[pltpu.VMEM((tm, GMLP_WIDTH), F32)],
        compiler_params=_params("arbitrary"),
        name="mix",
    )(p_uv, p_gate, y_a, xf, mod3, *consts)


def _first_argmax(vals, idx, big):
    m = jnp.max(vals, axis=0, keepdims=True)
    first = jnp.min(jnp.where(vals == m, idx, big), axis=0, keepdims=True)
    return m, first


def _route_kernel(h_ref, rwt_ref, bias_ref, eidx_ref, wts_ref):
    tm = h_ref.shape[0]
    ne = N_EXPERTS
    hhi, hmid, _ = _split3(h_ref[...])
    whi, wmid, _ = _split3(rwt_ref[...])
    nt = lambda p, q: lax.dot_general(p, q, (((1,), (1,)), ((), ())), preferred_element_type=F32)
    logits = nt(whi, hhi) + (nt(whi, hmid) + nt(wmid, hhi))
    scores = jax.nn.sigmoid(logits)
    sel = scores + bias_ref[...]
    neg = -jnp.inf

    gs = GROUP_SIZE
    gidx = lax.broadcasted_iota(jnp.int32, (gs, tm), 0)
    grp_rows = []
    for gi in range(N_EXPERT_GROUPS):
        blk = sel[gi * gs:(gi + 1) * gs, :]
        m1, i1 = _first_argmax(blk, gidx, gs)
        m2 = jnp.max(jnp.where(gidx == i1, neg, blk), axis=0, keepdims=True)
        grp_rows.append(m1 + m2)
    grp = jnp.concatenate(grp_rows, axis=0)

    ng = N_EXPERT_GROUPS
    giota = lax.broadcasted_iota(jnp.int32, (ng, tm), 0)
    gsel = jnp.zeros((ng, tm), jnp.bool_)
    work = grp
    for _ in range(TOPK_GROUPS):
        _, gi1 = _first_argmax(work, giota, ng)
        hit = giota == gi1
        gsel = gsel | hit
        work = jnp.where(hit, neg, work)
    gself = jnp.where(gsel, 1.0, 0.0)
    emask = jnp.concatenate(
        [jnp.broadcast_to(gself[gi:gi + 1, :], (gs, tm)) for gi in range(ng)], axis=0) > 0.5

    eiota = lax.broadcasted_iota(jnp.int32, (ne, tm), 0)
    work = jnp.where(emask, sel, neg)
    idx_rows, w_rows = [], []
    for _ in range(TOP_K):
        _, e1 = _first_argmax(work, eiota, ne)
        hit = eiota == e1
        idx_rows.append(e1)
        w_rows.append(jnp.sum(jnp.where(hit, scores, 0.0), axis=0, keepdims=True))
        work = jnp.where(hit, neg, work)
    eidx_ref[...] = jnp.concatenate(idx_rows, axis=0)
    w = jnp.concatenate(w_rows, axis=0)
    wts_ref[...] = w / jnp.sum(w, axis=0, keepdims=True) * ROUTED_SCALE


def _route(h2, router_w, router_bias):
    t, d = h2.shape
    tm = min(ROUTE_TILE, t)
    rwt = router_w.T
    bias = router_bias.reshape(N_EXPERTS, 1)
    return pl.pallas_call(
        _route_kernel,
        grid=(t // tm,),
        in_specs=[pl.BlockSpec((tm, d), lambda i: (i, 0)),
                  pl.BlockSpec(rwt.shape, lambda i: (0, 0)),
                  pl.BlockSpec(bias.shape, lambda i: (0, 0))],
        out_specs=[pl.BlockSpec((TOP_K, tm), lambda i: (0, i)), pl.BlockSpec((TOP_K, tm), lambda i: (0, i))],
        out_shape=[jax.ShapeDtypeStruct((TOP_K, t), jnp.int32), jax.ShapeDtypeStruct((TOP_K, t), F32)],
        compiler_params=_params("arbitrary"),
        name="route",
    )(h2, rwt, bias)


def _rank_kernel(e_ref, upper_ref, rank_ref, count_ref):
    tm = e_ref.shape[1]
    ne = N_EXPERTS

    @pl.when(pl.program_id(0) == 0)
    def _():
        count_ref[...] = jnp.zeros_like(count_ref)

    e = e_ref[...]
    eiota = lax.broadcasted_iota(jnp.int32, (ne, tm), 0)
    onehot = jnp.zeros((ne, tm), F32)
    for kk in range(TOP_K):
        onehot = onehot + jnp.where(eiota == e[kk:kk + 1, :], 1.0, 0.0)
    before = jnp.dot(onehot.astype(BF16), upper_ref[...], preferred_element_type=F32)
    base = count_ref[:, 0:1]
    tot = before + base
    rows = [jnp.sum(jnp.where(eiota == e[kk:kk + 1, :], tot, 0.0), axis=0, keepdims=True)
            for kk in range(TOP_K)]
    rank_ref[...] = jnp.concatenate(rows, axis=0).astype(jnp.int32)
    count_ref[...] = count_ref[...] + jnp.sum(onehot, axis=1, keepdims=True)


def _rank(eidx):
    k, t = eidx.shape
    tm = min(RANK_TILE, t)
    upper = jnp.triu(jnp.ones((tm, tm), F32), 1).astype(BF16)
    return pl.pallas_call(
        _rank_kernel,
        grid=(t // tm,),
        in_specs=[pl.BlockSpec((k, tm), lambda i: (0, i)), pl.BlockSpec((tm, tm), lambda i: (0, 0))],
        out_specs=[pl.BlockSpec((k, tm), lambda i: (0, i)), pl.BlockSpec((N_EXPERTS, 128), lambda i: (0, 0))],
        out_shape=[jax.ShapeDtypeStruct((k, t), jnp.int32), jax.ShapeDtypeStruct((N_EXPERTS, 128), F32)],
        compiler_params=_params("arbitrary"),
        name="rank",
    )(eidx, upper)


def _dest_kernel(e_ref, rank_ref, pst_ref, dest_ref):
    tm = e_ref.shape[1]
    e = e_ref[...]
    eiota = lax.broadcasted_iota(jnp.int32, (N_EXPERTS, tm), 0)
    pst = pst_ref[...]
    rows = [jnp.sum(jnp.where(eiota == e[kk:kk + 1, :], pst, 0.0), axis=0, keepdims=True)
            for kk in range(TOP_K)]
    dest_ref[...] = jnp.concatenate(rows, axis=0).astype(jnp.int32) + rank_ref[...]


def _dest(eidx, rank, pstarts):
    k, t = eidx.shape
    tm = min(RANK_TILE, t)
    pst = pstarts.astype(F32).reshape(N_EXPERTS, 1)
    return pl.pallas_call(
        _dest_kernel,
        grid=(t // tm,),
        in_specs=[pl.BlockSpec((k, tm), lambda i: (0, i)), pl.BlockSpec((k, tm), lambda i: (0, i)),
                  pl.BlockSpec((N_EXPERTS, 1), lambda i: (0, 0))],
        out_specs=pl.BlockSpec((k, tm), lambda i: (0, i)),
        out_shape=jax.ShapeDtypeStruct((k, t), jnp.int32),
        compiler_params=_params("arbitrary"),
        name="dest",
    )(eidx, rank, pst)


def _sc_scatter(rows, idx, n_out):
    n, lanes = rows.shape
    nstep, copies, w = idx.shape
    mesh = plsc.VectorSubcoreMesh(core_axis_name="core", subcore_axis_name="subcore")

    @functools.partial(pl.kernel, out_type=jax.ShapeDtypeStruct((n_out, lanes), rows.dtype), mesh=mesh,
                       scratch_types=[pltpu.SemaphoreType.DMA])
    def scatter_kernel(rows_hbm, idx_hbm, out_hbm, sem):
        def body(rows_vmem, idx_vmem):
            cps = [pltpu.make_async_copy(rows_vmem, out_hbm.at[idx_vmem.at[j]], sem) for j in range(copies)]
            for cp in cps:
                cp.start()
            for cp in cps:
                cp.wait()

        pltpu.emit_pipeline(
            body,
            grid=(nstep,),
            in_specs=[pl.BlockSpec((w, lanes), index_map=lambda i: (i, 0)),
                      pl.BlockSpec((copies, w), index_map=lambda i: (i, 0))],
            out_specs=[],
            core_axis_name=("core", "subcore"),
            dimension_semantics=(pltpu.PARALLEL,),
        )(rows_hbm, idx_hbm)

    return scatter_kernel(rows, idx.reshape(nstep * copies, w))


def _dispatch(h2p, dest_tk, n_rows):
    nchunk, t, lanes = h2p.shape
    k = dest_tk.shape[0]
    w = SC_WINDOW
    idx = dest_tk.reshape(k, t // w, w).transpose(1, 0, 2)[None]
    idx = idx + (jnp.arange(nchunk, dtype=jnp.int32) * n_rows)[:, None, None, None]
    out = _sc_scatter(h2p.reshape(nchunk * t, lanes), idx.reshape(nchunk * (t // w), k, w), nchunk * n_rows)
    return out.reshape(nchunk, n_rows, lanes)


def _expert_kernel(first_ref, nblk_ref, nv_ref, rows_ref, x_hbm, w1_ref, w3_ref, w2_ref, y_hbm,
                   xbuf, ybuf, xsem, ysem, w1b, w3b, w2b):
    e = pl.program_id(0)
    ring, nchunk, rb = xbuf.shape[0], xbuf.shape[1], xbuf.shape[2]
    half = nchunk * LANES
    total = nv_ref[0]

    def x_copy(g, slot):
        return pltpu.make_async_copy(x_hbm.at[:, pl.ds(g * rb, rb), :], xbuf.at[slot], xsem.at[slot])

    def y_copy(g, slot):
        return pltpu.make_async_copy(ybuf.at[slot], y_hbm.at[:, pl.ds(g * rb, rb), :], ysem.at[slot])

    @pl.when(e == 0)
    def _():
        for g0 in range(ring - 1):
            @pl.when(g0 < total)
            def _():
                x_copy(g0, g0).start()

    @pl.when(nblk_ref[e] > 0)
    def _():
        w1b[...] = w1_ref[0].astype(BF16)
        w3b[...] = w3_ref[0].astype(BF16)
        w2b[...] = w2_ref[0].astype(BF16)

    def block(j, carry):
        g = first_ref[e] + j
        slot = g % ring
        x_copy(g, slot).wait()

        @pl.when(g + ring - 1 < total)
        def _():
            x_copy(g + ring - 1, (g + ring - 1) % ring).start()

        live = lax.broadcasted_iota(jnp.int32, (rb, 1), 0) < rows_ref[g]
        x = jnp.concatenate([xbuf[slot, c] for c in range(nchunk)], axis=1)
        hi, lo = _unpack_halves(jnp.where(live, x, jnp.uint32(0)))
        xa = hi.astype(BF16)
        xb = lo.astype(BF16)
        dot = lambda p, q: jnp.dot(p, q, preferred_element_type=F32)
        h1 = dot(xa, w1b[:half, :]) + dot(xb, w1b[half:, :])
        h3 = dot(xa, w3b[:half, :]) + dot(xb, w3b[half:, :])
        hid = _silu(h1) * h3
        packed = _pack_halves(dot(hid.astype(BF16), w2b[...]))

        @pl.when(g >= ring)
        def _():
            y_copy(g - ring, slot).wait()

        for c in range(nchunk):
            ybuf[slot, c] = packed[:, c * LANES:(c + 1) * LANES]
        y_copy(g, slot).start()
        return carry

    lax.fori_loop(0, nblk_ref[e], block, 0)

    @pl.when(e == pl.num_programs(0) - 1)
    def _():
        for back in range(ring, 0, -1):
            @pl.when(total >= back)
            def _():
                y_copy(total - back, (total - back) % ring).wait()


def _expert(x_sorted, first_blk, nblk_e, nvalid, block_rows, w1, w3, w2):
    nchunk, n_rows, lanes = x_sorted.shape
    rb = ROW_BLOCK
    ne, d, f = w1.shape
    grid_spec = pltpu.PrefetchScalarGridSpec(
        num_scalar_prefetch=4,
        grid=(ne,),
        in_specs=[
            pl.BlockSpec(memory_space=pl.ANY),
            pl.BlockSpec((1, d, f), lambda e, *_: (e, 0, 0)),
            pl.BlockSpec((1, d, f), lambda e, *_: (e, 0, 0)),
            pl.BlockSpec((1, f, d), lambda e, *_: (e, 0, 0)),
        ],
        out_specs=pl.BlockSpec(memory_space=pl.ANY),
        scratch_shapes=[pltpu.VMEM((EXPERT_RING, nchunk, rb, lanes), jnp.uint32),
                        pltpu.VMEM((EXPERT_RING, nchunk, rb, lanes), jnp.uint32),
                        pltpu.SemaphoreType.DMA((EXPERT_RING,)), pltpu.SemaphoreType.DMA((EXPERT_RING,)),
                        pltpu.VMEM((d, f), BF16), pltpu.VMEM((d, f), BF16), pltpu.VMEM((f, d), BF16)],
    )
    return pl.pallas_call(
        _expert_kernel,
        grid_spec=grid_spec,
        out_shape=jax.ShapeDtypeStruct((nchunk, n_rows, lanes), jnp.uint32),
        compiler_params=_params("arbitrary"),
        name="expert",
    )(first_blk, nblk_e, nvalid, block_rows, x_sorted, w1, w3, w2)


def _sc_gather(table, idx):
    n = idx.shape[0]
    w = SC_WINDOW
    group = SC_GATHERS_PER_STEP
    mesh = plsc.VectorSubcoreMesh(core_axis_name="core", subcore_axis_name="subcore")

    @functools.partial(pl.kernel, out_type=jax.ShapeDtypeStruct((n, table.shape[1]), table.dtype), mesh=mesh,
                       scratch_types=[pltpu.SemaphoreType.DMA])
    def gather_kernel(table_hbm, idx_hbm, out_hbm, sem):
        def body(idx_vmem, out_vmem):
            copies = [pltpu.make_async_copy(table_hbm.at[idx_vmem.at[j]], out_vmem.at[pl.ds(j * w, w)], sem)
                      for j in range(group)]
            for cp in copies:
                cp.start()
            for cp in copies:
                cp.wait()

        pltpu.emit_pipeline(
            body,
            grid=(n // (group * w),),
            in_specs=[pl.BlockSpec((group, w), index_map=lambda i: (i, 0))],
            out_specs=[pl.BlockSpec((group * w, table.shape[1]), index_map=lambda i: (i, 0))],
            core_axis_name=("core", "subcore"),
            dimension_semantics=(pltpu.PARALLEL,),
        )(idx_hbm, out_hbm)

    return gather_kernel(table, idx.reshape(n // w, w))


def _gather_expert_rows(y_sorted, dest_tk):
    nchunk, n_rows, lanes = y_sorted.shape
    k, t = dest_tk.shape
    idx = dest_tk[None, :, :] + (jnp.arange(nchunk, dtype=jnp.int32) * n_rows)[:, None, None]
    rows = _sc_gather(y_sorted.reshape(nchunk * n_rows, lanes), idx.reshape(-1))
    return rows.reshape(nchunk, k, t, lanes)


def _combine_kernel(yg_ref, wts_ref, h_ref, x1_ref, mod_ref, sw1_ref, sw3_ref, sw2_ref, nfg_ref, *rest):
    o_ref = rest[-1]
    nchunk = yg_ref.shape[0]
    w = wts_ref[...]
    acc_hi = [None] * nchunk
    acc_lo = [None] * nchunk
    for kk in range(TOP_K):
        wk = w[:, kk:kk + 1]
        for j in range(nchunk):
            hi, lo = _unpack_halves(yg_ref[j, kk])
            acc_hi[j] = hi * wk if kk == 0 else acc_hi[j] + hi * wk
            acc_lo[j] = lo * wk if kk == 0 else acc_lo[j] + lo * wk
    routed = jnp.concatenate(acc_hi + acc_lo, axis=1)
    hb = h_ref[...].astype(BF16)
    hid = _silu(jnp.dot(hb, sw1_ref[...], preferred_element_type=F32)) * jnp.dot(
        hb, sw3_ref[...], preferred_element_type=F32)
    shared = jnp.dot(hid.astype(BF16), sw2_ref[...], preferred_element_type=F32)
    g2 = mod_ref[0, 5:6, :]
    x2 = x1_ref[...] + g2 * (routed + shared)
    o_ref[...] = _rms(x2) * nfg_ref[...]


def _combine(y_gathered, wts_tk, h2, x1, mod3, seq, sw1, sw3, sw2, normf_g, part, prev):
    t, d = h2.shape
    nchunk, k, tp, lanes = y_gathered.shape
    tm = min(COMBINE_TILE, seq, tp)
    per_seq = seq // tm
    off = part * (tp // tm)
    consts = (sw1.astype(BF16), sw3.astype(BF16), sw2.astype(BF16), normf_g.reshape(1, d))
    full = lambda a: pl.BlockSpec(a.shape, lambda i: (0,) * a.ndim)
    tile = lambda a: pl.BlockSpec((tm, a.shape[1]), lambda i: (off + i, 0))
    args = [y_gathered, wts_tk, h2, x1, mod3, *consts]
    in_specs = ([pl.BlockSpec((nchunk, k, tm, lanes), lambda i: (0, 0, i, 0)),
                 pl.BlockSpec((tm, k), lambda i: (i, 0)), tile(h2), tile(x1),
                 pl.BlockSpec((1,) + mod3.shape[1:], lambda i: ((off + i) // per_seq, 0, 0))]
                + [full(a) for a in consts])
    aliases = {}
    if prev is not None:
        aliases = {len(args): 0}
        args.append(prev)
        in_specs.append(pl.BlockSpec(memory_space=pl.ANY))
    return pl.pallas_call(
        _combine_kernel,
        grid=(tp // tm,),
        in_specs=in_specs,
        out_specs=pl.BlockSpec((tm, d), lambda i: (off + i, 0)),
        out_shape=jax.ShapeDtypeStruct((t, d), F32),
        input_output_aliases=aliases,
        compiler_params=_params("arbitrary"),
        name="combine",
    )(*args)


def _dispatch_plan(counts, t):
    rb = ROW_BLOCK
    padded = (counts + rb - 1) // rb * rb
    pends = jnp.cumsum(padded)
    pstarts = pends - padded
    n_rows = t * TOP_K + N_EXPERTS * rb
    block_start = jnp.arange(n_rows // rb, dtype=jnp.int32) * rb
    block_e = jnp.sum((pends[None, :] <= block_start[:, None]).astype(jnp.int32), axis=1)
    block_e = jnp.minimum(block_e, N_EXPERTS - 1)
    block_rows = jnp.clip(pstarts[block_e] + counts[block_e] - block_start, 0, rb).astype(jnp.int32)
    nvalid = (pends[-1:] // rb).astype(jnp.int32)
    return pstarts, pstarts // rb, padded // rb, block_rows, nvalid, n_rows


def kernel(x, c, ada_w, ada_b, norm1_g, norm2_g, w_in, tshift_mu, rwkv_w0, rwkv_w_up, rwkv_a0, rwkv_a_up,
           rwkv_g_up, rwkv_k_k, rwkv_k_a, rwkv_r_k, rwkv_ln_g, rwkv_ln_b, gmlp_ln_g, gmlp_ln_b, gmlp_ws,
           gmlp_bs, w_out_a, w_out_b, w_out, router_w, router_bias, exp_w1, exp_w3, exp_w2, shared_w1,
           shared_w3, shared_w2, normf_g):
    bsz, seq, d = x.shape
    t = bsz * seq
    xf = x.reshape(t, d)
    assert ada_w.shape[0] == 1, "single-layer block only"
    l = 0
    mod3 = _mod(c, ada_w[l], ada_b[l]).reshape(bsz, 6, d)
    p_rwkv, p_uv, p_gate = _inproj(xf, mod3, norm1_g[l], w_in[l], seq)
    y_a = _rwkv(p_rwkv, bsz, seq, tshift_mu[l], rwkv_w0[l], rwkv_w_up[l], rwkv_a0[l], rwkv_a_up[l],
                rwkv_g_up[l], rwkv_k_k[l], rwkv_k_a[l], rwkv_r_k[l], rwkv_ln_g[l], rwkv_ln_b[l])
    x1, h2, h2p = _mix(p_uv, p_gate, y_a, xf, mod3, seq, gmlp_ln_g[l], gmlp_ln_b[l], gmlp_ws[l], gmlp_bs[l],
                  w_out_a[l], w_out_b[l], w_out[l], norm2_g[l])
    eidx, wts = _route(h2, router_w[l], router_bias[l])
    rank, counts = _rank(eidx)
    counts = counts[:, 0].astype(jnp.int32)
    pstarts, first_blk, nblk_e, block_rows, nvalid, n_rows = _dispatch_plan(counts, t)
    dest = _dest(eidx, rank, pstarts)
    x_sorted = _dispatch(h2p, dest, n_rows)
    y_sorted = _expert(x_sorted, first_blk, nblk_e, nvalid, block_rows, exp_w1[l], exp_w3[l], exp_w2[l])
    wts_tk = wts.T
    tp = t // COMBINE_PARTS
    out = None
    for part in range(COMBINE_PARTS):
        sl = slice(part * tp, (part + 1) * tp)
        y_gathered = _gather_expert_rows(y_sorted, dest[:, sl])
        out = _combine(y_gathered, wts_tk[sl], h2, x1, mod3, seq, shared_w1[l], shared_w3[l], shared_w2[l],
                       normf_g, part, out)
    return out.reshape(bsz, seq, d)
```

```python
import functools

import jax
import jax.numpy as jnp
from jax import lax
from jax.experimental import pallas as pl
from jax.experimental.pallas import tpu as pltpu
from jax.experimental.pallas import tpu_sc as plsc

F32 = jnp.float32
BF16 = jnp.bfloat16

RWKV_HEAD = 64
RWKV_HEADS = 8
RWKV_WIDTH = RWKV_HEAD * RWKV_HEADS
DECAY_LORA = 64
AAA_LORA = 64
GATE_LORA = 128
RWKV_COLS = 3 * RWKV_WIDTH + DECAY_LORA + AAA_LORA + GATE_LORA
GMLP_WIDTH = 512
GMLP_BLOCK = 128
GMLP_GROUPS = 8
GMLP_GROUP_DIM = GMLP_WIDTH // GMLP_GROUPS
N_EXPERTS = 256
TOP_K = 8
N_EXPERT_GROUPS = 8
GROUP_SIZE = N_EXPERTS // N_EXPERT_GROUPS
TOPK_GROUPS = 4
ROUTED_SCALE = 2.5
RMS_EPS = 1e-6
LN_EPS = 1e-5
GN_EPS = 64e-5

VMEM_LIMIT_BYTES = 56 * 1024 * 1024

CHUNK = 64
RWKV_TILE = 256
ROW_BLOCK = 256
EXPERT_RING = 4
INPROJ_TILE = 256
MIX_TILE = 256
ROUTE_TILE = 512
RANK_TILE = 512
COMBINE_TILE = 256
COMBINE_PARTS = 4
LANES = 128
SC_WINDOW = 128
SC_GATHERS_PER_STEP = 2


def _params(*sem):
    return pltpu.CompilerParams(dimension_semantics=sem, vmem_limit_bytes=VMEM_LIMIT_BYTES)


def _mm(a, b):
    return jnp.dot(a.astype(BF16), b.astype(BF16), preferred_element_type=F32)


def _mm_nt(a, b):
    return lax.dot_general(a.astype(BF16), b.astype(BF16), (((1,), (1,)), ((), ())),
                           preferred_element_type=F32)


def _mm_tn(a, b):
    return lax.dot_general(a.astype(BF16), b.astype(BF16), (((0,), (0,)), ((), ())),
                           preferred_element_type=F32)


def _split3(a):
    hi = a.astype(BF16)
    r1 = a - hi.astype(F32)
    mid = r1.astype(BF16)
    lo = (r1 - mid.astype(F32)).astype(BF16)
    return hi, mid, lo


def _silu(x):
    return x * jax.nn.sigmoid(x)


def _rms(x):
    return x * lax.rsqrt(jnp.mean(x * x, axis=-1, keepdims=True) + RMS_EPS)


def _pack_halves(x):
    n = x.shape[1] // 2
    bits = lax.bitcast_convert_type(x.astype(BF16).astype(F32), jnp.uint32)
    return bits[:, :n] | (bits[:, n:] >> 16)


def _unpack_halves(u):
    hi = lax.bitcast_convert_type(u & jnp.uint32(0xFFFF0000), F32)
    lo = lax.bitcast_convert_type(u << 16, F32)
    return hi, lo


def _mod_kernel(c_ref, w_ref, b_ref, o_ref):
    s = _silu(c_ref[...])
    hi, mid, lo = _split3(s)
    whi, wmid, wlo = _split3(w_ref[...])
    dot = lambda p, q: jnp.dot(p, q, preferred_element_type=F32)
    acc = dot(hi, whi) + (dot(hi, wmid) + dot(mid, whi)) + (dot(hi, wlo) + dot(mid, wmid) + dot(lo, whi))
    o_ref[...] = acc + b_ref[...]


def _mod(c, ada_w, ada_b):
    bsz, d = c.shape
    n = ada_w.shape[1]
    tn = d
    return pl.pallas_call(
        _mod_kernel,
        grid=(n // tn,),
        in_specs=[pl.BlockSpec((bsz, d), lambda j: (0, 0)),
                  pl.BlockSpec((d, tn), lambda j: (0, j)),
                  pl.BlockSpec((1, tn), lambda j: (0, j))],
        out_specs=pl.BlockSpec((bsz, tn), lambda j: (0, j)),
        out_shape=jax.ShapeDtypeStruct((bsz, n), F32),
        compiler_params=_params("arbitrary"),
        name="mod",
    )(c, ada_w, ada_b.reshape(1, n))


def _inproj_kernel(x_ref, mod_ref, g_ref, wr_ref, wuv_ref, wg_ref, pr_ref, puv_ref, pg_ref):
    sh = mod_ref[0, 0:1, :]
    sc = mod_ref[0, 1:2, :]
    h = (_rms(x_ref[...]) * g_ref[...]) * (1.0 + sc) + sh
    hb = h.astype(BF16)
    pr_ref[...] = jnp.dot(hb, wr_ref[...], preferred_element_type=F32).astype(pr_ref.dtype)
    puv_ref[...] = jnp.dot(hb, wuv_ref[...], preferred_element_type=F32).astype(puv_ref.dtype)
    pg_ref[...] = jnp.dot(hb, wg_ref[...], preferred_element_type=F32).astype(pg_ref.dtype)


def _inproj(xf, mod3, norm1_g, w_in, seq):
    t, d = xf.shape
    tm = min(INPROJ_TILE, seq)
    per_seq = seq // tm
    wr = w_in[:, :RWKV_COLS].astype(BF16)
    wuv = w_in[:, RWKV_COLS:RWKV_COLS + 2 * GMLP_WIDTH].astype(BF16)
    wg = w_in[:, RWKV_COLS + 2 * GMLP_WIDTH:].astype(BF16)
    full = lambda a: pl.BlockSpec(a.shape, lambda i: (0,) * a.ndim)
    g = norm1_g.reshape(1, d)
    return pl.pallas_call(
        _inproj_kernel,
        grid=(t // tm,),
        in_specs=[pl.BlockSpec((tm, d), lambda i: (i, 0)),
                  pl.BlockSpec((1,) + mod3.shape[1:], lambda i: (i // per_seq, 0, 0)),
                  full(g), full(wr), full(wuv), full(wg)],
        out_specs=[pl.BlockSpec((tm, wr.shape[1]), lambda i: (i, 0)),
                   pl.BlockSpec((tm, wuv.shape[1]), lambda i: (i, 0)),
                   pl.BlockSpec((tm, wg.shape[1]), lambda i: (i, 0))],
        out_shape=[jax.ShapeDtypeStruct((t, wr.shape[1]), BF16),
                   jax.ShapeDtypeStruct((t, wuv.shape[1]), BF16),
                   jax.ShapeDtypeStruct((t, wg.shape[1]), BF16)],
        compiler_params=_params("arbitrary"),
        name="inproj",
    )(xf, mod3, g, wr, wuv, wg)


def _unit_lower_inverses(ns, row, col):
    c = ns[0].shape[0]
    eye = jnp.where(row == col, 1.0, 0.0).astype(F32)
    blk8 = (row // 8) == (col // 8)
    n8 = [jnp.where(blk8, n, 0.0) for n in ns]
    n8_2 = [_mm(a, a) for a in n8]
    n8_4 = [_mm(a, a) for a in n8_2]
    inv = [eye + a for a in n8]
    inv = [i + _mm(i, b) for i, b in zip(inv, n8_2)]
    inv = [i + _mm(i, b) for i, b in zip(inv, n8_4)]
    s = 8
    while s < c:
        sel = ((row // (2 * s)) == (col // (2 * s))) & ((row // s) != (col // s))
        low = [jnp.where(sel, n, 0.0) for n in ns]
        t1 = [_mm(l, i) for l, i in zip(low, inv)]
        inv = [i + _mm(i, t) for i, t in zip(inv, t1)]
        s *= 2
    return inv


def _rwkv_kernel(p_ref, mu_ref, w0_ref, wup_ref, a0_ref, aup_ref, gup_ref, kk_ref, ka_ref, rk_ref,
                 lng_ref, lnb_ref, bd_ref, y_ref, state_ref, carry_ref, ycat_ref, *, c):
    tm = p_ref.shape[0]
    nq = tm // c
    hd = RWKV_HEAD
    nh = RWKV_HEADS

    @pl.when(pl.program_id(1) == 0)
    def _():
        state_ref[...] = jnp.zeros_like(state_ref)
        carry_ref[...] = jnp.zeros_like(carry_ref)

    p = p_ref[...].astype(F32)
    rowc = lax.broadcasted_iota(jnp.int32, (tm, 1), 0)
    prev = jnp.where(rowc == 0, carry_ref[...], pltpu.roll(p, 1, axis=0))
    carry_ref[...] = p[tm - 1:tm, :]
    ps = p + (prev - p) * mu_ref[...]

    o = RWKV_WIDTH
    r = ps[:, 0:o]
    k = ps[:, o:2 * o]
    v = ps[:, 2 * o:3 * o]
    xw = ps[:, 3 * o:3 * o + DECAY_LORA]
    xa = ps[:, 3 * o + DECAY_LORA:3 * o + DECAY_LORA + AAA_LORA]
    xg = ps[:, 3 * o + DECAY_LORA + AAA_LORA:]

    z = -(w0_ref[...] + _mm(jnp.tanh(xw), wup_ref[...]))
    softplus = jnp.maximum(z, 0.0) + jnp.log1p(jnp.exp(-jnp.abs(z)))
    wlog = -softplus - 0.5
    ld = -jnp.exp(wlog)
    a = jax.nn.sigmoid(a0_ref[...] + _mm(xa, aup_ref[...]))
    g = _mm(jax.nn.sigmoid(xg), gup_ref[...])

    def head_sum(x):
        hi = x.astype(BF16)
        lo = (x - hi.astype(F32)).astype(BF16)
        bd = bd_ref[...]
        return jnp.dot(hi, bd, preferred_element_type=F32) + jnp.dot(lo, bd, preferred_element_type=F32)

    kkf = k * kk_ref[...]
    kkn = kkf * (1.0 / jnp.maximum(jnp.sqrt(head_sum(kkf * kkf)), 1e-12))
    k2 = k * (1.0 + (a - 1.0) * ka_ref[...])
    avec = -kkn
    bvec = kkn * a

    rowt = lax.broadcasted_iota(jnp.int32, (tm, tm), 0)
    colt = lax.broadcasted_iota(jnp.int32, (tm, tm), 1)
    tri = jnp.where((rowt >= colt) & ((rowt // c) == (colt // c)), 1.0, 0.0).astype(BF16)
    hi, mid, lo = _split3(ld)
    dot = lambda q: jnp.dot(tri, q, preferred_element_type=F32)
    cl = dot(hi) + dot(mid) + dot(lo)
    cl_end = jnp.concatenate(
        [jnp.broadcast_to(cl[(q + 1) * c - 1:(q + 1) * c, :], (c, o)) for q in range(nq)], axis=0)
    g_inv = jnp.exp(-cl)
    g_tail = jnp.exp(cl_end - cl)
    g_end = jnp.exp(cl_end)
    rt = (r * jnp.exp(cl)).astype(BF16)
    at = (avec * jnp.exp(cl - ld)).astype(BF16)
    kt = (k2 * g_inv).astype(BF16)
    bt = (bvec * g_inv).astype(BF16)
    kend = (k2 * g_tail).astype(BF16)
    bend = (bvec * g_tail).astype(BF16)
    vb = v.astype(BF16)

    row = lax.broadcasted_iota(jnp.int32, (c, c), 0)
    col = lax.broadcasted_iota(jnp.int32, (c, c), 1)
    row2 = lax.broadcasted_iota(jnp.int32, (2 * c, c), 0)
    col2 = lax.broadcasted_iota(jnp.int32, (2 * c, c), 1)
    causal2 = col2 <= jnp.where(row2 < c, row2 - 1, row2 - c)
    items = [(q, h) for q in range(nq) for h in range(nh)]
    blk = lambda x, q, h: x[q * c:(q + 1) * c, h * hd:(h + 1) * hd]

    lhs = [jnp.concatenate([blk(at, q, h), blk(rt, q, h)], axis=0) for q, h in items]
    mat_b = [jnp.where(causal2, _mm_nt(l, blk(bt, q, h)), 0.0) for l, (q, h) in zip(lhs, items)]
    mat_k = [jnp.where(causal2, _mm_nt(l, blk(kt, q, h)), 0.0).astype(BF16)
             for l, (q, h) in zip(lhs, items)]
    a_rb = [m[c:].astype(BF16) for m in mat_b]
    tinv = _unit_lower_inverses([m[:c] for m in mat_b], row, col)
    kv = [_mm(m, blk(vb, q, h)) for m, (q, h) in zip(mat_k, items)]
    y_loc = [m[c:] for m in kv]
    s_loc = [_mm_tn(blk(vb, q, h), blk(kend, q, h)) for q, h in items]
    sol = [_mm(t, jnp.concatenate([blk(at, q, h), m[:c].astype(BF16)], axis=1))
           for t, m, (q, h) in zip(tinv, kv, items)]
    wmat = [m[:, :hd].astype(BF16) for m in sol]
    ut = [m[:, hd:] for m in sol]

    state = [state_ref[h] for h in range(nh)]
    for q in range(nq):
        base = q * nh
        sb = [s.astype(BF16) for s in state]
        both = [_mm_nt(jnp.concatenate([wmat[base + h], blk(rt, q, h)], axis=0), sb[h]) for h in range(nh)]
        u = [both[h][:c] + ut[base + h] for h in range(nh)]
        ub = [x.astype(BF16) for x in u]
        yh = [both[h][c:] + y_loc[base + h] + _mm(a_rb[base + h], ub[h]) for h in range(nh)]
        state = [state[h] * g_end[q * c:q * c + 1, h * hd:(h + 1) * hd]
                 + _mm_tn(ub[h], blk(bend, q, h)) + s_loc[base + h] for h in range(nh)]
        for h in range(nh):
            ycat_ref[q * c:(q + 1) * c, h * hd:(h + 1) * hd] = yh[h]
    for h in range(nh):
        state_ref[h] = state[h]

    y = ycat_ref[...]
    dev = y - head_sum(y) * (1.0 / hd)
    rstd = lax.rsqrt(head_sum(dev * dev) * (1.0 / hd) + GN_EPS)
    bonus = head_sum(r * k2 * rk_ref[...])
    yn = dev * rstd * lng_ref[...] + lnb_ref[...]
    y_ref[...] = (yn + bonus * v) * g


def _rwkv(p_rwkv, bsz, seq, tshift_mu, w0, w_up, a0, a_up, g_up, k_k, k_a, r_k, ln_g, ln_b):
    t = p_rwkv.shape[0]
    c = min(CHUNK, seq)
    tm = min(RWKV_TILE, seq)
    nstep = seq // tm
    row = lambda a: a.reshape(1, -1)
    head_of = jnp.arange(RWKV_WIDTH, dtype=jnp.int32) // RWKV_HEAD
    block_diag = (head_of[:, None] == head_of[None, :]).astype(BF16)
    args = (row(tshift_mu), row(w0), w_up, row(a0), a_up, g_up, row(k_k), row(k_a), row(r_k),
            row(ln_g), row(ln_b), block_diag)
    full = lambda a: pl.BlockSpec(a.shape, lambda b, j: (0,) * a.ndim)
    return pl.pallas_call(
        functools.partial(_rwkv_kernel, c=c),
        grid=(bsz, nstep),
        in_specs=[pl.BlockSpec((tm, RWKV_COLS), lambda b, j: (b * nstep + j, 0))] + [full(a) for a in args],
        out_specs=pl.BlockSpec((tm, RWKV_WIDTH), lambda b, j: (b * nstep + j, 0)),
        out_shape=jax.ShapeDtypeStruct((t, RWKV_WIDTH), F32),
        scratch_shapes=[pltpu.VMEM((RWKV_HEADS, RWKV_HEAD, RWKV_HEAD), F32),
                        pltpu.VMEM((1, RWKV_COLS), F32),
                        pltpu.VMEM((tm, RWKV_WIDTH), F32)],
        compiler_params=_params("arbitrary", "arbitrary"),
        name="rwkv",
    )(p_rwkv, *args)


def _mix_kernel(puv_ref, pg_ref, ya_ref, x_ref, mod_ref, lng_ref, lnb_ref, ws_ref, bsm_ref,
                woa_ref, wob_ref, wo_ref, n2g_ref, x1_ref, h2_ref, h2p_ref, yb_ref):
    tm = x_ref.shape[0]
    gw = GMLP_WIDTH
    puv = puv_ref[...].astype(F32)
    u = jax.nn.gelu(puv[:, :gw])
    vg = jax.nn.gelu(puv[:, gw:])
    mu = jnp.mean(vg, axis=-1, keepdims=True)
    var = jnp.mean(jnp.square(vg - mu), axis=-1, keepdims=True)
    vln = (vg - mu) * lax.rsqrt(var + LN_EPS) * lng_ref[...] + lnb_ref[...]

    row = lax.broadcasted_iota(jnp.int32, (GMLP_BLOCK, GMLP_BLOCK), 0)
    col = lax.broadcasted_iota(jnp.int32, (GMLP_BLOCK, GMLP_BLOCK), 1)
    gd = GMLP_GROUP_DIM
    for gi in range(GMLP_GROUPS):
        wsm = jnp.where(row >= col, ws_ref[gi], 0.0).astype(BF16)
        for nb in range(tm // GMLP_BLOCK):
            rs = slice(nb * GMLP_BLOCK, (nb + 1) * GMLP_BLOCK)
            cs = slice(gi * gd, (gi + 1) * gd)
            sv = jnp.dot(wsm, vln[rs, cs].astype(BF16), preferred_element_type=F32) + bsm_ref[:, cs]
            yb_ref[rs, cs] = u[rs, cs] * sv

    pg = pg_ref[...].astype(F32)
    d = x_ref.shape[1]
    gate_a = jax.nn.sigmoid(pg[:, :d])
    gate_b = jax.nn.sigmoid(pg[:, d:])
    merged = gate_a * _mm(ya_ref[...], woa_ref[...]) + gate_b * _mm(yb_ref[...], wob_ref[...])
    g1 = mod_ref[0, 2:3, :]
    x1 = x_ref[...] + g1 * _mm(merged, wo_ref[...])
    x1_ref[...] = x1
    sh2 = mod_ref[0, 3:4, :]
    sc2 = mod_ref[0, 4:5, :]
    h2 = (_rms(x1) * n2g_ref[...]) * (1.0 + sc2) + sh2
    h2_ref[...] = h2
    packed = _pack_halves(h2)
    for j in range(h2p_ref.shape[0]):
        h2p_ref[j] = packed[:, j * LANES:(j + 1) * LANES]


def _mix(p_uv, p_gate, y_a, xf, mod3, seq, ln_g, ln_b, ws, bs, w_out_a, w_out_b, w_out, norm2_g):
    t, d = xf.shape
    tm = min(MIX_TILE, seq)
    per_seq = seq // tm
    row = lambda a: a.reshape(1, -1)
    bsm = jnp.repeat(bs.T, GMLP_GROUP_DIM, axis=1)
    consts = (row(ln_g), row(ln_b), ws, bsm, w_out_a.astype(BF16), w_out_b.astype(BF16),
              w_out.astype(BF16), row(norm2_g))
    full = lambda a: pl.BlockSpec(a.shape, lambda i: (0,) * a.ndim)
    tile = lambda a: pl.BlockSpec((tm, a.shape[1]), lambda i: (i, 0))
    return pl.pallas_call(
        _mix_kernel,
        grid=(t // tm,),
        in_specs=[tile(p_uv), tile(p_gate), tile(y_a), tile(xf),
                  pl.BlockSpec((1,) + mod3.shape[1:], lambda i: (i // per_seq, 0, 0))]
                 + [full(a) for a in consts],
        out_specs=[pl.BlockSpec((tm, d), lambda i: (i, 0)), pl.BlockSpec((tm, d), lambda i: (i, 0)),
                   pl.BlockSpec((d // 2 // LANES, tm, LANES), lambda i: (0, i, 0))],
        out_shape=[jax.ShapeDtypeStruct((t, d), F32), jax.ShapeDtypeStruct((t, d), F32),
                   jax.ShapeDtypeStruct((d // 2 // LANES, t, LANES), jnp.uint32)],
        scratch_shapes=[pltpu.VMEM((tm, GMLP_WIDTH), F32)],
        compiler_params=_params("arbitrary"),
        name="mix",
    )(p_uv, p_gate, y_a, xf, mod3, *consts)


def _first_argmax(vals, idx, big):
    m = jnp.max(vals, axis=0, keepdims=True)
    first = jnp.min(jnp.where(vals == m, idx, big), axis=0, keepdims=True)
    return m, first


def _route_kernel(h_ref, rwt_ref, bias_ref, eidx_ref, wts_ref):
    tm = h_ref.shape[0]
    ne = N_EXPERTS
    hhi, hmid, _ = _split3(h_ref[...])
    whi, wmid, _ = _split3(rwt_ref[...])
    nt = lambda p, q: lax.dot_general(p, q, (((1,), (1,)), ((), ())), preferred_element_type=F32)
    logits = nt(whi, hhi) + (nt(whi, hmid) + nt(wmid, hhi))
    scores = jax.nn.sigmoid(logits)
    sel = scores + bias_ref[...]
    neg = -jnp.inf

    gs = GROUP_SIZE
    gidx = lax.broadcasted_iota(jnp.int32, (gs, tm), 0)
    grp_rows = []
    for gi in range(N_EXPERT_GROUPS):
        blk = sel[gi * gs:(gi + 1) * gs, :]
        m1, i1 = _first_argmax(blk, gidx, gs)
        m2 = jnp.max(jnp.where(gidx == i1, neg, blk), axis=0, keepdims=True)
        grp_rows.append(m1 + m2)
    grp = jnp.concatenate(grp_rows, axis=0)

    ng = N_EXPERT_GROUPS
    giota = lax.broadcasted_iota(jnp.int32, (ng, tm), 0)
    gsel = jnp.zeros((ng, tm), jnp.bool_)
    work = grp
    for _ in range(TOPK_GROUPS):
        _, gi1 = _first_argmax(work, giota, ng)
        hit = giota == gi1
        gsel = gsel | hit
        work = jnp.where(hit, neg, work)
    gself = jnp.where(gsel, 1.0, 0.0)
    emask = jnp.concatenate(
        [jnp.broadcast_to(gself[gi:gi + 1, :], (gs, tm)) for gi in range(ng)], axis=0) > 0.5

    eiota = lax.broadcasted_iota(jnp.int32, (ne, tm), 0)
    work = jnp.where(emask, sel, neg)
    idx_rows, w_rows = [], []
    for _ in range(TOP_K):
        _, e1 = _first_argmax(work, eiota, ne)
        hit = eiota == e1
        idx_rows.append(e1)
        w_rows.append(jnp.sum(jnp.where(hit, scores, 0.0), axis=0, keepdims=True))
        work = jnp.where(hit, neg, work)
    eidx_ref[...] = jnp.concatenate(idx_rows, axis=0)
    w = jnp.concatenate(w_rows, axis=0)
    wts_ref[...] = w / jnp.sum(w, axis=0, keepdims=True) * ROUTED_SCALE


def _route(h2, router_w, router_bias):
    t, d = h2.shape
    tm = min(ROUTE_TILE, t)
    rwt = router_w.T
    bias = router_bias.reshape(N_EXPERTS, 1)
    return pl.pallas_call(
        _route_kernel,
        grid=(t // tm,),
        in_specs=[pl.BlockSpec((tm, d), lambda i: (i, 0)),
                  pl.BlockSpec(rwt.shape, lambda i: (0, 0)),
                  pl.BlockSpec(bias.shape, lambda i: (0, 0))],
        out_specs=[pl.BlockSpec((TOP_K, tm), lambda i: (0, i)), pl.BlockSpec((TOP_K, tm), lambda i: (0, i))],
        out_shape=[jax.ShapeDtypeStruct((TOP_K, t), jnp.int32), jax.ShapeDtypeStruct((TOP_K, t), F32)],
        compiler_params=_params("arbitrary"),
        name="route",
    )(h2, rwt, bias)


def _rank_kernel(e_ref, upper_ref, rank_ref, count_ref):
    tm = e_ref.shape[1]
    ne = N_EXPERTS

    @pl.when(pl.program_id(0) == 0)
    def _():
        count_ref[...] = jnp.zeros_like(count_ref)

    e = e_ref[...]
    eiota = lax.broadcasted_iota(jnp.int32, (ne, tm), 0)
    onehot = jnp.zeros((ne, tm), F32)
    for kk in range(TOP_K):
        onehot = onehot + jnp.where(eiota == e[kk:kk + 1, :], 1.0, 0.0)
    before = jnp.dot(onehot.astype(BF16), upper_ref[...], preferred_element_type=F32)
    base = count_ref[:, 0:1]
    tot = before + base
    rows = [jnp.sum(jnp.where(eiota == e[kk:kk + 1, :], tot, 0.0), axis=0, keepdims=True)
            for kk in range(TOP_K)]
    rank_ref[...] = jnp.concatenate(rows, axis=0).astype(jnp.int32)
    count_ref[...] = count_ref[...] + jnp.sum(onehot, axis=1, keepdims=True)


def _rank(eidx):
    k, t = eidx.shape
    tm = min(RANK_TILE, t)
    upper = jnp.triu(jnp.ones((tm, tm), F32), 1).astype(BF16)
    return pl.pallas_call(
        _rank_kernel,
        grid=(t // tm,),
        in_specs=[pl.BlockSpec((k, tm), lambda i: (0, i)), pl.BlockSpec((tm, tm), lambda i: (0, 0))],
        out_specs=[pl.BlockSpec((k, tm), lambda i: (0, i)), pl.BlockSpec((N_EXPERTS, 128), lambda i: (0, 0))],
        out_shape=[jax.ShapeDtypeStruct((k, t), jnp.int32), jax.ShapeDtypeStruct((N_EXPERTS, 128), F32)],
        compiler_params=_params("arbitrary"),
        name="rank",
    )(eidx, upper)


def _dest_kernel(e_ref, rank_ref, pst_ref, dest_ref):
    tm = e_ref.shape[1]
    e = e_ref[...]
    eiota = lax.broadcasted_iota(jnp.int32, (N_EXPERTS, tm), 0)
    pst = pst_ref[...]
    rows = [jnp.sum(jnp.where(eiota == e[kk:kk + 1, :], pst, 0.0), axis=0, keepdims=True)
            for kk in range(TOP_K)]
    dest_ref[...] = jnp.concatenate(rows, axis=0).astype(jnp.int32) + rank_ref[...]


def _dest(eidx, rank, pstarts):
    k, t = eidx.shape
    tm = min(RANK_TILE, t)
    pst = pstarts.astype(F32).reshape(N_EXPERTS, 1)
    return pl.pallas_call(
        _dest_kernel,
        grid=(t // tm,),
        in_specs=[pl.BlockSpec((k, tm), lambda i: (0, i)), pl.BlockSpec((k, tm), lambda i: (0, i)),
                  pl.BlockSpec((N_EXPERTS, 1), lambda i: (0, 0))],
        out_specs=pl.BlockSpec((k, tm), lambda i: (0, i)),
        out_shape=jax.ShapeDtypeStruct((k, t), jnp.int32),
        compiler_params=_params("arbitrary"),
        name="dest",
    )(eidx, rank, pst)


def _sc_scatter(rows, idx, n_out):
    n, lanes = rows.shape
    nstep, copies, w = idx.shape
    mesh = plsc.VectorSubcoreMesh(core_axis_name="core", subcore_axis_name="subcore")

    @functools.partial(pl.kernel, out_type=jax.ShapeDtypeStruct((n_out, lanes), rows.dtype), mesh=mesh,
                       scratch_types=[pltpu.SemaphoreType.DMA])
    def scatter_kernel(rows_hbm, idx_hbm, out_hbm, sem):
        def body(rows_vmem, idx_vmem):
            cps = [pltpu.make_async_copy(rows_vmem, out_hbm.at[idx_vmem.at[j]], sem) for j in range(copies)]
            for cp in cps:
                cp.start()
            for cp in cps:
                cp.wait()

        pltpu.emit_pipeline(
            body,
            grid=(nstep,),
            in_specs=[pl.BlockSpec((w, lanes), index_map=lambda i: (i, 0)),
                      pl.BlockSpec((copies, w), index_map=lambda i: (i, 0))],
            out_specs=[],
            core_axis_name=("core", "subcore"),
            dimension_semantics=(pltpu.PARALLEL,),
        )(rows_hbm, idx_hbm)

    return scatter_kernel(rows, idx.reshape(nstep * copies, w))


def _dispatch(h2p, dest_tk, n_rows):
    nchunk, t, lanes = h2p.shape
    k = dest_tk.shape[0]
    w = SC_WINDOW
    idx = dest_tk.reshape(k, t // w, w).transpose(1, 0, 2)[None]
    idx = idx + (jnp.arange(nchunk, dtype=jnp.int32) * n_rows)[:, None, None, None]
    out = _sc_scatter(h2p.reshape(nchunk * t, lanes), idx.reshape(nchunk * (t // w), k, w), nchunk * n_rows)
    return out.reshape(nchunk, n_rows, lanes)


def _expert_kernel(first_ref, nblk_ref, nv_ref, rows_ref, x_hbm, w1_ref, w3_ref, w2_ref, y_hbm,
                   xbuf, ybuf, xsem, ysem, w1b, w3b, w2b):
    e = pl.program_id(0)
    ring, nchunk, rb = xbuf.shape[0], xbuf.shape[1], xbuf.shape[2]
    half = nchunk * LANES
    total = nv_ref[0]

    def x_copy(g, slot):
        return pltpu.make_async_copy(x_hbm.at[:, pl.ds(g * rb, rb), :], xbuf.at[slot], xsem.at[slot])

    def y_copy(g, slot):
        return pltpu.make_async_copy(ybuf.at[slot], y_hbm.at[:, pl.ds(g * rb, rb), :], ysem.at[slot])

    @pl.when(e == 0)
    def _():
        for g0 in range(ring - 1):
            @pl.when(g0 < total)
            def _():
                x_copy(g0, g0).start()

    @pl.when(nblk_ref[e] > 0)
    def _():
        w1b[...] = w1_ref[0].astype(BF16)
        w3b[...] = w3_ref[0].astype(BF16)
        w2b[...] = w2_ref[0].astype(BF16)

    def block(j, carry):
        g = first_ref[e] + j
        slot = g % ring
        x_copy(g, slot).wait()

        @pl.when(g + ring - 1 < total)
        def _():
            x_copy(g + ring - 1, (g + ring - 1) % ring).start()

        live = lax.broadcasted_iota(jnp.int32, (rb, 1), 0) < rows_ref[g]
        x = jnp.concatenate([xbuf[slot, c] for c in range(nchunk)], axis=1)
        hi, lo = _unpack_halves(jnp.where(live, x, jnp.uint32(0)))
        xa = hi.astype(BF16)
        xb = lo.astype(BF16)
        dot = lambda p, q: jnp.dot(p, q, preferred_element_type=F32)
        h1 = dot(xa, w1b[:half, :]) + dot(xb, w1b[half:, :])
        h3 = dot(xa, w3b[:half, :]) + dot(xb, w3b[half:, :])
        hid = _silu(h1) * h3
        packed = _pack_halves(dot(hid.astype(BF16), w2b[...]))

        @pl.when(g >= ring)
        def _():
            y_copy(g - ring, slot).wait()

        for c in range(nchunk):
            ybuf[slot, c] = packed[:, c * LANES:(c + 1) * LANES]
        y_copy(g, slot).start()
        return carry

    lax.fori_loop(0, nblk_ref[e], block, 0)

    @pl.when(e == pl.num_programs(0) - 1)
    def _():
        for back in range(ring, 0, -1):
            @pl.when(total >= back)
            def _():
                y_copy(total - back, (total - back) % ring).wait()


def _expert(x_sorted, first_blk, nblk_e, nvalid, block_rows, w1, w3, w2):
    nchunk, n_rows, lanes = x_sorted.shape
    rb = ROW_BLOCK
    ne, d, f = w1.shape
    grid_spec = pltpu.PrefetchScalarGridSpec(
        num_scalar_prefetch=4,
        grid=(ne,),
        in_specs=[
            pl.BlockSpec(memory_space=pl.ANY),
            pl.BlockSpec((1, d, f), lambda e, *_: (e, 0, 0)),
            pl.BlockSpec((1, d, f), lambda e, *_: (e, 0, 0)),
            pl.BlockSpec((1, f, d), lambda e, *_: (e, 0, 0)),
        ],
        out_specs=pl.BlockSpec(memory_space=pl.ANY),
        scratch_shapes=[pltpu.VMEM((EXPERT_RING, nchunk, rb, lanes), jnp.uint32),
                        pltpu.VMEM((EXPERT_RING, nchunk, rb, lanes), jnp.uint32),
                        pltpu.SemaphoreType.DMA((EXPERT_RING,)), pltpu.SemaphoreType.DMA((EXPERT_RING,)),
                        pltpu.VMEM((d, f), BF16), pltpu.VMEM((d, f), BF16), pltpu.VMEM((f, d), BF16)],
    )
    return pl.pallas_call(
        _expert_kernel,
        grid_spec=grid_spec,
        out_shape=jax.ShapeDtypeStruct((nchunk, n_rows, lanes), jnp.uint32),
        compiler_params=_params("arbitrary"),
        name="expert",
    )(first_blk, nblk_e, nvalid, block_rows, x_sorted, w1, w3, w2)


def _sc_gather(table, idx):
    n = idx.shape[0]
    w = SC_WINDOW
    group = SC_GATHERS_PER_STEP
    mesh = plsc.VectorSubcoreMesh(core_axis_name="core", subcore_axis_name="subcore")

    @functools.partial(pl.kernel, out_type=jax.ShapeDtypeStruct((n, table.shape[1]), table.dtype), mesh=mesh,
                       scratch_types=[pltpu.SemaphoreType.DMA])
    def gather_kernel(table_hbm, idx_hbm, out_hbm, sem):
        def body(idx_vmem, out_vmem):
            copies = [pltpu.make_async_copy(table_hbm.at[idx_vmem.at[j]], out_vmem.at[pl.ds(j * w, w)], sem)
                      for j in range(group)]
            for cp in copies:
                cp.start()
            for cp in copies:
                cp.wait()

        pltpu.emit_pipeline(
            body,
            grid=(n // (group * w),),
            in_specs=[pl.BlockSpec((group, w), index_map=lambda i: (i, 0))],
            out_specs=[pl.BlockSpec((group * w, table.shape[1]), index_map=lambda i: (i, 0))],
            core_axis_name=("core", "subcore"),
            dimension_semantics=(pltpu.PARALLEL,),
        )(idx_hbm, out_hbm)

    return gather_kernel(table, idx.reshape(n // w, w))


def _gather_expert_rows(y_sorted, dest_tk):
    nchunk, n_rows, lanes = y_sorted.shape
    k, t = dest_tk.shape
    idx = dest_tk[None, :, :] + (jnp.arange(nchunk, dtype=jnp.int32) * n_rows)[:, None, None]
    rows = _sc_gather(y_sorted.reshape(nchunk * n_rows, lanes), idx.reshape(-1))
    return rows.reshape(nchunk, k, t, lanes)


def _combine_kernel(yg_ref, wts_ref, h_ref, x1_ref, mod_ref, sw1_ref, sw3_ref, sw2_ref, nfg_ref, *rest):
    o_ref = rest[-1]
    nchunk = yg_ref.shape[0]
    w = wts_ref[...]
    acc_hi = [None] * nchunk
    acc_lo = [None] * nchunk
    for kk in range(TOP_K):
        wk = w[:, kk:kk + 1]
        for j in range(nchunk):
            hi, lo = _unpack_halves(yg_ref[j, kk])
            acc_hi[j] = hi * wk if kk == 0 else acc_hi[j] + hi * wk
            acc_lo[j] = lo * wk if kk == 0 else acc_lo[j] + lo * wk
    routed = jnp.concatenate(acc_hi + acc_lo, axis=1)
    hb = h_ref[...].astype(BF16)
    hid = _silu(jnp.dot(hb, sw1_ref[...], preferred_element_type=F32)) * jnp.dot(
        hb, sw3_ref[...], preferred_element_type=F32)
    shared = jnp.dot(hid.astype(BF16), sw2_ref[...], preferred_element_type=F32)
    g2 = mod_ref[0, 5:6, :]
    x2 = x1_ref[...] + g2 * (routed + shared)
    o_ref[...] = _rms(x2) * nfg_ref[...]


def _combine(y_gathered, wts_tk, h2, x1, mod3, seq, sw1, sw3, sw2, normf_g, part, prev):
    t, d = h2.shape
    nchunk, k, tp, lanes = y_gathered.shape
    tm = min(COMBINE_TILE, seq, tp)
    per_seq = seq // tm
    off = part * (tp // tm)
    consts = (sw1.astype(BF16), sw3.astype(BF16), sw2.astype(BF16), normf_g.reshape(1, d))
    full = lambda a: pl.BlockSpec(a.shape, lambda i: (0,) * a.ndim)
    tile = lambda a: pl.BlockSpec((tm, a.shape[1]), lambda i: (off + i, 0))
    args = [y_gathered, wts_tk, h2, x1, mod3, *consts]
    in_specs = ([pl.BlockSpec((nchunk, k, tm, lanes), lambda i: (0, 0, i, 0)),
                 pl.BlockSpec((tm, k), lambda i: (i, 0)), tile(h2), tile(x1),
                 pl.BlockSpec((1,) + mod3.shape[1:], lambda i: ((off + i) // per_seq, 0, 0))]
                + [full(a) for a in consts])
    aliases = {}
    if prev is not None:
        aliases = {len(args): 0}
        args.append(prev)
        in_specs.append(pl.BlockSpec(memory_space=pl.ANY))
    return pl.pallas_call(
        _combine_kernel,
        grid=(tp // tm,),
        in_specs=in_specs,
        out_specs=pl.BlockSpec((tm, d), lambda i: (off + i, 0)),
        out_shape=jax.ShapeDtypeStruct((t, d), F32),
        input_output_aliases=aliases,
        compiler_params=_params("arbitrary"),
        name="combine",
    )(*args)


def _dispatch_plan(counts, t):
    rb = ROW_BLOCK
    padded = (counts + rb - 1) // rb * rb
    pends = jnp.cumsum(padded)
    pstarts = pends - padded
    n_rows = t * TOP_K + N_EXPERTS * rb
    block_start = jnp.arange(n_rows // rb, dtype=jnp.int32) * rb
    block_e = jnp.sum((pends[None, :] <= block_start[:, None]).astype(jnp.int32), axis=1)
    block_e = jnp.minimum(block_e, N_EXPERTS - 1)
    block_rows = jnp.clip(pstarts[block_e] + counts[block_e] - block_start, 0, rb).astype(jnp.int32)
    nvalid = (pends[-1:] // rb).astype(jnp.int32)
    return pstarts, pstarts // rb, padded // rb, block_rows, nvalid, n_rows


def kernel(x, c, ada_w, ada_b, norm1_g, norm2_g, w_in, tshift_mu, rwkv_w0, rwkv_w_up, rwkv_a0, rwkv_a_up,
           rwkv_g_up, rwkv_k_k, rwkv_k_a, rwkv_r_k, rwkv_ln_g, rwkv_ln_b, gmlp_ln_g, gmlp_ln_b, gmlp_ws,
           gmlp_bs, w_out_a, w_out_b, w_out, router_w, router_bias, exp_w1, exp_w3, exp_w2, shared_w1,
           shared_w3, shared_w2, normf_g):
    bsz, seq, d = x.shape
    t = bsz * seq
    xf = x.reshape(t, d)
    assert ada_w.shape[0] == 1, "single-layer block only"
    l = 0
    mod3 = _mod(c, ada_w[l], ada_b[l]).reshape(bsz, 6, d)
    p_rwkv, p_uv, p_gate = _inproj(xf, mod3, norm1_g[l], w_in[l], seq)
    y_a = _rwkv(p_rwkv, bsz, seq, tshift_mu[l], rwkv_w0[l], rwkv_w_up[l], rwkv_a0[l], rwkv_a_up[l],
                rwkv_g_up[l], rwkv_k_k[l], rwkv_k_a[l], rwkv_r_k[l], rwkv_ln_g[l], rwkv_ln_b[l])
    x1, h2, h2p = _mix(p_uv, p_gate, y_a, xf, mod3, seq, gmlp_ln_g[l], gmlp_ln_b[l], gmlp_ws[l], gmlp_bs[l],
                  w_out_a[l], w_out_b[l], w_out[l], norm2_g[l])
    eidx, wts = _route(h2, router_w[l], router_bias[l])
    rank, counts = _rank(eidx)
    counts = counts[:, 0].astype(jnp.int32)
    pstarts, first_blk, nblk_e, block_rows, nvalid, n_rows = _dispatch_plan(counts, t)
    dest = _dest(eidx, rank, pstarts)
    x_sorted = _dispatch(h2p, dest, n_rows)
    y_sorted = _expert(x_sorted, first_blk, nblk_e, nvalid, block_rows, exp_w1[l], exp_w3[l], exp_w2[l])
    wts_tk = wts.T
    tp = t // COMBINE_PARTS
    out = None
    for part in range(COMBINE_PARTS):
        sl = slice(part * tp, (part + 1) * tp)
        y_gathered = _gather_expert_rows(y_sorted, dest[:, sl])
        out = _combine(y_gathered, wts_tk[sl], h2, x1, mod3, seq, shared_w1[l], shared_w3[l], shared_w2[l],
                       normf_g, part, out)
    return out.reshape(bsz, seq, d)
```

```python
import functools

import jax
import jax.numpy as jnp
from jax import lax
from jax.experimental import pallas as pl
from jax.experimental.pallas import tpu as pltpu
from jax.experimental.pallas import tpu_sc as plsc

F32 = jnp.float32
BF16 = jnp.bfloat16

RWKV_HEAD = 64
RWKV_HEADS = 8
RWKV_WIDTH = RWKV_HEAD * RWKV_HEADS
DECAY_LORA = 64
AAA_LORA = 64
GATE_LORA = 128
RWKV_COLS = 3 * RWKV_WIDTH + DECAY_LORA + AAA_LORA + GATE_LORA
GMLP_WIDTH = 512
GMLP_BLOCK = 128
GMLP_GROUPS = 8
GMLP_GROUP_DIM = GMLP_WIDTH // GMLP_GROUPS
N_EXPERTS = 256
TOP_K = 8
N_EXPERT_GROUPS = 8
GROUP_SIZE = N_EXPERTS // N_EXPERT_GROUPS
TOPK_GROUPS = 4
ROUTED_SCALE = 2.5
RMS_EPS = 1e-6
LN_EPS = 1e-5
GN_EPS = 64e-5

VMEM_LIMIT_BYTES = 56 * 1024 * 1024

CHUNK = 64
RWKV_TILE = 256
ROW_BLOCK = 512
EXPERT_RING = 3
INPROJ_TILE = 256
MIX_TILE = 256
ROUTE_TILE = 512
RANK_TILE = 512
COMBINE_TILE = 256
COMBINE_PARTS = 4
LANES = 128
SC_WINDOW = 128
SC_GATHERS_PER_STEP = 2


def _params(*sem):
    return pltpu.CompilerParams(dimension_semantics=sem, vmem_limit_bytes=VMEM_LIMIT_BYTES)


def _mm(a, b):
    return jnp.dot(a.astype(BF16), b.astype(BF16), preferred_element_type=F32)


def _mm_nt(a, b):
    return lax.dot_general(a.astype(BF16), b.astype(BF16), (((1,), (1,)), ((), ())),
                           preferred_element_type=F32)


def _mm_tn(a, b):
    return lax.dot_general(a.astype(BF16), b.astype(BF16), (((0,), (0,)), ((), ())),
                           preferred_element_type=F32)


def _split3(a):
    hi = a.astype(BF16)
    r1 = a - hi.astype(F32)
    mid = r1.astype(BF16)
    lo = (r1 - mid.astype(F32)).astype(BF16)
    return hi, mid, lo


def _silu(x):
    return x * jax.nn.sigmoid(x)


def _rms(x):
    return x * lax.rsqrt(jnp.mean(x * x, axis=-1, keepdims=True) + RMS_EPS)


def _pack_halves(x):
    n = x.shape[1] // 2
    bits = lax.bitcast_convert_type(x.astype(BF16).astype(F32), jnp.uint32)
    return bits[:, :n] | (bits[:, n:] >> 16)


def _unpack_halves(u):
    hi = lax.bitcast_convert_type(u & jnp.uint32(0xFFFF0000), F32)
    lo = lax.bitcast_convert_type(u << 16, F32)
    return hi, lo


def _mod_kernel(c_ref, w_ref, b_ref, o_ref):
    s = _silu(c_ref[...])
    hi, mid, lo = _split3(s)
    whi, wmid, wlo = _split3(w_ref[...])
    dot = lambda p, q: jnp.dot(p, q, preferred_element_type=F32)
    acc = dot(hi, whi) + (dot(hi, wmid) + dot(mid, whi)) + (dot(hi, wlo) + dot(mid, wmid) + dot(lo, whi))
    o_ref[...] = acc + b_ref[...]


def _mod(c, ada_w, ada_b):
    bsz, d = c.shape
    n = ada_w.shape[1]
    tn = d
    return pl.pallas_call(
        _mod_kernel,
        grid=(n // tn,),
        in_specs=[pl.BlockSpec((bsz, d), lambda j: (0, 0)),
                  pl.BlockSpec((d, tn), lambda j: (0, j)),
                  pl.BlockSpec((1, tn), lambda j: (0, j))],
        out_specs=pl.BlockSpec((bsz, tn), lambda j: (0, j)),
        out_shape=jax.ShapeDtypeStruct((bsz, n), F32),
        compiler_params=_params("arbitrary"),
        name="mod",
    )(c, ada_w, ada_b.reshape(1, n))


def _inproj_kernel(x_ref, mod_ref, g_ref, wr_ref, wuv_ref, wg_ref, pr_ref, puv_ref, pg_ref):
    sh = mod_ref[0, 0:1, :]
    sc = mod_ref[0, 1:2, :]
    h = (_rms(x_ref[...]) * g_ref[...]) * (1.0 + sc) + sh
    hb = h.astype(BF16)
    pr_ref[...] = jnp.dot(hb, wr_ref[...], preferred_element_type=F32).astype(pr_ref.dtype)
    puv_ref[...] = jnp.dot(hb, wuv_ref[...], preferred_element_type=F32).astype(puv_ref.dtype)
    pg_ref[...] = jnp.dot(hb, wg_ref[...], preferred_element_type=F32).astype(pg_ref.dtype)


def _inproj(xf, mod3, norm1_g, w_in, seq):
    t, d = xf.shape
    tm = min(INPROJ_TILE, seq)
    per_seq = seq // tm
    wr = w_in[:, :RWKV_COLS].astype(BF16)
    wuv = w_in[:, RWKV_COLS:RWKV_COLS + 2 * GMLP_WIDTH].astype(BF16)
    wg = w_in[:, RWKV_COLS + 2 * GMLP_WIDTH:].astype(BF16)
    full = lambda a: pl.BlockSpec(a.shape, lambda i: (0,) * a.ndim)
    g = norm1_g.reshape(1, d)
    return pl.pallas_call(
        _inproj_kernel,
        grid=(t // tm,),
        in_specs=[pl.BlockSpec((tm, d), lambda i: (i, 0)),
                  pl.BlockSpec((1,) + mod3.shape[1:], lambda i: (i // per_seq, 0, 0)),
                  full(g), full(wr), full(wuv), full(wg)],
        out_specs=[pl.BlockSpec((tm, wr.shape[1]), lambda i: (i, 0)),
                   pl.BlockSpec((tm, wuv.shape[1]), lambda i: (i, 0)),
                   pl.BlockSpec((tm, wg.shape[1]), lambda i: (i, 0))],
        out_shape=[jax.ShapeDtypeStruct((t, wr.shape[1]), BF16),
                   jax.ShapeDtypeStruct((t, wuv.shape[1]), BF16),
                   jax.ShapeDtypeStruct((t, wg.shape[1]), BF16)],
        compiler_params=_params("arbitrary"),
        name="inproj",
    )(xf, mod3, g, wr, wuv, wg)


def _unit_lower_inverses(ns, row, col):
    c = ns[0].shape[0]
    eye = jnp.where(row == col, 1.0, 0.0).astype(F32)
    blk8 = (row // 8) == (col // 8)
    n8 = [jnp.where(blk8, n, 0.0) for n in ns]
    n8_2 = [_mm(a, a) for a in n8]
    n8_4 = [_mm(a, a) for a in n8_2]
    inv = [eye + a for a in n8]
    inv = [i + _mm(i, b) for i, b in zip(inv, n8_2)]
    inv = [i + _mm(i, b) for i, b in zip(inv, n8_4)]
    s = 8
    while s < c:
        sel = ((row // (2 * s)) == (col // (2 * s))) & ((row // s) != (col // s))
        low = [jnp.where(sel, n, 0.0) for n in ns]
        t1 = [_mm(l, i) for l, i in zip(low, inv)]
        inv = [i + _mm(i, t) for i, t in zip(inv, t1)]
        s *= 2
    return inv


def _rwkv_kernel(p_ref, mu_ref, w0_ref, wup_ref, a0_ref, aup_ref, gup_ref, kk_ref, ka_ref, rk_ref,
                 lng_ref, lnb_ref, bd_ref, y_ref, state_ref, carry_ref, ycat_ref, *, c):
    tm = p_ref.shape[0]
    nq = tm // c
    hd = RWKV_HEAD
    nh = RWKV_HEADS

    @pl.when(pl.program_id(1) == 0)
    def _():
        state_ref[...] = jnp.zeros_like(state_ref)
        carry_ref[...] = jnp.zeros_like(carry_ref)

    p = p_ref[...].astype(F32)
    rowc = lax.broadcasted_iota(jnp.int32, (tm, 1), 0)
    prev = jnp.where(rowc == 0, carry_ref[...], pltpu.roll(p, 1, axis=0))
    carry_ref[...] = p[tm - 1:tm, :]
    ps = p + (prev - p) * mu_ref[...]

    o = RWKV_WIDTH
    r = ps[:, 0:o]
    k = ps[:, o:2 * o]
    v = ps[:, 2 * o:3 * o]
    xw = ps[:, 3 * o:3 * o + DECAY_LORA]
    xa = ps[:, 3 * o + DECAY_LORA:3 * o + DECAY_LORA + AAA_LORA]
    xg = ps[:, 3 * o + DECAY_LORA + AAA_LORA:]

    z = -(w0_ref[...] + _mm(jnp.tanh(xw), wup_ref[...]))
    softplus = jnp.maximum(z, 0.0) + jnp.log1p(jnp.exp(-jnp.abs(z)))
    wlog = -softplus - 0.5
    ld = -jnp.exp(wlog)
    a = jax.nn.sigmoid(a0_ref[...] + _mm(xa, aup_ref[...]))
    g = _mm(jax.nn.sigmoid(xg), gup_ref[...])

    def head_sum(x):
        hi = x.astype(BF16)
        lo = (x - hi.astype(F32)).astype(BF16)
        bd = bd_ref[...]
        return jnp.dot(hi, bd, preferred_element_type=F32) + jnp.dot(lo, bd, preferred_element_type=F32)

    kkf = k * kk_ref[...]
    kkn = kkf * (1.0 / jnp.maximum(jnp.sqrt(head_sum(kkf * kkf)), 1e-12))
    k2 = k * (1.0 + (a - 1.0) * ka_ref[...])
    avec = -kkn
    bvec = kkn * a

    rowt = lax.broadcasted_iota(jnp.int32, (tm, tm), 0)
    colt = lax.broadcasted_iota(jnp.int32, (tm, tm), 1)
    tri = jnp.where((rowt >= colt) & ((rowt // c) == (colt // c)), 1.0, 0.0).astype(BF16)
    hi, mid, lo = _split3(ld)
    dot = lambda q: jnp.dot(tri, q, preferred_element_type=F32)
    cl = dot(hi) + dot(mid) + dot(lo)
    cl_end = jnp.concatenate(
        [jnp.broadcast_to(cl[(q + 1) * c - 1:(q + 1) * c, :], (c, o)) for q in range(nq)], axis=0)
    g_inv = jnp.exp(-cl)
    g_tail = jnp.exp(cl_end - cl)
    g_end = jnp.exp(cl_end)
    rt = (r * jnp.exp(cl)).astype(BF16)
    at = (avec * jnp.exp(cl - ld)).astype(BF16)
    kt = (k2 * g_inv).astype(BF16)
    bt = (bvec * g_inv).astype(BF16)
    kend = (k2 * g_tail).astype(BF16)
    bend = (bvec * g_tail).astype(BF16)
    vb = v.astype(BF16)

    row = lax.broadcasted_iota(jnp.int32, (c, c), 0)
    col = lax.broadcasted_iota(jnp.int32, (c, c), 1)
    strict = row > col
    incl = row >= col
    items = [(q, h) for q in range(nq) for h in range(nh)]
    blk = lambda x, q, h: x[q * c:(q + 1) * c, h * hd:(h + 1) * hd]

    lhs = [jnp.concatenate([blk(at, q, h), blk(rt, q, h)], axis=0) for q, h in items]
    mat_b = [_mm_nt(l, blk(bt, q, h)) for l, (q, h) in zip(lhs, items)]
    mat_k = [_mm_nt(l, blk(kt, q, h)) for l, (q, h) in zip(lhs, items)]
    a_ab = [jnp.where(strict, m[:c], 0.0) for m in mat_b]
    a_rb = [jnp.where(incl, m[c:], 0.0).astype(BF16) for m in mat_b]
    a_ak = [jnp.where(strict, m[:c], 0.0) for m in mat_k]
    a_rk = [jnp.where(incl, m[c:], 0.0) for m in mat_k]
    tinv = _unit_lower_inverses(a_ab, row, col)
    akv = [_mm(m, blk(vb, q, h)) for m, (q, h) in zip(a_ak, items)]
    y_loc = [_mm(m, blk(vb, q, h)) for m, (q, h) in zip(a_rk, items)]
    s_loc = [_mm_tn(blk(vb, q, h), blk(kend, q, h)) for q, h in items]
    wmat = [_mm(t, blk(at, q, h)).astype(BF16) for t, (q, h) in zip(tinv, items)]
    ut = [_mm(t, m) for t, m in zip(tinv, akv)]

    state = [state_ref[h] for h in range(nh)]
    for q in range(nq):
        base = q * nh
        sb = [s.astype(BF16) for s in state]
        u = [_mm_nt(wmat[base + h], sb[h]) + ut[base + h] for h in range(nh)]
        yh = [_mm_nt(blk(rt, q, h), sb[h]) + y_loc[base + h] for h in range(nh)]
        ub = [x.astype(BF16) for x in u]
        yh = [y0 + _mm(a_rb[base + h], ub[h]) for h, y0 in enumerate(yh)]
        state = [state[h] * g_end[q * c:q * c + 1, h * hd:(h + 1) * hd]
                 + _mm_tn(ub[h], blk(bend, q, h)) + s_loc[base + h] for h in range(nh)]
        for h in range(nh):
            ycat_ref[q * c:(q + 1) * c, h * hd:(h + 1) * hd] = yh[h]
    for h in range(nh):
        state_ref[h] = state[h]

    y = ycat_ref[...]
    dev = y - head_sum(y) * (1.0 / hd)
    rstd = lax.rsqrt(head_sum(dev * dev) * (1.0 / hd) + GN_EPS)
    bonus = head_sum(r * k2 * rk_ref[...])
    yn = dev * rstd * lng_ref[...] + lnb_ref[...]
    y_ref[...] = (yn + bonus * v) * g


def _rwkv(p_rwkv, bsz, seq, tshift_mu, w0, w_up, a0, a_up, g_up, k_k, k_a, r_k, ln_g, ln_b):
    t = p_rwkv.shape[0]
    c = min(CHUNK, seq)
    tm = min(RWKV_TILE, seq)
    nstep = seq // tm
    row = lambda a: a.reshape(1, -1)
    head_of = jnp.arange(RWKV_WIDTH, dtype=jnp.int32) // RWKV_HEAD
    block_diag = (head_of[:, None] == head_of[None, :]).astype(BF16)
    args = (row(tshift_mu), row(w0), w_up, row(a0), a_up, g_up, row(k_k), row(k_a), row(r_k),
            row(ln_g), row(ln_b), block_diag)
    full = lambda a: pl.BlockSpec(a.shape, lambda b, j: (0,) * a.ndim)
    return pl.pallas_call(
        functools.partial(_rwkv_kernel, c=c),
        grid=(bsz, nstep),
        in_specs=[pl.BlockSpec((tm, RWKV_COLS), lambda b, j: (b * nstep + j, 0))] + [full(a) for a in args],
        out_specs=pl.BlockSpec((tm, RWKV_WIDTH), lambda b, j: (b * nstep + j, 0)),
        out_shape=jax.ShapeDtypeStruct((t, RWKV_WIDTH), F32),
        scratch_shapes=[pltpu.VMEM((RWKV_HEADS, RWKV_HEAD, RWKV_HEAD), F32),
                        pltpu.VMEM((1, RWKV_COLS), F32),
                        pltpu.VMEM((tm, RWKV_WIDTH), F32)],
        compiler_params=_params("arbitrary", "arbitrary"),
        name="rwkv",
    )(p_rwkv, *args)


def _mix_kernel(puv_ref, pg_ref, ya_ref, x_ref, mod_ref, lng_ref, lnb_ref, ws_ref, bsm_ref,
                woa_ref, wob_ref, wo_ref, n2g_ref, x1_ref, h2_ref, h2p_ref, yb_ref):
    tm = x_ref.shape[0]
    gw = GMLP_WIDTH
    puv = puv_ref[...].astype(F32)
    u = jax.nn.gelu(puv[:, :gw])
    vg = jax.nn.gelu(puv[:, gw:])
    mu = jnp.mean(vg, axis=-1, keepdims=True)
    var = jnp.mean(jnp.square(vg - mu), axis=-1, keepdims=True)
    vln = (vg - mu) * lax.rsqrt(var + LN_EPS) * lng_ref[...] + lnb_ref[...]

    row = lax.broadcasted_iota(jnp.int32, (GMLP_BLOCK, GMLP_BLOCK), 0)
    col = lax.broadcasted_iota(jnp.int32, (GMLP_BLOCK, GMLP_BLOCK), 1)
    gd = GMLP_GROUP_DIM
    for gi in range(GMLP_GROUPS):
        wsm = jnp.where(row >= col, ws_ref[gi], 0.0).astype(BF16)
        for nb in range(tm // GMLP_BLOCK):
            rs = slice(nb * GMLP_BLOCK, (nb + 1) * GMLP_BLOCK)
            cs = slice(gi * gd, (gi + 1) * gd)
            sv = jnp.dot(wsm, vln[rs, cs].astype(BF16), preferred_element_type=F32) + bsm_ref[:, cs]
            yb_ref[rs, cs] = u[rs, cs] * sv

    pg = pg_ref[...].astype(F32)
    d = x_ref.shape[1]
    gate_a = jax.nn.sigmoid(pg[:, :d])
    gate_b = jax.nn.sigmoid(pg[:, d:])
    merged = gate_a * _mm(ya_ref[...], woa_ref[...]) + gate_b * _mm(yb_ref[...], wob_ref[...])
    g1 = mod_ref[0, 2:3, :]
    x1 = x_ref[...] + g1 * _mm(merged, wo_ref[...])
    x1_ref[...] = x1
    sh2 = mod_ref[0, 3:4, :]
    sc2 = mod_ref[0, 4:5, :]
    h2 = (_rms(x1) * n2g_ref[...]) * (1.0 + sc2) + sh2
    h2_ref[...] = h2
    packed = _pack_halves(h2)
    for j in range(h2p_ref.shape[0]):
        h2p_ref[j] = packed[:, j * LANES:(j + 1) * LANES]


def _mix(p_uv, p_gate, y_a, xf, mod3, seq, ln_g, ln_b, ws, bs, w_out_a, w_out_b, w_out, norm2_g):
    t, d = xf.shape
    tm = min(MIX_TILE, seq)
    per_seq = seq // tm
    row = lambda a: a.reshape(1, -1)
    bsm = jnp.repeat(bs.T, GMLP_GROUP_DIM, axis=1)
    consts = (row(ln_g), row(ln_b), ws, bsm, w_out_a.astype(BF16), w_out_b.astype(BF16),
              w_out.astype(BF16), row(norm2_g))
    full = lambda a: pl.BlockSpec(a.shape, lambda i: (0,) * a.ndim)
    tile = lambda a: pl.BlockSpec((tm, a.shape[1]), lambda i: (i, 0))
    return pl.pallas_call(
        _mix_kernel,
        grid=(t // tm,),
        in_specs=[tile(p_uv), tile(p_gate), tile(y_a), tile(xf),
                  pl.BlockSpec((1,) + mod3.shape[1:], lambda i: (i // per_seq, 0, 0))]
                 + [full(a) for a in consts],
        out_specs=[pl.BlockSpec((tm, d), lambda i: (i, 0)), pl.BlockSpec((tm, d), lambda i: (i, 0)),
                   pl.BlockSpec((d // 2 // LANES, tm, LANES), lambda i: (0, i, 0))],
        out_shape=[jax.ShapeDtypeStruct((t, d), F32), jax.ShapeDtypeStruct((t, d), F32),
                   jax.ShapeDtypeStruct((d // 2 // LANES, t, LANES), jnp.uint32)],
        scratch_shapes=[pltpu.VMEM((tm, GMLP_WIDTH), F32)],
        compiler_params=_params("arbitrary"),
        name="mix",
    )(p_uv, p_gate, y_a, xf, mod3, *consts)


def _first_argmax(vals, idx, big):
    m = jnp.max(vals, axis=0, keepdims=True)
    first = jnp.min(jnp.where(vals == m, idx, big), axis=0, keepdims=True)
    return m, first


def _route_kernel(h_ref, rwt_ref, bias_ref, eidx_ref, wts_ref):
    tm = h_ref.shape[0]
    ne = N_EXPERTS
    hhi, hmid, _ = _split3(h_ref[...])
    whi, wmid, _ = _split3(rwt_ref[...])
    nt = lambda p, q: lax.dot_general(p, q, (((1,), (1,)), ((), ())), preferred_element_type=F32)
    logits = nt(whi, hhi) + (nt(whi, hmid) + nt(wmid, hhi))
    scores = jax.nn.sigmoid(logits)
    sel = scores + bias_ref[...]
    neg = -jnp.inf

    gs = GROUP_SIZE
    gidx = lax.broadcasted_iota(jnp.int32, (gs, tm), 0)
    grp_rows = []
    for gi in range(N_EXPERT_GROUPS):
        blk = sel[gi * gs:(gi + 1) * gs, :]
        m1, i1 = _first_argmax(blk, gidx, gs)
        m2 = jnp.max(jnp.where(gidx == i1, neg, blk), axis=0, keepdims=True)
        grp_rows.append(m1 + m2)
    grp = jnp.concatenate(grp_rows, axis=0)

    ng = N_EXPERT_GROUPS
    giota = lax.broadcasted_iota(jnp.int32, (ng, tm), 0)
    gsel = jnp.zeros((ng, tm), jnp.bool_)
    work = grp
    for _ in range(TOPK_GROUPS):
        _, gi1 = _first_argmax(work, giota, ng)
        hit = giota == gi1
        gsel = gsel | hit
        work = jnp.where(hit, neg, work)
    gself = jnp.where(gsel, 1.0, 0.0)
    emask = jnp.concatenate(
        [jnp.broadcast_to(gself[gi:gi + 1, :], (gs, tm)) for gi in range(ng)], axis=0) > 0.5

    eiota = lax.broadcasted_iota(jnp.int32, (ne, tm), 0)
    work = jnp.where(emask, sel, neg)
    idx_rows, w_rows = [], []
    for _ in range(TOP_K):
        _, e1 = _first_argmax(work, eiota, ne)
        hit = eiota == e1
        idx_rows.append(e1)
        w_rows.append(jnp.sum(jnp.where(hit, scores, 0.0), axis=0, keepdims=True))
        work = jnp.where(hit, neg, work)
    eidx_ref[...] = jnp.concatenate(idx_rows, axis=0)
    w = jnp.concatenate(w_rows, axis=0)
    wts_ref[...] = w / jnp.sum(w, axis=0, keepdims=True) * ROUTED_SCALE


def _route(h2, router_w, router_bias):
    t, d = h2.shape
    tm = min(ROUTE_TILE, t)
    rwt = router_w.T
    bias = router_bias.reshape(N_EXPERTS, 1)
    return pl.pallas_call(
        _route_kernel,
        grid=(t // tm,),
        in_specs=[pl.BlockSpec((tm, d), lambda i: (i, 0)),
                  pl.BlockSpec(rwt.shape, lambda i: (0, 0)),
                  pl.BlockSpec(bias.shape, lambda i: (0, 0))],
        out_specs=[pl.BlockSpec((TOP_K, tm), lambda i: (0, i)), pl.BlockSpec((TOP_K, tm), lambda i: (0, i))],
        out_shape=[jax.ShapeDtypeStruct((TOP_K, t), jnp.int32), jax.ShapeDtypeStruct((TOP_K, t), F32)],
        compiler_params=_params("arbitrary"),
        name="route",
    )(h2, rwt, bias)


def _rank_kernel(e_ref, upper_ref, rank_ref, count_ref):
    tm = e_ref.shape[1]
    ne = N_EXPERTS

    @pl.when(pl.program_id(0) == 0)
    def _():
        count_ref[...] = jnp.zeros_like(count_ref)

    e = e_ref[...]
    eiota = lax.broadcasted_iota(jnp.int32, (ne, tm), 0)
    onehot = jnp.zeros((ne, tm), F32)
    for kk in range(TOP_K):
        onehot = onehot + jnp.where(eiota == e[kk:kk + 1, :], 1.0, 0.0)
    before = jnp.dot(onehot.astype(BF16), upper_ref[...], preferred_element_type=F32)
    base = count_ref[:, 0:1]
    tot = before + base
    rows = [jnp.sum(jnp.where(eiota == e[kk:kk + 1, :], tot, 0.0), axis=0, keepdims=True)
            for kk in range(TOP_K)]
    rank_ref[...] = jnp.concatenate(rows, axis=0).astype(jnp.int32)
    count_ref[...] = count_ref[...] + jnp.sum(onehot, axis=1, keepdims=True)


def _rank(eidx):
    k, t = eidx.shape
    tm = min(RANK_TILE, t)
    upper = jnp.triu(jnp.ones((tm, tm), F32), 1).astype(BF16)
    return pl.pallas_call(
        _rank_kernel,
        grid=(t // tm,),
        in_specs=[pl.BlockSpec((k, tm), lambda i: (0, i)), pl.BlockSpec((tm, tm), lambda i: (0, 0))],
        out_specs=[pl.BlockSpec((k, tm), lambda i: (0, i)), pl.BlockSpec((N_EXPERTS, 128), lambda i: (0, 0))],
        out_shape=[jax.ShapeDtypeStruct((k, t), jnp.int32), jax.ShapeDtypeStruct((N_EXPERTS, 128), F32)],
        compiler_params=_params("arbitrary"),
        name="rank",
    )(eidx, upper)


def _dest_kernel(e_ref, rank_ref, pst_ref, dest_ref):
    tm = e_ref.shape[1]
    e = e_ref[...]
    eiota = lax.broadcasted_iota(jnp.int32, (N_EXPERTS, tm), 0)
    pst = pst_ref[...]
    rows = [jnp.sum(jnp.where(eiota == e[kk:kk + 1, :], pst, 0.0), axis=0, keepdims=True)
            for kk in range(TOP_K)]
    dest_ref[...] = jnp.concatenate(rows, axis=0).astype(jnp.int32) + rank_ref[...]


def _dest(eidx, rank, pstarts):
    k, t = eidx.shape
    tm = min(RANK_TILE, t)
    pst = pstarts.astype(F32).reshape(N_EXPERTS, 1)
    return pl.pallas_call(
        _dest_kernel,
        grid=(t // tm,),
        in_specs=[pl.BlockSpec((k, tm), lambda i: (0, i)), pl.BlockSpec((k, tm), lambda i: (0, i)),
                  pl.BlockSpec((N_EXPERTS, 1), lambda i: (0, 0))],
        out_specs=pl.BlockSpec((k, tm), lambda i: (0, i)),
        out_shape=jax.ShapeDtypeStruct((k, t), jnp.int32),
        compiler_params=_params("arbitrary"),
        name="dest",
    )(eidx, rank, pst)


def _sc_scatter(rows, idx, n_out):
    n, lanes = rows.shape
    nstep, copies, w = idx.shape
    mesh = plsc.VectorSubcoreMesh(core_axis_name="core", subcore_axis_name="subcore")

    @functools.partial(pl.kernel, out_type=jax.ShapeDtypeStruct((n_out, lanes), rows.dtype), mesh=mesh,
                       scratch_types=[pltpu.SemaphoreType.DMA])
    def scatter_kernel(rows_hbm, idx_hbm, out_hbm, sem):
        def body(rows_vmem, idx_vmem):
            cps = [pltpu.make_async_copy(rows_vmem, out_hbm.at[idx_vmem.at[j]], sem) for j in range(copies)]
            for cp in cps:
                cp.start()
            for cp in cps:
                cp.wait()

        pltpu.emit_pipeline(
            body,
            grid=(nstep,),
            in_specs=[pl.BlockSpec((w, lanes), index_map=lambda i: (i, 0)),
                      pl.BlockSpec((copies, w), index_map=lambda i: (i, 0))],
            out_specs=[],
            core_axis_name=("core", "subcore"),
            dimension_semantics=(pltpu.PARALLEL,),
        )(rows_hbm, idx_hbm)

    return scatter_kernel(rows, idx.reshape(nstep * copies, w))


def _dispatch(h2p, dest_tk, n_rows):
    nchunk, t, lanes = h2p.shape
    k = dest_tk.shape[0]
    w = SC_WINDOW
    idx = dest_tk.reshape(k, t // w, w).transpose(1, 0, 2)[None]
    idx = idx + (jnp.arange(nchunk, dtype=jnp.int32) * n_rows)[:, None, None, None]
    out = _sc_scatter(h2p.reshape(nchunk * t, lanes), idx.reshape(nchunk * (t // w), k, w), nchunk * n_rows)
    return out.reshape(nchunk, n_rows, lanes)


def _expert_kernel(first_ref, nblk_ref, nv_ref, rows_ref, x_hbm, w1_ref, w3_ref, w2_ref, y_hbm,
                   xbuf, ybuf, xsem, ysem, w1b, w3b, w2b):
    e = pl.program_id(0)
    ring, nchunk, rb = xbuf.shape[0], xbuf.shape[1], xbuf.shape[2]
    half = nchunk * LANES
    total = nv_ref[0]

    def x_copy(g, slot):
        return pltpu.make_async_copy(x_hbm.at[:, pl.ds(g * rb, rb), :], xbuf.at[slot], xsem.at[slot])

    def y_copy(g, slot):
        return pltpu.make_async_copy(ybuf.at[slot], y_hbm.at[:, pl.ds(g * rb, rb), :], ysem.at[slot])

    @pl.when(e == 0)
    def _():
        for g0 in range(ring - 1):
            @pl.when(g0 < total)
            def _():
                x_copy(g0, g0).start()

    @pl.when(nblk_ref[e] > 0)
    def _():
        w1b[...] = w1_ref[0].astype(BF16)
        w3b[...] = w3_ref[0].astype(BF16)
        w2b[...] = w2_ref[0].astype(BF16)

    def block(j, carry):
        g = first_ref[e] + j
        slot = g % ring
        x_copy(g, slot).wait()

        @pl.when(g + ring - 1 < total)
        def _():
            x_copy(g + ring - 1, (g + ring - 1) % ring).start()

        live = lax.broadcasted_iota(jnp.int32, (rb, 1), 0) < rows_ref[g]
        x = jnp.concatenate([xbuf[slot, c] for c in range(nchunk)], axis=1)
        hi, lo = _unpack_halves(jnp.where(live, x, jnp.uint32(0)))
        xa = hi.astype(BF16)
        xb = lo.astype(BF16)
        dot = lambda p, q: jnp.dot(p, q, preferred_element_type=F32)
        h1 = dot(xa, w1b[:half, :]) + dot(xb, w1b[half:, :])
        h3 = dot(xa, w3b[:half, :]) + dot(xb, w3b[half:, :])
        hid = _silu(h1) * h3
        packed = _pack_halves(dot(hid.astype(BF16), w2b[...]))

        @pl.when(g >= ring)
        def _():
            y_copy(g - ring, slot).wait()

        for c in range(nchunk):
            ybuf[slot, c] = packed[:, c * LANES:(c + 1) * LANES]
        y_copy(g, slot).start()
        return carry

    lax.fori_loop(0, nblk_ref[e], block, 0)

    @pl.when(e == pl.num_programs(0) - 1)
    def _():
        for back in range(ring, 0, -1):
            @pl.when(total >= back)
            def _():
                y_copy(total - back, (total - back) % ring).wait()


def _expert(x_sorted, first_blk, nblk_e, nvalid, block_rows, w1, w3, w2):
    nchunk, n_rows, lanes = x_sorted.shape
    rb = ROW_BLOCK
    ne, d, f = w1.shape
    grid_spec = pltpu.PrefetchScalarGridSpec(
        num_scalar_prefetch=4,
        grid=(ne,),
        in_specs=[
            pl.BlockSpec(memory_space=pl.ANY),
            pl.BlockSpec((1, d, f), lambda e, *_: (e, 0, 0)),
            pl.BlockSpec((1, d, f), lambda e, *_: (e, 0, 0)),
            pl.BlockSpec((1, f, d), lambda e, *_: (e, 0, 0)),
        ],
        out_specs=pl.BlockSpec(memory_space=pl.ANY),
        scratch_shapes=[pltpu.VMEM((EXPERT_RING, nchunk, rb, lanes), jnp.uint32),
                        pltpu.VMEM((EXPERT_RING, nchunk, rb, lanes), jnp.uint32),
                        pltpu.SemaphoreType.DMA((EXPERT_RING,)), pltpu.SemaphoreType.DMA((EXPERT_RING,)),
                        pltpu.VMEM((d, f), BF16), pltpu.VMEM((d, f), BF16), pltpu.VMEM((f, d), BF16)],
    )
    return pl.pallas_call(
        _expert_kernel,
        grid_spec=grid_spec,
        out_shape=jax.ShapeDtypeStruct((nchunk, n_rows, lanes), jnp.uint32),
        compiler_params=_params("arbitrary"),
        name="expert",
    )(first_blk, nblk_e, nvalid, block_rows, x_sorted, w1, w3, w2)


def _sc_gather(table, idx):
    n = idx.shape[0]
    w = SC_WINDOW
    group = SC_GATHERS_PER_STEP
    mesh = plsc.VectorSubcoreMesh(core_axis_name="core", subcore_axis_name="subcore")

    @functools.partial(pl.kernel, out_type=jax.ShapeDtypeStruct((n, table.shape[1]), table.dtype), mesh=mesh,
                       scratch_types=[pltpu.SemaphoreType.DMA])
    def gather_kernel(table_hbm, idx_hbm, out_hbm, sem):
        def body(idx_vmem, out_vmem):
            copies = [pltpu.make_async_copy(table_hbm.at[idx_vmem.at[j]], out_vmem.at[pl.ds(j * w, w)], sem)
                      for j in range(group)]
            for cp in copies:
                cp.start()
            for cp in copies:
                cp.wait()

        pltpu.emit_pipeline(
            body,
            grid=(n // (group * w),),
            in_specs=[pl.BlockSpec((group, w), index_map=lambda i: (i, 0))],
            out_specs=[pl.BlockSpec((group * w, table.shape[1]), index_map=lambda i: (i, 0))],
            core_axis_name=("core", "subcore"),
            dimension_semantics=(pltpu.PARALLEL,),
        )(idx_hbm, out_hbm)

    return gather_kernel(table, idx.reshape(n // w, w))


def _gather_expert_rows(y_sorted, dest_tk):
    nchunk, n_rows, lanes = y_sorted.shape
    k, t = dest_tk.shape
    idx = dest_tk[None, :, :] + (jnp.arange(nchunk, dtype=jnp.int32) * n_rows)[:, None, None]
    rows = _sc_gather(y_sorted.reshape(nchunk * n_rows, lanes), idx.reshape(-1))
    return rows.reshape(nchunk, k, t, lanes)


def _combine_kernel(yg_ref, wts_ref, h_ref, x1_ref, mod_ref, sw1_ref, sw3_ref, sw2_ref, nfg_ref, *rest):
    o_ref = rest[-1]
    nchunk = yg_ref.shape[0]
    w = wts_ref[...]
    acc_hi = [None] * nchunk
    acc_lo = [None] * nchunk
    for kk in range(TOP_K):
        wk = w[:, kk:kk + 1]
        for j in range(nchunk):
            hi, lo = _unpack_halves(yg_ref[j, kk])
            acc_hi[j] = hi * wk if kk == 0 else acc_hi[j] + hi * wk
            acc_lo[j] = lo * wk if kk == 0 else acc_lo[j] + lo * wk
    routed = jnp.concatenate(acc_hi + acc_lo, axis=1)
    hb = h_ref[...].astype(BF16)
    hid = _silu(jnp.dot(hb, sw1_ref[...], preferred_element_type=F32)) * jnp.dot(
        hb, sw3_ref[...], preferred_element_type=F32)
    shared = jnp.dot(hid.astype(BF16), sw2_ref[...], preferred_element_type=F32)
    g2 = mod_ref[0, 5:6, :]
    x2 = x1_ref[...] + g2 * (routed + shared)
    o_ref[...] = _rms(x2) * nfg_ref[...]


def _combine(y_gathered, wts_tk, h2, x1, mod3, seq, sw1, sw3, sw2, normf_g, part, prev):
    t, d = h2.shape
    nchunk, k, tp, lanes = y_gathered.shape
    tm = min(COMBINE_TILE, seq, tp)
    per_seq = seq // tm
    off = part * (tp // tm)
    consts = (sw1.astype(BF16), sw3.astype(BF16), sw2.astype(BF16), normf_g.reshape(1, d))
    full = lambda a: pl.BlockSpec(a.shape, lambda i: (0,) * a.ndim)
    tile = lambda a: pl.BlockSpec((tm, a.shape[1]), lambda i: (off + i, 0))
    args = [y_gathered, wts_tk, h2, x1, mod3, *consts]
    in_specs = ([pl.BlockSpec((nchunk, k, tm, lanes), lambda i: (0, 0, i, 0)),
                 pl.BlockSpec((tm, k), lambda i: (i, 0)), tile(h2), tile(x1),
                 pl.BlockSpec((1,) + mod3.shape[1:], lambda i: ((off + i) // per_seq, 0, 0))]
                + [full(a) for a in consts])
    aliases = {}
    if prev is not None:
        aliases = {len(args): 0}
        args.append(prev)
        in_specs.append(pl.BlockSpec(memory_space=pl.ANY))
    return pl.pallas_call(
        _combine_kernel,
        grid=(tp // tm,),
        in_specs=in_specs,
        out_specs=pl.BlockSpec((tm, d), lambda i: (off + i, 0)),
        out_shape=jax.ShapeDtypeStruct((t, d), F32),
        input_output_aliases=aliases,
        compiler_params=_params("arbitrary"),
        name="combine",
    )(*args)


def _dispatch_plan(counts, t):
    rb = ROW_BLOCK
    padded = (counts + rb - 1) // rb * rb
    pends = jnp.cumsum(padded)
    pstarts = pends - padded
    n_rows = t * TOP_K + N_EXPERTS * rb
    block_start = jnp.arange(n_rows // rb, dtype=jnp.int32) * rb
    block_e = jnp.sum((pends[None, :] <= block_start[:, None]).astype(jnp.int32), axis=1)
    block_e = jnp.minimum(block_e, N_EXPERTS - 1)
    block_rows = jnp.clip(pstarts[block_e] + counts[block_e] - block_start, 0, rb).astype(jnp.int32)
    nvalid = (pends[-1:] // rb).astype(jnp.int32)
    return pstarts, pstarts // rb, padded // rb, block_rows, nvalid, n_rows


def kernel(x, c, ada_w, ada_b, norm1_g, norm2_g, w_in, tshift_mu, rwkv_w0, rwkv_w_up, rwkv_a0, rwkv_a_up,
           rwkv_g_up, rwkv_k_k, rwkv_k_a, rwkv_r_k, rwkv_ln_g, rwkv_ln_b, gmlp_ln_g, gmlp_ln_b, gmlp_ws,
           gmlp_bs, w_out_a, w_out_b, w_out, router_w, router_bias, exp_w1, exp_w3, exp_w2, shared_w1,
           shared_w3, shared_w2, normf_g):
    bsz, seq, d = x.shape
    t = bsz * seq
    xf = x.reshape(t, d)
    assert ada_w.shape[0] == 1, "single-layer block only"
    l = 0
    mod3 = _mod(c, ada_w[l], ada_b[l]).reshape(bsz, 6, d)
    p_rwkv, p_uv, p_gate = _inproj(xf, mod3, norm1_g[l], w_in[l], seq)
    y_a = _rwkv(p_rwkv, bsz, seq, tshift_mu[l], rwkv_w0[l], rwkv_w_up[l], rwkv_a0[l], rwkv_a_up[l],
                rwkv_g_up[l], rwkv_k_k[l], rwkv_k_a[l], rwkv_r_k[l], rwkv_ln_g[l], rwkv_ln_b[l])
    x1, h2, h2p = _mix(p_uv, p_gate, y_a, xf, mod3, seq, gmlp_ln_g[l], gmlp_ln_b[l], gmlp_ws[l], gmlp_bs[l],
                  w_out_a[l], w_out_b[l], w_out[l], norm2_g[l])
    eidx, wts = _route(h2, router_w[l], router_bias[l])
    rank, counts = _rank(eidx)
    counts = counts[:, 0].astype(jnp.int32)
    pstarts, first_blk, nblk_e, block_rows, nvalid, n_rows = _dispatch_plan(counts, t)
    dest = _dest(eidx, rank, pstarts)
    x_sorted = _dispatch(h2p, dest, n_rows)
    y_sorted = _expert(x_sorted, first_blk, nblk_e, nvalid, block_rows, exp_w1[l], exp_w3[l], exp_w2[l])
    wts_tk = wts.T
    tp = t // COMBINE_PARTS
    out = None
    for part in range(COMBINE_PARTS):
        sl = slice(part * tp, (part + 1) * tp)
        y_gathered = _gather_expert_rows(y_sorted, dest[:, sl])
        out = _combine(y_gathered, wts_tk[sl], h2, x1, mod3, seq, shared_w1[l], shared_w3[l], shared_w2[l],
                       normf_g, part, out)
    return out.reshape(bsz, seq, d)
```

```python
import functools

import jax
import jax.numpy as jnp
from jax import lax
from jax.experimental import pallas as pl
from jax.experimental.pallas import tpu as pltpu
from jax.experimental.pallas import tpu_sc as plsc

F32 = jnp.float32
BF16 = jnp.bfloat16

RWKV_HEAD = 64
RWKV_HEADS = 8
RWKV_WIDTH = RWKV_HEAD * RWKV_HEADS
DECAY_LORA = 64
AAA_LORA = 64
GATE_LORA = 128
RWKV_COLS = 3 * RWKV_WIDTH + DECAY_LORA + AAA_LORA + GATE_LORA
GMLP_WIDTH = 512
GMLP_BLOCK = 128
GMLP_GROUPS = 8
GMLP_GROUP_DIM = GMLP_WIDTH // GMLP_GROUPS
N_EXPERTS = 256
TOP_K = 8
N_EXPERT_GROUPS = 8
GROUP_SIZE = N_EXPERTS // N_EXPERT_GROUPS
TOPK_GROUPS = 4
ROUTED_SCALE = 2.5
RMS_EPS = 1e-6
LN_EPS = 1e-5
GN_EPS = 64e-5

VMEM_LIMIT_BYTES = 56 * 1024 * 1024

CHUNK = 64
RWKV_TILE = 256
ROW_BLOCK = 512
EXPERT_RING = 3
INPROJ_TILE = 512
MIX_TILE = 512
ROUTE_TILE = 1024
RANK_TILE = 512
COMBINE_TILE = 256
COMBINE_PARTS = 4
LANES = 128
SC_WINDOW = 128
SC_GATHERS_PER_STEP = 2


def _params(*sem):
    return pltpu.CompilerParams(dimension_semantics=sem, vmem_limit_bytes=VMEM_LIMIT_BYTES)


def _mm(a, b):
    return jnp.dot(a.astype(BF16), b.astype(BF16), preferred_element_type=F32)


def _mm_nt(a, b):
    return lax.dot_general(a.astype(BF16), b.astype(BF16), (((1,), (1,)), ((), ())),
                           preferred_element_type=F32)


def _mm_tn(a, b):
    return lax.dot_general(a.astype(BF16), b.astype(BF16), (((0,), (0,)), ((), ())),
                           preferred_element_type=F32)


def _split3(a):
    hi = a.astype(BF16)
    r1 = a - hi.astype(F32)
    mid = r1.astype(BF16)
    lo = (r1 - mid.astype(F32)).astype(BF16)
    return hi, mid, lo


def _silu(x):
    return x * jax.nn.sigmoid(x)


def _rms(x):
    return x * lax.rsqrt(jnp.mean(x * x, axis=-1, keepdims=True) + RMS_EPS)


def _pack_halves(x):
    n = x.shape[1] // 2
    bits = lax.bitcast_convert_type(x.astype(BF16).astype(F32), jnp.uint32)
    return bits[:, :n] | (bits[:, n:] >> 16)


def _unpack_halves(u):
    hi = lax.bitcast_convert_type(u & jnp.uint32(0xFFFF0000), F32)
    lo = lax.bitcast_convert_type(u << 16, F32)
    return hi, lo


def _mod_kernel(c_ref, w_ref, b_ref, o_ref):
    s = _silu(c_ref[...])
    hi, mid, lo = _split3(s)
    whi, wmid, wlo = _split3(w_ref[...])
    dot = lambda p, q: jnp.dot(p, q, preferred_element_type=F32)
    acc = dot(hi, whi) + (dot(hi, wmid) + dot(mid, whi)) + (dot(hi, wlo) + dot(mid, wmid) + dot(lo, whi))
    o_ref[...] = acc + b_ref[...]


def _mod(c, ada_w, ada_b):
    bsz, d = c.shape
    n = ada_w.shape[1]
    tn = d
    return pl.pallas_call(
        _mod_kernel,
        grid=(n // tn,),
        in_specs=[pl.BlockSpec((bsz, d), lambda j: (0, 0)),
                  pl.BlockSpec((d, tn), lambda j: (0, j)),
                  pl.BlockSpec((1, tn), lambda j: (0, j))],
        out_specs=pl.BlockSpec((bsz, tn), lambda j: (0, j)),
        out_shape=jax.ShapeDtypeStruct((bsz, n), F32),
        compiler_params=_params("arbitrary"),
        name="mod",
    )(c, ada_w, ada_b.reshape(1, n))


def _inproj_kernel(x_ref, mod_ref, g_ref, wr_ref, wuv_ref, wg_ref, pr_ref, puv_ref, pg_ref):
    sh = mod_ref[0, 0:1, :]
    sc = mod_ref[0, 1:2, :]
    h = (_rms(x_ref[...]) * g_ref[...]) * (1.0 + sc) + sh
    hb = h.astype(BF16)
    pr_ref[...] = jnp.dot(hb, wr_ref[...], preferred_element_type=F32).astype(pr_ref.dtype)
    puv_ref[...] = jnp.dot(hb, wuv_ref[...], preferred_element_type=F32).astype(puv_ref.dtype)
    pg_ref[...] = jnp.dot(hb, wg_ref[...], preferred_element_type=F32).astype(pg_ref.dtype)


def _inproj(xf, mod3, norm1_g, w_in, seq):
    t, d = xf.shape
    tm = min(INPROJ_TILE, seq)
    per_seq = seq // tm
    wr = w_in[:, :RWKV_COLS].astype(BF16)
    wuv = w_in[:, RWKV_COLS:RWKV_COLS + 2 * GMLP_WIDTH].astype(BF16)
    wg = w_in[:, RWKV_COLS + 2 * GMLP_WIDTH:].astype(BF16)
    full = lambda a: pl.BlockSpec(a.shape, lambda i: (0,) * a.ndim)
    g = norm1_g.reshape(1, d)
    return pl.pallas_call(
        _inproj_kernel,
        grid=(t // tm,),
        in_specs=[pl.BlockSpec((tm, d), lambda i: (i, 0)),
                  pl.BlockSpec((1,) + mod3.shape[1:], lambda i: (i // per_seq, 0, 0)),
                  full(g), full(wr), full(wuv), full(wg)],
        out_specs=[pl.BlockSpec((tm, wr.shape[1]), lambda i: (i, 0)),
                   pl.BlockSpec((tm, wuv.shape[1]), lambda i: (i, 0)),
                   pl.BlockSpec((tm, wg.shape[1]), lambda i: (i, 0))],
        out_shape=[jax.ShapeDtypeStruct((t, wr.shape[1]), BF16),
                   jax.ShapeDtypeStruct((t, wuv.shape[1]), BF16),
                   jax.ShapeDtypeStruct((t, wg.shape[1]), BF16)],
        compiler_params=_params("arbitrary"),
        name="inproj",
    )(xf, mod3, g, wr, wuv, wg)


def _unit_lower_inverses(ns, row, col):
    c = ns[0].shape[0]
    eye = jnp.where(row == col, 1.0, 0.0).astype(F32)
    blk8 = (row // 8) == (col // 8)
    n8 = [jnp.where(blk8, n, 0.0) for n in ns]
    n8_2 = [_mm(a, a) for a in n8]
    n8_4 = [_mm(a, a) for a in n8_2]
    inv = [eye + a for a in n8]
    inv = [i + _mm(i, b) for i, b in zip(inv, n8_2)]
    inv = [i + _mm(i, b) for i, b in zip(inv, n8_4)]
    s = 8
    while s < c:
        sel = ((row // (2 * s)) == (col // (2 * s))) & ((row // s) != (col // s))
        low = [jnp.where(sel, n, 0.0) for n in ns]
        t1 = [_mm(l, i) for l, i in zip(low, inv)]
        inv = [i + _mm(i, t) for i, t in zip(inv, t1)]
        s *= 2
    return inv


def _rwkv_kernel(p_ref, mu_ref, w0_ref, wup_ref, a0_ref, aup_ref, gup_ref, kk_ref, ka_ref, rk_ref,
                 lng_ref, lnb_ref, bd_ref, y_ref, state_ref, carry_ref, ycat_ref, *, c):
    tm = p_ref.shape[0]
    nq = tm // c
    hd = RWKV_HEAD
    nh = RWKV_HEADS

    @pl.when(pl.program_id(1) == 0)
    def _():
        state_ref[...] = jnp.zeros_like(state_ref)
        carry_ref[...] = jnp.zeros_like(carry_ref)

    p = p_ref[...].astype(F32)
    rowc = lax.broadcasted_iota(jnp.int32, (tm, 1), 0)
    prev = jnp.where(rowc == 0, carry_ref[...], pltpu.roll(p, 1, axis=0))
    carry_ref[...] = p[tm - 1:tm, :]
    ps = p + (prev - p) * mu_ref[...]

    o = RWKV_WIDTH
    r = ps[:, 0:o]
    k = ps[:, o:2 * o]
    v = ps[:, 2 * o:3 * o]
    xw = ps[:, 3 * o:3 * o + DECAY_LORA]
    xa = ps[:, 3 * o + DECAY_LORA:3 * o + DECAY_LORA + AAA_LORA]
    xg = ps[:, 3 * o + DECAY_LORA + AAA_LORA:]

    z = -(w0_ref[...] + _mm(jnp.tanh(xw), wup_ref[...]))
    softplus = jnp.maximum(z, 0.0) + jnp.log1p(jnp.exp(-jnp.abs(z)))
    wlog = -softplus - 0.5
    ld = -jnp.exp(wlog)
    a = jax.nn.sigmoid(a0_ref[...] + _mm(xa, aup_ref[...]))
    g = _mm(jax.nn.sigmoid(xg), gup_ref[...])

    def head_sum(x):
        hi = x.astype(BF16)
        lo = (x - hi.astype(F32)).astype(BF16)
        bd = bd_ref[...]
        return jnp.dot(hi, bd, preferred_element_type=F32) + jnp.dot(lo, bd, preferred_element_type=F32)

    kkf = k * kk_ref[...]
    kkn = kkf * (1.0 / jnp.maximum(jnp.sqrt(head_sum(kkf * kkf)), 1e-12))
    k2 = k * (1.0 + (a - 1.0) * ka_ref[...])
    avec = -kkn
    bvec = kkn * a

    rowt = lax.broadcasted_iota(jnp.int32, (tm, tm), 0)
    colt = lax.broadcasted_iota(jnp.int32, (tm, tm), 1)
    tri = jnp.where((rowt >= colt) & ((rowt // c) == (colt // c)), 1.0, 0.0).astype(BF16)
    hi, mid, lo = _split3(ld)
    dot = lambda q: jnp.dot(tri, q, preferred_element_type=F32)
    cl = dot(hi) + dot(mid) + dot(lo)
    cl_end = jnp.concatenate(
        [jnp.broadcast_to(cl[(q + 1) * c - 1:(q + 1) * c, :], (c, o)) for q in range(nq)], axis=0)
    g_inv = jnp.exp(-cl)
    g_tail = jnp.exp(cl_end - cl)
    g_end = jnp.exp(cl_end)
    rt = (r * jnp.exp(cl)).astype(BF16)
    at = (avec * jnp.exp(cl - ld)).astype(BF16)
    kt = (k2 * g_inv).astype(BF16)
    bt = (bvec * g_inv).astype(BF16)
    kend = (k2 * g_tail).astype(BF16)
    bend = (bvec * g_tail).astype(BF16)
    vb = v.astype(BF16)

    row = lax.broadcasted_iota(jnp.int32, (c, c), 0)
    col = lax.broadcasted_iota(jnp.int32, (c, c), 1)
    strict = row > col
    incl = row >= col
    items = [(q, h) for q in range(nq) for h in range(nh)]
    blk = lambda x, q, h: x[q * c:(q + 1) * c, h * hd:(h + 1) * hd]

    lhs = [jnp.concatenate([blk(at, q, h), blk(rt, q, h)], axis=0) for q, h in items]
    mat_b = [_mm_nt(l, blk(bt, q, h)) for l, (q, h) in zip(lhs, items)]
    mat_k = [_mm_nt(l, blk(kt, q, h)) for l, (q, h) in zip(lhs, items)]
    a_ab = [jnp.where(strict, m[:c], 0.0) for m in mat_b]
    a_rb = [jnp.where(incl, m[c:], 0.0).astype(BF16) for m in mat_b]
    a_ak = [jnp.where(strict, m[:c], 0.0) for m in mat_k]
    a_rk = [jnp.where(incl, m[c:], 0.0) for m in mat_k]
    tinv = _unit_lower_inverses(a_ab, row, col)
    akv = [_mm(m, blk(vb, q, h)) for m, (q, h) in zip(a_ak, items)]
    y_loc = [_mm(m, blk(vb, q, h)) for m, (q, h) in zip(a_rk, items)]
    s_loc = [_mm_tn(blk(vb, q, h), blk(kend, q, h)) for q, h in items]
    wmat = [_mm(t, blk(at, q, h)).astype(BF16) for t, (q, h) in zip(tinv, items)]
    ut = [_mm(t, m) for t, m in zip(tinv, akv)]

    state = [state_ref[h] for h in range(nh)]
    for q in range(nq):
        base = q * nh
        sb = [s.astype(BF16) for s in state]
        u = [_mm_nt(wmat[base + h], sb[h]) + ut[base + h] for h in range(nh)]
        yh = [_mm_nt(blk(rt, q, h), sb[h]) + y_loc[base + h] for h in range(nh)]
        ub = [x.astype(BF16) for x in u]
        yh = [y0 + _mm(a_rb[base + h], ub[h]) for h, y0 in enumerate(yh)]
        state = [state[h] * g_end[q * c:q * c + 1, h * hd:(h + 1) * hd]
                 + _mm_tn(ub[h], blk(bend, q, h)) + s_loc[base + h] for h in range(nh)]
        for h in range(nh):
            ycat_ref[q * c:(q + 1) * c, h * hd:(h + 1) * hd] = yh[h]
    for h in range(nh):
        state_ref[h] = state[h]

    y = ycat_ref[...]
    dev = y - head_sum(y) * (1.0 / hd)
    rstd = lax.rsqrt(head_sum(dev * dev) * (1.0 / hd) + GN_EPS)
    bonus = head_sum(r * k2 * rk_ref[...])
    yn = dev * rstd * lng_ref[...] + lnb_ref[...]
    y_ref[...] = (yn + bonus * v) * g


def _rwkv(p_rwkv, bsz, seq, tshift_mu, w0, w_up, a0, a_up, g_up, k_k, k_a, r_k, ln_g, ln_b):
    t = p_rwkv.shape[0]
    c = min(CHUNK, seq)
    tm = min(RWKV_TILE, seq)
    nstep = seq // tm
    row = lambda a: a.reshape(1, -1)
    head_of = jnp.arange(RWKV_WIDTH, dtype=jnp.int32) // RWKV_HEAD
    block_diag = (head_of[:, None] == head_of[None, :]).astype(BF16)
    args = (row(tshift_mu), row(w0), w_up, row(a0), a_up, g_up, row(k_k), row(k_a), row(r_k),
            row(ln_g), row(ln_b), block_diag)
    full = lambda a: pl.BlockSpec(a.shape, lambda b, j: (0,) * a.ndim)
    return pl.pallas_call(
        functools.partial(_rwkv_kernel, c=c),
        grid=(bsz, nstep),
        in_specs=[pl.BlockSpec((tm, RWKV_COLS), lambda b, j: (b * nstep + j, 0))] + [full(a) for a in args],
        out_specs=pl.BlockSpec((tm, RWKV_WIDTH), lambda b, j: (b * nstep + j, 0)),
        out_shape=jax.ShapeDtypeStruct((t, RWKV_WIDTH), F32),
        scratch_shapes=[pltpu.VMEM((RWKV_HEADS, RWKV_HEAD, RWKV_HEAD), F32),
                        pltpu.VMEM((1, RWKV_COLS), F32),
                        pltpu.VMEM((tm, RWKV_WIDTH), F32)],
        compiler_params=_params("arbitrary", "arbitrary"),
        name="rwkv",
    )(p_rwkv, *args)


def _mix_kernel(puv_ref, pg_ref, ya_ref, x_ref, mod_ref, lng_ref, lnb_ref, ws_ref, bsm_ref,
                woa_ref, wob_ref, wo_ref, n2g_ref, x1_ref, h2_ref, h2p_ref, yb_ref):
    tm = x_ref.shape[0]
    gw = GMLP_WIDTH
    puv = puv_ref[...].astype(F32)
    u = jax.nn.gelu(puv[:, :gw])
    vg = jax.nn.gelu(puv[:, gw:])
    mu = jnp.mean(vg, axis=-1, keepdims=True)
    var = jnp.mean(jnp.square(vg - mu), axis=-1, keepdims=True)
    vln = (vg - mu) * lax.rsqrt(var + LN_EPS) * lng_ref[...] + lnb_ref[...]

    row = lax.broadcasted_iota(jnp.int32, (GMLP_BLOCK, GMLP_BLOCK), 0)
    col = lax.broadcasted_iota(jnp.int32, (GMLP_BLOCK, GMLP_BLOCK), 1)
    gd = GMLP_GROUP_DIM
    for gi in range(GMLP_GROUPS):
        wsm = jnp.where(row >= col, ws_ref[gi], 0.0).astype(BF16)
        for nb in range(tm // GMLP_BLOCK):
            rs = slice(nb * GMLP_BLOCK, (nb + 1) * GMLP_BLOCK)
            cs = slice(gi * gd, (gi + 1) * gd)
            sv = jnp.dot(wsm, vln[rs, cs].astype(BF16), preferred_element_type=F32) + bsm_ref[:, cs]
            yb_ref[rs, cs] = u[rs, cs] * sv

    pg = pg_ref[...].astype(F32)
    d = x_ref.shape[1]
    gate_a = jax.nn.sigmoid(pg[:, :d])
    gate_b = jax.nn.sigmoid(pg[:, d:])
    merged = gate_a * _mm(ya_ref[...], woa_ref[...]) + gate_b * _mm(yb_ref[...], wob_ref[...])
    g1 = mod_ref[0, 2:3, :]
    x1 = x_ref[...] + g1 * _mm(merged, wo_ref[...])
    x1_ref[...] = x1
    sh2 = mod_ref[0, 3:4, :]
    sc2 = mod_ref[0, 4:5, :]
    h2 = (_rms(x1) * n2g_ref[...]) * (1.0 + sc2) + sh2
    h2_ref[...] = h2
    packed = _pack_halves(h2)
    for j in range(h2p_ref.shape[0]):
        h2p_ref[j] = packed[:, j * LANES:(j + 1) * LANES]


def _mix(p_uv, p_gate, y_a, xf, mod3, seq, ln_g, ln_b, ws, bs, w_out_a, w_out_b, w_out, norm2_g):
    t, d = xf.shape
    tm = min(MIX_TILE, seq)
    per_seq = seq // tm
    row = lambda a: a.reshape(1, -1)
    bsm = jnp.repeat(bs.T, GMLP_GROUP_DIM, axis=1)
    consts = (row(ln_g), row(ln_b), ws, bsm, w_out_a.astype(BF16), w_out_b.astype(BF16),
              w_out.astype(BF16), row(norm2_g))
    full = lambda a: pl.BlockSpec(a.shape, lambda i: (0,) * a.ndim)
    tile = lambda a: pl.BlockSpec((tm, a.shape[1]), lambda i: (i, 0))
    return pl.pallas_call(
        _mix_kernel,
        grid=(t // tm,),
        in_specs=[tile(p_uv), tile(p_gate), tile(y_a), tile(xf),
                  pl.BlockSpec((1,) + mod3.shape[1:], lambda i: (i // per_seq, 0, 0))]
                 + [full(a) for a in consts],
        out_specs=[pl.BlockSpec((tm, d), lambda i: (i, 0)), pl.BlockSpec((tm, d), lambda i: (i, 0)),
                   pl.BlockSpec((d // 2 // LANES, tm, LANES), lambda i: (0, i, 0))],
        out_shape=[jax.ShapeDtypeStruct((t, d), F32), jax.ShapeDtypeStruct((t, d), F32),
                   jax.ShapeDtypeStruct((d // 2 // LANES, t, LANES), jnp.uint32)],
        scratch_shapes=[pltpu.VMEM((tm, GMLP_WIDTH), F32)],
        compiler_params=_params("arbitrary"),
        name="mix",
    )(p_uv, p_gate, y_a, xf, mod3, *consts)


def _first_argmax(vals, idx, big):
    m = jnp.max(vals, axis=0, keepdims=True)
    first = jnp.min(jnp.where(vals == m, idx, big), axis=0, keepdims=True)
    return m, first


def _route_kernel(h_ref, rwt_ref, bias_ref, eidx_ref, wts_ref):
    tm = h_ref.shape[0]
    ne = N_EXPERTS
    hhi, hmid, _ = _split3(h_ref[...])
    whi, wmid, _ = _split3(rwt_ref[...])
    nt = lambda p, q: lax.dot_general(p, q, (((1,), (1,)), ((), ())), preferred_element_type=F32)
    logits = nt(whi, hhi) + (nt(whi, hmid) + nt(wmid, hhi))
    scores = jax.nn.sigmoid(logits)
    sel = scores + bias_ref[...]
    neg = -jnp.inf

    gs = GROUP_SIZE
    gidx = lax.broadcasted_iota(jnp.int32, (gs, tm), 0)
    grp_rows = []
    for gi in range(N_EXPERT_GROUPS):
        blk = sel[gi * gs:(gi + 1) * gs, :]
        m1, i1 = _first_argmax(blk, gidx, gs)
        m2 = jnp.max(jnp.where(gidx == i1, neg, blk), axis=0, keepdims=True)
        grp_rows.append(m1 + m2)
    grp = jnp.concatenate(grp_rows, axis=0)

    ng = N_EXPERT_GROUPS
    giota = lax.broadcasted_iota(jnp.int32, (ng, tm), 0)
    gsel = jnp.zeros((ng, tm), jnp.bool_)
    work = grp
    for _ in range(TOPK_GROUPS):
        _, gi1 = _first_argmax(work, giota, ng)
        hit = giota == gi1
        gsel = gsel | hit
        work = jnp.where(hit, neg, work)
    gself = jnp.where(gsel, 1.0, 0.0)
    emask = jnp.concatenate(
        [jnp.broadcast_to(gself[gi:gi + 1, :], (gs, tm)) for gi in range(ng)], axis=0) > 0.5

    eiota = lax.broadcasted_iota(jnp.int32, (ne, tm), 0)
    work = jnp.where(emask, sel, neg)
    idx_rows, w_rows = [], []
    for _ in range(TOP_K):
        _, e1 = _first_argmax(work, eiota, ne)
        hit = eiota == e1
        idx_rows.append(e1)
        w_rows.append(jnp.sum(jnp.where(hit, scores, 0.0), axis=0, keepdims=True))
        work = jnp.where(hit, neg, work)
    eidx_ref[...] = jnp.concatenate(idx_rows, axis=0)
    w = jnp.concatenate(w_rows, axis=0)
    wts_ref[...] = w / jnp.sum(w, axis=0, keepdims=True) * ROUTED_SCALE


def _route(h2, router_w, router_bias):
    t, d = h2.shape
    tm = min(ROUTE_TILE, t)
    rwt = router_w.T
    bias = router_bias.reshape(N_EXPERTS, 1)
    return pl.pallas_call(
        _route_kernel,
        grid=(t // tm,),
        in_specs=[pl.BlockSpec((tm, d), lambda i: (i, 0)),
                  pl.BlockSpec(rwt.shape, lambda i: (0, 0)),
                  pl.BlockSpec(bias.shape, lambda i: (0, 0))],
        out_specs=[pl.BlockSpec((TOP_K, tm), lambda i: (0, i)), pl.BlockSpec((TOP_K, tm), lambda i: (0, i))],
        out_shape=[jax.ShapeDtypeStruct((TOP_K, t), jnp.int32), jax.ShapeDtypeStruct((TOP_K, t), F32)],
        compiler_params=_params("arbitrary"),
        name="route",
    )(h2, rwt, bias)


def _rank_kernel(e_ref, upper_ref, rank_ref, count_ref):
    tm = e_ref.shape[1]
    ne = N_EXPERTS

    @pl.when(pl.program_id(0) == 0)
    def _():
        count_ref[...] = jnp.zeros_like(count_ref)

    e = e_ref[...]
    eiota = lax.broadcasted_iota(jnp.int32, (ne, tm), 0)
    onehot = jnp.zeros((ne, tm), F32)
    for kk in range(TOP_K):
        onehot = onehot + jnp.where(eiota == e[kk:kk + 1, :], 1.0, 0.0)
    before = jnp.dot(onehot.astype(BF16), upper_ref[...], preferred_element_type=F32)
    base = count_ref[:, 0:1]
    tot = before + base
    rows = [jnp.sum(jnp.where(eiota == e[kk:kk + 1, :], tot, 0.0), axis=0, keepdims=True)
            for kk in range(TOP_K)]
    rank_ref[...] = jnp.concatenate(rows, axis=0).astype(jnp.int32)
    count_ref[...] = count_ref[...] + jnp.sum(onehot, axis=1, keepdims=True)


def _rank(eidx):
    k, t = eidx.shape
    tm = min(RANK_TILE, t)
    upper = jnp.triu(jnp.ones((tm, tm), F32), 1).astype(BF16)
    return pl.pallas_call(
        _rank_kernel,
        grid=(t // tm,),
        in_specs=[pl.BlockSpec((k, tm), lambda i: (0, i)), pl.BlockSpec((tm, tm), lambda i: (0, 0))],
        out_specs=[pl.BlockSpec((k, tm), lambda i: (0, i)), pl.BlockSpec((N_EXPERTS, 128), lambda i: (0, 0))],
        out_shape=[jax.ShapeDtypeStruct((k, t), jnp.int32), jax.ShapeDtypeStruct((N_EXPERTS, 128), F32)],
        compiler_params=_params("arbitrary"),
        name="rank",
    )(eidx, upper)


def _dest_kernel(e_ref, rank_ref, pst_ref, dest_ref):
    tm = e_ref.shape[1]
    e = e_ref[...]
    eiota = lax.broadcasted_iota(jnp.int32, (N_EXPERTS, tm), 0)
    pst = pst_ref[...]
    rows = [jnp.sum(jnp.where(eiota == e[kk:kk + 1, :], pst, 0.0), axis=0, keepdims=True)
            for kk in range(TOP_K)]
    dest_ref[...] = jnp.concatenate(rows, axis=0).astype(jnp.int32) + rank_ref[...]


def _dest(eidx, rank, pstarts):
    k, t = eidx.shape
    tm = min(RANK_TILE, t)
    pst = pstarts.astype(F32).reshape(N_EXPERTS, 1)
    return pl.pallas_call(
        _dest_kernel,
        grid=(t // tm,),
        in_specs=[pl.BlockSpec((k, tm), lambda i: (0, i)), pl.BlockSpec((k, tm), lambda i: (0, i)),
                  pl.BlockSpec((N_EXPERTS, 1), lambda i: (0, 0))],
        out_specs=pl.BlockSpec((k, tm), lambda i: (0, i)),
        out_shape=jax.ShapeDtypeStruct((k, t), jnp.int32),
        compiler_params=_params("arbitrary"),
        name="dest",
    )(eidx, rank, pst)


def _sc_scatter(rows, idx, n_out):
    n, lanes = rows.shape
    nstep, copies, w = idx.shape
    mesh = plsc.VectorSubcoreMesh(core_axis_name="core", subcore_axis_name="subcore")

    @functools.partial(pl.kernel, out_type=jax.ShapeDtypeStruct((n_out, lanes), rows.dtype), mesh=mesh,
                       scratch_types=[pltpu.SemaphoreType.DMA])
    def scatter_kernel(rows_hbm, idx_hbm, out_hbm, sem):
        def body(rows_vmem, idx_vmem):
            cps = [pltpu.make_async_copy(rows_vmem, out_hbm.at[idx_vmem.at[j]], sem) for j in range(copies)]
            for cp in cps:
                cp.start()
            for cp in cps:
                cp.wait()

        pltpu.emit_pipeline(
            body,
            grid=(nstep,),
            in_specs=[pl.BlockSpec((w, lanes), index_map=lambda i: (i, 0)),
                      pl.BlockSpec((copies, w), index_map=lambda i: (i, 0))],
            out_specs=[],
            core_axis_name=("core", "subcore"),
            dimension_semantics=(pltpu.PARALLEL,),
        )(rows_hbm, idx_hbm)

    return scatter_kernel(rows, idx.reshape(nstep * copies, w))


def _dispatch(h2p, dest_tk, n_rows):
    nchunk, t, lanes = h2p.shape
    k = dest_tk.shape[0]
    w = SC_WINDOW
    idx = dest_tk.reshape(k, t // w, w).transpose(1, 0, 2)[None]
    idx = idx + (jnp.arange(nchunk, dtype=jnp.int32) * n_rows)[:, None, None, None]
    out = _sc_scatter(h2p.reshape(nchunk * t, lanes), idx.reshape(nchunk * (t // w), k, w), nchunk * n_rows)
    return out.reshape(nchunk, n_rows, lanes)


def _expert_kernel(first_ref, nblk_ref, nv_ref, rows_ref, x_hbm, w1_ref, w3_ref, w2_ref, y_hbm,
                   xbuf, ybuf, xsem, ysem, w1b, w3b, w2b):
    e = pl.program_id(0)
    ring, nchunk, rb = xbuf.shape[0], xbuf.shape[1], xbuf.shape[2]
    half = nchunk * LANES
    total = nv_ref[0]

    def x_copy(g, slot):
        return pltpu.make_async_copy(x_hbm.at[:, pl.ds(g * rb, rb), :], xbuf.at[slot], xsem.at[slot])

    def y_copy(g, slot):
        return pltpu.make_async_copy(ybuf.at[slot], y_hbm.at[:, pl.ds(g * rb, rb), :], ysem.at[slot])

    @pl.when(e == 0)
    def _():
        for g0 in range(ring - 1):
            @pl.when(g0 < total)
            def _():
                x_copy(g0, g0).start()

    @pl.when(nblk_ref[e] > 0)
    def _():
        w1b[...] = w1_ref[0].astype(BF16)
        w3b[...] = w3_ref[0].astype(BF16)
        w2b[...] = w2_ref[0].astype(BF16)

    def block(j, carry):
        g = first_ref[e] + j
        slot = g % ring
        x_copy(g, slot).wait()

        @pl.when(g + ring - 1 < total)
        def _():
            x_copy(g + ring - 1, (g + ring - 1) % ring).start()

        live = lax.broadcasted_iota(jnp.int32, (rb, 1), 0) < rows_ref[g]
        x = jnp.concatenate([xbuf[slot, c] for c in range(nchunk)], axis=1)
        hi, lo = _unpack_halves(jnp.where(live, x, jnp.uint32(0)))
        xa = hi.astype(BF16)
        xb = lo.astype(BF16)
        dot = lambda p, q: jnp.dot(p, q, preferred_element_type=F32)
        h1 = dot(xa, w1b[:half, :]) + dot(xb, w1b[half:, :])
        h3 = dot(xa, w3b[:half, :]) + dot(xb, w3b[half:, :])
        hid = _silu(h1) * h3
        packed = _pack_halves(dot(hid.astype(BF16), w2b[...]))

        @pl.when(g >= ring)
        def _():
            y_copy(g - ring, slot).wait()

        for c in range(nchunk):
            ybuf[slot, c] = packed[:, c * LANES:(c + 1) * LANES]
        y_copy(g, slot).start()
        return carry

    lax.fori_loop(0, nblk_ref[e], block, 0)

    @pl.when(e == pl.num_programs(0) - 1)
    def _():
        for back in range(ring, 0, -1):
            @pl.when(total >= back)
            def _():
                y_copy(total - back, (total - back) % ring).wait()


def _expert(x_sorted, first_blk, nblk_e, nvalid, block_rows, w1, w3, w2):
    nchunk, n_rows, lanes = x_sorted.shape
    rb = ROW_BLOCK
    ne, d, f = w1.shape
    grid_spec = pltpu.PrefetchScalarGridSpec(
        num_scalar_prefetch=4,
        grid=(ne,),
        in_specs=[
            pl.BlockSpec(memory_space=pl.ANY),
            pl.BlockSpec((1, d, f), lambda e, *_: (e, 0, 0)),
            pl.BlockSpec((1, d, f), lambda e, *_: (e, 0, 0)),
            pl.BlockSpec((1, f, d), lambda e, *_: (e, 0, 0)),
        ],
        out_specs=pl.BlockSpec(memory_space=pl.ANY),
        scratch_shapes=[pltpu.VMEM((EXPERT_RING, nchunk, rb, lanes), jnp.uint32),
                        pltpu.VMEM((EXPERT_RING, nchunk, rb, lanes), jnp.uint32),
                        pltpu.SemaphoreType.DMA((EXPERT_RING,)), pltpu.SemaphoreType.DMA((EXPERT_RING,)),
                        pltpu.VMEM((d, f), BF16), pltpu.VMEM((d, f), BF16), pltpu.VMEM((f, d), BF16)],
    )
    return pl.pallas_call(
        _expert_kernel,
        grid_spec=grid_spec,
        out_shape=jax.ShapeDtypeStruct((nchunk, n_rows, lanes), jnp.uint32),
        compiler_params=_params("arbitrary"),
        name="expert",
    )(first_blk, nblk_e, nvalid, block_rows, x_sorted, w1, w3, w2)


def _sc_gather(table, idx):
    n = idx.shape[0]
    w = SC_WINDOW
    group = SC_GATHERS_PER_STEP
    mesh = plsc.VectorSubcoreMesh(core_axis_name="core", subcore_axis_name="subcore")

    @functools.partial(pl.kernel, out_type=jax.ShapeDtypeStruct((n, table.shape[1]), table.dtype), mesh=mesh,
                       scratch_types=[pltpu.SemaphoreType.DMA])
    def gather_kernel(table_hbm, idx_hbm, out_hbm, sem):
        def body(idx_vmem, out_vmem):
            copies = [pltpu.make_async_copy(table_hbm.at[idx_vmem.at[j]], out_vmem.at[pl.ds(j * w, w)], sem)
                      for j in range(group)]
            for cp in copies:
                cp.start()
            for cp in copies:
                cp.wait()

        pltpu.emit_pipeline(
            body,
            grid=(n // (group * w),),
            in_specs=[pl.BlockSpec((group, w), index_map=lambda i: (i, 0))],
            out_specs=[pl.BlockSpec((group * w, table.shape[1]), index_map=lambda i: (i, 0))],
            core_axis_name=("core", "subcore"),
            dimension_semantics=(pltpu.PARALLEL,),
        )(idx_hbm, out_hbm)

    return gather_kernel(table, idx.reshape(n // w, w))


def _gather_expert_rows(y_sorted, dest_tk):
    nchunk, n_rows, lanes = y_sorted.shape
    k, t = dest_tk.shape
    idx = dest_tk[None, :, :] + (jnp.arange(nchunk, dtype=jnp.int32) * n_rows)[:, None, None]
    rows = _sc_gather(y_sorted.reshape(nchunk * n_rows, lanes), idx.reshape(-1))
    return rows.reshape(nchunk, k, t, lanes)


def _combine_kernel(yg_ref, wts_ref, h_ref, x1_ref, mod_ref, sw1_ref, sw3_ref, sw2_ref, nfg_ref, *rest):
    o_ref = rest[-1]
    nchunk = yg_ref.shape[0]
    w = wts_ref[...]
    acc_hi = [None] * nchunk
    acc_lo = [None] * nchunk
    for kk in range(TOP_K):
        wk = w[:, kk:kk + 1]
        for j in range(nchunk):
            hi, lo = _unpack_halves(yg_ref[j, kk])
            acc_hi[j] = hi * wk if kk == 0 else acc_hi[j] + hi * wk
            acc_lo[j] = lo * wk if kk == 0 else acc_lo[j] + lo * wk
    routed = jnp.concatenate(acc_hi + acc_lo, axis=1)
    hb = h_ref[...].astype(BF16)
    hid = _silu(jnp.dot(hb, sw1_ref[...], preferred_element_type=F32)) * jnp.dot(
        hb, sw3_ref[...], preferred_element_type=F32)
    shared = jnp.dot(hid.astype(BF16), sw2_ref[...], preferred_element_type=F32)
    g2 = mod_ref[0, 5:6, :]
    x2 = x1_ref[...] + g2 * (routed + shared)
    o_ref[...] = _rms(x2) * nfg_ref[...]


def _combine(y_gathered, wts_tk, h2, x1, mod3, seq, sw1, sw3, sw2, normf_g, part, prev):
    t, d = h2.shape
    nchunk, k, tp, lanes = y_gathered.shape
    tm = min(COMBINE_TILE, seq, tp)
    per_seq = seq // tm
    off = part * (tp // tm)
    consts = (sw1.astype(BF16), sw3.astype(BF16), sw2.astype(BF16), normf_g.reshape(1, d))
    full = lambda a: pl.BlockSpec(a.shape, lambda i: (0,) * a.ndim)
    tile = lambda a: pl.BlockSpec((tm, a.shape[1]), lambda i: (off + i, 0))
    args = [y_gathered, wts_tk, h2, x1, mod3, *consts]
    in_specs = ([pl.BlockSpec((nchunk, k, tm, lanes), lambda i: (0, 0, i, 0)),
                 pl.BlockSpec((tm, k), lambda i: (i, 0)), tile(h2), tile(x1),
                 pl.BlockSpec((1,) + mod3.shape[1:], lambda i: ((off + i) // per_seq, 0, 0))]
                + [full(a) for a in consts])
    aliases = {}
    if prev is not None:
        aliases = {len(args): 0}
        args.append(prev)
        in_specs.append(pl.BlockSpec(memory_space=pl.ANY))
    return pl.pallas_call(
        _combine_kernel,
        grid=(tp // tm,),
        in_specs=in_specs,
        out_specs=pl.BlockSpec((tm, d), lambda i: (off + i, 0)),
        out_shape=jax.ShapeDtypeStruct((t, d), F32),
        input_output_aliases=aliases,
        compiler_params=_params("arbitrary"),
        name="combine",
    )(*args)


def _dispatch_plan(counts, t):
    rb = ROW_BLOCK
    padded = (counts + rb - 1) // rb * rb
    pends = jnp.cumsum(padded)
    pstarts = pends - padded
    n_rows = t * TOP_K + N_EXPERTS * rb
    block_start = jnp.arange(n_rows // rb, dtype=jnp.int32) * rb
    block_e = jnp.sum((pends[None, :] <= block_start[:, None]).astype(jnp.int32), axis=1)
    block_e = jnp.minimum(block_e, N_EXPERTS - 1)
    block_rows = jnp.clip(pstarts[block_e] + counts[block_e] - block_start, 0, rb).astype(jnp.int32)
    nvalid = (pends[-1:] // rb).astype(jnp.int32)
    return pstarts, pstarts // rb, padded // rb, block_rows, nvalid, n_rows


def kernel(x, c, ada_w, ada_b, norm1_g, norm2_g, w_in, tshift_mu, rwkv_w0, rwkv_w_up, rwkv_a0, rwkv_a_up,
           rwkv_g_up, rwkv_k_k, rwkv_k_a, rwkv_r_k, rwkv_ln_g, rwkv_ln_b, gmlp_ln_g, gmlp_ln_b, gmlp_ws,
           gmlp_bs, w_out_a, w_out_b, w_out, router_w, router_bias, exp_w1, exp_w3, exp_w2, shared_w1,
           shared_w3, shared_w2, normf_g):
    bsz, seq, d = x.shape
    t = bsz * seq
    xf = x.reshape(t, d)
    assert ada_w.shape[0] == 1, "single-layer block only"
    l = 0
    mod3 = _mod(c, ada_w[l], ada_b[l]).reshape(bsz, 6, d)
    p_rwkv, p_uv, p_gate = _inproj(xf, mod3, norm1_g[l], w_in[l], seq)
    y_a = _rwkv(p_rwkv, bsz, seq, tshift_mu[l], rwkv_w0[l], rwkv_w_up[l], rwkv_a0[l], rwkv_a_up[l],
                rwkv_g_up[l], rwkv_k_k[l], rwkv_k_a[l], rwkv_r_k[l], rwkv_ln_g[l], rwkv_ln_b[l])
    x1, h2, h2p = _mix(p_uv, p_gate, y_a, xf, mod3, seq, gmlp_ln_g[l], gmlp_ln_b[l], gmlp_ws[l], gmlp_bs[l],
                  w_out_a[l], w_out_b[l], w_out[l], norm2_g[l])
    eidx, wts = _route(h2, router_w[l], router_bias[l])
    rank, counts = _rank(eidx)
    counts = counts[:, 0].astype(jnp.int32)
    pstarts, first_blk, nblk_e, block_rows, nvalid, n_rows = _dispatch_plan(counts, t)
    dest = _dest(eidx, rank, pstarts)
    x_sorted = _dispatch(h2p, dest, n_rows)
    y_sorted = _expert(x_sorted, first_blk, nblk_e, nvalid, block_rows, exp_w1[l], exp_w3[l], exp_w2[l])
    wts_tk = wts.T
    tp = t // COMBINE_PARTS
    out = None
    for part in range(COMBINE_PARTS):
        sl = slice(part * tp, (part + 1) * tp)
        y_gathered = _gather_expert_rows(y_sorted, dest[:, sl])
        out = _combine(y_gathered, wts_tk[sl], h2, x1, mod3, seq, shared_w1[l], shared_w3[l], shared_w2[l],
                       normf_g, part, out)
    return out.reshape(bsz, seq, d)
```

```python
import functools

import jax
import jax.numpy as jnp
from jax import lax
from jax.experimental import pallas as pl
from jax.experimental.pallas import tpu as pltpu
from jax.experimental.pallas import tpu_sc as plsc

F32 = jnp.float32
BF16 = jnp.bfloat16

RWKV_HEAD = 64
RWKV_HEADS = 8
RWKV_WIDTH = RWKV_HEAD * RWKV_HEADS
DECAY_LORA = 64
AAA_LORA = 64
GATE_LORA = 128
RWKV_COLS = 3 * RWKV_WIDTH + DECAY_LORA + AAA_LORA + GATE_LORA
GMLP_WIDTH = 512
GMLP_BLOCK = 128
GMLP_GROUPS = 8
GMLP_GROUP_DIM = GMLP_WIDTH // GMLP_GROUPS
N_EXPERTS = 256
TOP_K = 8
N_EXPERT_GROUPS = 8
GROUP_SIZE = N_EXPERTS // N_EXPERT_GROUPS
TOPK_GROUPS = 4
ROUTED_SCALE = 2.5
RMS_EPS = 1e-6
LN_EPS = 1e-5
GN_EPS = 64e-5

VMEM_LIMIT_BYTES = 56 * 1024 * 1024

CHUNK = 64
RWKV_TILE = 256
ROW_BLOCK = 512
EXPERT_RING = 3
INPROJ_TILE = 512
MIX_TILE = 512
ROUTE_TILE = 1024
RANK_TILE = 512
COMBINE_TILE = 256
COMBINE_PARTS = 8
LANES = 128
SC_WINDOW = 128
SC_GATHERS_PER_STEP = 2


def _params(*sem):
    return pltpu.CompilerParams(dimension_semantics=sem, vmem_limit_bytes=VMEM_LIMIT_BYTES)


def _mm(a, b):
    return jnp.dot(a.astype(BF16), b.astype(BF16), preferred_element_type=F32)


def _mm_nt(a, b):
    return lax.dot_general(a.astype(BF16), b.astype(BF16), (((1,), (1,)), ((), ())),
                           preferred_element_type=F32)


def _mm_tn(a, b):
    return lax.dot_general(a.astype(BF16), b.astype(BF16), (((0,), (0,)), ((), ())),
                           preferred_element_type=F32)


def _split3(a):
    hi = a.astype(BF16)
    r1 = a - hi.astype(F32)
    mid = r1.astype(BF16)
    lo = (r1 - mid.astype(F32)).astype(BF16)
    return hi, mid, lo


def _silu(x):
    return x * jax.nn.sigmoid(x)


def _rms(x):
    return x * lax.rsqrt(jnp.mean(x * x, axis=-1, keepdims=True) + RMS_EPS)


def _pack_halves(x):
    n = x.shape[1] // 2
    bits = lax.bitcast_convert_type(x.astype(BF16).astype(F32), jnp.uint32)
    return bits[:, :n] | (bits[:, n:] >> 16)


def _unpack_halves(u):
    hi = lax.bitcast_convert_type(u & jnp.uint32(0xFFFF0000), F32)
    lo = lax.bitcast_convert_type(u << 16, F32)
    return hi, lo


def _mod_kernel(c_ref, w_ref, b_ref, o_ref):
    s = _silu(c_ref[...])
    hi, mid, lo = _split3(s)
    whi, wmid, wlo = _split3(w_ref[...])
    dot = lambda p, q: jnp.dot(p, q, preferred_element_type=F32)
    acc = dot(hi, whi) + (dot(hi, wmid) + dot(mid, whi)) + (dot(hi, wlo) + dot(mid, wmid) + dot(lo, whi))
    o_ref[...] = acc + b_ref[...]


def _mod(c, ada_w, ada_b):
    bsz, d = c.shape
    n = ada_w.shape[1]
    tn = d
    return pl.pallas_call(
        _mod_kernel,
        grid=(n // tn,),
        in_specs=[pl.BlockSpec((bsz, d), lambda j: (0, 0)),
                  pl.BlockSpec((d, tn), lambda j: (0, j)),
                  pl.BlockSpec((1, tn), lambda j: (0, j))],
        out_specs=pl.BlockSpec((bsz, tn), lambda j: (0, j)),
        out_shape=jax.ShapeDtypeStruct((bsz, n), F32),
        compiler_params=_params("arbitrary"),
        name="mod",
    )(c, ada_w, ada_b.reshape(1, n))


def _inproj_kernel(x_ref, mod_ref, g_ref, wr_ref, wuv_ref, wg_ref, pr_ref, puv_ref, pg_ref):
    sh = mod_ref[0, 0:1, :]
    sc = mod_ref[0, 1:2, :]
    h = (_rms(x_ref[...]) * g_ref[...]) * (1.0 + sc) + sh
    hb = h.astype(BF16)
    pr_ref[...] = jnp.dot(hb, wr_ref[...], preferred_element_type=F32).astype(pr_ref.dtype)
    puv_ref[...] = jnp.dot(hb, wuv_ref[...], preferred_element_type=F32).astype(puv_ref.dtype)
    pg_ref[...] = jnp.dot(hb, wg_ref[...], preferred_element_type=F32).astype(pg_ref.dtype)


def _inproj(xf, mod3, norm1_g, w_in, seq):
    t, d = xf.shape
    tm = min(INPROJ_TILE, seq)
    per_seq = seq // tm
    wr = w_in[:, :RWKV_COLS].astype(BF16)
    wuv = w_in[:, RWKV_COLS:RWKV_COLS + 2 * GMLP_WIDTH].astype(BF16)
    wg = w_in[:, RWKV_COLS + 2 * GMLP_WIDTH:].astype(BF16)
    full = lambda a: pl.BlockSpec(a.shape, lambda i: (0,) * a.ndim)
    g = norm1_g.reshape(1, d)
    return pl.pallas_call(
        _inproj_kernel,
        grid=(t // tm,),
        in_specs=[pl.BlockSpec((tm, d), lambda i: (i, 0)),
                  pl.BlockSpec((1,) + mod3.shape[1:], lambda i: (i // per_seq, 0, 0)),
                  full(g), full(wr), full(wuv), full(wg)],
        out_specs=[pl.BlockSpec((tm, wr.shape[1]), lambda i: (i, 0)),
                   pl.BlockSpec((tm, wuv.shape[1]), lambda i: (i, 0)),
                   pl.BlockSpec((tm, wg.shape[1]), lambda i: (i, 0))],
        out_shape=[jax.ShapeDtypeStruct((t, wr.shape[1]), BF16),
                   jax.ShapeDtypeStruct((t, wuv.shape[1]), BF16),
                   jax.ShapeDtypeStruct((t, wg.shape[1]), BF16)],
        compiler_params=_params("arbitrary"),
        name="inproj",
    )(xf, mod3, g, wr, wuv, wg)


def _unit_lower_inverses(ns, row, col):
    c = ns[0].shape[0]
    eye = jnp.where(row == col, 1.0, 0.0).astype(F32)
    blk8 = (row // 8) == (col // 8)
    n8 = [jnp.where(blk8, n, 0.0) for n in ns]
    n8_2 = [_mm(a, a) for a in n8]
    n8_4 = [_mm(a, a) for a in n8_2]
    inv = [eye + a for a in n8]
    inv = [i + _mm(i, b) for i, b in zip(inv, n8_2)]
    inv = [i + _mm(i, b) for i, b in zip(inv, n8_4)]
    s = 8
    while s < c:
        sel = ((row // (2 * s)) == (col // (2 * s))) & ((row // s) != (col // s))
        low = [jnp.where(sel, n, 0.0) for n in ns]
        t1 = [_mm(l, i) for l, i in zip(low, inv)]
        inv = [i + _mm(i, t) for i, t in zip(inv, t1)]
        s *= 2
    return inv


def _rwkv_kernel(p_ref, mu_ref, w0_ref, wup_ref, a0_ref, aup_ref, gup_ref, kk_ref, ka_ref, rk_ref,
                 lng_ref, lnb_ref, bd_ref, y_ref, state_ref, carry_ref, ycat_ref, *, c):
    tm = p_ref.shape[0]
    nq = tm // c
    hd = RWKV_HEAD
    nh = RWKV_HEADS

    @pl.when(pl.program_id(1) == 0)
    def _():
        state_ref[...] = jnp.zeros_like(state_ref)
        carry_ref[...] = jnp.zeros_like(carry_ref)

    p = p_ref[...].astype(F32)
    rowc = lax.broadcasted_iota(jnp.int32, (tm, 1), 0)
    prev = jnp.where(rowc == 0, carry_ref[...], pltpu.roll(p, 1, axis=0))
    carry_ref[...] = p[tm - 1:tm, :]
    ps = p + (prev - p) * mu_ref[...]

    o = RWKV_WIDTH
    r = ps[:, 0:o]
    k = ps[:, o:2 * o]
    v = ps[:, 2 * o:3 * o]
    xw = ps[:, 3 * o:3 * o + DECAY_LORA]
    xa = ps[:, 3 * o + DECAY_LORA:3 * o + DECAY_LORA + AAA_LORA]
    xg = ps[:, 3 * o + DECAY_LORA + AAA_LORA:]

    z = -(w0_ref[...] + _mm(jnp.tanh(xw), wup_ref[...]))
    softplus = jnp.maximum(z, 0.0) + jnp.log1p(jnp.exp(-jnp.abs(z)))
    wlog = -softplus - 0.5
    ld = -jnp.exp(wlog)
    a = jax.nn.sigmoid(a0_ref[...] + _mm(xa, aup_ref[...]))
    g = _mm(jax.nn.sigmoid(xg), gup_ref[...])

    def head_sum(x):
        hi = x.astype(BF16)
        lo = (x - hi.astype(F32)).astype(BF16)
        bd = bd_ref[...]
        return jnp.dot(hi, bd, preferred_element_type=F32) + jnp.dot(lo, bd, preferred_element_type=F32)

    kkf = k * kk_ref[...]
    kkn = kkf * (1.0 / jnp.maximum(jnp.sqrt(head_sum(kkf * kkf)), 1e-12))
    k2 = k * (1.0 + (a - 1.0) * ka_ref[...])
    avec = -kkn
    bvec = kkn * a

    rowt = lax.broadcasted_iota(jnp.int32, (tm, tm), 0)
    colt = lax.broadcasted_iota(jnp.int32, (tm, tm), 1)
    tri = jnp.where((rowt >= colt) & ((rowt // c) == (colt // c)), 1.0, 0.0).astype(BF16)
    hi, mid, lo = _split3(ld)
    dot = lambda q: jnp.dot(tri, q, preferred_element_type=F32)
    cl = dot(hi) + dot(mid) + dot(lo)
    cl_end = jnp.concatenate(
        [jnp.broadcast_to(cl[(q + 1) * c - 1:(q + 1) * c, :], (c, o)) for q in range(nq)], axis=0)
    g_inv = jnp.exp(-cl)
    g_tail = jnp.exp(cl_end - cl)
    g_end = jnp.exp(cl_end)
    rt = (r * jnp.exp(cl)).astype(BF16)
    at = (avec * jnp.exp(cl - ld)).astype(BF16)
    kt = (k2 * g_inv).astype(BF16)
    bt = (bvec * g_inv).astype(BF16)
    kend = (k2 * g_tail).astype(BF16)
    bend = (bvec * g_tail).astype(BF16)
    vb = v.astype(BF16)

    row = lax.broadcasted_iota(jnp.int32, (c, c), 0)
    col = lax.broadcasted_iota(jnp.int32, (c, c), 1)
    strict = row > col
    incl = row >= col
    items = [(q, h) for q in range(nq) for h in range(nh)]
    blk = lambda x, q, h: x[q * c:(q + 1) * c, h * hd:(h + 1) * hd]

    lhs = [jnp.concatenate([blk(at, q, h), blk(rt, q, h)], axis=0) for q, h in items]
    mat_b = [_mm_nt(l, blk(bt, q, h)) for l, (q, h) in zip(lhs, items)]
    mat_k = [_mm_nt(l, blk(kt, q, h)) for l, (q, h) in zip(lhs, items)]
    a_ab = [jnp.where(strict, m[:c], 0.0) for m in mat_b]
    a_rb = [jnp.where(incl, m[c:], 0.0).astype(BF16) for m in mat_b]
    a_ak = [jnp.where(strict, m[:c], 0.0) for m in mat_k]
    a_rk = [jnp.where(incl, m[c:], 0.0) for m in mat_k]
    tinv = _unit_lower_inverses(a_ab, row, col)
    akv = [_mm(m, blk(vb, q, h)) for m, (q, h) in zip(a_ak, items)]
    y_loc = [_mm(m, blk(vb, q, h)) for m, (q, h) in zip(a_rk, items)]
    s_loc = [_mm_tn(blk(vb, q, h), blk(kend, q, h)) for q, h in items]
    wmat = [_mm(t, blk(at, q, h)).astype(BF16) for t, (q, h) in zip(tinv, items)]
    ut = [_mm(t, m) for t, m in zip(tinv, akv)]

    state = [state_ref[h] for h in range(nh)]
    for q in range(nq):
        base = q * nh
        sb = [s.astype(BF16) for s in state]
        u = [_mm_nt(wmat[base + h], sb[h]) + ut[base + h] for h in range(nh)]
        yh = [_mm_nt(blk(rt, q, h), sb[h]) + y_loc[base + h] for h in range(nh)]
        ub = [x.astype(BF16) for x in u]
        yh = [y0 + _mm(a_rb[base + h], ub[h]) for h, y0 in enumerate(yh)]
        state = [state[h] * g_end[q * c:q * c + 1, h * hd:(h + 1) * hd]
                 + _mm_tn(ub[h], blk(bend, q, h)) + s_loc[base + h] for h in range(nh)]
        for h in range(nh):
            ycat_ref[q * c:(q + 1) * c, h * hd:(h + 1) * hd] = yh[h]
    for h in range(nh):
        state_ref[h] = state[h]

    y = ycat_ref[...]
    dev = y - head_sum(y) * (1.0 / hd)
    rstd = lax.rsqrt(head_sum(dev * dev) * (1.0 / hd) + GN_EPS)
    bonus = head_sum(r * k2 * rk_ref[...])
    yn = dev * rstd * lng_ref[...] + lnb_ref[...]
    y_ref[...] = (yn + bonus * v) * g


def _rwkv(p_rwkv, bsz, seq, tshift_mu, w0, w_up, a0, a_up, g_up, k_k, k_a, r_k, ln_g, ln_b):
    t = p_rwkv.shape[0]
    c = min(CHUNK, seq)
    tm = min(RWKV_TILE, seq)
    nstep = seq // tm
    row = lambda a: a.reshape(1, -1)
    head_of = jnp.arange(RWKV_WIDTH, dtype=jnp.int32) // RWKV_HEAD
    block_diag = (head_of[:, None] == head_of[None, :]).astype(BF16)
    args = (row(tshift_mu), row(w0), w_up, row(a0), a_up, g_up, row(k_k), row(k_a), row(r_k),
            row(ln_g), row(ln_b), block_diag)
    full = lambda a: pl.BlockSpec(a.shape, lambda b, j: (0,) * a.ndim)
    return pl.pallas_call(
        functools.partial(_rwkv_kernel, c=c),
        grid=(bsz, nstep),
        in_specs=[pl.BlockSpec((tm, RWKV_COLS), lambda b, j: (b * nstep + j, 0))] + [full(a) for a in args],
        out_specs=pl.BlockSpec((tm, RWKV_WIDTH), lambda b, j: (b * nstep + j, 0)),
        out_shape=jax.ShapeDtypeStruct((t, RWKV_WIDTH), F32),
        scratch_shapes=[pltpu.VMEM((RWKV_HEADS, RWKV_HEAD, RWKV_HEAD), F32),
                        pltpu.VMEM((1, RWKV_COLS), F32),
                        pltpu.VMEM((tm, RWKV_WIDTH), F32)],
        compiler_params=_params("arbitrary", "arbitrary"),
        name="rwkv",
    )(p_rwkv, *args)


def _mix_kernel(puv_ref, pg_ref, ya_ref, x_ref, mod_ref, lng_ref, lnb_ref, ws_ref, bsm_ref,
                woa_ref, wob_ref, wo_ref, n2g_ref, x1_ref, h2_ref, h2p_ref, yb_ref):
    tm = x_ref.shape[0]
    gw = GMLP_WIDTH
    puv = puv_ref[...].astype(F32)
    u = jax.nn.gelu(puv[:, :gw])
    vg = jax.nn.gelu(puv[:, gw:])
    mu = jnp.mean(vg, axis=-1, keepdims=True)
    var = jnp.mean(jnp.square(vg - mu), axis=-1, keepdims=True)
    vln = (vg - mu) * lax.rsqrt(var + LN_EPS) * lng_ref[...] + lnb_ref[...]

    row = lax.broadcasted_iota(jnp.int32, (GMLP_BLOCK, GMLP_BLOCK), 0)
    col = lax.broadcasted_iota(jnp.int32, (GMLP_BLOCK, GMLP_BLOCK), 1)
    gd = GMLP_GROUP_DIM
    for gi in range(GMLP_GROUPS):
        wsm = jnp.where(row >= col, ws_ref[gi], 0.0).astype(BF16)
        for nb in range(tm // GMLP_BLOCK):
            rs = slice(nb * GMLP_BLOCK, (nb + 1) * GMLP_BLOCK)
            cs = slice(gi * gd, (gi + 1) * gd)
            sv = jnp.dot(wsm, vln[rs, cs].astype(BF16), preferred_element_type=F32) + bsm_ref[:, cs]
            yb_ref[rs, cs] = u[rs, cs] * sv

    pg = pg_ref[...].astype(F32)
    d = x_ref.shape[1]
    gate_a = jax.nn.sigmoid(pg[:, :d])
    gate_b = jax.nn.sigmoid(pg[:, d:])
    merged = gate_a * _mm(ya_ref[...], woa_ref[...]) + gate_b * _mm(yb_ref[...], wob_ref[...])
    g1 = mod_ref[0, 2:3, :]
    x1 = x_ref[...] + g1 * _mm(merged, wo_ref[...])
    x1_ref[...] = x1
    sh2 = mod_ref[0, 3:4, :]
    sc2 = mod_ref[0, 4:5, :]
    h2 = (_rms(x1) * n2g_ref[...]) * (1.0 + sc2) + sh2
    h2_ref[...] = h2
    packed = _pack_halves(h2)
    for j in range(h2p_ref.shape[0]):
        h2p_ref[j] = packed[:, j * LANES:(j + 1) * LANES]


def _mix(p_uv, p_gate, y_a, xf, mod3, seq, ln_g, ln_b, ws, bs, w_out_a, w_out_b, w_out, norm2_g):
    t, d = xf.shape
    tm = min(MIX_TILE, seq)
    per_seq = seq // tm
    row = lambda a: a.reshape(1, -1)
    bsm = jnp.repeat(bs.T, GMLP_GROUP_DIM, axis=1)
    consts = (row(ln_g), row(ln_b), ws, bsm, w_out_a.astype(BF16), w_out_b.astype(BF16),
              w_out.astype(BF16), row(norm2_g))
    full = lambda a: pl.BlockSpec(a.shape, lambda i: (0,) * a.ndim)
    tile = lambda a: pl.BlockSpec((tm, a.shape[1]), lambda i: (i, 0))
    return pl.pallas_call(
        _mix_kernel,
        grid=(t // tm,),
        in_specs=[tile(p_uv), tile(p_gate), tile(y_a), tile(xf),
                  pl.BlockSpec((1,) + mod3.shape[1:], lambda i: (i // per_seq, 0, 0))]
                 + [full(a) for a in consts],
        out_specs=[pl.BlockSpec((tm, d), lambda i: (i, 0)), pl.BlockSpec((tm, d), lambda i: (i, 0)),
                   pl.BlockSpec((d // 2 // LANES, tm, LANES), lambda i: (0, i, 0))],
        out_shape=[jax.ShapeDtypeStruct((t, d), F32), jax.ShapeDtypeStruct((t, d), F32),
                   jax.ShapeDtypeStruct((d // 2 // LANES, t, LANES), jnp.uint32)],
        scratch_shapes=[pltpu.VMEM((tm, GMLP_WIDTH), F32)],
        compiler_params=_params("arbitrary"),
        name="mix",
    )(p_uv, p_gate, y_a, xf, mod3, *consts)


def _first_argmax(vals, idx, big):
    m = jnp.max(vals, axis=0, keepdims=True)
    first = jnp.min(jnp.where(vals == m, idx, big), axis=0, keepdims=True)
    return m, first


def _route_kernel(h_ref, rwt_ref, bias_ref, eidx_ref, wts_ref):
    tm = h_ref.shape[0]
    ne = N_EXPERTS
    hhi, hmid, _ = _split3(h_ref[...])
    whi, wmid, _ = _split3(rwt_ref[...])
    nt = lambda p, q: lax.dot_general(p, q, (((1,), (1,)), ((), ())), preferred_element_type=F32)
    logits = nt(whi, hhi) + (nt(whi, hmid) + nt(wmid, hhi))
    scores = jax.nn.sigmoid(logits)
    sel = scores + bias_ref[...]
    neg = -jnp.inf

    gs = GROUP_SIZE
    gidx = lax.broadcasted_iota(jnp.int32, (gs, tm), 0)
    grp_rows = []
    for gi in range(N_EXPERT_GROUPS):
        blk = sel[gi * gs:(gi + 1) * gs, :]
        m1, i1 = _first_argmax(blk, gidx, gs)
        m2 = jnp.max(jnp.where(gidx == i1, neg, blk), axis=0, keepdims=True)
        grp_rows.append(m1 + m2)
    grp = jnp.concatenate(grp_rows, axis=0)

    ng = N_EXPERT_GROUPS
    giota = lax.broadcasted_iota(jnp.int32, (ng, tm), 0)
    gsel = jnp.zeros((ng, tm), jnp.bool_)
    work = grp
    for _ in range(TOPK_GROUPS):
        _, gi1 = _first_argmax(work, giota, ng)
        hit = giota == gi1
        gsel = gsel | hit
        work = jnp.where(hit, neg, work)
    gself = jnp.where(gsel, 1.0, 0.0)
    emask = jnp.concatenate(
        [jnp.broadcast_to(gself[gi:gi + 1, :], (gs, tm)) for gi in range(ng)], axis=0) > 0.5

    eiota = lax.broadcasted_iota(jnp.int32, (ne, tm), 0)
    work = jnp.where(emask, sel, neg)
    idx_rows, w_rows = [], []
    for _ in range(TOP_K):
        _, e1 = _first_argmax(work, eiota, ne)
        hit = eiota == e1
        idx_rows.append(e1)
        w_rows.append(jnp.sum(jnp.where(hit, scores, 0.0), axis=0, keepdims=True))
        work = jnp.where(hit, neg, work)
    eidx_ref[...] = jnp.concatenate(idx_rows, axis=0)
    w = jnp.concatenate(w_rows, axis=0)
    wts_ref[...] = w / jnp.sum(w, axis=0, keepdims=True) * ROUTED_SCALE


def _route(h2, router_w, router_bias):
    t, d = h2.shape
    tm = min(ROUTE_TILE, t)
    rwt = router_w.T
    bias = router_bias.reshape(N_EXPERTS, 1)
    return pl.pallas_call(
        _route_kernel,
        grid=(t // tm,),
        in_specs=[pl.BlockSpec((tm, d), lambda i: (i, 0)),
                  pl.BlockSpec(rwt.shape, lambda i: (0, 0)),
                  pl.BlockSpec(bias.shape, lambda i: (0, 0))],
        out_specs=[pl.BlockSpec((TOP_K, tm), lambda i: (0, i)), pl.BlockSpec((TOP_K, tm), lambda i: (0, i))],
        out_shape=[jax.ShapeDtypeStruct((TOP_K, t), jnp.int32), jax.ShapeDtypeStruct((TOP_K, t), F32)],
        compiler_params=_params("arbitrary"),
        name="route",
    )(h2, rwt, bias)


def _rank_kernel(e_ref, upper_ref, rank_ref, count_ref):
    tm = e_ref.shape[1]
    ne = N_EXPERTS

    @pl.when(pl.program_id(0) == 0)
    def _():
        count_ref[...] = jnp.zeros_like(count_ref)

    e = e_ref[...]
    eiota = lax.broadcasted_iota(jnp.int32, (ne, tm), 0)
    onehot = jnp.zeros((ne, tm), F32)
    for kk in range(TOP_K):
        onehot = onehot + jnp.where(eiota == e[kk:kk + 1, :], 1.0, 0.0)
    before = jnp.dot(onehot.astype(BF16), upper_ref[...], preferred_element_type=F32)
    base = count_ref[:, 0:1]
    tot = before + base
    rows = [jnp.sum(jnp.where(eiota == e[kk:kk + 1, :], tot, 0.0), axis=0, keepdims=True)
            for kk in range(TOP_K)]
    rank_ref[...] = jnp.concatenate(rows, axis=0).astype(jnp.int32)
    count_ref[...] = count_ref[...] + jnp.sum(onehot, axis=1, keepdims=True)


def _rank(eidx):
    k, t = eidx.shape
    tm = min(RANK_TILE, t)
    upper = jnp.triu(jnp.ones((tm, tm), F32), 1).astype(BF16)
    return pl.pallas_call(
        _rank_kernel,
        grid=(t // tm,),
        in_specs=[pl.BlockSpec((k, tm), lambda i: (0, i)), pl.BlockSpec((tm, tm), lambda i: (0, 0))],
        out_specs=[pl.BlockSpec((k, tm), lambda i: (0, i)), pl.BlockSpec((N_EXPERTS, 128), lambda i: (0, 0))],
        out_shape=[jax.ShapeDtypeStruct((k, t), jnp.int32), jax.ShapeDtypeStruct((N_EXPERTS, 128), F32)],
        compiler_params=_params("arbitrary"),
        name="rank",
    )(eidx, upper)


def _dest_kernel(e_ref, rank_ref, pst_ref, dest_ref):
    tm = e_ref.shape[1]
    e = e_ref[...]
    eiota = lax.broadcasted_iota(jnp.int32, (N_EXPERTS, tm), 0)
    pst = pst_ref[...]
    rows = [jnp.sum(jnp.where(eiota == e[kk:kk + 1, :], pst, 0.0), axis=0, keepdims=True)
            for kk in range(TOP_K)]
    dest_ref[...] = jnp.concatenate(rows, axis=0).astype(jnp.int32) + rank_ref[...]


def _dest(eidx, rank, pstarts):
    k, t = eidx.shape
    tm = min(RANK_TILE, t)
    pst = pstarts.astype(F32).reshape(N_EXPERTS, 1)
    return pl.pallas_call(
        _dest_kernel,
        grid=(t // tm,),
        in_specs=[pl.BlockSpec((k, tm), lambda i: (0, i)), pl.BlockSpec((k, tm), lambda i: (0, i)),
                  pl.BlockSpec((N_EXPERTS, 1), lambda i: (0, 0))],
        out_specs=pl.BlockSpec((k, tm), lambda i: (0, i)),
        out_shape=jax.ShapeDtypeStruct((k, t), jnp.int32),
        compiler_params=_params("arbitrary"),
        name="dest",
    )(eidx, rank, pst)


def _sc_scatter(rows, idx, n_out):
    n, lanes = rows.shape
    nstep, copies, w = idx.shape
    mesh = plsc.VectorSubcoreMesh(core_axis_name="core", subcore_axis_name="subcore")

    @functools.partial(pl.kernel, out_type=jax.ShapeDtypeStruct((n_out, lanes), rows.dtype), mesh=mesh,
                       scratch_types=[pltpu.SemaphoreType.DMA])
    def scatter_kernel(rows_hbm, idx_hbm, out_hbm, sem):
        def body(rows_vmem, idx_vmem):
            cps = [pltpu.make_async_copy(rows_vmem, out_hbm.at[idx_vmem.at[j]], sem) for j in range(copies)]
            for cp in cps:
                cp.start()
            for cp in cps:
                cp.wait()

        pltpu.emit_pipeline(
            body,
            grid=(nstep,),
            in_specs=[pl.BlockSpec((w, lanes), index_map=lambda i: (i, 0)),
                      pl.BlockSpec((copies, w), index_map=lambda i: (i, 0))],
            out_specs=[],
            core_axis_name=("core", "subcore"),
            dimension_semantics=(pltpu.PARALLEL,),
        )(rows_hbm, idx_hbm)

    return scatter_kernel(rows, idx.reshape(nstep * copies, w))


def _dispatch(h2p, dest_tk, n_rows):
    nchunk, t, lanes = h2p.shape
    k = dest_tk.shape[0]
    w = SC_WINDOW
    idx = dest_tk.reshape(k, t // w, w).transpose(1, 0, 2)[None]
    idx = idx + (jnp.arange(nchunk, dtype=jnp.int32) * n_rows)[:, None, None, None]
    out = _sc_scatter(h2p.reshape(nchunk * t, lanes), idx.reshape(nchunk * (t // w), k, w), nchunk * n_rows)
    return out.reshape(nchunk, n_rows, lanes)


def _expert_kernel(first_ref, nblk_ref, nv_ref, rows_ref, x_hbm, w1_ref, w3_ref, w2_ref, y_hbm,
                   xbuf, ybuf, xsem, ysem, w1b, w3b, w2b):
    e = pl.program_id(0)
    ring, nchunk, rb = xbuf.shape[0], xbuf.shape[1], xbuf.shape[2]
    half = nchunk * LANES
    total = nv_ref[0]

    def x_copy(g, slot):
        return pltpu.make_async_copy(x_hbm.at[:, pl.ds(g * rb, rb), :], xbuf.at[slot], xsem.at[slot])

    def y_copy(g, slot):
        return pltpu.make_async_copy(ybuf.at[slot], y_hbm.at[:, pl.ds(g * rb, rb), :], ysem.at[slot])

    @pl.when(e == 0)
    def _():
        for g0 in range(ring - 1):
            @pl.when(g0 < total)
            def _():
                x_copy(g0, g0).start()

    @pl.when(nblk_ref[e] > 0)
    def _():
        w1b[...] = w1_ref[0].astype(BF16)
        w3b[...] = w3_ref[0].astype(BF16)
        w2b[...] = w2_ref[0].astype(BF16)

    def block(j, carry):
        g = first_ref[e] + j
        slot = g % ring
        x_copy(g, slot).wait()

        @pl.when(g + ring - 1 < total)
        def _():
            x_copy(g + ring - 1, (g + ring - 1) % ring).start()

        live = lax.broadcasted_iota(jnp.int32, (rb, 1), 0) < rows_ref[g]
        x = jnp.concatenate([xbuf[slot, c] for c in range(nchunk)], axis=1)
        hi, lo = _unpack_halves(jnp.where(live, x, jnp.uint32(0)))
        xa = hi.astype(BF16)
        xb = lo.astype(BF16)
        dot = lambda p, q: jnp.dot(p, q, preferred_element_type=F32)
        h1 = dot(xa, w1b[:half, :]) + dot(xb, w1b[half:, :])
        h3 = dot(xa, w3b[:half, :]) + dot(xb, w3b[half:, :])
        hid = _silu(h1) * h3
        packed = _pack_halves(dot(hid.astype(BF16), w2b[...]))

        @pl.when(g >= ring)
        def _():
            y_copy(g - ring, slot).wait()

        for c in range(nchunk):
            ybuf[slot, c] = packed[:, c * LANES:(c + 1) * LANES]
        y_copy(g, slot).start()
        return carry

    lax.fori_loop(0, nblk_ref[e], block, 0)

    @pl.when(e == pl.num_programs(0) - 1)
    def _():
        for back in range(ring, 0, -1):
            @pl.when(total >= back)
            def _():
                y_copy(total - back, (total - back) % ring).wait()


def _expert(x_sorted, first_blk, nblk_e, nvalid, block_rows, w1, w3, w2):
    nchunk, n_rows, lanes = x_sorted.shape
    rb = ROW_BLOCK
    ne, d, f = w1.shape
    grid_spec = pltpu.PrefetchScalarGridSpec(
        num_scalar_prefetch=4,
        grid=(ne,),
        in_specs=[
            pl.BlockSpec(memory_space=pl.ANY),
            pl.BlockSpec((1, d, f), lambda e, *_: (e, 0, 0)),
            pl.BlockSpec((1, d, f), lambda e, *_: (e, 0, 0)),
            pl.BlockSpec((1, f, d), lambda e, *_: (e, 0, 0)),
        ],
        out_specs=pl.BlockSpec(memory_space=pl.ANY),
        scratch_shapes=[pltpu.VMEM((EXPERT_RING, nchunk, rb, lanes), jnp.uint32),
                        pltpu.VMEM((EXPERT_RING, nchunk, rb, lanes), jnp.uint32),
                        pltpu.SemaphoreType.DMA((EXPERT_RING,)), pltpu.SemaphoreType.DMA((EXPERT_RING,)),
                        pltpu.VMEM((d, f), BF16), pltpu.VMEM((d, f), BF16), pltpu.VMEM((f, d), BF16)],
    )
    return pl.pallas_call(
        _expert_kernel,
        grid_spec=grid_spec,
        out_shape=jax.ShapeDtypeStruct((nchunk, n_rows, lanes), jnp.uint32),
        compiler_params=_params("arbitrary"),
        name="expert",
    )(first_blk, nblk_e, nvalid, block_rows, x_sorted, w1, w3, w2)


def _sc_gather(table, idx):
    n = idx.shape[0]
    w = SC_WINDOW
    group = SC_GATHERS_PER_STEP
    mesh = plsc.VectorSubcoreMesh(core_axis_name="core", subcore_axis_name="subcore")

    @functools.partial(pl.kernel, out_type=jax.ShapeDtypeStruct((n, table.shape[1]), table.dtype), mesh=mesh,
                       scratch_types=[pltpu.SemaphoreType.DMA])
    def gather_kernel(table_hbm, idx_hbm, out_hbm, sem):
        def body(idx_vmem, out_vmem):
            copies = [pltpu.make_async_copy(table_hbm.at[idx_vmem.at[j]], out_vmem.at[pl.ds(j * w, w)], sem)
                      for j in range(group)]
            for cp in copies:
                cp.start()
            for cp in copies:
                cp.wait()

        pltpu.emit_pipeline(
            body,
            grid=(n // (group * w),),
            in_specs=[pl.BlockSpec((group, w), index_map=lambda i: (i, 0))],
            out_specs=[pl.BlockSpec((group * w, table.shape[1]), index_map=lambda i: (i, 0))],
            core_axis_name=("core", "subcore"),
            dimension_semantics=(pltpu.PARALLEL,),
        )(idx_hbm, out_hbm)

    return gather_kernel(table, idx.reshape(n // w, w))


def _gather_expert_rows(y_sorted, dest_tk):
    nchunk, n_rows, lanes = y_sorted.shape
    k, t = dest_tk.shape
    idx = dest_tk[None, :, :] + (jnp.arange(nchunk, dtype=jnp.int32) * n_rows)[:, None, None]
    rows = _sc_gather(y_sorted.reshape(nchunk * n_rows, lanes), idx.reshape(-1))
    return rows.reshape(nchunk, k, t, lanes)


def _combine_kernel(yg_ref, wts_ref, h_ref, x1_ref, mod_ref, sw1_ref, sw3_ref, sw2_ref, nfg_ref, *rest):
    o_ref = rest[-1]
    nchunk = yg_ref.shape[0]
    w = wts_ref[...]
    acc_hi = [None] * nchunk
    acc_lo = [None] * nchunk
    for kk in range(TOP_K):
        wk = w[:, kk:kk + 1]
        for j in range(nchunk):
            hi, lo = _unpack_halves(yg_ref[j, kk])
            acc_hi[j] = hi * wk if kk == 0 else acc_hi[j] + hi * wk
            acc_lo[j] = lo * wk if kk == 0 else acc_lo[j] + lo * wk
    routed = jnp.concatenate(acc_hi + acc_lo, axis=1)
    hb = h_ref[...].astype(BF16)
    hid = _silu(jnp.dot(hb, sw1_ref[...], preferred_element_type=F32)) * jnp.dot(
        hb, sw3_ref[...], preferred_element_type=F32)
    shared = jnp.dot(hid.astype(BF16), sw2_ref[...], preferred_element_type=F32)
    g2 = mod_ref[0, 5:6, :]
    x2 = x1_ref[...] + g2 * (routed + shared)
    o_ref[...] = _rms(x2) * nfg_ref[...]


def _combine(y_gathered, wts_tk, h2, x1, mod3, seq, sw1, sw3, sw2, normf_g, part, prev):
    t, d = h2.shape
    nchunk, k, tp, lanes = y_gathered.shape
    tm = min(COMBINE_TILE, seq, tp)
    per_seq = seq // tm
    off = part * (tp // tm)
    consts = (sw1.astype(BF16), sw3.astype(BF16), sw2.astype(BF16), normf_g.reshape(1, d))
    full = lambda a: pl.BlockSpec(a.shape, lambda i: (0,) * a.ndim)
    tile = lambda a: pl.BlockSpec((tm, a.shape[1]), lambda i: (off + i, 0))
    args = [y_gathered, wts_tk, h2, x1, mod3, *consts]
    in_specs = ([pl.BlockSpec((nchunk, k, tm, lanes), lambda i: (0, 0, i, 0)),
                 pl.BlockSpec((tm, k), lambda i: (i, 0)), tile(h2), tile(x1),
                 pl.BlockSpec((1,) + mod3.shape[1:], lambda i: ((off + i) // per_seq, 0, 0))]
                + [full(a) for a in consts])
    aliases = {}
    if prev is not None:
        aliases = {len(args): 0}
        args.append(prev)
        in_specs.append(pl.BlockSpec(memory_space=pl.ANY))
    return pl.pallas_call(
        _combine_kernel,
        grid=(tp // tm,),
        in_specs=in_specs,
        out_specs=pl.BlockSpec((tm, d), lambda i: (off + i, 0)),
        out_shape=jax.ShapeDtypeStruct((t, d), F32),
        input_output_aliases=aliases,
        compiler_params=_params("arbitrary"),
        name="combine",
    )(*args)


def _dispatch_plan(counts, t):
    rb = ROW_BLOCK
    padded = (counts + rb - 1) // rb * rb
    pends = jnp.cumsum(padded)
    pstarts = pends - padded
    n_rows = t * TOP_K + N_EXPERTS * rb
    block_start = jnp.arange(n_rows // rb, dtype=jnp.int32) * rb
    block_e = jnp.sum((pends[None, :] <= block_start[:, None]).astype(jnp.int32), axis=1)
    block_e = jnp.minimum(block_e, N_EXPERTS - 1)
    block_rows = jnp.clip(pstarts[block_e] + counts[block_e] - block_start, 0, rb).astype(jnp.int32)
    nvalid = (pends[-1:] // rb).astype(jnp.int32)
    return pstarts, pstarts // rb, padded // rb, block_rows, nvalid, n_rows


def kernel(x, c, ada_w, ada_b, norm1_g, norm2_g, w_in, tshift_mu, rwkv_w0, rwkv_w_up, rwkv_a0, rwkv_a_up,
           rwkv_g_up, rwkv_k_k, rwkv_k_a, rwkv_r_k, rwkv_ln_g, rwkv_ln_b, gmlp_ln_g, gmlp_ln_b, gmlp_ws,
           gmlp_bs, w_out_a, w_out_b, w_out, router_w, router_bias, exp_w1, exp_w3, exp_w2, shared_w1,
           shared_w3, shared_w2, normf_g):
    bsz, seq, d = x.shape
    t = bsz * seq
    xf = x.reshape(t, d)
    assert ada_w.shape[0] == 1, "single-layer block only"
    l = 0
    mod3 = _mod(c, ada_w[l], ada_b[l]).reshape(bsz, 6, d)
    p_rwkv, p_uv, p_gate = _inproj(xf, mod3, norm1_g[l], w_in[l], seq)
    y_a = _rwkv(p_rwkv, bsz, seq, tshift_mu[l], rwkv_w0[l], rwkv_w_up[l], rwkv_a0[l], rwkv_a_up[l],
                rwkv_g_up[l], rwkv_k_k[l], rwkv_k_a[l], rwkv_r_k[l], rwkv_ln_g[l], rwkv_ln_b[l])
    x1, h2, h2p = _mix(p_uv, p_gate, y_a, xf, mod3, seq, gmlp_ln_g[l], gmlp_ln_b[l], gmlp_ws[l], gmlp_bs[l],
                  w_out_a[l], w_out_b[l], w_out[l], norm2_g[l])
    eidx, wts = _route(h2, router_w[l], router_bias[l])
    rank, counts = _rank(eidx)
    counts = counts[:, 0].astype(jnp.int32)
    pstarts, first_blk, nblk_e, block_rows, nvalid, n_rows = _dispatch_plan(counts, t)
    dest = _dest(eidx, rank, pstarts)
    x_sorted = _dispatch(h2p, dest, n_rows)
    y_sorted = _expert(x_sorted, first_blk, nblk_e, nvalid, block_rows, exp_w1[l], exp_w3[l], exp_w2[l])
    wts_tk = wts.T
    tp = t // COMBINE_PARTS
    out = None
    for part in range(COMBINE_PARTS):
        sl = slice(part * tp, (part + 1) * tp)
        y_gathered = _gather_expert_rows(y_sorted, dest[:, sl])
        out = _combine(y_gathered, wts_tk[sl], h2, x1, mod3, seq, shared_w1[l], shared_w3[l], shared_w2[l],
                       normf_g, part, out)
    return out.reshape(bsz, seq, d)
```

```python
import functools

import jax
import jax.numpy as jnp
from jax import lax
from jax.experimental import pallas as pl
from jax.experimental.pallas import tpu as pltpu
from jax.experimental.pallas import tpu_sc as plsc

F32 = jnp.float32
BF16 = jnp.bfloat16

RWKV_HEAD = 64
RWKV_HEADS = 8
RWKV_WIDTH = RWKV_HEAD * RWKV_HEADS
DECAY_LORA = 64
AAA_LORA = 64
GATE_LORA = 128
RWKV_COLS = 3 * RWKV_WIDTH + DECAY_LORA + AAA_LORA + GATE_LORA
GMLP_WIDTH = 512
GMLP_BLOCK = 128
GMLP_GROUPS = 8
GMLP_GROUP_DIM = GMLP_WIDTH // GMLP_GROUPS
N_EXPERTS = 256
TOP_K = 8
N_EXPERT_GROUPS = 8
GROUP_SIZE = N_EXPERTS // N_EXPERT_GROUPS
TOPK_GROUPS = 4
ROUTED_SCALE = 2.5
RMS_EPS = 1e-6
LN_EPS = 1e-5
GN_EPS = 64e-5

VMEM_LIMIT_BYTES = 56 * 1024 * 1024

CHUNK = 64
RWKV_TILE = 256
ROW_BLOCK = 512
EXPERT_RING = 3
INPROJ_TILE = 512
MIX_TILE = 512
ROUTE_TILE = 1024
RANK_TILE = 512
COMBINE_TILE = 256
COMBINE_PARTS = 4
MOE_HALVES = 2
LANES = 128
SC_WINDOW = 128
SC_GATHERS_PER_STEP = 2


def _params(*sem):
    return pltpu.CompilerParams(dimension_semantics=sem, vmem_limit_bytes=VMEM_LIMIT_BYTES)


def _mm(a, b):
    return jnp.dot(a.astype(BF16), b.astype(BF16), preferred_element_type=F32)


def _mm_nt(a, b):
    return lax.dot_general(a.astype(BF16), b.astype(BF16), (((1,), (1,)), ((), ())),
                           preferred_element_type=F32)


def _mm_tn(a, b):
    return lax.dot_general(a.astype(BF16), b.astype(BF16), (((0,), (0,)), ((), ())),
                           preferred_element_type=F32)


def _split3(a):
    hi = a.astype(BF16)
    r1 = a - hi.astype(F32)
    mid = r1.astype(BF16)
    lo = (r1 - mid.astype(F32)).astype(BF16)
    return hi, mid, lo


def _silu(x):
    return x * jax.nn.sigmoid(x)


def _rms(x):
    return x * lax.rsqrt(jnp.mean(x * x, axis=-1, keepdims=True) + RMS_EPS)


def _pack_halves(x):
    n = x.shape[1] // 2
    bits = lax.bitcast_convert_type(x.astype(BF16).astype(F32), jnp.uint32)
    return bits[:, :n] | (bits[:, n:] >> 16)


def _unpack_halves(u):
    hi = lax.bitcast_convert_type(u & jnp.uint32(0xFFFF0000), F32)
    lo = lax.bitcast_convert_type(u << 16, F32)
    return hi, lo


def _mod_kernel(c_ref, w_ref, b_ref, o_ref):
    s = _silu(c_ref[...])
    hi, mid, lo = _split3(s)
    whi, wmid, wlo = _split3(w_ref[...])
    dot = lambda p, q: jnp.dot(p, q, preferred_element_type=F32)
    acc = dot(hi, whi) + (dot(hi, wmid) + dot(mid, whi)) + (dot(hi, wlo) + dot(mid, wmid) + dot(lo, whi))
    o_ref[...] = acc + b_ref[...]


def _mod(c, ada_w, ada_b):
    bsz, d = c.shape
    n = ada_w.shape[1]
    tn = d
    return pl.pallas_call(
        _mod_kernel,
        grid=(n // tn,),
        in_specs=[pl.BlockSpec((bsz, d), lambda j: (0, 0)),
                  pl.BlockSpec((d, tn), lambda j: (0, j)),
                  pl.BlockSpec((1, tn), lambda j: (0, j))],
        out_specs=pl.BlockSpec((bsz, tn), lambda j: (0, j)),
        out_shape=jax.ShapeDtypeStruct((bsz, n), F32),
        compiler_params=_params("arbitrary"),
        name="mod",
    )(c, ada_w, ada_b.reshape(1, n))


def _inproj_kernel(x_ref, mod_ref, g_ref, wr_ref, wuv_ref, wg_ref, pr_ref, puv_ref, pg_ref):
    sh = mod_ref[0, 0:1, :]
    sc = mod_ref[0, 1:2, :]
    h = (_rms(x_ref[...]) * g_ref[...]) * (1.0 + sc) + sh
    hb = h.astype(BF16)
    pr_ref[...] = jnp.dot(hb, wr_ref[...], preferred_element_type=F32).astype(pr_ref.dtype)
    puv_ref[...] = jnp.dot(hb, wuv_ref[...], preferred_element_type=F32).astype(puv_ref.dtype)
    pg_ref[...] = jnp.dot(hb, wg_ref[...], preferred_element_type=F32).astype(pg_ref.dtype)


def _inproj(xf, mod3, norm1_g, w_in, seq):
    t, d = xf.shape
    tm = min(INPROJ_TILE, seq)
    per_seq = seq // tm
    wr = w_in[:, :RWKV_COLS].astype(BF16)
    wuv = w_in[:, RWKV_COLS:RWKV_COLS + 2 * GMLP_WIDTH].astype(BF16)
    wg = w_in[:, RWKV_COLS + 2 * GMLP_WIDTH:].astype(BF16)
    full = lambda a: pl.BlockSpec(a.shape, lambda i: (0,) * a.ndim)
    g = norm1_g.reshape(1, d)
    return pl.pallas_call(
        _inproj_kernel,
        grid=(t // tm,),
        in_specs=[pl.BlockSpec((tm, d), lambda i: (i, 0)),
                  pl.BlockSpec((1,) + mod3.shape[1:], lambda i: (i // per_seq, 0, 0)),
                  full(g), full(wr), full(wuv), full(wg)],
        out_specs=[pl.BlockSpec((tm, wr.shape[1]), lambda i: (i, 0)),
                   pl.BlockSpec((tm, wuv.shape[1]), lambda i: (i, 0)),
                   pl.BlockSpec((tm, wg.shape[1]), lambda i: (i, 0))],
        out_shape=[jax.ShapeDtypeStruct((t, wr.shape[1]), BF16),
                   jax.ShapeDtypeStruct((t, wuv.shape[1]), BF16),
                   jax.ShapeDtypeStruct((t, wg.shape[1]), BF16)],
        compiler_params=_params("arbitrary"),
        name="inproj",
    )(xf, mod3, g, wr, wuv, wg)


def _unit_lower_inverses(ns, row, col):
    c = ns[0].shape[0]
    eye = jnp.where(row == col, 1.0, 0.0).astype(F32)
    blk8 = (row // 8) == (col // 8)
    n8 = [jnp.where(blk8, n, 0.0) for n in ns]
    n8_2 = [_mm(a, a) for a in n8]
    n8_4 = [_mm(a, a) for a in n8_2]
    inv = [eye + a for a in n8]
    inv = [i + _mm(i, b) for i, b in zip(inv, n8_2)]
    inv = [i + _mm(i, b) for i, b in zip(inv, n8_4)]
    s = 8
    while s < c:
        sel = ((row // (2 * s)) == (col // (2 * s))) & ((row // s) != (col // s))
        low = [jnp.where(sel, n, 0.0) for n in ns]
        t1 = [_mm(l, i) for l, i in zip(low, inv)]
        inv = [i + _mm(i, t) for i, t in zip(inv, t1)]
        s *= 2
    return inv


def _rwkv_kernel(p_ref, mu_ref, w0_ref, wup_ref, a0_ref, aup_ref, gup_ref, kk_ref, ka_ref, rk_ref,
                 lng_ref, lnb_ref, bd_ref, y_ref, state_ref, carry_ref, ycat_ref, *, c):
    tm = p_ref.shape[0]
    nq = tm // c
    hd = RWKV_HEAD
    nh = RWKV_HEADS

    @pl.when(pl.program_id(1) == 0)
    def _():
        state_ref[...] = jnp.zeros_like(state_ref)
        carry_ref[...] = jnp.zeros_like(carry_ref)

    p = p_ref[...].astype(F32)
    rowc = lax.broadcasted_iota(jnp.int32, (tm, 1), 0)
    prev = jnp.where(rowc == 0, carry_ref[...], pltpu.roll(p, 1, axis=0))
    carry_ref[...] = p[tm - 1:tm, :]
    ps = p + (prev - p) * mu_ref[...]

    o = RWKV_WIDTH
    r = ps[:, 0:o]
    k = ps[:, o:2 * o]
    v = ps[:, 2 * o:3 * o]
    xw = ps[:, 3 * o:3 * o + DECAY_LORA]
    xa = ps[:, 3 * o + DECAY_LORA:3 * o + DECAY_LORA + AAA_LORA]
    xg = ps[:, 3 * o + DECAY_LORA + AAA_LORA:]

    z = -(w0_ref[...] + _mm(jnp.tanh(xw), wup_ref[...]))
    softplus = jnp.maximum(z, 0.0) + jnp.log1p(jnp.exp(-jnp.abs(z)))
    wlog = -softplus - 0.5
    ld = -jnp.exp(wlog)
    a = jax.nn.sigmoid(a0_ref[...] + _mm(xa, aup_ref[...]))
    g = _mm(jax.nn.sigmoid(xg), gup_ref[...])

    def head_sum(x):
        hi = x.astype(BF16)
        lo = (x - hi.astype(F32)).astype(BF16)
        bd = bd_ref[...]
        return jnp.dot(hi, bd, preferred_element_type=F32) + jnp.dot(lo, bd, preferred_element_type=F32)

    kkf = k * kk_ref[...]
    kkn = kkf * (1.0 / jnp.maximum(jnp.sqrt(head_sum(kkf * kkf)), 1e-12))
    k2 = k * (1.0 + (a - 1.0) * ka_ref[...])
    avec = -kkn
    bvec = kkn * a

    rowt = lax.broadcasted_iota(jnp.int32, (tm, tm), 0)
    colt = lax.broadcasted_iota(jnp.int32, (tm, tm), 1)
    tri = jnp.where((rowt >= colt) & ((rowt // c) == (colt // c)), 1.0, 0.0).astype(BF16)
    hi, mid, lo = _split3(ld)
    dot = lambda q: jnp.dot(tri, q, preferred_element_type=F32)
    cl = dot(hi) + dot(mid) + dot(lo)
    cl_end = jnp.concatenate(
        [jnp.broadcast_to(cl[(q + 1) * c - 1:(q + 1) * c, :], (c, o)) for q in range(nq)], axis=0)
    g_inv = jnp.exp(-cl)
    g_tail = jnp.exp(cl_end - cl)
    g_end = jnp.exp(cl_end)
    rt = (r * jnp.exp(cl)).astype(BF16)
    at = (avec * jnp.exp(cl - ld)).astype(BF16)
    kt = (k2 * g_inv).astype(BF16)
    bt = (bvec * g_inv).astype(BF16)
    kend = (k2 * g_tail).astype(BF16)
    bend = (bvec * g_tail).astype(BF16)
    vb = v.astype(BF16)

    row = lax.broadcasted_iota(jnp.int32, (c, c), 0)
    col = lax.broadcasted_iota(jnp.int32, (c, c), 1)
    strict = row > col
    incl = row >= col
    items = [(q, h) for q in range(nq) for h in range(nh)]
    blk = lambda x, q, h: x[q * c:(q + 1) * c, h * hd:(h + 1) * hd]

    lhs = [jnp.concatenate([blk(at, q, h), blk(rt, q, h)], axis=0) for q, h in items]
    mat_b = [_mm_nt(l, blk(bt, q, h)) for l, (q, h) in zip(lhs, items)]
    mat_k = [_mm_nt(l, blk(kt, q, h)) for l, (q, h) in zip(lhs, items)]
    a_ab = [jnp.where(strict, m[:c], 0.0) for m in mat_b]
    a_rb = [jnp.where(incl, m[c:], 0.0).astype(BF16) for m in mat_b]
    a_ak = [jnp.where(strict, m[:c], 0.0) for m in mat_k]
    a_rk = [jnp.where(incl, m[c:], 0.0) for m in mat_k]
    tinv = _unit_lower_inverses(a_ab, row, col)
    akv = [_mm(m, blk(vb, q, h)) for m, (q, h) in zip(a_ak, items)]
    y_loc = [_mm(m, blk(vb, q, h)) for m, (q, h) in zip(a_rk, items)]
    s_loc = [_mm_tn(blk(vb, q, h), blk(kend, q, h)) for q, h in items]
    wmat = [_mm(t, blk(at, q, h)).astype(BF16) for t, (q, h) in zip(tinv, items)]
    ut = [_mm(t, m) for t, m in zip(tinv, akv)]

    state = [state_ref[h] for h in range(nh)]
    for q in range(nq):
        base = q * nh
        sb = [s.astype(BF16) for s in state]
        u = [_mm_nt(wmat[base + h], sb[h]) + ut[base + h] for h in range(nh)]
        yh = [_mm_nt(blk(rt, q, h), sb[h]) + y_loc[base + h] for h in range(nh)]
        ub = [x.astype(BF16) for x in u]
        yh = [y0 + _mm(a_rb[base + h], ub[h]) for h, y0 in enumerate(yh)]
        state = [state[h] * g_end[q * c:q * c + 1, h * hd:(h + 1) * hd]
                 + _mm_tn(ub[h], blk(bend, q, h)) + s_loc[base + h] for h in range(nh)]
        for h in range(nh):
            ycat_ref[q * c:(q + 1) * c, h * hd:(h + 1) * hd] = yh[h]
    for h in range(nh):
        state_ref[h] = state[h]

    y = ycat_ref[...]
    dev = y - head_sum(y) * (1.0 / hd)
    rstd = lax.rsqrt(head_sum(dev * dev) * (1.0 / hd) + GN_EPS)
    bonus = head_sum(r * k2 * rk_ref[...])
    yn = dev * rstd * lng_ref[...] + lnb_ref[...]
    y_ref[...] = (yn + bonus * v) * g


def _rwkv(p_rwkv, bsz, seq, tshift_mu, w0, w_up, a0, a_up, g_up, k_k, k_a, r_k, ln_g, ln_b):
    t = p_rwkv.shape[0]
    c = min(CHUNK, seq)
    tm = min(RWKV_TILE, seq)
    nstep = seq // tm
    row = lambda a: a.reshape(1, -1)
    head_of = jnp.arange(RWKV_WIDTH, dtype=jnp.int32) // RWKV_HEAD
    block_diag = (head_of[:, None] == head_of[None, :]).astype(BF16)
    args = (row(tshift_mu), row(w0), w_up, row(a0), a_up, g_up, row(k_k), row(k_a), row(r_k),
            row(ln_g), row(ln_b), block_diag)
    full = lambda a: pl.BlockSpec(a.shape, lambda b, j: (0,) * a.ndim)
    return pl.pallas_call(
        functools.partial(_rwkv_kernel, c=c),
        grid=(bsz, nstep),
        in_specs=[pl.BlockSpec((tm, RWKV_COLS), lambda b, j: (b * nstep + j, 0))] + [full(a) for a in args],
        out_specs=pl.BlockSpec((tm, RWKV_WIDTH), lambda b, j: (b * nstep + j, 0)),
        out_shape=jax.ShapeDtypeStruct((t, RWKV_WIDTH), F32),
        scratch_shapes=[pltpu.VMEM((RWKV_HEADS, RWKV_HEAD, RWKV_HEAD), F32),
                        pltpu.VMEM((1, RWKV_COLS), F32),
                        pltpu.VMEM((tm, RWKV_WIDTH), F32)],
        compiler_params=_params("arbitrary", "arbitrary"),
        name="rwkv",
    )(p_rwkv, *args)


def _mix_kernel(puv_ref, pg_ref, ya_ref, x_ref, mod_ref, lng_ref, lnb_ref, ws_ref, bsm_ref,
                woa_ref, wob_ref, wo_ref, n2g_ref, x1_ref, h2_ref, h2p_ref, yb_ref):
    tm = x_ref.shape[0]
    gw = GMLP_WIDTH
    puv = puv_ref[...].astype(F32)
    u = jax.nn.gelu(puv[:, :gw])
    vg = jax.nn.gelu(puv[:, gw:])
    mu = jnp.mean(vg, axis=-1, keepdims=True)
    var = jnp.mean(jnp.square(vg - mu), axis=-1, keepdims=True)
    vln = (vg - mu) * lax.rsqrt(var + LN_EPS) * lng_ref[...] + lnb_ref[...]

    row = lax.broadcasted_iota(jnp.int32, (GMLP_BLOCK, GMLP_BLOCK), 0)
    col = lax.broadcasted_iota(jnp.int32, (GMLP_BLOCK, GMLP_BLOCK), 1)
    gd = GMLP_GROUP_DIM
    for gi in range(GMLP_GROUPS):
        wsm = jnp.where(row >= col, ws_ref[gi], 0.0).astype(BF16)
        for nb in range(tm // GMLP_BLOCK):
            rs = slice(nb * GMLP_BLOCK, (nb + 1) * GMLP_BLOCK)
            cs = slice(gi * gd, (gi + 1) * gd)
            sv = jnp.dot(wsm, vln[rs, cs].astype(BF16), preferred_element_type=F32) + bsm_ref[:, cs]
            yb_ref[rs, cs] = u[rs, cs] * sv

    pg = pg_ref[...].astype(F32)
    d = x_ref.shape[1]
    gate_a = jax.nn.sigmoid(pg[:, :d])
    gate_b = jax.nn.sigmoid(pg[:, d:])
    merged = gate_a * _mm(ya_ref[...], woa_ref[...]) + gate_b * _mm(yb_ref[...], wob_ref[...])
    g1 = mod_ref[0, 2:3, :]
    x1 = x_ref[...] + g1 * _mm(merged, wo_ref[...])
    x1_ref[...] = x1
    sh2 = mod_ref[0, 3:4, :]
    sc2 = mod_ref[0, 4:5, :]
    h2 = (_rms(x1) * n2g_ref[...]) * (1.0 + sc2) + sh2
    h2_ref[...] = h2
    packed = _pack_halves(h2)
    for j in range(h2p_ref.shape[0]):
        h2p_ref[j] = packed[:, j * LANES:(j + 1) * LANES]


def _mix(p_uv, p_gate, y_a, xf, mod3, seq, ln_g, ln_b, ws, bs, w_out_a, w_out_b, w_out, norm2_g):
    t, d = xf.shape
    tm = min(MIX_TILE, seq)
    per_seq = seq // tm
    row = lambda a: a.reshape(1, -1)
    bsm = jnp.repeat(bs.T, GMLP_GROUP_DIM, axis=1)
    consts = (row(ln_g), row(ln_b), ws, bsm, w_out_a.astype(BF16), w_out_b.astype(BF16),
              w_out.astype(BF16), row(norm2_g))
    full = lambda a: pl.BlockSpec(a.shape, lambda i: (0,) * a.ndim)
    tile = lambda a: pl.BlockSpec((tm, a.shape[1]), lambda i: (i, 0))
    return pl.pallas_call(
        _mix_kernel,
        grid=(t // tm,),
        in_specs=[tile(p_uv), tile(p_gate), tile(y_a), tile(xf),
                  pl.BlockSpec((1,) + mod3.shape[1:], lambda i: (i // per_seq, 0, 0))]
                 + [full(a) for a in consts],
        out_specs=[pl.BlockSpec((tm, d), lambda i: (i, 0)), pl.BlockSpec((tm, d), lambda i: (i, 0)),
                   pl.BlockSpec((d // 2 // LANES, tm, LANES), lambda i: (0, i, 0))],
        out_shape=[jax.ShapeDtypeStruct((t, d), F32), jax.ShapeDtypeStruct((t, d), F32),
                   jax.ShapeDtypeStruct((d // 2 // LANES, t, LANES), jnp.uint32)],
        scratch_shapes=[pltpu.VMEM((tm, GMLP_WIDTH), F32)],
        compiler_params=_params("arbitrary"),
        name="mix",
    )(p_uv, p_gate, y_a, xf, mod3, *consts)


def _first_argmax(vals, idx, big):
    m = jnp.max(vals, axis=0, keepdims=True)
    first = jnp.min(jnp.where(vals == m, idx, big), axis=0, keepdims=True)
    return m, first


def _route_kernel(h_ref, rwt_ref, bias_ref, eidx_ref, wts_ref):
    tm = h_ref.shape[0]
    ne = N_EXPERTS
    hhi, hmid, _ = _split3(h_ref[...])
    whi, wmid, _ = _split3(rwt_ref[...])
    nt = lambda p, q: lax.dot_general(p, q, (((1,), (1,)), ((), ())), preferred_element_type=F32)
    logits = nt(whi, hhi) + (nt(whi, hmid) + nt(wmid, hhi))
    scores = jax.nn.sigmoid(logits)
    sel = scores + bias_ref[...]
    neg = -jnp.inf

    gs = GROUP_SIZE
    gidx = lax.broadcasted_iota(jnp.int32, (gs, tm), 0)
    grp_rows = []
    for gi in range(N_EXPERT_GROUPS):
        blk = sel[gi * gs:(gi + 1) * gs, :]
        m1, i1 = _first_argmax(blk, gidx, gs)
        m2 = jnp.max(jnp.where(gidx == i1, neg, blk), axis=0, keepdims=True)
        grp_rows.append(m1 + m2)
    grp = jnp.concatenate(grp_rows, axis=0)

    ng = N_EXPERT_GROUPS
    giota = lax.broadcasted_iota(jnp.int32, (ng, tm), 0)
    gsel = jnp.zeros((ng, tm), jnp.bool_)
    work = grp
    for _ in range(TOPK_GROUPS):
        _, gi1 = _first_argmax(work, giota, ng)
        hit = giota == gi1
        gsel = gsel | hit
        work = jnp.where(hit, neg, work)
    gself = jnp.where(gsel, 1.0, 0.0)
    emask = jnp.concatenate(
        [jnp.broadcast_to(gself[gi:gi + 1, :], (gs, tm)) for gi in range(ng)], axis=0) > 0.5

    eiota = lax.broadcasted_iota(jnp.int32, (ne, tm), 0)
    work = jnp.where(emask, sel, neg)
    idx_rows, w_rows = [], []
    for _ in range(TOP_K):
        _, e1 = _first_argmax(work, eiota, ne)
        hit = eiota == e1
        idx_rows.append(e1)
        w_rows.append(jnp.sum(jnp.where(hit, scores, 0.0), axis=0, keepdims=True))
        work = jnp.where(hit, neg, work)
    eidx_ref[...] = jnp.concatenate(idx_rows, axis=0)
    w = jnp.concatenate(w_rows, axis=0)
    wts_ref[...] = w / jnp.sum(w, axis=0, keepdims=True) * ROUTED_SCALE


def _route(h2, router_w, router_bias, half, nhalves):
    d = h2.shape[1]
    t = h2.shape[0] // nhalves
    tm = min(ROUTE_TILE, t)
    off = half * (t // tm)
    rwt = router_w.T
    bias = router_bias.reshape(N_EXPERTS, 1)
    return pl.pallas_call(
        _route_kernel,
        grid=(t // tm,),
        in_specs=[pl.BlockSpec((tm, d), lambda i: (off + i, 0)),
                  pl.BlockSpec(rwt.shape, lambda i: (0, 0)),
                  pl.BlockSpec(bias.shape, lambda i: (0, 0))],
        out_specs=[pl.BlockSpec((TOP_K, tm), lambda i: (0, i)), pl.BlockSpec((TOP_K, tm), lambda i: (0, i))],
        out_shape=[jax.ShapeDtypeStruct((TOP_K, t), jnp.int32), jax.ShapeDtypeStruct((TOP_K, t), F32)],
        compiler_params=_params("arbitrary"),
        name="route",
    )(h2, rwt, bias)


def _rank_kernel(e_ref, upper_ref, rank_ref, count_ref):
    tm = e_ref.shape[1]
    ne = N_EXPERTS

    @pl.when(pl.program_id(0) == 0)
    def _():
        count_ref[...] = jnp.zeros_like(count_ref)

    e = e_ref[...]
    eiota = lax.broadcasted_iota(jnp.int32, (ne, tm), 0)
    onehot = jnp.zeros((ne, tm), F32)
    for kk in range(TOP_K):
        onehot = onehot + jnp.where(eiota == e[kk:kk + 1, :], 1.0, 0.0)
    before = jnp.dot(onehot.astype(BF16), upper_ref[...], preferred_element_type=F32)
    base = count_ref[:, 0:1]
    tot = before + base
    rows = [jnp.sum(jnp.where(eiota == e[kk:kk + 1, :], tot, 0.0), axis=0, keepdims=True)
            for kk in range(TOP_K)]
    rank_ref[...] = jnp.concatenate(rows, axis=0).astype(jnp.int32)
    count_ref[...] = count_ref[...] + jnp.sum(onehot, axis=1, keepdims=True)


def _rank(eidx):
    k, t = eidx.shape
    tm = min(RANK_TILE, t)
    upper = jnp.triu(jnp.ones((tm, tm), F32), 1).astype(BF16)
    return pl.pallas_call(
        _rank_kernel,
        grid=(t // tm,),
        in_specs=[pl.BlockSpec((k, tm), lambda i: (0, i)), pl.BlockSpec((tm, tm), lambda i: (0, 0))],
        out_specs=[pl.BlockSpec((k, tm), lambda i: (0, i)), pl.BlockSpec((N_EXPERTS, 128), lambda i: (0, 0))],
        out_shape=[jax.ShapeDtypeStruct((k, t), jnp.int32), jax.ShapeDtypeStruct((N_EXPERTS, 128), F32)],
        compiler_params=_params("arbitrary"),
        name="rank",
    )(eidx, upper)


def _dest_kernel(e_ref, rank_ref, pst_ref, dest_ref):
    tm = e_ref.shape[1]
    e = e_ref[...]
    eiota = lax.broadcasted_iota(jnp.int32, (N_EXPERTS, tm), 0)
    pst = pst_ref[...]
    rows = [jnp.sum(jnp.where(eiota == e[kk:kk + 1, :], pst, 0.0), axis=0, keepdims=True)
            for kk in range(TOP_K)]
    dest_ref[...] = jnp.concatenate(rows, axis=0).astype(jnp.int32) + rank_ref[...]


def _dest(eidx, rank, pstarts):
    k, t = eidx.shape
    tm = min(RANK_TILE, t)
    pst = pstarts.astype(F32).reshape(N_EXPERTS, 1)
    return pl.pallas_call(
        _dest_kernel,
        grid=(t // tm,),
        in_specs=[pl.BlockSpec((k, tm), lambda i: (0, i)), pl.BlockSpec((k, tm), lambda i: (0, i)),
                  pl.BlockSpec((N_EXPERTS, 1), lambda i: (0, 0))],
        out_specs=pl.BlockSpec((k, tm), lambda i: (0, i)),
        out_shape=jax.ShapeDtypeStruct((k, t), jnp.int32),
        compiler_params=_params("arbitrary"),
        name="dest",
    )(eidx, rank, pst)


def _sc_scatter(rows, idx, n_out, src_block):
    lanes = rows.shape[1]
    nstep, copies, w = idx.shape
    mesh = plsc.VectorSubcoreMesh(core_axis_name="core", subcore_axis_name="subcore")

    @functools.partial(pl.kernel, out_type=jax.ShapeDtypeStruct((n_out, lanes), rows.dtype), mesh=mesh,
                       scratch_types=[pltpu.SemaphoreType.DMA])
    def scatter_kernel(rows_hbm, idx_hbm, out_hbm, sem):
        def body(rows_vmem, idx_vmem):
            cps = [pltpu.make_async_copy(rows_vmem, out_hbm.at[idx_vmem.at[j]], sem) for j in range(copies)]
            for cp in cps:
                cp.start()
            for cp in cps:
                cp.wait()

        pltpu.emit_pipeline(
            body,
            grid=(nstep,),
            in_specs=[pl.BlockSpec((w, lanes), index_map=lambda i: (src_block(i), 0)),
                      pl.BlockSpec((copies, w), index_map=lambda i: (i, 0))],
            out_specs=[],
            core_axis_name=("core", "subcore"),
            dimension_semantics=(pltpu.PARALLEL,),
        )(rows_hbm, idx_hbm)

    return scatter_kernel(rows, idx.reshape(nstep * copies, w))


def _dispatch(h2p, dest_tk, n_rows, half):
    nchunk, t, lanes = h2p.shape
    k, th = dest_tk.shape
    w = SC_WINDOW
    wh, wt = th // w, t // w
    idx = dest_tk.reshape(k, wh, w).transpose(1, 0, 2)[None]
    idx = idx + (jnp.arange(nchunk, dtype=jnp.int32) * n_rows)[:, None, None, None]
    src_block = lambda i: (i // wh) * wt + half * wh + i % wh
    out = _sc_scatter(h2p.reshape(nchunk * t, lanes), idx.reshape(nchunk * wh, k, w), nchunk * n_rows, src_block)
    return out.reshape(nchunk, n_rows, lanes)


def _expert_kernel(first_ref, nblk_ref, nv_ref, rows_ref, x_hbm, w1_ref, w3_ref, w2_ref, y_hbm,
                   xbuf, ybuf, xsem, ysem, w1b, w3b, w2b):
    e = pl.program_id(0)
    ring, nchunk, rb = xbuf.shape[0], xbuf.shape[1], xbuf.shape[2]
    half = nchunk * LANES
    total = nv_ref[0]

    def x_copy(g, slot):
        return pltpu.make_async_copy(x_hbm.at[:, pl.ds(g * rb, rb), :], xbuf.at[slot], xsem.at[slot])

    def y_copy(g, slot):
        return pltpu.make_async_copy(ybuf.at[slot], y_hbm.at[:, pl.ds(g * rb, rb), :], ysem.at[slot])

    @pl.when(e == 0)
    def _():
        for g0 in range(ring - 1):
            @pl.when(g0 < total)
            def _():
                x_copy(g0, g0).start()

    @pl.when(nblk_ref[e] > 0)
    def _():
        w1b[...] = w1_ref[0].astype(BF16)
        w3b[...] = w3_ref[0].astype(BF16)
        w2b[...] = w2_ref[0].astype(BF16)

    def block(j, carry):
        g = first_ref[e] + j
        slot = g % ring
        x_copy(g, slot).wait()

        @pl.when(g + ring - 1 < total)
        def _():
            x_copy(g + ring - 1, (g + ring - 1) % ring).start()

        live = lax.broadcasted_iota(jnp.int32, (rb, 1), 0) < rows_ref[g]
        x = jnp.concatenate([xbuf[slot, c] for c in range(nchunk)], axis=1)
        hi, lo = _unpack_halves(jnp.where(live, x, jnp.uint32(0)))
        xa = hi.astype(BF16)
        xb = lo.astype(BF16)
        dot = lambda p, q: jnp.dot(p, q, preferred_element_type=F32)
        h1 = dot(xa, w1b[:half, :]) + dot(xb, w1b[half:, :])
        h3 = dot(xa, w3b[:half, :]) + dot(xb, w3b[half:, :])
        hid = _silu(h1) * h3
        packed = _pack_halves(dot(hid.astype(BF16), w2b[...]))

        @pl.when(g >= ring)
        def _():
            y_copy(g - ring, slot).wait()

        for c in range(nchunk):
            ybuf[slot, c] = packed[:, c * LANES:(c + 1) * LANES]
        y_copy(g, slot).start()
        return carry

    lax.fori_loop(0, nblk_ref[e], block, 0)

    @pl.when(e == pl.num_programs(0) - 1)
    def _():
        for back in range(ring, 0, -1):
            @pl.when(total >= back)
            def _():
                y_copy(total - back, (total - back) % ring).wait()


def _expert(x_sorted, first_blk, nblk_e, nvalid, block_rows, w1, w3, w2):
    nchunk, n_rows, lanes = x_sorted.shape
    rb = ROW_BLOCK
    ne, d, f = w1.shape
    grid_spec = pltpu.PrefetchScalarGridSpec(
        num_scalar_prefetch=4,
        grid=(ne,),
        in_specs=[
            pl.BlockSpec(memory_space=pl.ANY),
            pl.BlockSpec((1, d, f), lambda e, *_: (e, 0, 0)),
            pl.BlockSpec((1, d, f), lambda e, *_: (e, 0, 0)),
            pl.BlockSpec((1, f, d), lambda e, *_: (e, 0, 0)),
        ],
        out_specs=pl.BlockSpec(memory_space=pl.ANY),
        scratch_shapes=[pltpu.VMEM((EXPERT_RING, nchunk, rb, lanes), jnp.uint32),
                        pltpu.VMEM((EXPERT_RING, nchunk, rb, lanes), jnp.uint32),
                        pltpu.SemaphoreType.DMA((EXPERT_RING,)), pltpu.SemaphoreType.DMA((EXPERT_RING,)),
                        pltpu.VMEM((d, f), BF16), pltpu.VMEM((d, f), BF16), pltpu.VMEM((f, d), BF16)],
    )
    return pl.pallas_call(
        _expert_kernel,
        grid_spec=grid_spec,
        out_shape=jax.ShapeDtypeStruct((nchunk, n_rows, lanes), jnp.uint32),
        compiler_params=_params("arbitrary"),
        name="expert",
    )(first_blk, nblk_e, nvalid, block_rows, x_sorted, w1, w3, w2)


def _sc_gather(table, idx):
    n = idx.shape[0]
    w = SC_WINDOW
    group = SC_GATHERS_PER_STEP
    mesh = plsc.VectorSubcoreMesh(core_axis_name="core", subcore_axis_name="subcore")

    @functools.partial(pl.kernel, out_type=jax.ShapeDtypeStruct((n, table.shape[1]), table.dtype), mesh=mesh,
                       scratch_types=[pltpu.SemaphoreType.DMA])
    def gather_kernel(table_hbm, idx_hbm, out_hbm, sem):
        def body(idx_vmem, out_vmem):
            copies = [pltpu.make_async_copy(table_hbm.at[idx_vmem.at[j]], out_vmem.at[pl.ds(j * w, w)], sem)
                      for j in range(group)]
            for cp in copies:
                cp.start()
            for cp in copies:
                cp.wait()

        pltpu.emit_pipeline(
            body,
            grid=(n // (group * w),),
            in_specs=[pl.BlockSpec((group, w), index_map=lambda i: (i, 0))],
            out_specs=[pl.BlockSpec((group * w, table.shape[1]), index_map=lambda i: (i, 0))],
            core_axis_name=("core", "subcore"),
            dimension_semantics=(pltpu.PARALLEL,),
        )(idx_hbm, out_hbm)

    return gather_kernel(table, idx.reshape(n // w, w))


def _gather_expert_rows(y_sorted, dest_tk):
    nchunk, n_rows, lanes = y_sorted.shape
    k, t = dest_tk.shape
    idx = dest_tk[None, :, :] + (jnp.arange(nchunk, dtype=jnp.int32) * n_rows)[:, None, None]
    rows = _sc_gather(y_sorted.reshape(nchunk * n_rows, lanes), idx.reshape(-1))
    return rows.reshape(nchunk, k, t, lanes)


def _combine_kernel(yg_ref, wts_ref, h_ref, x1_ref, mod_ref, sw1_ref, sw3_ref, sw2_ref, nfg_ref, *rest):
    o_ref = rest[-1]
    nchunk = yg_ref.shape[0]
    w = wts_ref[...]
    acc_hi = [None] * nchunk
    acc_lo = [None] * nchunk
    for kk in range(TOP_K):
        wk = w[:, kk:kk + 1]
        for j in range(nchunk):
            hi, lo = _unpack_halves(yg_ref[j, kk])
            acc_hi[j] = hi * wk if kk == 0 else acc_hi[j] + hi * wk
            acc_lo[j] = lo * wk if kk == 0 else acc_lo[j] + lo * wk
    routed = jnp.concatenate(acc_hi + acc_lo, axis=1)
    hb = h_ref[...].astype(BF16)
    hid = _silu(jnp.dot(hb, sw1_ref[...], preferred_element_type=F32)) * jnp.dot(
        hb, sw3_ref[...], preferred_element_type=F32)
    shared = jnp.dot(hid.astype(BF16), sw2_ref[...], preferred_element_type=F32)
    g2 = mod_ref[0, 5:6, :]
    x2 = x1_ref[...] + g2 * (routed + shared)
    o_ref[...] = _rms(x2) * nfg_ref[...]


def _combine(y_gathered, wts_tk, h2, x1, mod3, seq, sw1, sw3, sw2, normf_g, part, prev):
    t, d = h2.shape
    nchunk, k, tp, lanes = y_gathered.shape
    tm = min(COMBINE_TILE, seq, tp)
    per_seq = seq // tm
    off = part * (tp // tm)
    consts = (sw1.astype(BF16), sw3.astype(BF16), sw2.astype(BF16), normf_g.reshape(1, d))
    full = lambda a: pl.BlockSpec(a.shape, lambda i: (0,) * a.ndim)
    tile = lambda a: pl.BlockSpec((tm, a.shape[1]), lambda i: (off + i, 0))
    args = [y_gathered, wts_tk, h2, x1, mod3, *consts]
    in_specs = ([pl.BlockSpec((nchunk, k, tm, lanes), lambda i: (0, 0, i, 0)),
                 pl.BlockSpec((tm, k), lambda i: (i, 0)), tile(h2), tile(x1),
                 pl.BlockSpec((1,) + mod3.shape[1:], lambda i: ((off + i) // per_seq, 0, 0))]
                + [full(a) for a in consts])
    aliases = {}
    if prev is not None:
        aliases = {len(args): 0}
        args.append(prev)
        in_specs.append(pl.BlockSpec(memory_space=pl.ANY))
    return pl.pallas_call(
        _combine_kernel,
        grid=(tp // tm,),
        in_specs=in_specs,
        out_specs=pl.BlockSpec((tm, d), lambda i: (off + i, 0)),
        out_shape=jax.ShapeDtypeStruct((t, d), F32),
        input_output_aliases=aliases,
        compiler_params=_params("arbitrary"),
        name="combine",
    )(*args)


def _dispatch_plan(counts, t):
    rb = ROW_BLOCK
    padded = (counts + rb - 1) // rb * rb
    pends = jnp.cumsum(padded)
    pstarts = pends - padded
    n_rows = t * TOP_K + N_EXPERTS * rb
    block_start = jnp.arange(n_rows // rb, dtype=jnp.int32) * rb
    block_e = jnp.sum((pends[None, :] <= block_start[:, None]).astype(jnp.int32), axis=1)
    block_e = jnp.minimum(block_e, N_EXPERTS - 1)
    block_rows = jnp.clip(pstarts[block_e] + counts[block_e] - block_start, 0, rb).astype(jnp.int32)
    nvalid = (pends[-1:] // rb).astype(jnp.int32)
    return pstarts, pstarts // rb, padded // rb, block_rows, nvalid, n_rows


def kernel(x, c, ada_w, ada_b, norm1_g, norm2_g, w_in, tshift_mu, rwkv_w0, rwkv_w_up, rwkv_a0, rwkv_a_up,
           rwkv_g_up, rwkv_k_k, rwkv_k_a, rwkv_r_k, rwkv_ln_g, rwkv_ln_b, gmlp_ln_g, gmlp_ln_b, gmlp_ws,
           gmlp_bs, w_out_a, w_out_b, w_out, router_w, router_bias, exp_w1, exp_w3, exp_w2, shared_w1,
           shared_w3, shared_w2, normf_g):
    bsz, seq, d = x.shape
    t = bsz * seq
    xf = x.reshape(t, d)
    assert ada_w.shape[0] == 1, "single-layer block only"
    l = 0
    mod3 = _mod(c, ada_w[l], ada_b[l]).reshape(bsz, 6, d)
    p_rwkv, p_uv, p_gate = _inproj(xf, mod3, norm1_g[l], w_in[l], seq)
    y_a = _rwkv(p_rwkv, bsz, seq, tshift_mu[l], rwkv_w0[l], rwkv_w_up[l], rwkv_a0[l], rwkv_a_up[l],
                rwkv_g_up[l], rwkv_k_k[l], rwkv_k_a[l], rwkv_r_k[l], rwkv_ln_g[l], rwkv_ln_b[l])
    x1, h2, h2p = _mix(p_uv, p_gate, y_a, xf, mod3, seq, gmlp_ln_g[l], gmlp_ln_b[l], gmlp_ws[l], gmlp_bs[l],
                  w_out_a[l], w_out_b[l], w_out[l], norm2_g[l])
    th = t // MOE_HALVES
    routed = []
    for half in range(MOE_HALVES):
        eidx, wts = _route(h2, router_w[l], router_bias[l], half, MOE_HALVES)
        rank, counts = _rank(eidx)
        counts = counts[:, 0].astype(jnp.int32)
        pstarts, first_blk, nblk_e, block_rows, nvalid, n_rows = _dispatch_plan(counts, th)
        dest = _dest(eidx, rank, pstarts)
        x_sorted = _dispatch(h2p, dest, n_rows, half)
        routed.append((x_sorted, first_blk, nblk_e, nvalid, block_rows, dest, wts.T))
    y_sorted = [_expert(xs, fb, nb, nv, br, exp_w1[l], exp_w3[l], exp_w2[l]) for xs, fb, nb, nv, br, _, _ in routed]
    parts = COMBINE_PARTS // MOE_HALVES
    tp = th // parts
    out = None
    for half in range(MOE_HALVES):
        dest, wts_tk = routed[half][5], routed[half][6]
        for p in range(parts):
            sl = slice(p * tp, (p + 1) * tp)
            y_gathered = _gather_expert_rows(y_sorted[half], dest[:, sl])
            out = _combine(y_gathered, wts_tk[sl], h2, x1, mod3, seq, shared_w1[l], shared_w3[l], shared_w2[l],
                           normf_g, half * parts + p, out)
    return out.reshape(bsz, seq, d)
```

```python
import functools

import jax
import jax.numpy as jnp
from jax import lax
from jax.experimental import pallas as pl
from jax.experimental.pallas import tpu as pltpu
from jax.experimental.pallas import tpu_sc as plsc

F32 = jnp.float32
BF16 = jnp.bfloat16

RWKV_HEAD = 64
RWKV_HEADS = 8
RWKV_WIDTH = RWKV_HEAD * RWKV_HEADS
DECAY_LORA = 64
AAA_LORA = 64
GATE_LORA = 128
RWKV_COLS = 3 * RWKV_WIDTH + DECAY_LORA + AAA_LORA + GATE_LORA
GMLP_WIDTH = 512
GMLP_BLOCK = 128
GMLP_GROUPS = 8
GMLP_GROUP_DIM = GMLP_WIDTH // GMLP_GROUPS
N_EXPERTS = 256
TOP_K = 8
N_EXPERT_GROUPS = 8
GROUP_SIZE = N_EXPERTS // N_EXPERT_GROUPS
TOPK_GROUPS = 4
ROUTED_SCALE = 2.5
RMS_EPS = 1e-6
LN_EPS = 1e-5
GN_EPS = 64e-5

VMEM_LIMIT_BYTES = 56 * 1024 * 1024

CHUNK = 64
RWKV_TILE = 256
ROW_BLOCK = 512
EXPERT_RING = 4
INPROJ_TILE = 512
MIX_TILE = 512
ROUTE_TILE = 1024
RANK_TILE = 512
DEST_TILE = 2048
COMBINE_TILE = 256
COMBINE_PARTS = 4
LANES = 128
SC_WINDOW = 128
SC_GATHERS_PER_STEP = 2


def _params(*sem):
    return pltpu.CompilerParams(dimension_semantics=sem, vmem_limit_bytes=VMEM_LIMIT_BYTES)


def _mm(a, b):
    return jnp.dot(a.astype(BF16), b.astype(BF16), preferred_element_type=F32)


def _mm_nt(a, b):
    return lax.dot_general(a.astype(BF16), b.astype(BF16), (((1,), (1,)), ((), ())),
                           preferred_element_type=F32)


def _mm_tn(a, b):
    return lax.dot_general(a.astype(BF16), b.astype(BF16), (((0,), (0,)), ((), ())),
                           preferred_element_type=F32)


def _split3(a):
    hi = a.astype(BF16)
    r1 = a - hi.astype(F32)
    mid = r1.astype(BF16)
    lo = (r1 - mid.astype(F32)).astype(BF16)
    return hi, mid, lo


def _silu(x):
    return x * jax.nn.sigmoid(x)


def _rms(x):
    return x * lax.rsqrt(jnp.mean(x * x, axis=-1, keepdims=True) + RMS_EPS)


def _pack_halves(x):
    n = x.shape[1] // 2
    bits = lax.bitcast_convert_type(x.astype(BF16).astype(F32), jnp.uint32)
    return bits[:, :n] | (bits[:, n:] >> 16)


def _unpack_halves(u):
    hi = lax.bitcast_convert_type(u & jnp.uint32(0xFFFF0000), F32)
    lo = lax.bitcast_convert_type(u << 16, F32)
    return hi, lo


def _mod_kernel(c_ref, w_ref, b_ref, o_ref):
    s = _silu(c_ref[...])
    hi, mid, lo = _split3(s)
    whi, wmid, wlo = _split3(w_ref[...])
    dot = lambda p, q: jnp.dot(p, q, preferred_element_type=F32)
    acc = dot(hi, whi) + (dot(hi, wmid) + dot(mid, whi)) + (dot(hi, wlo) + dot(mid, wmid) + dot(lo, whi))
    o_ref[...] = acc + b_ref[...]


def _mod(c, ada_w, ada_b):
    bsz, d = c.shape
    n = ada_w.shape[1]
    tn = d
    return pl.pallas_call(
        _mod_kernel,
        grid=(n // tn,),
        in_specs=[pl.BlockSpec((bsz, d), lambda j: (0, 0)),
                  pl.BlockSpec((d, tn), lambda j: (0, j)),
                  pl.BlockSpec((1, tn), lambda j: (0, j))],
        out_specs=pl.BlockSpec((bsz, tn), lambda j: (0, j)),
        out_shape=jax.ShapeDtypeStruct((bsz, n), F32),
        compiler_params=_params("arbitrary"),
        name="mod",
    )(c, ada_w, ada_b.reshape(1, n))


def _inproj_kernel(x_ref, mod_ref, g_ref, wr_ref, wuv_ref, wg_ref, pr_ref, puv_ref, pg_ref):
    sh = mod_ref[0, 0:1, :]
    sc = mod_ref[0, 1:2, :]
    h = (_rms(x_ref[...]) * g_ref[...]) * (1.0 + sc) + sh
    hb = h.astype(BF16)
    pr_ref[...] = jnp.dot(hb, wr_ref[...], preferred_element_type=F32).astype(pr_ref.dtype)
    puv_ref[...] = jnp.dot(hb, wuv_ref[...], preferred_element_type=F32).astype(puv_ref.dtype)
    pg_ref[...] = jnp.dot(hb, wg_ref[...], preferred_element_type=F32).astype(pg_ref.dtype)


def _inproj(xf, mod3, norm1_g, w_in, seq):
    t, d = xf.shape
    tm = min(INPROJ_TILE, seq)
    per_seq = seq // tm
    wr = w_in[:, :RWKV_COLS].astype(BF16)
    wuv = w_in[:, RWKV_COLS:RWKV_COLS + 2 * GMLP_WIDTH].astype(BF16)
    wg = w_in[:, RWKV_COLS + 2 * GMLP_WIDTH:].astype(BF16)
    full = lambda a: pl.BlockSpec(a.shape, lambda i: (0,) * a.ndim)
    g = norm1_g.reshape(1, d)
    return pl.pallas_call(
        _inproj_kernel,
        grid=(t // tm,),
        in_specs=[pl.BlockSpec((tm, d), lambda i: (i, 0)),
                  pl.BlockSpec((1,) + mod3.shape[1:], lambda i: (i // per_seq, 0, 0)),
                  full(g), full(wr), full(wuv), full(wg)],
        out_specs=[pl.BlockSpec((tm, wr.shape[1]), lambda i: (i, 0)),
                   pl.BlockSpec((tm, wuv.shape[1]), lambda i: (i, 0)),
                   pl.BlockSpec((tm, wg.shape[1]), lambda i: (i, 0))],
        out_shape=[jax.ShapeDtypeStruct((t, wr.shape[1]), BF16),
                   jax.ShapeDtypeStruct((t, wuv.shape[1]), BF16),
                   jax.ShapeDtypeStruct((t, wg.shape[1]), BF16)],
        compiler_params=_params("arbitrary"),
        name="inproj",
    )(xf, mod3, g, wr, wuv, wg)


def _unit_lower_inverses(ns, row, col):
    c = ns[0].shape[0]
    eye = jnp.where(row == col, 1.0, 0.0).astype(F32)
    blk8 = (row // 8) == (col // 8)
    n8 = [jnp.where(blk8, n, 0.0) for n in ns]
    n8_2 = [_mm(a, a) for a in n8]
    n8_4 = [_mm(a, a) for a in n8_2]
    inv = [eye + a for a in n8]
    inv = [i + _mm(i, b) for i, b in zip(inv, n8_2)]
    inv = [i + _mm(i, b) for i, b in zip(inv, n8_4)]
    s = 8
    while s < c:
        sel = ((row // (2 * s)) == (col // (2 * s))) & ((row // s) != (col // s))
        low = [jnp.where(sel, n, 0.0) for n in ns]
        t1 = [_mm(l, i) for l, i in zip(low, inv)]
        inv = [i + _mm(i, t) for i, t in zip(inv, t1)]
        s *= 2
    return inv


def _rwkv_kernel(p_ref, mu_ref, w0_ref, wup_ref, a0_ref, aup_ref, gup_ref, kk_ref, ka_ref, rk_ref,
                 lng_ref, lnb_ref, bd_ref, y_ref, state_ref, carry_ref, ycat_ref, *, c):
    tm = p_ref.shape[0]
    nq = tm // c
    hd = RWKV_HEAD
    nh = RWKV_HEADS

    @pl.when(pl.program_id(1) == 0)
    def _():
        state_ref[...] = jnp.zeros_like(state_ref)
        carry_ref[...] = jnp.zeros_like(carry_ref)

    p = p_ref[...].astype(F32)
    rowc = lax.broadcasted_iota(jnp.int32, (tm, 1), 0)
    prev = jnp.where(rowc == 0, carry_ref[...], pltpu.roll(p, 1, axis=0))
    carry_ref[...] = p[tm - 1:tm, :]
    ps = p + (prev - p) * mu_ref[...]

    o = RWKV_WIDTH
    r = ps[:, 0:o]
    k = ps[:, o:2 * o]
    v = ps[:, 2 * o:3 * o]
    xw = ps[:, 3 * o:3 * o + DECAY_LORA]
    xa = ps[:, 3 * o + DECAY_LORA:3 * o + DECAY_LORA + AAA_LORA]
    xg = ps[:, 3 * o + DECAY_LORA + AAA_LORA:]

    z = -(w0_ref[...] + _mm(jnp.tanh(xw), wup_ref[...]))
    softplus = jnp.maximum(z, 0.0) + jnp.log1p(jnp.exp(-jnp.abs(z)))
    wlog = -softplus - 0.5
    ld = -jnp.exp(wlog)
    a = jax.nn.sigmoid(a0_ref[...] + _mm(xa, aup_ref[...]))
    g = _mm(jax.nn.sigmoid(xg), gup_ref[...])

    def head_sum(x):
        hi = x.astype(BF16)
        lo = (x - hi.astype(F32)).astype(BF16)
        bd = bd_ref[...]
        return jnp.dot(hi, bd, preferred_element_type=F32) + jnp.dot(lo, bd, preferred_element_type=F32)

    kkf = k * kk_ref[...]
    kkn = kkf * (1.0 / jnp.maximum(jnp.sqrt(head_sum(kkf * kkf)), 1e-12))
    k2 = k * (1.0 + (a - 1.0) * ka_ref[...])
    avec = -kkn
    bvec = kkn * a

    rowt = lax.broadcasted_iota(jnp.int32, (tm, tm), 0)
    colt = lax.broadcasted_iota(jnp.int32, (tm, tm), 1)
    tri = jnp.where((rowt >= colt) & ((rowt // c) == (colt // c)), 1.0, 0.0).astype(BF16)
    hi, mid, lo = _split3(ld)
    dot = lambda q: jnp.dot(tri, q, preferred_element_type=F32)
    cl = dot(hi) + dot(mid) + dot(lo)
    cl_end = jnp.concatenate(
        [jnp.broadcast_to(cl[(q + 1) * c - 1:(q + 1) * c, :], (c, o)) for q in range(nq)], axis=0)
    g_inv = jnp.exp(-cl)
    g_tail = jnp.exp(cl_end - cl)
    g_end = jnp.exp(cl_end)
    rt = (r * jnp.exp(cl)).astype(BF16)
    at = (avec * jnp.exp(cl - ld)).astype(BF16)
    kt = (k2 * g_inv).astype(BF16)
    bt = (bvec * g_inv).astype(BF16)
    kend = (k2 * g_tail).astype(BF16)
    bend = (bvec * g_tail).astype(BF16)
    vb = v.astype(BF16)

    row = lax.broadcasted_iota(jnp.int32, (c, c), 0)
    col = lax.broadcasted_iota(jnp.int32, (c, c), 1)
    strict = row > col
    incl = row >= col
    items = [(q, h) for q in range(nq) for h in range(nh)]
    blk = lambda x, q, h: x[q * c:(q + 1) * c, h * hd:(h + 1) * hd]

    lhs = [jnp.concatenate([blk(at, q, h), blk(rt, q, h)], axis=0) for q, h in items]
    mat_b = [_mm_nt(l, blk(bt, q, h)) for l, (q, h) in zip(lhs, items)]
    mat_k = [_mm_nt(l, blk(kt, q, h)) for l, (q, h) in zip(lhs, items)]
    a_ab = [jnp.where(strict, m[:c], 0.0) for m in mat_b]
    a_rb = [jnp.where(incl, m[c:], 0.0).astype(BF16) for m in mat_b]
    a_ak = [jnp.where(strict, m[:c], 0.0) for m in mat_k]
    a_rk = [jnp.where(incl, m[c:], 0.0) for m in mat_k]
    tinv = _unit_lower_inverses(a_ab, row, col)
    akv = [_mm(m, blk(vb, q, h)) for m, (q, h) in zip(a_ak, items)]
    y_loc = [_mm(m, blk(vb, q, h)) for m, (q, h) in zip(a_rk, items)]
    s_loc = [_mm_tn(blk(vb, q, h), blk(kend, q, h)) for q, h in items]
    wmat = [_mm(t, blk(at, q, h)).astype(BF16) for t, (q, h) in zip(tinv, items)]
    ut = [_mm(t, m) for t, m in zip(tinv, akv)]

    state = [state_ref[h] for h in range(nh)]
    for q in range(nq):
        base = q * nh
        sb = [s.astype(BF16) for s in state]
        u = [_mm_nt(wmat[base + h], sb[h]) + ut[base + h] for h in range(nh)]
        yh = [_mm_nt(blk(rt, q, h), sb[h]) + y_loc[base + h] for h in range(nh)]
        ub = [x.astype(BF16) for x in u]
        yh = [y0 + _mm(a_rb[base + h], ub[h]) for h, y0 in enumerate(yh)]
        state = [state[h] * g_end[q * c:q * c + 1, h * hd:(h + 1) * hd]
                 + _mm_tn(ub[h], blk(bend, q, h)) + s_loc[base + h] for h in range(nh)]
        for h in range(nh):
            ycat_ref[q * c:(q + 1) * c, h * hd:(h + 1) * hd] = yh[h]
    for h in range(nh):
        state_ref[h] = state[h]

    y = ycat_ref[...]
    dev = y - head_sum(y) * (1.0 / hd)
    rstd = lax.rsqrt(head_sum(dev * dev) * (1.0 / hd) + GN_EPS)
    bonus = head_sum(r * k2 * rk_ref[...])
    yn = dev * rstd * lng_ref[...] + lnb_ref[...]
    y_ref[...] = (yn + bonus * v) * g


def _rwkv(p_rwkv, bsz, seq, tshift_mu, w0, w_up, a0, a_up, g_up, k_k, k_a, r_k, ln_g, ln_b):
    t = p_rwkv.shape[0]
    c = min(CHUNK, seq)
    tm = min(RWKV_TILE, seq)
    nstep = seq // tm
    row = lambda a: a.reshape(1, -1)
    head_of = jnp.arange(RWKV_WIDTH, dtype=jnp.int32) // RWKV_HEAD
    block_diag = (head_of[:, None] == head_of[None, :]).astype(BF16)
    args = (row(tshift_mu), row(w0), w_up, row(a0), a_up, g_up, row(k_k), row(k_a), row(r_k),
            row(ln_g), row(ln_b), block_diag)
    full = lambda a: pl.BlockSpec(a.shape, lambda b, j: (0,) * a.ndim)
    return pl.pallas_call(
        functools.partial(_rwkv_kernel, c=c),
        grid=(bsz, nstep),
        in_specs=[pl.BlockSpec((tm, RWKV_COLS), lambda b, j: (b * nstep + j, 0))] + [full(a) for a in args],
        out_specs=pl.BlockSpec((tm, RWKV_WIDTH), lambda b, j: (b * nstep + j, 0)),
        out_shape=jax.ShapeDtypeStruct((t, RWKV_WIDTH), F32),
        scratch_shapes=[pltpu.VMEM((RWKV_HEADS, RWKV_HEAD, RWKV_HEAD), F32),
                        pltpu.VMEM((1, RWKV_COLS), F32),
                        pltpu.VMEM((tm, RWKV_WIDTH), F32)],
        compiler_params=_params("arbitrary", "arbitrary"),
        name="rwkv",
    )(p_rwkv, *args)


def _mix_kernel(puv_ref, pg_ref, ya_ref, x_ref, mod_ref, lng_ref, lnb_ref, ws_ref, bsm_ref,
                woa_ref, wob_ref, wo_ref, n2g_ref, x1_ref, h2_ref, h2p_ref, yb_ref):
    tm = x_ref.shape[0]
    gw = GMLP_WIDTH
    puv = puv_ref[...].astype(F32)
    u = jax.nn.gelu(puv[:, :gw])
    vg = jax.nn.gelu(puv[:, gw:])
    mu = jnp.mean(vg, axis=-1, keepdims=True)
    var = jnp.mean(jnp.square(vg - mu), axis=-1, keepdims=True)
    vln = (vg - mu) * lax.rsqrt(var + LN_EPS) * lng_ref[...] + lnb_ref[...]

    row = lax.broadcasted_iota(jnp.int32, (GMLP_BLOCK, GMLP_BLOCK), 0)
    col = lax.broadcasted_iota(jnp.int32, (GMLP_BLOCK, GMLP_BLOCK), 1)
    gd = GMLP_GROUP_DIM
    for gi in range(GMLP_GROUPS):
        wsm = jnp.where(row >= col, ws_ref[gi], 0.0).astype(BF16)
        for nb in range(tm // GMLP_BLOCK):
            rs = slice(nb * GMLP_BLOCK, (nb + 1) * GMLP_BLOCK)
            cs = slice(gi * gd, (gi + 1) * gd)
            sv = jnp.dot(wsm, vln[rs, cs].astype(BF16), preferred_element_type=F32) + bsm_ref[:, cs]
            yb_ref[rs, cs] = u[rs, cs] * sv

    pg = pg_ref[...].astype(F32)
    d = x_ref.shape[1]
    gate_a = jax.nn.sigmoid(pg[:, :d])
    gate_b = jax.nn.sigmoid(pg[:, d:])
    merged = gate_a * _mm(ya_ref[...], woa_ref[...]) + gate_b * _mm(yb_ref[...], wob_ref[...])
    g1 = mod_ref[0, 2:3, :]
    x1 = x_ref[...] + g1 * _mm(merged, wo_ref[...])
    x1_ref[...] = x1
    sh2 = mod_ref[0, 3:4, :]
    sc2 = mod_ref[0, 4:5, :]
    h2 = (_rms(x1) * n2g_ref[...]) * (1.0 + sc2) + sh2
    h2_ref[...] = h2
    packed = _pack_halves(h2)
    for j in range(h2p_ref.shape[0]):
        h2p_ref[j] = packed[:, j * LANES:(j + 1) * LANES]


def _mix(p_uv, p_gate, y_a, xf, mod3, seq, ln_g, ln_b, ws, bs, w_out_a, w_out_b, w_out, norm2_g):
    t, d = xf.shape
    tm = min(MIX_TILE, seq)
    per_seq = seq // tm
    row = lambda a: a.reshape(1, -1)
    bsm = jnp.repeat(bs.T, GMLP_GROUP_DIM, axis=1)
    consts = (row(ln_g), row(ln_b), ws, bsm, w_out_a.astype(BF16), w_out_b.astype(BF16),
              w_out.astype(BF16), row(norm2_g))
    full = lambda a: pl.BlockSpec(a.shape, lambda i: (0,) * a.ndim)
    tile = lambda a: pl.BlockSpec((tm, a.shape[1]), lambda i: (i, 0))
    return pl.pallas_call(
        _mix_kernel,
        grid=(t // tm,),
        in_specs=[tile(p_uv), tile(p_gate), tile(y_a), tile(xf),
                  pl.BlockSpec((1,) + mod3.shape[1:], lambda i: (i // per_seq, 0, 0))]
                 + [full(a) for a in consts],
        out_specs=[pl.BlockSpec((tm, d), lambda i: (i, 0)), pl.BlockSpec((tm, d), lambda i: (i, 0)),
                   pl.BlockSpec((d // 2 // LANES, tm, LANES), lambda i: (0, i, 0))],
        out_shape=[jax.ShapeDtypeStruct((t, d), F32), jax.ShapeDtypeStruct((t, d), F32),
                   jax.ShapeDtypeStruct((d // 2 // LANES, t, LANES), jnp.uint32)],
        scratch_shapes=[pltpu.VMEM((tm, GMLP_WIDTH), F32)],
        compiler_params=_params("arbitrary"),
        name="mix",
    )(p_uv, p_gate, y_a, xf, mod3, *consts)


def _first_argmax(vals, idx, big):
    m = jnp.max(vals, axis=0, keepdims=True)
    first = jnp.min(jnp.where(vals == m, idx, big), axis=0, keepdims=True)
    return m, first


def _route_kernel(h_ref, rwt_ref, bias_ref, eidx_ref, wts_ref):
    tm = h_ref.shape[0]
    ne = N_EXPERTS
    hhi, hmid, _ = _split3(h_ref[...])
    whi, wmid, _ = _split3(rwt_ref[...])
    nt = lambda p, q: lax.dot_general(p, q, (((1,), (1,)), ((), ())), preferred_element_type=F32)
    logits = nt(whi, hhi) + (nt(whi, hmid) + nt(wmid, hhi))
    scores = jax.nn.sigmoid(logits)
    sel = scores + bias_ref[...]
    neg = -jnp.inf

    gs = GROUP_SIZE
    gidx = lax.broadcasted_iota(jnp.int32, (gs, tm), 0)
    grp_rows = []
    for gi in range(N_EXPERT_GROUPS):
        blk = sel[gi * gs:(gi + 1) * gs, :]
        m1, i1 = _first_argmax(blk, gidx, gs)
        m2 = jnp.max(jnp.where(gidx == i1, neg, blk), axis=0, keepdims=True)
        grp_rows.append(m1 + m2)
    grp = jnp.concatenate(grp_rows, axis=0)

    ng = N_EXPERT_GROUPS
    giota = lax.broadcasted_iota(jnp.int32, (ng, tm), 0)
    gsel = jnp.zeros((ng, tm), jnp.bool_)
    work = grp
    for _ in range(TOPK_GROUPS):
        _, gi1 = _first_argmax(work, giota, ng)
        hit = giota == gi1
        gsel = gsel | hit
        work = jnp.where(hit, neg, work)
    gself = jnp.where(gsel, 1.0, 0.0)
    emask = jnp.concatenate(
        [jnp.broadcast_to(gself[gi:gi + 1, :], (gs, tm)) for gi in range(ng)], axis=0) > 0.5

    eiota = lax.broadcasted_iota(jnp.int32, (ne, tm), 0)
    work = jnp.where(emask, sel, neg)
    idx_rows, w_rows = [], []
    for _ in range(TOP_K):
        _, e1 = _first_argmax(work, eiota, ne)
        hit = eiota == e1
        idx_rows.append(e1)
        w_rows.append(jnp.sum(jnp.where(hit, scores, 0.0), axis=0, keepdims=True))
        work = jnp.where(hit, neg, work)
    eidx_ref[...] = jnp.concatenate(idx_rows, axis=0)
    w = jnp.concatenate(w_rows, axis=0)
    wts_ref[...] = w / jnp.sum(w, axis=0, keepdims=True) * ROUTED_SCALE


def _route(h2, router_w, router_bias):
    t, d = h2.shape
    tm = min(ROUTE_TILE, t)
    rwt = router_w.T
    bias = router_bias.reshape(N_EXPERTS, 1)
    return pl.pallas_call(
        _route_kernel,
        grid=(t // tm,),
        in_specs=[pl.BlockSpec((tm, d), lambda i: (i, 0)),
                  pl.BlockSpec(rwt.shape, lambda i: (0, 0)),
                  pl.BlockSpec(bias.shape, lambda i: (0, 0))],
        out_specs=[pl.BlockSpec((TOP_K, tm), lambda i: (0, i)), pl.BlockSpec((TOP_K, tm), lambda i: (0, i))],
        out_shape=[jax.ShapeDtypeStruct((TOP_K, t), jnp.int32), jax.ShapeDtypeStruct((TOP_K, t), F32)],
        compiler_params=_params("arbitrary"),
        name="route",
    )(h2, rwt, bias)


def _rank_kernel(e_ref, upper_ref, rank_ref, count_ref):
    tm = e_ref.shape[1]
    ne = N_EXPERTS

    @pl.when(pl.program_id(0) == 0)
    def _():
        count_ref[...] = jnp.zeros_like(count_ref)

    e = e_ref[...]
    eiota = lax.broadcasted_iota(jnp.int32, (ne, tm), 0)
    onehot = jnp.zeros((ne, tm), F32)
    for kk in range(TOP_K):
        onehot = onehot + jnp.where(eiota == e[kk:kk + 1, :], 1.0, 0.0)
    before = jnp.dot(onehot.astype(BF16), upper_ref[...], preferred_element_type=F32)
    base = count_ref[:, 0:1]
    tot = before + base
    rows = [jnp.sum(jnp.where(eiota == e[kk:kk + 1, :], tot, 0.0), axis=0, keepdims=True)
            for kk in range(TOP_K)]
    rank_ref[...] = jnp.concatenate(rows, axis=0).astype(jnp.int32)
    count_ref[...] = count_ref[...] + jnp.sum(onehot, axis=1, keepdims=True)


def _rank(eidx):
    k, t = eidx.shape
    tm = min(RANK_TILE, t)
    upper = jnp.triu(jnp.ones((tm, tm), F32), 1).astype(BF16)
    return pl.pallas_call(
        _rank_kernel,
        grid=(t // tm,),
        in_specs=[pl.BlockSpec((k, tm), lambda i: (0, i)), pl.BlockSpec((tm, tm), lambda i: (0, 0))],
        out_specs=[pl.BlockSpec((k, tm), lambda i: (0, i)), pl.BlockSpec((N_EXPERTS, 128), lambda i: (0, 0))],
        out_shape=[jax.ShapeDtypeStruct((k, t), jnp.int32), jax.ShapeDtypeStruct((N_EXPERTS, 128), F32)],
        compiler_params=_params("arbitrary"),
        name="rank",
    )(eidx, upper)


def _dest_kernel(e_ref, rank_ref, pst_ref, dest_ref):
    tm = e_ref.shape[1]
    e = e_ref[...]
    eiota = lax.broadcasted_iota(jnp.int32, (N_EXPERTS, tm), 0)
    pst = pst_ref[...]
    rows = [jnp.sum(jnp.where(eiota == e[kk:kk + 1, :], pst, 0.0), axis=0, keepdims=True)
            for kk in range(TOP_K)]
    dest_ref[...] = jnp.concatenate(rows, axis=0).astype(jnp.int32) + rank_ref[...]


def _dest(eidx, rank, pstarts):
    k, t = eidx.shape
    tm = min(DEST_TILE, t)
    pst = pstarts.astype(F32).reshape(N_EXPERTS, 1)
    return pl.pallas_call(
        _dest_kernel,
        grid=(t // tm,),
        in_specs=[pl.BlockSpec((k, tm), lambda i: (0, i)), pl.BlockSpec((k, tm), lambda i: (0, i)),
                  pl.BlockSpec((N_EXPERTS, 1), lambda i: (0, 0))],
        out_specs=pl.BlockSpec((k, tm), lambda i: (0, i)),
        out_shape=jax.ShapeDtypeStruct((k, t), jnp.int32),
        compiler_params=_params("arbitrary"),
        name="dest",
    )(eidx, rank, pst)


def _sc_scatter(rows, idx, n_out):
    n, lanes = rows.shape
    nstep, copies, w = idx.shape
    mesh = plsc.VectorSubcoreMesh(core_axis_name="core", subcore_axis_name="subcore")

    @functools.partial(pl.kernel, out_type=jax.ShapeDtypeStruct((n_out, lanes), rows.dtype), mesh=mesh,
                       scratch_types=[pltpu.SemaphoreType.DMA])
    def scatter_kernel(rows_hbm, idx_hbm, out_hbm, sem):
        def body(rows_vmem, idx_vmem):
            cps = [pltpu.make_async_copy(rows_vmem, out_hbm.at[idx_vmem.at[j]], sem) for j in range(copies)]
            for cp in cps:
                cp.start()
            for cp in cps:
                cp.wait()

        pltpu.emit_pipeline(
            body,
            grid=(nstep,),
            in_specs=[pl.BlockSpec((w, lanes), index_map=lambda i: (i, 0)),
                      pl.BlockSpec((copies, w), index_map=lambda i: (i, 0))],
            out_specs=[],
            core_axis_name=("core", "subcore"),
            dimension_semantics=(pltpu.PARALLEL,),
        )(rows_hbm, idx_hbm)

    return scatter_kernel(rows, idx.reshape(nstep * copies, w))


def _dispatch(h2p, dest_tk, n_rows):
    nchunk, t, lanes = h2p.shape
    k = dest_tk.shape[0]
    w = SC_WINDOW
    idx = dest_tk.reshape(k, t // w, w).transpose(1, 0, 2)[None]
    idx = idx + (jnp.arange(nchunk, dtype=jnp.int32) * n_rows)[:, None, None, None]
    out = _sc_scatter(h2p.reshape(nchunk * t, lanes), idx.reshape(nchunk * (t // w), k, w), nchunk * n_rows)
    return out.reshape(nchunk, n_rows, lanes)


def _expert_kernel(first_ref, nblk_ref, nv_ref, rows_ref, x_hbm, w1_ref, w3_ref, w2_ref, y_hbm,
                   xbuf, ybuf, xsem, ysem, w1b, w3b, w2b):
    e = pl.program_id(0)
    ring, nchunk, rb = xbuf.shape[0], xbuf.shape[1], xbuf.shape[2]
    half = nchunk * LANES
    total = nv_ref[0]

    def x_copy(g, slot):
        return pltpu.make_async_copy(x_hbm.at[:, pl.ds(g * rb, rb), :], xbuf.at[slot], xsem.at[slot])

    def y_copy(g, slot):
        return pltpu.make_async_copy(ybuf.at[slot], y_hbm.at[:, pl.ds(g * rb, rb), :], ysem.at[slot])

    @pl.when(e == 0)
    def _():
        for g0 in range(ring - 1):
            @pl.when(g0 < total)
            def _():
                x_copy(g0, g0).start()

    @pl.when(nblk_ref[e] > 0)
    def _():
        w1b[...] = w1_ref[0].astype(BF16)
        w3b[...] = w3_ref[0].astype(BF16)
        w2b[...] = w2_ref[0].astype(BF16)

    def block(j, carry):
        g = first_ref[e] + j
        slot = g % ring
        x_copy(g, slot).wait()

        @pl.when(g + ring - 1 < total)
        def _():
            x_copy(g + ring - 1, (g + ring - 1) % ring).start()

        live = lax.broadcasted_iota(jnp.int32, (rb, 1), 0) < rows_ref[g]
        x = jnp.concatenate([xbuf[slot, c] for c in range(nchunk)], axis=1)
        hi, lo = _unpack_halves(jnp.where(live, x, jnp.uint32(0)))
        xa = hi.astype(BF16)
        xb = lo.astype(BF16)
        dot = lambda p, q: jnp.dot(p, q, preferred_element_type=F32)
        h1 = dot(xa, w1b[:half, :]) + dot(xb, w1b[half:, :])
        h3 = dot(xa, w3b[:half, :]) + dot(xb, w3b[half:, :])
        hid = _silu(h1) * h3
        packed = _pack_halves(dot(hid.astype(BF16), w2b[...]))

        @pl.when(g >= ring)
        def _():
            y_copy(g - ring, slot).wait()

        for c in range(nchunk):
            ybuf[slot, c] = packed[:, c * LANES:(c + 1) * LANES]
        y_copy(g, slot).start()
        return carry

    lax.fori_loop(0, nblk_ref[e], block, 0)

    @pl.when(e == pl.num_programs(0) - 1)
    def _():
        for back in range(ring, 0, -1):
            @pl.when(total >= back)
            def _():
                y_copy(total - back, (total - back) % ring).wait()


def _expert(x_sorted, first_blk, nblk_e, nvalid, block_rows, w1, w3, w2):
    nchunk, n_rows, lanes = x_sorted.shape
    rb = ROW_BLOCK
    ne, d, f = w1.shape
    grid_spec = pltpu.PrefetchScalarGridSpec(
        num_scalar_prefetch=4,
        grid=(ne,),
        in_specs=[
            pl.BlockSpec(memory_space=pl.ANY),
            pl.BlockSpec((1, d, f), lambda e, *_: (e, 0, 0)),
            pl.BlockSpec((1, d, f), lambda e, *_: (e, 0, 0)),
            pl.BlockSpec((1, f, d), lambda e, *_: (e, 0, 0)),
        ],
        out_specs=pl.BlockSpec(memory_space=pl.ANY),
        scratch_shapes=[pltpu.VMEM((EXPERT_RING, nchunk, rb, lanes), jnp.uint32),
                        pltpu.VMEM((EXPERT_RING, nchunk, rb, lanes), jnp.uint32),
                        pltpu.SemaphoreType.DMA((EXPERT_RING,)), pltpu.SemaphoreType.DMA((EXPERT_RING,)),
                        pltpu.VMEM((d, f), BF16), pltpu.VMEM((d, f), BF16), pltpu.VMEM((f, d), BF16)],
    )
    return pl.pallas_call(
        _expert_kernel,
        grid_spec=grid_spec,
        out_shape=jax.ShapeDtypeStruct((nchunk, n_rows, lanes), jnp.uint32),
        compiler_params=_params("arbitrary"),
        name="expert",
    )(first_blk, nblk_e, nvalid, block_rows, x_sorted, w1, w3, w2)


def _sc_gather(table, idx):
    n = idx.shape[0]
    w = SC_WINDOW
    group = SC_GATHERS_PER_STEP
    mesh = plsc.VectorSubcoreMesh(core_axis_name="core", subcore_axis_name="subcore")

    @functools.partial(pl.kernel, out_type=jax.ShapeDtypeStruct((n, table.shape[1]), table.dtype), mesh=mesh,
                       scratch_types=[pltpu.SemaphoreType.DMA])
    def gather_kernel(table_hbm, idx_hbm, out_hbm, sem):
        def body(idx_vmem, out_vmem):
            copies = [pltpu.make_async_copy(table_hbm.at[idx_vmem.at[j]], out_vmem.at[pl.ds(j * w, w)], sem)
                      for j in range(group)]
            for cp in copies:
                cp.start()
            for cp in copies:
                cp.wait()

        pltpu.emit_pipeline(
            body,
            grid=(n // (group * w),),
            in_specs=[pl.BlockSpec((group, w), index_map=lambda i: (i, 0))],
            out_specs=[pl.BlockSpec((group * w, table.shape[1]), index_map=lambda i: (i, 0))],
            core_axis_name=("core", "subcore"),
            dimension_semantics=(pltpu.PARALLEL,),
        )(idx_hbm, out_hbm)

    return gather_kernel(table, idx.reshape(n // w, w))


def _gather_expert_rows(y_sorted, dest_tk):
    nchunk, n_rows, lanes = y_sorted.shape
    k, t = dest_tk.shape
    idx = dest_tk[None, :, :] + (jnp.arange(nchunk, dtype=jnp.int32) * n_rows)[:, None, None]
    rows = _sc_gather(y_sorted.reshape(nchunk * n_rows, lanes), idx.reshape(-1))
    return rows.reshape(nchunk, k, t, lanes)


def _combine_kernel(yg_ref, wts_ref, h_ref, x1_ref, mod_ref, sw1_ref, sw3_ref, sw2_ref, nfg_ref, *rest):
    o_ref = rest[-1]
    nchunk = yg_ref.shape[0]
    w = wts_ref[...]
    acc_hi = [None] * nchunk
    acc_lo = [None] * nchunk
    for kk in range(TOP_K):
        wk = w[:, kk:kk + 1]
        for j in range(nchunk):
            hi, lo = _unpack_halves(yg_ref[j, kk])
            acc_hi[j] = hi * wk if kk == 0 else acc_hi[j] + hi * wk
            acc_lo[j] = lo * wk if kk == 0 else acc_lo[j] + lo * wk
    routed = jnp.concatenate(acc_hi + acc_lo, axis=1)
    hi, lo = _unpack_halves(jnp.concatenate([h_ref[j] for j in range(h_ref.shape[0])], axis=1))
    xa = hi.astype(BF16)
    xb = lo.astype(BF16)
    half = xa.shape[1]
    dot = lambda p, q: jnp.dot(p, q, preferred_element_type=F32)
    hid = _silu(dot(xa, sw1_ref[:half, :]) + dot(xb, sw1_ref[half:, :])) * (
        dot(xa, sw3_ref[:half, :]) + dot(xb, sw3_ref[half:, :]))
    shared = dot(hid.astype(BF16), sw2_ref[...])
    g2 = mod_ref[0, 5:6, :]
    x2 = x1_ref[...] + g2 * (routed + shared)
    o_ref[...] = _rms(x2) * nfg_ref[...]


def _combine(y_gathered, wts_tk, h2p, x1, mod3, seq, sw1, sw3, sw2, normf_g, part, prev):
    t, d = x1.shape
    nchunk, k, tp, lanes = y_gathered.shape
    tm = min(COMBINE_TILE, seq, tp)
    per_seq = seq // tm
    off = part * (tp // tm)
    consts = (sw1.astype(BF16), sw3.astype(BF16), sw2.astype(BF16), normf_g.reshape(1, d))
    full = lambda a: pl.BlockSpec(a.shape, lambda i: (0,) * a.ndim)
    tile = lambda a: pl.BlockSpec((tm, a.shape[1]), lambda i: (off + i, 0))
    args = [y_gathered, wts_tk, h2p, x1, mod3, *consts]
    in_specs = ([pl.BlockSpec((nchunk, k, tm, lanes), lambda i: (0, 0, i, 0)),
                 pl.BlockSpec((tm, k), lambda i: (i, 0)),
                 pl.BlockSpec((h2p.shape[0], tm, lanes), lambda i: (0, off + i, 0)), tile(x1),
                 pl.BlockSpec((1,) + mod3.shape[1:], lambda i: ((off + i) // per_seq, 0, 0))]
                + [full(a) for a in consts])
    aliases = {}
    if prev is not None:
        aliases = {len(args): 0}
        args.append(prev)
        in_specs.append(pl.BlockSpec(memory_space=pl.ANY))
    return pl.pallas_call(
        _combine_kernel,
        grid=(tp // tm,),
        in_specs=in_specs,
        out_specs=pl.BlockSpec((tm, d), lambda i: (off + i, 0)),
        out_shape=jax.ShapeDtypeStruct((t, d), F32),
        input_output_aliases=aliases,
        compiler_params=_params("arbitrary"),
        name="combine",
    )(*args)


def _dispatch_plan(counts, t):
    rb = ROW_BLOCK
    padded = (counts + rb - 1) // rb * rb
    pends = jnp.cumsum(padded)
    pstarts = pends - padded
    n_rows = t * TOP_K + N_EXPERTS * rb
    block_start = jnp.arange(n_rows // rb, dtype=jnp.int32) * rb
    block_e = jnp.sum((pends[None, :] <= block_start[:, None]).astype(jnp.int32), axis=1)
    block_e = jnp.minimum(block_e, N_EXPERTS - 1)
    block_rows = jnp.clip(pstarts[block_e] + counts[block_e] - block_start, 0, rb).astype(jnp.int32)
    nvalid = (pends[-1:] // rb).astype(jnp.int32)
    return pstarts, pstarts // rb, padded // rb, block_rows, nvalid, n_rows


def kernel(x, c, ada_w, ada_b, norm1_g, norm2_g, w_in, tshift_mu, rwkv_w0, rwkv_w_up, rwkv_a0, rwkv_a_up,
           rwkv_g_up, rwkv_k_k, rwkv_k_a, rwkv_r_k, rwkv_ln_g, rwkv_ln_b, gmlp_ln_g, gmlp_ln_b, gmlp_ws,
           gmlp_bs, w_out_a, w_out_b, w_out, router_w, router_bias, exp_w1, exp_w3, exp_w2, shared_w1,
           shared_w3, shared_w2, normf_g):
    bsz, seq, d = x.shape
    t = bsz * seq
    xf = x.reshape(t, d)
    assert ada_w.shape[0] == 1, "single-layer block only"
    l = 0
    mod3 = _mod(c, ada_w[l], ada_b[l]).reshape(bsz, 6, d)
    p_rwkv, p_uv, p_gate = _inproj(xf, mod3, norm1_g[l], w_in[l], seq)
    y_a = _rwkv(p_rwkv, bsz, seq, tshift_mu[l], rwkv_w0[l], rwkv_w_up[l], rwkv_a0[l], rwkv_a_up[l],
                rwkv_g_up[l], rwkv_k_k[l], rwkv_k_a[l], rwkv_r_k[l], rwkv_ln_g[l], rwkv_ln_b[l])
    x1, h2, h2p = _mix(p_uv, p_gate, y_a, xf, mod3, seq, gmlp_ln_g[l], gmlp_ln_b[l], gmlp_ws[l], gmlp_bs[l],
                  w_out_a[l], w_out_b[l], w_out[l], norm2_g[l])
    eidx, wts = _route(h2, router_w[l], router_bias[l])
    rank, counts = _rank(eidx)
    counts = counts[:, 0].astype(jnp.int32)
    pstarts, first_blk, nblk_e, block_rows, nvalid, n_rows = _dispatch_plan(counts, t)
    dest = _dest(eidx, rank, pstarts)
    x_sorted = _dispatch(h2p, dest, n_rows)
    y_sorted = _expert(x_sorted, first_blk, nblk_e, nvalid, block_rows, exp_w1[l], exp_w3[l], exp_w2[l])
    wts_tk = wts.T
    tp = t // COMBINE_PARTS
    out = None
    for part in range(COMBINE_PARTS):
        sl = slice(part * tp, (part + 1) * tp)
        y_gathered = _gather_expert_rows(y_sorted, dest[:, sl])
        out = _combine(y_gathered, wts_tk[sl], h2p, x1, mod3, seq, shared_w1[l], shared_w3[l], shared_w2[l],
                       normf_g, part, out)
    return out.reshape(bsz, seq, d)
```
